```python
import math
import jax, jax.numpy as jnp
from jax import lax
import numpy as np

D_MODEL = 2048
BATCH = 2
SEQ = 8192
DEPTH = 1

CHUNK = 64
EPS = 1e-6

POOL_WIDTH = D_MODEL
POOL_WINDOWS = (2, 4, 8, 16)
POOL_GROUPS = len(POOL_WINDOWS)
POOL_GROUP_DIM = POOL_WIDTH // POOL_GROUPS

SSD_EXPAND = 2
SSD_INNER = SSD_EXPAND * D_MODEL
SSD_HEAD_DIM = 64
SSD_HEADS = SSD_INNER // SSD_HEAD_DIM
SSD_GROUPS = 8
SSD_HEADS_PER_GROUP = SSD_HEADS // SSD_GROUPS
SSD_STATE = 128
SSD_CONV = 4
SSD_BC_DIM = SSD_GROUPS * SSD_STATE
SSD_CONV_DIM = SSD_INNER + 2 * SSD_BC_DIM

N_BRANCHES = 2
_SPLIT_1 = POOL_WIDTH
_SPLIT_2 = _SPLIT_1 + SSD_INNER
_SPLIT_3 = _SPLIT_2 + SSD_CONV_DIM
_SPLIT_4 = _SPLIT_3 + SSD_HEADS
_SPLIT_5 = _SPLIT_4 + D_MODEL
IN_PROJ_DIM = _SPLIT_5 + D_MODEL

N_EXPERTS = 32
TOP_K = 4
D_FF = D_MODEL
SWIGLU_LIMIT = 7.0
SWIGLU_ALPHA = 1.702
EXPERT_BLOCK = 128

N_MOD = 6

kernel_name = "hybrid_pool_ssd_moe_adaln_block"


def rmsnorm(x):
    x32 = x.astype(jnp.float32)
    return x32 * lax.rsqrt(jnp.mean(x32 * x32, axis=-1, keepdims=True) + EPS)


def modulate(h, shift, scale):
    return h * (1.0 + scale[:, None, :]) + shift[:, None, :]


def pool_mixer(p, pool_w, pool_scale):
    bsz, s, _ = p.shape
    p = p.astype(jnp.float32).reshape(bsz, s, POOL_GROUPS, POOL_GROUP_DIM)
    cs = jnp.cumsum(p, axis=1)
    pos = jnp.arange(1, s + 1, dtype=jnp.float32)
    means = []
    for g, w in enumerate(POOL_WINDOWS):
        hi = cs[:, :, g]
        lo = jnp.concatenate([jnp.zeros((bsz, w, POOL_GROUP_DIM), jnp.float32),
                              cs[:, :s - w, g]], axis=1)
        count = jnp.minimum(pos, float(w))
        means.append((hi - lo) / count[None, :, None])
    mean = jnp.stack(means, axis=2)
    mixed = jnp.einsum('bsgc,gcd->bsgd', mean - p, pool_w)
    return mixed.reshape(bsz, s, POOL_WIDTH) * pool_scale


def causal_depthwise_conv(x, w, b):
    y = lax.conv_general_dilated(
        x, w.astype(x.dtype)[:, None, :], window_strides=(1,),
        padding=[(SSD_CONV - 1, 0)], dimension_numbers=('NWC', 'WIO', 'NWC'),
        feature_group_count=x.shape[-1])
    return y + b


def ssd_scan(xh, dt, a, bm, cm):
    bsz, s = xh.shape[:2]
    nc = s // CHUNK
    g, hg, p, n = SSD_GROUPS, SSD_HEADS_PER_GROUP, SSD_HEAD_DIM, SSD_STATE
    x = xh.reshape(bsz, nc, CHUNK, g, hg, p)
    dt = dt.reshape(bsz, nc, CHUNK, g, hg)
    bc = bm.reshape(bsz, nc, CHUNK, g, n)
    cc = cm.reshape(bsz, nc, CHUNK, g, n)
    da_cs = jnp.cumsum(dt * a.reshape(g, hg), axis=2)
    xdt = x * dt[..., None]
    diff = da_cs[:, :, :, None] - da_cs[:, :, None, :]
    causal = jnp.tril(jnp.ones((CHUNK, CHUNK), dtype=bool))
    lmat = jnp.exp(jnp.where(causal[:, :, None, None], diff, -jnp.inf))
    cb = jnp.einsum('bclgn,bcsgn->bclsg', cc, bc)
    y_diag = jnp.einsum('bclsgh,bcsghp->bclghp', cb[..., None] * lmat, xdt)
    decay_to_end = jnp.exp(da_cs[:, :, -1:] - da_cs)
    states = jnp.einsum('bclgn,bclghp->bcghpn', bc, xdt * decay_to_end[..., None])
    chunk_decay = jnp.exp(da_cs[:, :, -1])

    def step(carry, inp):
        st, dec = inp
        return carry * dec[..., None, None] + st, carry

    _, prev = lax.scan(step, jnp.zeros_like(states[:, 0]),
                       (jnp.moveaxis(states, 1, 0), jnp.moveaxis(chunk_decay, 1, 0)))
    prev = jnp.moveaxis(prev, 0, 1)
    y_off = jnp.einsum('bclgn,bcghpn->bclghp', cc, prev) * jnp.exp(da_cs)[..., None]
    return (y_diag + y_off).reshape(bsz, s, SSD_HEADS, p)


def hybrid_mixer(u, w_in, pool_w, pool_scale, conv_w, conv_b, dt_bias, a_log, d_skip,
                 ssd_norm_w, w_branch_pool, w_branch_ssd, w_out):
    bsz, s, _ = u.shape
    proj = jnp.matmul(u, w_in)
    p_in, z, xbc, dt_raw, g_pool, g_ssd = jnp.split(
        proj, [_SPLIT_1, _SPLIT_2, _SPLIT_3, _SPLIT_4, _SPLIT_5], axis=-1)
    y_pool = pool_mixer(p_in, pool_w, pool_scale)
    xbc = jax.nn.silu(causal_depthwise_conv(xbc.astype(jnp.float32), conv_w, conv_b))
    xs, bm, cm = jnp.split(xbc, [SSD_INNER, SSD_INNER + SSD_BC_DIM], axis=-1)
    dt = jax.nn.softplus(dt_raw.astype(jnp.float32) + dt_bias.astype(jnp.float32))
    a = -jnp.exp(a_log.astype(jnp.float32))
    xh = xs.reshape(bsz, s, SSD_HEADS, SSD_HEAD_DIM)
    y = ssd_scan(xh, dt, a,
                 bm.reshape(bsz, s, SSD_GROUPS, SSD_STATE),
                 cm.reshape(bsz, s, SSD_GROUPS, SSD_STATE))
    y = y + d_skip.astype(jnp.float32)[:, None] * xh
    y = y.reshape(bsz, s, SSD_INNER) * jax.nn.silu(z.astype(jnp.float32))
    y = rmsnorm(y.reshape(bsz, s, SSD_GROUPS, SSD_INNER // SSD_GROUPS)).reshape(bsz, s, SSD_INNER)
    y_ssd = y * ssd_norm_w
    merged = (jax.nn.sigmoid(g_pool.astype(jnp.float32)) * jnp.matmul(y_pool, w_branch_pool)
              + jax.nn.sigmoid(g_ssd.astype(jnp.float32)) * jnp.matmul(y_ssd, w_branch_ssd))
    return jnp.matmul(merged, w_out)


def moe_ffn(h, w_router, b_router, w_gate_up, b_gate_up, w_down, b_down):
    bsz, s, d = h.shape
    t = bsz * s
    n_assign = t * TOP_K
    n_blocks = n_assign // EXPERT_BLOCK + N_EXPERTS
    n_slots = n_blocks * EXPERT_BLOCK
    xf = h.reshape(t, d)
    logits = jnp.matmul(xf, w_router).astype(jnp.float32) + b_router.astype(jnp.float32)
    top_vals, top_idx = lax.top_k(logits, TOP_K)
    gates = jax.nn.softmax(top_vals, axis=-1)
    flat_e = top_idx.reshape(-1)
    order = jnp.argsort(flat_e)
    sorted_e = flat_e[order]
    sorted_tok = (order // TOP_K).astype(jnp.int32)
    sorted_gate = gates.reshape(-1)[order]
    counts = jnp.bincount(flat_e, length=N_EXPERTS)
    padded = (counts + EXPERT_BLOCK - 1) // EXPERT_BLOCK * EXPERT_BLOCK
    padded_end = jnp.cumsum(padded)
    padded_start = padded_end - padded
    group_start = jnp.cumsum(counts) - counts
    slot = padded_start[sorted_e] + jnp.arange(n_assign) - group_start[sorted_e]
    slot_tok = jnp.full((n_slots,), t, jnp.int32).at[slot].set(sorted_tok)
    slot_gate = jnp.zeros((n_slots,), jnp.float32).at[slot].set(sorted_gate)
    block_expert = jnp.minimum(
        jnp.searchsorted(padded_end, jnp.arange(n_blocks) * EXPERT_BLOCK, side='right'),
        N_EXPERTS - 1)
    x_pad = jnp.concatenate([xf, jnp.zeros((1, d), xf.dtype)], axis=0).astype(w_gate_up.dtype)
    x_blocks = x_pad[slot_tok].reshape(n_blocks, EXPERT_BLOCK, d)

    def expert_block(args):
        xb, e = args
        gu = jnp.matmul(xb, w_gate_up[e]).astype(jnp.float32) + b_gate_up[e].astype(jnp.float32)
        gate, up = jnp.split(gu, 2, axis=-1)
        gate = jnp.minimum(gate, SWIGLU_LIMIT)
        up = jnp.clip(up, -SWIGLU_LIMIT, SWIGLU_LIMIT)
        act = (up + 1.0) * gate * jax.nn.sigmoid(SWIGLU_ALPHA * gate)
        return (jnp.matmul(act.astype(w_down.dtype), w_down[e]).astype(jnp.float32)
                + b_down[e].astype(jnp.float32))

    y_blocks = lax.map(expert_block, (x_blocks, block_expert))
    y = jax.ops.segment_sum(y_blocks.reshape(n_slots, d) * slot_gate[:, None], slot_tok,
                            num_segments=t + 1)
    return y[:t].reshape(bsz, s, d)


def setup_inputs(seed: int = 0) -> dict:
    key = jax.random.key(seed)
    ks = jax.random.split(key, 24)
    f32 = jnp.float32
    L = DEPTH

    def nrm(k, shape, scale):
        return jax.random.normal(k, shape, f32) * scale

    dt0 = jnp.exp(jax.random.uniform(ks[9], (L, SSD_HEADS), f32, math.log(1e-3), math.log(1e-1)))
    return {
        "x": nrm(ks[0], (BATCH, SEQ, D_MODEL), 1.0),
        "c": nrm(ks[1], (BATCH, D_MODEL), 1.0),
        "w_ada": nrm(ks[2], (L, D_MODEL, N_MOD * D_MODEL), 0.5 * D_MODEL ** -0.5),
        "b_ada": nrm(ks[3], (L, N_MOD * D_MODEL), 0.02),
        "w_in": nrm(ks[4], (L, D_MODEL, IN_PROJ_DIM), D_MODEL ** -0.5),
        "pool_w": nrm(ks[5], (L, POOL_GROUPS, POOL_GROUP_DIM, POOL_GROUP_DIM), POOL_GROUP_DIM ** -0.5),
        "pool_scale": 1.0 + nrm(ks[6], (L, POOL_WIDTH), 0.02),
        "conv_w": nrm(ks[7], (L, SSD_CONV, SSD_CONV_DIM), SSD_CONV ** -0.5),
        "conv_b": nrm(ks[8], (L, SSD_CONV_DIM), 0.02),
        "dt_bias": dt0 + jnp.log(-jnp.expm1(-dt0)),
        "a_log": jnp.log(jax.random.uniform(ks[10], (L, SSD_HEADS), f32, 1.0, 16.0)),
        "d_skip": 1.0 + nrm(ks[11], (L, SSD_HEADS), 0.1),
        "ssd_norm_w": 1.0 + nrm(ks[12], (L, SSD_INNER), 0.02),
        "w_branch_pool": nrm(ks[13], (L, POOL_WIDTH, D_MODEL), POOL_WIDTH ** -0.5),
        "w_branch_ssd": nrm(ks[14], (L, SSD_INNER, D_MODEL), SSD_INNER ** -0.5),
        "w_out": nrm(ks[15], (L, D_MODEL, D_MODEL), D_MODEL ** -0.5),
        "w_router": nrm(ks[16], (L, D_MODEL, N_EXPERTS), D_MODEL ** -0.5),
        "b_router": nrm(ks[17], (L, N_EXPERTS), 0.01),
        "w_gate_up": nrm(ks[18], (L, N_EXPERTS, D_MODEL, 2 * D_FF), D_MODEL ** -0.5),
        "b_gate_up": nrm(ks[19], (L, N_EXPERTS, 2 * D_FF), 0.01),
        "w_down": nrm(ks[20], (L, N_EXPERTS, D_FF, D_MODEL), D_FF ** -0.5),
        "b_down": nrm(ks[21], (L, N_EXPERTS, D_MODEL), 0.01),
        "final_norm_w": 1.0 + nrm(ks[22], (D_MODEL,), 0.02),
    }


def reference(x, c, w_ada, b_ada, w_in, pool_w, pool_scale, conv_w, conv_b, dt_bias, a_log,
              d_skip, ssd_norm_w, w_branch_pool, w_branch_ssd, w_out, w_router, b_router,
              w_gate_up, b_gate_up, w_down, b_down, final_norm_w):
    h = x.astype(jnp.float32)
    c_act = jax.nn.silu(c.astype(jnp.float32))
    for layer in range(DEPTH):
        mod = jnp.matmul(c_act, w_ada[layer]) + b_ada[layer]
        sh1, sc1, g1, sh2, sc2, g2 = jnp.split(mod.astype(jnp.float32), N_MOD, axis=-1)
        u = modulate(rmsnorm(h), sh1, sc1)
        mix = hybrid_mixer(u, w_in[layer], pool_w[layer], pool_scale[layer], conv_w[layer],
                           conv_b[layer], dt_bias[layer], a_log[layer], d_skip[layer],
                           ssd_norm_w[layer], w_branch_pool[layer], w_branch_ssd[layer],
                           w_out[layer])
        h = h + g1[:, None, :] * mix.astype(jnp.float32)
        u = modulate(rmsnorm(h), sh2, sc2)
        ffn = moe_ffn(u, w_router[layer], b_router[layer], w_gate_up[layer], b_gate_up[layer],
                      w_down[layer], b_down[layer])
        h = h + g2[:, None, :] * ffn
    return (rmsnorm(h) * final_norm_w).astype(x.dtype)
```

```python
import functools

import jax
import jax.numpy as jnp
from jax import lax
from jax.experimental import pallas as pl
from jax.experimental.pallas import tpu as pltpu

F32 = jnp.float32
BF16 = jnp.bfloat16
HIGHEST = lax.Precision.HIGHEST

EPS = 1e-6
POOL_WINDOWS = (2, 4, 8, 16)
SSD_GROUPS = 8
SSD_STATE = 128
SSD_CONV = 4
SSD_HEAD_DIM = 64
TOP_K = 4
SWIGLU_LIMIT = 7.0
SWIGLU_ALPHA = 1.702
N_MOD = 6

LANES = 128
CONV_HALO = 8
V7X_VMEM_LIMIT = 56 * 1024 * 1024

SEQ_TILE = 256
INPROJ_TM, INPROJ_TN = 1024, 512
MERGE_TM, MERGE_TN = 512, 512
OUT_TM = 256
MOE_BM, MOE_TF = 512, 512
DISPATCH_TS = 256
FINAL_TC = 256


def _params(*sem):
    return pltpu.CompilerParams(dimension_semantics=sem, vmem_limit_bytes=V7X_VMEM_LIMIT)


def _silu(v):
    return v * jax.nn.sigmoid(v)


def _ada_kernel(cb_ref, w_ref, b_ref, o_ref):
    nb, tn = cb_ref.shape[0], w_ref.shape[1]
    for b in range(nb):
        ca = _silu(cb_ref[b])
        cols = [jnp.sum(w_ref[:, j * LANES:(j + 1) * LANES] * ca, axis=0, keepdims=True)
                for j in range(tn // LANES)]
        o_ref[b:b + 1, :] = jnp.concatenate(cols, axis=1) + b_ref[...]


def _ada(c, w_ada, b_ada):
    nb, k = c.shape
    n = w_ada.shape[1]
    tn = 1024 if n % 1024 == 0 else n
    cb = jnp.broadcast_to(c.astype(F32)[:, :, None], (nb, k, LANES))
    return pl.pallas_call(
        _ada_kernel,
        grid=(n // tn,),
        in_specs=[pl.BlockSpec((nb, k, LANES), lambda j: (0, 0, 0)),
                  pl.BlockSpec((k, tn), lambda j: (0, j)),
                  pl.BlockSpec((1, tn), lambda j: (0, j))],
        out_specs=pl.BlockSpec((nb, tn), lambda j: (0, j)),
        out_shape=jax.ShapeDtypeStruct((nb, n), F32),
        compiler_params=_params("arbitrary"),
        name="ada",
    )(cb, w_ada, b_ada.reshape(1, n))


def _inproj_kernel(x_ref, mod_ref, w_ref, wdt_ref, o_ref, dt_ref, u_s, *, rows_per_chunk):
    j = pl.program_id(1)
    tm = x_ref.shape[0]

    @pl.when(j == 0)
    def _():
        sh = mod_ref[0, 0:1, :]
        sc = mod_ref[0, 1:2, :]

        def body(r, carry):
            rows = pl.ds(pl.multiple_of(r * rows_per_chunk, rows_per_chunk), rows_per_chunk)
            xv = x_ref[rows, :]
            ms = jnp.mean(xv * xv, axis=-1, keepdims=True)
            u = xv * lax.rsqrt(ms + EPS) * (1.0 + sc) + sh
            u_s[rows, :] = u.astype(BF16)
            dt_ref[rows, :] = jnp.dot(u, wdt_ref[...], precision=HIGHEST, preferred_element_type=F32)
            return carry

        lax.fori_loop(0, tm // rows_per_chunk, body, 0)

    o_ref[...] = jnp.dot(u_s[...], w_ref[...], preferred_element_type=F32).astype(o_ref.dtype)


def _inproj(x2, mod3, w_main, w_dt, seq):
    t, d = x2.shape
    n = w_main.shape[1]
    tm = min(INPROJ_TM, seq)
    tn = INPROJ_TN
    tiles_per_seq = seq // tm
    return pl.pallas_call(
        functools.partial(_inproj_kernel, rows_per_chunk=min(128, tm)),
        grid=(t // tm, n // tn),
        in_specs=[pl.BlockSpec((tm, d), lambda i, j: (i, 0)),
                  pl.BlockSpec((1, N_MOD, d), lambda i, j: (i // tiles_per_seq, 0, 0)),
                  pl.BlockSpec((d, tn), lambda i, j: (0, j)),
                  pl.BlockSpec((d, LANES), lambda i, j: (0, 0))],
        out_specs=[pl.BlockSpec((tm, tn), lambda i, j: (i, j)),
                   pl.BlockSpec((tm, LANES), lambda i, j: (i, 0))],
        out_shape=[jax.ShapeDtypeStruct((t, n), BF16), jax.ShapeDtypeStruct((t, LANES), F32)],
        scratch_shapes=[pltpu.VMEM((tm, d), BF16)],
        compiler_params=_params("arbitrary", "arbitrary"),
        name="inproj",
    )(x2, mod3, w_main, w_dt)


def _pool_kernel(p_ref, pw_ref, ps_ref, o_ref, prev_s, *, tiles_per_seq):
    i = pl.program_id(0)
    tl, d = p_ref.shape
    gd = d // len(POOL_WINDOWS)
    it = i % tiles_per_seq

    @pl.when(it == 0)
    def _():
        prev_s[...] = jnp.zeros_like(prev_s)

    row = lax.broadcasted_iota(jnp.int32, (tl, 2 * tl), 0)
    col = lax.broadcasted_iota(jnp.int32, (tl, 2 * tl), 1)
    pos = (it * tl + lax.broadcasted_iota(jnp.int32, (tl, 1), 0) + 1).astype(F32)
    for g, w in enumerate(POOL_WINDOWS):
        sl = slice(g * gd, (g + 1) * gd)
        cur = p_ref[:, sl]
        ext = jnp.concatenate([prev_s[:, sl], cur], axis=0)
        band = ((col <= row + tl) & (col > row + tl - w)).astype(BF16)
        win_sum = jnp.dot(band, ext, preferred_element_type=F32)
        mean = win_sum / jnp.minimum(pos, float(w))
        dlt = (mean - cur.astype(F32)).astype(BF16)
        mixed = jnp.dot(dlt, pw_ref[g], preferred_element_type=F32)
        o_ref[:, sl] = (mixed * ps_ref[:, sl]).astype(o_ref.dtype)
    prev_s[...] = p_ref[...]


def _pool(proj, p_blk, pool_w_bf, pool_scale, seq, d):
    t = proj.shape[0]
    tl = SEQ_TILE
    g, gd = pool_w_bf.shape[0], pool_w_bf.shape[1]
    return pl.pallas_call(
        functools.partial(_pool_kernel, tiles_per_seq=seq // tl),
        grid=(t // tl,),
        in_specs=[pl.BlockSpec((tl, d), lambda i: (i, p_blk)),
                  pl.BlockSpec((g, gd, gd), lambda i: (0, 0, 0)),
                  pl.BlockSpec((1, d), lambda i: (0, 0))],
        out_specs=pl.BlockSpec((tl, d), lambda i: (i, 0)),
        out_shape=jax.ShapeDtypeStruct((t, d), BF16),
        scratch_shapes=[pltpu.VMEM((tl, d), BF16)],
        compiler_params=_params("arbitrary"),
        name="pool",
    )(proj, pool_w_bf, pool_scale.reshape(1, d))


def _ssd_kernel(z_ref, xs_ref, bm_ref, cm_ref, dt_ref,
                cwx_ref, cwb_ref, cwc_ref, cbx_ref, cbb_ref, cbc_ref,
                dtb_ref, alog_ref, dsk_ref, nw_ref, exp_ref,
                o_ref,
                extx_s, extb_s, extc_s, state_s, xdt_s, y_s, cb_s, cs_s, cst_s, xc_s, bc_s, cc_s,
                *, tiles_per_seq, heads):
    i = pl.program_id(0)
    tl, inner = xs_ref.shape
    gw = inner // SSD_GROUPS
    pairs_per_group = gw // LANES
    n_pairs = inner // LANES

    @pl.when(i % tiles_per_seq == 0)
    def _():
        extx_s[0:CONV_HALO, :] = jnp.zeros((CONV_HALO, inner), F32)
        extb_s[0:CONV_HALO, :] = jnp.zeros((CONV_HALO, extb_s.shape[1]), F32)
        extc_s[0:CONV_HALO, :] = jnp.zeros((CONV_HALO, extc_s.shape[1]), F32)
        state_s[...] = jnp.zeros_like(state_s)

    def conv_silu(ext_ref, src_ref, w_ref, b_ref, dst_ref, width, cw):
        for c0 in range(0, width, cw):
            cs = slice(c0, c0 + cw)
            ext_ref[CONV_HALO:CONV_HALO + tl, cs] = src_ref[:, cs].astype(F32)
            acc = b_ref[:, cs] + w_ref[SSD_CONV - 1:SSD_CONV, cs] * ext_ref[CONV_HALO:CONV_HALO + tl, cs]
            for k in range(1, SSD_CONV):
                acc = acc + w_ref[SSD_CONV - 1 - k:SSD_CONV - k, cs] * ext_ref[pl.ds(CONV_HALO - k, tl), cs]
            dst_ref[:, cs] = _silu(acc)
            ext_ref[0:CONV_HALO, cs] = ext_ref[tl:tl + CONV_HALO, cs]

    conv_silu(extx_s, xs_ref, cwx_ref, cbx_ref, xc_s, inner, gw)
    conv_silu(extb_s, bm_ref, cwb_ref, cbb_ref, bc_s, bm_ref.shape[1], bm_ref.shape[1])
    conv_silu(extc_s, cm_ref, cwc_ref, cbc_ref, cc_s, cm_ref.shape[1], cm_ref.shape[1])

    dtv = dt_ref[...] + dtb_ref[...]
    dt = jnp.maximum(dtv, 0.0) + jnp.log1p(jnp.exp(-jnp.abs(dtv)))
    a = -jnp.exp(alog_ref[...])
    da = dt * a
    ri = lax.broadcasted_iota(jnp.int32, (tl, tl), 0)
    ci = lax.broadcasted_iota(jnp.int32, (tl, tl), 1)
    causal = ri >= ci
    cs = jnp.dot(causal.astype(F32), da, precision=HIGHEST, preferred_element_type=F32)
    cs_s[...] = cs
    cst_s[...] = cs.T
    last = cs[tl - 1:tl, :]
    exp_m = exp_ref[...]
    dt_x = jnp.dot(dt.astype(BF16), exp_m, preferred_element_type=F32)
    ecs_x = jnp.dot(jnp.exp(cs).astype(BF16), exp_m, preferred_element_type=F32)
    dte_x = jnp.dot(jnp.exp(last - cs).astype(BF16), exp_m, preferred_element_type=F32)

    for g in range(SSD_GROUPS):
        gs = slice(g * gw, (g + 1) * gw)
        ns = slice(g * SSD_STATE, (g + 1) * SSD_STATE)
        xg = xc_s[:, gs]
        xdt = xg * dt_x[:, gs]
        xdt_bf = xdt.astype(BF16)
        xd_bf = (xdt * dte_x[:, gs]).astype(BF16)
        bg = bc_s[:, ns]
        cg = cc_s[:, ns].astype(BF16)
        cb_s[g] = lax.dot_general(cg, bg.astype(BF16), (((1,), (1,)), ((), ())), preferred_element_type=F32)
        s_old = state_s[g]
        y_off = jnp.dot(cg, s_old.astype(BF16), preferred_element_type=F32) * ecs_x[:, gs]
        state_s[g] = (s_old * ecs_x[tl - 1:tl, gs]
                      + jnp.dot(bg.T.astype(BF16), xd_bf, preferred_element_type=F32))
        y0 = y_off + dsk_ref[:, gs] * xg
        for q in range(pairs_per_group):
            qs = slice(q * LANES, (q + 1) * LANES)
            xdt_s[g * pairs_per_group + q] = xdt_bf[:, qs]
            y_s[g * pairs_per_group + q] = y0[:, qs]

    lane_h = lax.broadcasted_iota(jnp.int32, (tl, LANES), 1)
    low_half = lane_h < SSD_HEAD_DIM

    def pair_body(hp, carry):
        g = hp // pairs_per_group
        cbg = cb_s[g]
        xp = xdt_s[hp]
        csv = cs_s[...]
        ys = []
        for e in range(2):
            h = 2 * hp + e
            colv = jnp.sum(jnp.where(lane_h == h, csv, 0.0), axis=1, keepdims=True)
            rowv = cst_s[pl.ds(h, 1), :]
            m = jnp.exp(jnp.where(causal, colv - rowv, -jnp.inf)) * cbg
            ys.append(jnp.dot(m.astype(BF16), xp, preferred_element_type=F32))
        y_s[hp] = y_s[hp] + jnp.where(low_half, ys[0], ys[1])
        return carry

    lax.fori_loop(0, n_pairs, pair_body, 0)

    for g in range(SSD_GROUPS):
        gs = slice(g * gw, (g + 1) * gw)
        yg = jnp.concatenate([y_s[g * pairs_per_group + q] for q in range(pairs_per_group)], axis=1)
        yg = yg * _silu(z_ref[:, gs].astype(F32))
        ms = jnp.mean(yg * yg, axis=-1, keepdims=True)
        o_ref[:, gs] = (yg * lax.rsqrt(ms + EPS) * nw_ref[:, gs]).astype(o_ref.dtype)


def _ssd(proj, dt_raw, blk, conv_w, conv_b, dt_bias, a_log, d_skip, ssd_norm_w, seq):
    t = proj.shape[0]
    tl = SEQ_TILE
    heads = dt_bias.shape[0]
    inner = ssd_norm_w.shape[0]
    bc = SSD_GROUPS * SSD_STATE
    gw = inner // SSD_GROUPS
    pad = LANES - heads
    row = lambda v: v.astype(F32).reshape(1, -1)
    dtb = jnp.pad(row(dt_bias), ((0, 0), (0, pad)))
    alog = jnp.pad(row(a_log), ((0, 0), (0, pad)))
    dsk = jnp.repeat(d_skip.astype(F32), SSD_HEAD_DIM).reshape(1, inner)
    expand = (lax.broadcasted_iota(jnp.int32, (LANES, inner), 1) // SSD_HEAD_DIM
              == lax.broadcasted_iota(jnp.int32, (LANES, inner), 0)).astype(BF16)
    cw = conv_w.astype(F32)
    cbias = row(conv_b)
    full = lambda shape: pl.BlockSpec(shape, lambda i: tuple(0 for _ in shape))
    return pl.pallas_call(
        functools.partial(_ssd_kernel, tiles_per_seq=seq // tl, heads=heads),
        grid=(t // tl,),
        in_specs=[pl.BlockSpec((tl, inner), lambda i: (i, blk["z"])),
                  pl.BlockSpec((tl, inner), lambda i: (i, blk["xs"])),
                  pl.BlockSpec((tl, bc), lambda i: (i, blk["B"])),
                  pl.BlockSpec((tl, bc), lambda i: (i, blk["C"])),
                  pl.BlockSpec((tl, LANES), lambda i: (i, 0)),
                  full((SSD_CONV, inner)), full((SSD_CONV, bc)), full((SSD_CONV, bc)),
                  full((1, inner)), full((1, bc)), full((1, bc)),
                  full((1, LANES)), full((1, LANES)), full((1, inner)), full((1, inner)),
                  full((LANES, inner))],
        out_specs=pl.BlockSpec((tl, inner), lambda i: (i, 0)),
        out_shape=jax.ShapeDtypeStruct((t, inner), BF16),
        scratch_shapes=[pltpu.VMEM((tl + CONV_HALO, inner), F32),
                        pltpu.VMEM((tl + CONV_HALO, bc), F32),
                        pltpu.VMEM((tl + CONV_HALO, bc), F32),
                        pltpu.VMEM((SSD_GROUPS, SSD_STATE, gw), F32),
                        pltpu.VMEM((inner // LANES, tl, LANES), BF16),
                        pltpu.VMEM((inner // LANES, tl, LANES), F32),
                        pltpu.VMEM((SSD_GROUPS, tl, tl), F32),
                        pltpu.VMEM((tl, LANES), F32),
                        pltpu.VMEM((LANES, tl), F32),
                        pltpu.VMEM((tl, inner), F32),
                        pltpu.VMEM((tl, bc), F32),
                        pltpu.VMEM((tl, bc), F32)],
        compiler_params=_params("arbitrary"),
        name="ssd",
    )(proj, proj, proj, proj, dt_raw,
      cw[:, :inner], cw[:, inner:inner + bc], cw[:, inner + bc:],
      cbias[:, :inner], cbias[:, inner:inner + bc], cbias[:, inner + bc:],
      dtb, alog, dsk, row(ssd_norm_w), expand)


def _merge_kernel(yp_ref, ys_ref, gp_ref, gs_ref, wp_ref, ws_ref, o_ref):
    a = jnp.dot(yp_ref[...], wp_ref[...], preferred_element_type=F32)
    b = jnp.dot(ys_ref[...], ws_ref[...], preferred_element_type=F32)
    o_ref[...] = (jax.nn.sigmoid(gp_ref[...].astype(F32)) * a
                  + jax.nn.sigmoid(gs_ref[...].astype(F32)) * b).astype(o_ref.dtype)


def _merge(ypool, yssd, proj, gp_off, gs_off, wbp, wbs):
    t, d = ypool.shape
    inner = yssd.shape[1]
    tm, tn = min(MERGE_TM, t), MERGE_TN
    return pl.pallas_call(
        _merge_kernel,
        grid=(t // tm, d // tn),
        in_specs=[pl.BlockSpec((tm, d), lambda i, j: (i, 0)),
                  pl.BlockSpec((tm, inner), lambda i, j: (i, 0)),
                  pl.BlockSpec((tm, tn), lambda i, j: (i, gp_off // tn + j)),
                  pl.BlockSpec((tm, tn), lambda i, j: (i, gs_off // tn + j)),
                  pl.BlockSpec((d, tn), lambda i, j: (0, j)),
                  pl.BlockSpec((inner, tn), lambda i, j: (0, j))],
        out_specs=pl.BlockSpec((tm, tn), lambda i, j: (i, j)),
        out_shape=jax.ShapeDtypeStruct((t, d), BF16),
        compiler_params=_params("arbitrary", "arbitrary"),
        name="merge",
    )(ypool, yssd, proj, proj, wbp, wbs)


def _out_kernel(m_ref, wo_ref, x_ref, mod_ref, wr_ref, br_ref,
                h_ref, u_ref, idx_ref, gate_ref, rank_ref, cnt_ref, carry_s):
    i = pl.program_id(0)
    tm = m_ref.shape[0]
    ne = wr_ref.shape[0]

    @pl.when(i == 0)
    def _():
        carry_s[...] = jnp.zeros_like(carry_s)

    mix = jnp.dot(m_ref[...], wo_ref[...], preferred_element_type=F32)
    h1 = x_ref[...] + mod_ref[0, 2:3, :] * mix
    h_ref[...] = h1
    ms = jnp.mean(h1 * h1, axis=-1, keepdims=True)
    u = h1 * lax.rsqrt(ms + EPS) * (1.0 + mod_ref[0, 4:5, :]) + mod_ref[0, 3:4, :]
    u_ref[...] = u
    logits = lax.dot_general(wr_ref[...], u, (((1,), (1,)), ((), ())), precision=HIGHEST,
                             preferred_element_type=F32) + br_ref[...]
    eidx = lax.broadcasted_iota(jnp.int32, (ne, tm), 0)
    work = logits
    vals, idxs, hots = [], [], []
    for _ in range(TOP_K):
        mx = jnp.max(work, axis=0, keepdims=True)
        sel = jnp.min(jnp.where(work == mx, eidx, ne), axis=0, keepdims=True)
        hot = eidx == sel
        vals.append(mx)
        idxs.append(sel)
        hots.append(hot)
        work = jnp.where(hot, -jnp.inf, work)
    exps = [jnp.exp(v - vals[0]) for v in vals]
    den = exps[0]
    for e in exps[1:]:
        den = den + e
    cnt = hots[0].astype(F32)
    for hot in hots[1:]:
        cnt = cnt + hot.astype(F32)
    ti = lax.broadcasted_iota(jnp.int32, (tm, tm), 0)
    tj = lax.broadcasted_iota(jnp.int32, (tm, tm), 1)
    before = (ti < tj).astype(BF16)
    prefix = jnp.dot(cnt.astype(BF16), before, preferred_element_type=F32)
    base = carry_s[:, 0:1] + prefix
    pad_rows = idx_ref.shape[0] - TOP_K
    ranks = [jnp.sum(jnp.where(hot, base, 0.0), axis=0, keepdims=True) for hot in hots]
    idx_ref[...] = jnp.concatenate(idxs + [jnp.zeros((pad_rows, tm), jnp.int32)], axis=0)
    gate_ref[...] = jnp.concatenate([e / den for e in exps] + [jnp.zeros((pad_rows, tm), F32)], axis=0)
    rank_ref[...] = jnp.concatenate([r.astype(jnp.int32) for r in ranks]
                                    + [jnp.zeros((pad_rows, tm), jnp.int32)], axis=0)
    carry_s[...] = carry_s[...] + jnp.sum(cnt, axis=1, keepdims=True)
    cnt_ref[...] = carry_s[...].astype(jnp.int32)


def _outproj(merged, wo, x2, mod3, w_router, b_router, seq):
    t, d = x2.shape
    ne = w_router.shape[1]
    tm = min(OUT_TM, seq)
    tiles_per_seq = seq // tm
    rows = 8
    return pl.pallas_call(
        _out_kernel,
        grid=(t // tm,),
        in_specs=[pl.BlockSpec((tm, d), lambda i: (i, 0)),
                  pl.BlockSpec((d, d), lambda i: (0, 0)),
                  pl.BlockSpec((tm, d), lambda i: (i, 0)),
                  pl.BlockSpec((1, N_MOD, d), lambda i: (i // tiles_per_seq, 0, 0)),
                  pl.BlockSpec((ne, d), lambda i: (0, 0)),
                  pl.BlockSpec((ne, 1), lambda i: (0, 0))],
        out_specs=[pl.BlockSpec((tm, d), lambda i: (i, 0)),
                   pl.BlockSpec((tm, d), lambda i: (i, 0)),
                   pl.BlockSpec((rows, tm), lambda i: (0, i)),
                   pl.BlockSpec((rows, tm), lambda i: (0, i)),
                   pl.BlockSpec((rows, tm), lambda i: (0, i)),
                   pl.BlockSpec((ne, LANES), lambda i: (0, 0))],
        out_shape=[jax.ShapeDtypeStruct((t, d), F32), jax.ShapeDtypeStruct((t, d), F32),
                   jax.ShapeDtypeStruct((rows, t), jnp.int32), jax.ShapeDtypeStruct((rows, t), F32),
                   jax.ShapeDtypeStruct((rows, t), jnp.int32), jax.ShapeDtypeStruct((ne, LANES), jnp.int32)],
        scratch_shapes=[pltpu.VMEM((ne, LANES), F32)],
        compiler_params=_params("arbitrary"),
        name="outproj",
    )(merged, wo, x2, mod3, w_router.astype(F32).T, b_router.astype(F32).reshape(ne, 1))


def _row_copy(src_hbm, src_row, dst_ref, dst_row, sem):
    return pltpu.make_async_copy(src_hbm.at[pl.ds(src_row, 1)], dst_ref.at[pl.ds(dst_row, 1)], sem)


def _dispatch_kernel(slot_ref, u_hbm, init_hbm, xs_hbm, sem):
    del init_hbm
    i = pl.program_id(0)
    ts = slot_ref.shape[2] // TOP_K

    def issue(tk, carry):
        tok = i * ts + tk // TOP_K
        _row_copy(u_hbm, tok, xs_hbm, slot_ref[0, 0, tk], sem).start()
        return carry

    lax.fori_loop(0, ts * TOP_K, issue, 0)

    def drain(tk, carry):
        _row_copy(u_hbm, 0, xs_hbm, 0, sem).wait()
        return carry

    lax.fori_loop(0, ts * TOP_K, drain, 0)


def _dispatch(u2, slot_tk, n_slots):
    t, d = u2.shape
    ts = min(DISPATCH_TS, t)
    slots3 = slot_tk.reshape(t // ts, 1, ts * TOP_K)
    init = jnp.zeros((n_slots, d), u2.dtype)
    return pl.pallas_call(
        _dispatch_kernel,
        grid=(t // ts,),
        in_specs=[pl.BlockSpec((1, 1, ts * TOP_K), lambda i: (i, 0, 0), memory_space=pltpu.SMEM),
                  pl.BlockSpec(memory_space=pl.ANY),
                  pl.BlockSpec(memory_space=pl.ANY)],
        out_specs=pl.BlockSpec(memory_space=pl.ANY),
        out_shape=jax.ShapeDtypeStruct((n_slots, d), u2.dtype),
        scratch_shapes=[pltpu.SemaphoreType.DMA(())],
        input_output_aliases={2: 0},
        compiler_params=_params("arbitrary"),
        name="dispatch",
    )(slots3, u2, init)


def _moe_kernel(be_ref, na_ref, x_ref, wg_ref, wu_ref, bg_ref, bu_ref, wd_ref, bd_ref, o_ref):
    del be_ref
    b, j = pl.program_id(0), pl.program_id(1)

    @pl.when(b < na_ref[0])
    def _():
        xb = x_ref[...].astype(BF16)
        gate = jnp.dot(xb, wg_ref[...], preferred_element_type=F32) + bg_ref[...]
        up = jnp.dot(xb, wu_ref[...], preferred_element_type=F32) + bu_ref[...]
        gate = jnp.minimum(gate, SWIGLU_LIMIT)
        up = jnp.clip(up, -SWIGLU_LIMIT, SWIGLU_LIMIT)
        act = (up + 1.0) * gate * jax.nn.sigmoid(SWIGLU_ALPHA * gate)
        part = jnp.dot(act.astype(BF16), wd_ref[...], preferred_element_type=F32)

        @pl.when(j == 0)
        def _():
            o_ref[...] = part + bd_ref[...]

        @pl.when(j > 0)
        def _():
            o_ref[...] = o_ref[...] + part

    @pl.when((b >= na_ref[0]) & (j == 0))
    def _():
        o_ref[...] = jnp.zeros_like(o_ref)


def _moe(xs, block_expert, n_active, wgu, bgu, wd, bd):
    n_slots, d = xs.shape
    ne, _, f2 = wgu.shape
    f = f2 // 2
    bm, tf = MOE_BM, min(MOE_TF, f)
    nb, ft = n_slots // bm, f // tf

    def blk(b, na):
        return jnp.minimum(b, na[0] - 1)

    def jj(b, j, na):
        return jnp.where(b < na[0], j, ft - 1)

    grid_spec = pltpu.PrefetchScalarGridSpec(
        num_scalar_prefetch=2,
        grid=(nb, ft),
        in_specs=[pl.BlockSpec((bm, d), lambda b, j, be, na: (blk(b, na), 0)),
                  pl.BlockSpec((None, d, tf), lambda b, j, be, na: (be[b], 0, jj(b, j, na))),
                  pl.BlockSpec((None, d, tf), lambda b, j, be, na: (be[b], 0, ft + jj(b, j, na))),
                  pl.BlockSpec((None, 1, tf), lambda b, j, be, na: (be[b], 0, jj(b, j, na))),
                  pl.BlockSpec((None, 1, tf), lambda b, j, be, na: (be[b], 0, ft + jj(b, j, na))),
                  pl.BlockSpec((None, tf, d), lambda b, j, be, na: (be[b], jj(b, j, na), 0)),
                  pl.BlockSpec((None, 1, d), lambda b, j, be, na: (be[b], 0, 0))],
        out_specs=pl.BlockSpec((bm, d), lambda b, j, be, na: (b, 0)),
    )
    return pl.pallas_call(
        _moe_kernel,
        grid_spec=grid_spec,
        out_shape=jax.ShapeDtypeStruct((n_slots, d), F32),
        compiler_params=_params("arbitrary", "arbitrary"),
        name="moe",
    )(block_expert, n_active, xs, wgu, wgu, bgu, bgu, wd, bd)


def _final_kernel(slot_ref, h_ref, gate_ref, mod_ref, fw_ref, y_hbm, o_ref, ybuf, sem):
    tc = h_ref.shape[0]

    def issue(tk, carry):
        _row_copy(y_hbm, slot_ref[0, 0, tk], ybuf, tk, sem).start()
        return carry

    lax.fori_loop(0, tc * TOP_K, issue, 0)

    def drain(tk, carry):
        _row_copy(y_hbm, 0, ybuf, 0, sem).wait()
        return carry

    lax.fori_loop(0, tc * TOP_K, drain, 0)

    ffn = gate_ref[:, 0:1] * ybuf[0:tc, :]
    for k in range(1, TOP_K):
        ffn = ffn + gate_ref[:, k:k + 1] * ybuf[k * tc:(k + 1) * tc, :]
    h2 = h_ref[...] + mod_ref[0, 5:6, :] * ffn
    ms = jnp.mean(h2 * h2, axis=-1, keepdims=True)
    o_ref[...] = (h2 * lax.rsqrt(ms + EPS) * fw_ref[...]).astype(o_ref.dtype)


def _final(h1, gates_tk, slot_kt, mod3, final_norm_w, y_slots, seq, out_dtype):
    t, d = h1.shape
    tc = min(FINAL_TC, seq)
    tiles_per_seq = seq // tc
    slots3 = slot_kt.reshape(TOP_K, t // tc, tc).transpose(1, 0, 2).reshape(t // tc, 1, TOP_K * tc)
    return pl.pallas_call(
        _final_kernel,
        grid=(t // tc,),
        in_specs=[pl.BlockSpec((1, 1, TOP_K * tc), lambda i: (i, 0, 0), memory_space=pltpu.SMEM),
                  pl.BlockSpec((tc, d), lambda i: (i, 0)),
                  pl.BlockSpec((tc, gates_tk.shape[1]), lambda i: (i, 0)),
                  pl.BlockSpec((1, N_MOD, d), lambda i: (i // tiles_per_seq, 0, 0)),
                  pl.BlockSpec((1, d), lambda i: (0, 0)),
                  pl.BlockSpec(memory_space=pl.ANY)],
        out_specs=pl.BlockSpec((tc, d), lambda i: (i, 0)),
        out_shape=jax.ShapeDtypeStruct((t, d), out_dtype),
        scratch_shapes=[pltpu.VMEM((TOP_K * tc, d), F32), pltpu.SemaphoreType.DMA(())],
        compiler_params=_params("arbitrary"),
        name="final",
    )(slots3, h1, gates_tk, mod3, final_norm_w.astype(F32).reshape(1, d), y_slots)


def kernel(x, c, w_ada, b_ada, w_in, pool_w, pool_scale, conv_w, conv_b, dt_bias, a_log, d_skip, ssd_norm_w,
           w_branch_pool, w_branch_ssd, w_out, w_router, b_router, w_gate_up, b_gate_up, w_down, b_down,
           final_norm_w):
    bsz, seq, d = x.shape
    depth = w_ada.shape[0]
    t = bsz * seq
    inner = ssd_norm_w.shape[1]
    heads = dt_bias.shape[1]
    bc = SSD_GROUPS * SSD_STATE
    ne = w_router.shape[2]
    assert depth == 1, "the final RMSNorm is fused into the last kernel of the single layer"
    assert seq % SEQ_TILE == 0 and heads <= LANES and (7 * d) % bc == 0 and (4 * d) % inner == 0
    assert inner // SSD_GROUPS % LANES == 0 and d % (len(POOL_WINDOWS) * LANES) == 0

    s1 = d
    s2 = s1 + inner
    s3 = s2 + inner + 2 * bc
    s4 = s3 + heads
    s5 = s4 + d
    blk = {"z": 0, "xs": 1, "p": (2 * inner) // d, "B": (2 * inner + 3 * d) // bc, "C": (2 * inner + 3 * d) // bc + 1}
    gp_off, gs_off = 2 * inner + d, 2 * inner + 2 * d

    h = x.astype(F32).reshape(t, d)
    for layer in range(depth):
        wl = w_in[layer]
        w_main = jnp.concatenate([wl[:, s1:s2], wl[:, s2:s2 + inner], wl[:, :s1], wl[:, s4:s5], wl[:, s5:],
                                  wl[:, s2 + inner:s2 + inner + bc], wl[:, s2 + inner + bc:s3]], axis=1).astype(BF16)
        w_dt = jnp.pad(wl[:, s3:s4].astype(F32), ((0, 0), (0, LANES - heads)))

        mod3 = _ada(c, w_ada[layer], b_ada[layer]).reshape(bsz, N_MOD, d)
        proj, dt_raw = _inproj(h, mod3, w_main, w_dt, seq)
        ypool = _pool(proj, blk["p"], pool_w[layer].astype(BF16), pool_scale[layer], seq, d)
        yssd = _ssd(proj, dt_raw, blk, conv_w[layer], conv_b[layer], dt_bias[layer], a_log[layer], d_skip[layer],
                    ssd_norm_w[layer], seq)
        merged = _merge(ypool, yssd, proj, gp_off, gs_off, w_branch_pool[layer].astype(BF16),
                        w_branch_ssd[layer].astype(BF16))
        h1, u2, idx_kt, gate_kt, rank_kt, counts = _outproj(merged, w_out[layer].astype(BF16), h, mod3,
                                                            w_router[layer], b_router[layer], seq)

        counts = counts[:, 0]
        padded = (counts + MOE_BM - 1) // MOE_BM * MOE_BM
        pend = jnp.cumsum(padded)
        pstart = pend - padded
        n_blocks = (t * TOP_K) // MOE_BM + ne
        n_active = (pend[-1] // MOE_BM).astype(jnp.int32).reshape(1)
        blocks = jnp.minimum(jnp.arange(n_blocks, dtype=jnp.int32), n_active[0] - 1)
        block_expert = jnp.minimum(jnp.searchsorted(pend, blocks * MOE_BM, side="right"), ne - 1).astype(jnp.int32)
        slot_kt = (pstart[idx_kt[:TOP_K]] + rank_kt[:TOP_K]).astype(jnp.int32)

        xs = _dispatch(u2, slot_kt.T, n_blocks * MOE_BM)
        y_slots = _moe(xs, block_expert, n_active, w_gate_up[layer].astype(BF16),
                       b_gate_up[layer].astype(F32)[:, None, :], w_down[layer].astype(BF16),
                       b_down[layer].astype(F32)[:, None, :])
        h = _final(h1, gate_kt.T, slot_kt, mod3, final_norm_w, y_slots, seq, x.dtype)
    return h.reshape(bsz, seq, d)
```

```python
import functools

import jax
import jax.numpy as jnp
from jax import lax
from jax.experimental import pallas as pl
from jax.experimental.pallas import tpu as pltpu

F32 = jnp.float32
BF16 = jnp.bfloat16
HIGHEST = lax.Precision.HIGHEST

EPS = 1e-6
POOL_WINDOWS = (2, 4, 8, 16)
SSD_GROUPS = 8
SSD_STATE = 128
SSD_CONV = 4
SSD_HEAD_DIM = 64
TOP_K = 4
SWIGLU_LIMIT = 7.0
SWIGLU_ALPHA = 1.702
N_MOD = 6

LANES = 128
CONV_HALO = 8
V7X_VMEM_LIMIT = 56 * 1024 * 1024

SEQ_TILE = 256
INPROJ_TM, INPROJ_TN = 1024, 512
MERGE_TM, MERGE_TN = 512, 512
OUT_TM = 512
MOE_BM, MOE_TF = 512, 512
DISPATCH_TS = 256
FINAL_TC = 256


def _params(*sem):
    return pltpu.CompilerParams(dimension_semantics=sem, vmem_limit_bytes=V7X_VMEM_LIMIT)


def _silu(v):
    return v * jax.nn.sigmoid(v)


def _ada_kernel(cb_ref, w_ref, b_ref, o_ref):
    nb, tn = cb_ref.shape[0], w_ref.shape[1]
    for b in range(nb):
        ca = _silu(cb_ref[b])
        cols = [jnp.sum(w_ref[:, j * LANES:(j + 1) * LANES] * ca, axis=0, keepdims=True)
                for j in range(tn // LANES)]
        o_ref[b:b + 1, :] = jnp.concatenate(cols, axis=1) + b_ref[...]


def _ada(c, w_ada, b_ada):
    nb, k = c.shape
    n = w_ada.shape[1]
    tn = 1024 if n % 1024 == 0 else n
    cb = jnp.broadcast_to(c.astype(F32)[:, :, None], (nb, k, LANES))
    return pl.pallas_call(
        _ada_kernel,
        grid=(n // tn,),
        in_specs=[pl.BlockSpec((nb, k, LANES), lambda j: (0, 0, 0)),
                  pl.BlockSpec((k, tn), lambda j: (0, j)),
                  pl.BlockSpec((1, tn), lambda j: (0, j))],
        out_specs=pl.BlockSpec((nb, tn), lambda j: (0, j)),
        out_shape=jax.ShapeDtypeStruct((nb, n), F32),
        compiler_params=_params("arbitrary"),
        name="ada",
    )(cb, w_ada, b_ada.reshape(1, n))


def _inproj_kernel(x_ref, mod_ref, w_ref, wdt_ref, o_ref, dt_ref, u_s, *, rows_per_chunk):
    j = pl.program_id(1)
    tm = x_ref.shape[0]

    @pl.when(j == 0)
    def _():
        sh = mod_ref[0, 0:1, :]
        sc = mod_ref[0, 1:2, :]

        def body(r, carry):
            rows = pl.ds(pl.multiple_of(r * rows_per_chunk, rows_per_chunk), rows_per_chunk)
            xv = x_ref[rows, :]
            ms = jnp.mean(xv * xv, axis=-1, keepdims=True)
            u = xv * lax.rsqrt(ms + EPS) * (1.0 + sc) + sh
            u_s[rows, :] = u.astype(BF16)
            dt_ref[rows, :] = jnp.dot(u, wdt_ref[...], precision=HIGHEST, preferred_element_type=F32)
            return carry

        lax.fori_loop(0, tm // rows_per_chunk, body, 0)

    o_ref[...] = jnp.dot(u_s[...], w_ref[...], preferred_element_type=F32).astype(o_ref.dtype)


def _inproj(x2, mod3, w_main, w_dt, seq):
    t, d = x2.shape
    n = w_main.shape[1]
    tm = min(INPROJ_TM, seq)
    tn = INPROJ_TN
    tiles_per_seq = seq // tm
    return pl.pallas_call(
        functools.partial(_inproj_kernel, rows_per_chunk=min(128, tm)),
        grid=(t // tm, n // tn),
        in_specs=[pl.BlockSpec((tm, d), lambda i, j: (i, 0)),
                  pl.BlockSpec((1, N_MOD, d), lambda i, j: (i // tiles_per_seq, 0, 0)),
                  pl.BlockSpec((d, tn), lambda i, j: (0, j)),
                  pl.BlockSpec((d, LANES), lambda i, j: (0, 0))],
        out_specs=[pl.BlockSpec((tm, tn), lambda i, j: (i, j)),
                   pl.BlockSpec((tm, LANES), lambda i, j: (i, 0))],
        out_shape=[jax.ShapeDtypeStruct((t, n), BF16), jax.ShapeDtypeStruct((t, LANES), F32)],
        scratch_shapes=[pltpu.VMEM((tm, d), BF16)],
        compiler_params=_params("arbitrary", "arbitrary"),
        name="inproj",
    )(x2, mod3, w_main, w_dt)


def _pool_kernel(p_ref, pw_ref, ps_ref, o_ref, prev_s, *, tiles_per_seq):
    i = pl.program_id(0)
    tl, d = p_ref.shape
    gd = d // len(POOL_WINDOWS)
    it = i % tiles_per_seq

    @pl.when(it == 0)
    def _():
        prev_s[...] = jnp.zeros_like(prev_s)

    row = lax.broadcasted_iota(jnp.int32, (tl, 2 * tl), 0)
    col = lax.broadcasted_iota(jnp.int32, (tl, 2 * tl), 1)
    pos = (it * tl + lax.broadcasted_iota(jnp.int32, (tl, 1), 0) + 1).astype(F32)
    for g, w in enumerate(POOL_WINDOWS):
        sl = slice(g * gd, (g + 1) * gd)
        cur = p_ref[:, sl]
        ext = jnp.concatenate([prev_s[:, sl], cur], axis=0)
        band = ((col <= row + tl) & (col > row + tl - w)).astype(BF16)
        win_sum = jnp.dot(band, ext, preferred_element_type=F32)
        mean = win_sum / jnp.minimum(pos, float(w))
        dlt = (mean - cur.astype(F32)).astype(BF16)
        mixed = jnp.dot(dlt, pw_ref[g], preferred_element_type=F32)
        o_ref[:, sl] = (mixed * ps_ref[:, sl]).astype(o_ref.dtype)
    prev_s[...] = p_ref[...]


def _pool(proj, p_blk, pool_w_bf, pool_scale, seq, d):
    t = proj.shape[0]
    tl = SEQ_TILE
    g, gd = pool_w_bf.shape[0], pool_w_bf.shape[1]
    return pl.pallas_call(
        functools.partial(_pool_kernel, tiles_per_seq=seq // tl),
        grid=(t // tl,),
        in_specs=[pl.BlockSpec((tl, d), lambda i: (i, p_blk)),
                  pl.BlockSpec((g, gd, gd), lambda i: (0, 0, 0)),
                  pl.BlockSpec((1, d), lambda i: (0, 0))],
        out_specs=pl.BlockSpec((tl, d), lambda i: (i, 0)),
        out_shape=jax.ShapeDtypeStruct((t, d), BF16),
        scratch_shapes=[pltpu.VMEM((tl, d), BF16)],
        compiler_params=_params("arbitrary"),
        name="pool",
    )(proj, pool_w_bf, pool_scale.reshape(1, d))


def _ssd_kernel(z_ref, xs_ref, bm_ref, cm_ref, dt_ref,
                cwx_ref, cwb_ref, cwc_ref, cbx_ref, cbb_ref, cbc_ref,
                dtb_ref, alog_ref, dsk_ref, nw_ref, exp_ref,
                o_ref,
                extx_s, extb_s, extc_s, state_s, xdt_s, y_s, cb_s, cs_s, cst_s, xc_s, bc_s, cc_s,
                *, tiles_per_seq, heads):
    i = pl.program_id(0)
    tl, inner = xs_ref.shape
    gw = inner // SSD_GROUPS
    pairs_per_group = gw // LANES
    n_pairs = inner // LANES

    @pl.when(i % tiles_per_seq == 0)
    def _():
        extx_s[0:CONV_HALO, :] = jnp.zeros((CONV_HALO, inner), F32)
        extb_s[0:CONV_HALO, :] = jnp.zeros((CONV_HALO, extb_s.shape[1]), F32)
        extc_s[0:CONV_HALO, :] = jnp.zeros((CONV_HALO, extc_s.shape[1]), F32)
        state_s[...] = jnp.zeros_like(state_s)

    def conv_silu(ext_ref, src_ref, w_ref, b_ref, dst_ref, width, cw):
        for c0 in range(0, width, cw):
            cs = slice(c0, c0 + cw)
            ext_ref[CONV_HALO:CONV_HALO + tl, cs] = src_ref[:, cs].astype(F32)
            acc = b_ref[:, cs] + w_ref[SSD_CONV - 1:SSD_CONV, cs] * ext_ref[CONV_HALO:CONV_HALO + tl, cs]
            for k in range(1, SSD_CONV):
                acc = acc + w_ref[SSD_CONV - 1 - k:SSD_CONV - k, cs] * ext_ref[pl.ds(CONV_HALO - k, tl), cs]
            dst_ref[:, cs] = _silu(acc)
            ext_ref[0:CONV_HALO, cs] = ext_ref[tl:tl + CONV_HALO, cs]

    conv_silu(extx_s, xs_ref, cwx_ref, cbx_ref, xc_s, inner, gw)
    conv_silu(extb_s, bm_ref, cwb_ref, cbb_ref, bc_s, bm_ref.shape[1], bm_ref.shape[1])
    conv_silu(extc_s, cm_ref, cwc_ref, cbc_ref, cc_s, cm_ref.shape[1], cm_ref.shape[1])

    dtv = dt_ref[...] + dtb_ref[...]
    dt = jnp.maximum(dtv, 0.0) + jnp.log1p(jnp.exp(-jnp.abs(dtv)))
    a = -jnp.exp(alog_ref[...])
    da = dt * a
    ri = lax.broadcasted_iota(jnp.int32, (tl, tl), 0)
    ci = lax.broadcasted_iota(jnp.int32, (tl, tl), 1)
    causal = ri >= ci
    cs = jnp.dot(causal.astype(F32), da, precision=HIGHEST, preferred_element_type=F32)
    cs_s[...] = cs
    cst_s[...] = cs.T
    last = cs[tl - 1:tl, :]
    exp_m = exp_ref[...]
    dt_x = jnp.dot(dt.astype(BF16), exp_m, preferred_element_type=F32)
    ecs_x = jnp.dot(jnp.exp(cs).astype(BF16), exp_m, preferred_element_type=F32)
    dte_x = jnp.dot(jnp.exp(last - cs).astype(BF16), exp_m, preferred_element_type=F32)

    for g in range(SSD_GROUPS):
        gs = slice(g * gw, (g + 1) * gw)
        ns = slice(g * SSD_STATE, (g + 1) * SSD_STATE)
        xg = xc_s[:, gs]
        xdt = xg * dt_x[:, gs]
        xdt_bf = xdt.astype(BF16)
        xd_bf = (xdt * dte_x[:, gs]).astype(BF16)
        bg = bc_s[:, ns]
        cg = cc_s[:, ns].astype(BF16)
        cb_s[g] = lax.dot_general(cg, bg.astype(BF16), (((1,), (1,)), ((), ())), preferred_element_type=F32)
        s_old = state_s[g]
        y_off = jnp.dot(cg, s_old.astype(BF16), preferred_element_type=F32) * ecs_x[:, gs]
        state_s[g] = (s_old * ecs_x[tl - 1:tl, gs]
                      + jnp.dot(bg.T.astype(BF16), xd_bf, preferred_element_type=F32))
        y0 = y_off + dsk_ref[:, gs] * xg
        for q in range(pairs_per_group):
            qs = slice(q * LANES, (q + 1) * LANES)
            xdt_s[g * pairs_per_group + q] = xdt_bf[:, qs]
            y_s[g * pairs_per_group + q] = y0[:, qs]

    lane_h = lax.broadcasted_iota(jnp.int32, (tl, LANES), 1)
    low_half = lane_h < SSD_HEAD_DIM

    def pair_body(hp, carry):
        g = hp // pairs_per_group
        cbg = cb_s[g]
        xp = xdt_s[hp]
        csv = cs_s[...]
        ys = []
        for e in range(2):
            h = 2 * hp + e
            colv = jnp.sum(jnp.where(lane_h == h, csv, 0.0), axis=1, keepdims=True)
            rowv = cst_s[pl.ds(h, 1), :]
            m = jnp.exp(jnp.where(causal, colv - rowv, -jnp.inf)) * cbg
            ys.append(jnp.dot(m.astype(BF16), xp, preferred_element_type=F32))
        y_s[hp] = y_s[hp] + jnp.where(low_half, ys[0], ys[1])
        return carry

    lax.fori_loop(0, n_pairs, pair_body, 0, unroll=2)

    for g in range(SSD_GROUPS):
        gs = slice(g * gw, (g + 1) * gw)
        yg = jnp.concatenate([y_s[g * pairs_per_group + q] for q in range(pairs_per_group)], axis=1)
        yg = yg * _silu(z_ref[:, gs].astype(F32))
        ms = jnp.mean(yg * yg, axis=-1, keepdims=True)
        o_ref[:, gs] = (yg * lax.rsqrt(ms + EPS) * nw_ref[:, gs]).astype(o_ref.dtype)


def _ssd(proj, dt_raw, blk, conv_w, conv_b, dt_bias, a_log, d_skip, ssd_norm_w, seq):
    t = proj.shape[0]
    tl = SEQ_TILE
    heads = dt_bias.shape[0]
    inner = ssd_norm_w.shape[0]
    bc = SSD_GROUPS * SSD_STATE
    gw = inner // SSD_GROUPS
    pad = LANES - heads
    row = lambda v: v.astype(F32).reshape(1, -1)
    dtb = jnp.pad(row(dt_bias), ((0, 0), (0, pad)))
    alog = jnp.pad(row(a_log), ((0, 0), (0, pad)))
    dsk = jnp.repeat(d_skip.astype(F32), SSD_HEAD_DIM).reshape(1, inner)
    expand = (lax.broadcasted_iota(jnp.int32, (LANES, inner), 1) // SSD_HEAD_DIM
              == lax.broadcasted_iota(jnp.int32, (LANES, inner), 0)).astype(BF16)
    cw = conv_w.astype(F32)
    cbias = row(conv_b)
    full = lambda shape: pl.BlockSpec(shape, lambda i: tuple(0 for _ in shape))
    return pl.pallas_call(
        functools.partial(_ssd_kernel, tiles_per_seq=seq // tl, heads=heads),
        grid=(t // tl,),
        in_specs=[pl.BlockSpec((tl, inner), lambda i: (i, blk["z"])),
                  pl.BlockSpec((tl, inner), lambda i: (i, blk["xs"])),
                  pl.BlockSpec((tl, bc), lambda i: (i, blk["B"])),
                  pl.BlockSpec((tl, bc), lambda i: (i, blk["C"])),
                  pl.BlockSpec((tl, LANES), lambda i: (i, 0)),
                  full((SSD_CONV, inner)), full((SSD_CONV, bc)), full((SSD_CONV, bc)),
                  full((1, inner)), full((1, bc)), full((1, bc)),
                  full((1, LANES)), full((1, LANES)), full((1, inner)), full((1, inner)),
                  full((LANES, inner))],
        out_specs=pl.BlockSpec((tl, inner), lambda i: (i, 0)),
        out_shape=jax.ShapeDtypeStruct((t, inner), BF16),
        scratch_shapes=[pltpu.VMEM((tl + CONV_HALO, inner), F32),
                        pltpu.VMEM((tl + CONV_HALO, bc), F32),
                        pltpu.VMEM((tl + CONV_HALO, bc), F32),
                        pltpu.VMEM((SSD_GROUPS, SSD_STATE, gw), F32),
                        pltpu.VMEM((inner // LANES, tl, LANES), BF16),
                        pltpu.VMEM((inner // LANES, tl, LANES), F32),
                        pltpu.VMEM((SSD_GROUPS, tl, tl), F32),
                        pltpu.VMEM((tl, LANES), F32),
                        pltpu.VMEM((LANES, tl), F32),
                        pltpu.VMEM((tl, inner), F32),
                        pltpu.VMEM((tl, bc), F32),
                        pltpu.VMEM((tl, bc), F32)],
        compiler_params=_params("arbitrary"),
        name="ssd",
    )(proj, proj, proj, proj, dt_raw,
      cw[:, :inner], cw[:, inner:inner + bc], cw[:, inner + bc:],
      cbias[:, :inner], cbias[:, inner:inner + bc], cbias[:, inner + bc:],
      dtb, alog, dsk, row(ssd_norm_w), expand)


def _merge_kernel(yp_ref, ys_ref, gp_ref, gs_ref, wp_ref, ws_ref, o_ref):
    a = jnp.dot(yp_ref[...], wp_ref[...], preferred_element_type=F32)
    b = jnp.dot(ys_ref[...], ws_ref[...], preferred_element_type=F32)
    o_ref[...] = (jax.nn.sigmoid(gp_ref[...].astype(F32)) * a
                  + jax.nn.sigmoid(gs_ref[...].astype(F32)) * b).astype(o_ref.dtype)


def _merge(ypool, yssd, proj, gp_off, gs_off, wbp, wbs):
    t, d = ypool.shape
    inner = yssd.shape[1]
    tm, tn = min(MERGE_TM, t), MERGE_TN
    return pl.pallas_call(
        _merge_kernel,
        grid=(t // tm, d // tn),
        in_specs=[pl.BlockSpec((tm, d), lambda i, j: (i, 0)),
                  pl.BlockSpec((tm, inner), lambda i, j: (i, 0)),
                  pl.BlockSpec((tm, tn), lambda i, j: (i, gp_off // tn + j)),
                  pl.BlockSpec((tm, tn), lambda i, j: (i, gs_off // tn + j)),
                  pl.BlockSpec((d, tn), lambda i, j: (0, j)),
                  pl.BlockSpec((inner, tn), lambda i, j: (0, j))],
        out_specs=pl.BlockSpec((tm, tn), lambda i, j: (i, j)),
        out_shape=jax.ShapeDtypeStruct((t, d), BF16),
        compiler_params=_params("arbitrary", "arbitrary"),
        name="merge",
    )(ypool, yssd, proj, proj, wbp, wbs)


def _out_kernel(m_ref, wo_ref, x_ref, mod_ref, wr_ref, br_ref,
                h_ref, u_ref, idx_ref, gate_ref, rank_ref, cnt_ref, carry_s):
    i = pl.program_id(0)
    tm = m_ref.shape[0]
    ne = wr_ref.shape[0]

    @pl.when(i == 0)
    def _():
        carry_s[...] = jnp.zeros_like(carry_s)

    mix = jnp.dot(m_ref[...], wo_ref[...], preferred_element_type=F32)
    h1 = x_ref[...] + mod_ref[0, 2:3, :] * mix
    h_ref[...] = h1
    ms = jnp.mean(h1 * h1, axis=-1, keepdims=True)
    u = h1 * lax.rsqrt(ms + EPS) * (1.0 + mod_ref[0, 4:5, :]) + mod_ref[0, 3:4, :]
    u_ref[...] = u
    logits = lax.dot_general(wr_ref[...], u, (((1,), (1,)), ((), ())), precision=HIGHEST,
                             preferred_element_type=F32) + br_ref[...]
    eidx = lax.broadcasted_iota(jnp.int32, (ne, tm), 0)
    work = logits
    vals, idxs, hots = [], [], []
    for _ in range(TOP_K):
        mx = jnp.max(work, axis=0, keepdims=True)
        sel = jnp.min(jnp.where(work == mx, eidx, ne), axis=0, keepdims=True)
        hot = eidx == sel
        vals.append(mx)
        idxs.append(sel)
        hots.append(hot)
        work = jnp.where(hot, -jnp.inf, work)
    exps = [jnp.exp(v - vals[0]) for v in vals]
    den = exps[0]
    for e in exps[1:]:
        den = den + e
    cnt = hots[0].astype(F32)
    for hot in hots[1:]:
        cnt = cnt + hot.astype(F32)
    ti = lax.broadcasted_iota(jnp.int32, (tm, tm), 0)
    tj = lax.broadcasted_iota(jnp.int32, (tm, tm), 1)
    before = (ti < tj).astype(BF16)
    prefix = jnp.dot(cnt.astype(BF16), before, preferred_element_type=F32)
    base = carry_s[:, 0:1] + prefix
    pad_rows = idx_ref.shape[0] - TOP_K
    ranks = [jnp.sum(jnp.where(hot, base, 0.0), axis=0, keepdims=True) for hot in hots]
    idx_ref[...] = jnp.concatenate(idxs + [jnp.zeros((pad_rows, tm), jnp.int32)], axis=0)
    gate_ref[...] = jnp.concatenate([e / den for e in exps] + [jnp.zeros((pad_rows, tm), F32)], axis=0)
    rank_ref[...] = jnp.concatenate([r.astype(jnp.int32) for r in ranks]
                                    + [jnp.zeros((pad_rows, tm), jnp.int32)], axis=0)
    carry_s[...] = carry_s[...] + jnp.sum(cnt, axis=1, keepdims=True)
    cnt_ref[...] = carry_s[...].astype(jnp.int32)


def _outproj(merged, wo, x2, mod3, w_router, b_router, seq):
    t, d = x2.shape
    ne = w_router.shape[1]
    tm = min(OUT_TM, seq)
    tiles_per_seq = seq // tm
    rows = 8
    return pl.pallas_call(
        _out_kernel,
        grid=(t // tm,),
        in_specs=[pl.BlockSpec((tm, d), lambda i: (i, 0)),
                  pl.BlockSpec((d, d), lambda i: (0, 0), pipeline_mode=pl.Buffered(1)),
                  pl.BlockSpec((tm, d), lambda i: (i, 0)),
                  pl.BlockSpec((1, N_MOD, d), lambda i: (i // tiles_per_seq, 0, 0)),
                  pl.BlockSpec((ne, d), lambda i: (0, 0)),
                  pl.BlockSpec((ne, 1), lambda i: (0, 0))],
        out_specs=[pl.BlockSpec((tm, d), lambda i: (i, 0)),
                   pl.BlockSpec((tm, d), lambda i: (i, 0)),
                   pl.BlockSpec((rows, tm), lambda i: (0, i)),
                   pl.BlockSpec((rows, tm), lambda i: (0, i)),
                   pl.BlockSpec((rows, tm), lambda i: (0, i)),
                   pl.BlockSpec((ne, LANES), lambda i: (0, 0))],
        out_shape=[jax.ShapeDtypeStruct((t, d), F32), jax.ShapeDtypeStruct((t, d), F32),
                   jax.ShapeDtypeStruct((rows, t), jnp.int32), jax.ShapeDtypeStruct((rows, t), F32),
                   jax.ShapeDtypeStruct((rows, t), jnp.int32), jax.ShapeDtypeStruct((ne, LANES), jnp.int32)],
        scratch_shapes=[pltpu.VMEM((ne, LANES), F32)],
        compiler_params=_params("arbitrary"),
        name="outproj",
    )(merged, wo, x2, mod3, w_router.astype(F32).T, b_router.astype(F32).reshape(ne, 1))


def _row_copy(src_hbm, src_row, dst_ref, dst_row, sem):
    return pltpu.make_async_copy(src_hbm.at[pl.ds(src_row, 1)], dst_ref.at[pl.ds(dst_row, 1)], sem)


def _dispatch_kernel(slot_ref, u_ref, init_hbm, xs_hbm, sem):
    del init_hbm
    ts = u_ref.shape[0]

    def issue(tk, carry):
        _row_copy(u_ref, tk // TOP_K, xs_hbm, slot_ref[0, 0, tk], sem).start()
        return carry

    lax.fori_loop(0, ts * TOP_K, issue, 0)

    def drain(tk, carry):
        _row_copy(u_ref, 0, xs_hbm, 0, sem).wait()
        return carry

    lax.fori_loop(0, ts * TOP_K, drain, 0)


def _dispatch(u2, slot_tk, n_slots):
    t, d = u2.shape
    ts = min(DISPATCH_TS, t)
    slots3 = slot_tk.reshape(t // ts, 1, ts * TOP_K)
    init = jnp.zeros((n_slots, d), u2.dtype)
    return pl.pallas_call(
        _dispatch_kernel,
        grid=(t // ts,),
        in_specs=[pl.BlockSpec((1, 1, ts * TOP_K), lambda i: (i, 0, 0), memory_space=pltpu.SMEM),
                  pl.BlockSpec((ts, d), lambda i: (i, 0)),
                  pl.BlockSpec(memory_space=pl.ANY)],
        out_specs=pl.BlockSpec(memory_space=pl.ANY),
        out_shape=jax.ShapeDtypeStruct((n_slots, d), u2.dtype),
        scratch_shapes=[pltpu.SemaphoreType.DMA(())],
        input_output_aliases={2: 0},
        compiler_params=_params("arbitrary"),
        name="dispatch",
    )(slots3, u2, init)


def _moe_kernel(be_ref, na_ref, x_ref, wg_ref, wu_ref, bg_ref, bu_ref, wd_ref, bd_ref, o_ref):
    del be_ref
    b, j = pl.program_id(0), pl.program_id(1)

    @pl.when(b < na_ref[0])
    def _():
        xb = x_ref[...].astype(BF16)
        gate = jnp.dot(xb, wg_ref[...], preferred_element_type=F32) + bg_ref[...]
        up = jnp.dot(xb, wu_ref[...], preferred_element_type=F32) + bu_ref[...]
        gate = jnp.minimum(gate, SWIGLU_LIMIT)
        up = jnp.clip(up, -SWIGLU_LIMIT, SWIGLU_LIMIT)
        act = (up + 1.0) * gate * jax.nn.sigmoid(SWIGLU_ALPHA * gate)
        part = jnp.dot(act.astype(BF16), wd_ref[...], preferred_element_type=F32)

        @pl.when(j == 0)
        def _():
            o_ref[...] = part + bd_ref[...]

        @pl.when(j > 0)
        def _():
            o_ref[...] = o_ref[...] + part

    @pl.when((b >= na_ref[0]) & (j == 0))
    def _():
        o_ref[...] = jnp.zeros_like(o_ref)


def _moe(xs, block_expert, n_active, wgu, bgu, wd, bd):
    n_slots, d = xs.shape
    ne, _, f2 = wgu.shape
    f = f2 // 2
    bm, tf = MOE_BM, min(MOE_TF, f)
    nb, ft = n_slots // bm, f // tf

    def blk(b, na):
        return jnp.minimum(b, na[0] - 1)

    def jj(b, j, na):
        return jnp.where(b < na[0], j, ft - 1)

    grid_spec = pltpu.PrefetchScalarGridSpec(
        num_scalar_prefetch=2,
        grid=(nb, ft),
        in_specs=[pl.BlockSpec((bm, d), lambda b, j, be, na: (blk(b, na), 0)),
                  pl.BlockSpec((None, d, tf), lambda b, j, be, na: (be[b], 0, jj(b, j, na))),
                  pl.BlockSpec((None, d, tf), lambda b, j, be, na: (be[b], 0, ft + jj(b, j, na))),
                  pl.BlockSpec((None, 1, tf), lambda b, j, be, na: (be[b], 0, jj(b, j, na))),
                  pl.BlockSpec((None, 1, tf), lambda b, j, be, na: (be[b], 0, ft + jj(b, j, na))),
                  pl.BlockSpec((None, tf, d), lambda b, j, be, na: (be[b], jj(b, j, na), 0)),
                  pl.BlockSpec((None, 1, d), lambda b, j, be, na: (be[b], 0, 0))],
        out_specs=pl.BlockSpec((bm, d), lambda b, j, be, na: (b, 0)),
    )
    return pl.pallas_call(
        _moe_kernel,
        grid_spec=grid_spec,
        out_shape=jax.ShapeDtypeStruct((n_slots, d), F32),
        compiler_params=_params("arbitrary", "arbitrary"),
        name="moe",
    )(block_expert, n_active, xs, wgu, wgu, bgu, bgu, wd, bd)


def _final_kernel(slot_ref, h_ref, gate_ref, mod_ref, fw_ref, y_hbm, o_ref, ybuf, sem):
    tc = h_ref.shape[0]

    def issue(tk, carry):
        _row_copy(y_hbm, slot_ref[0, 0, tk], ybuf, tk, sem).start()
        return carry

    lax.fori_loop(0, tc * TOP_K, issue, 0)

    def drain(tk, carry):
        _row_copy(y_hbm, 0, ybuf, 0, sem).wait()
        return carry

    lax.fori_loop(0, tc * TOP_K, drain, 0)

    ffn = gate_ref[:, 0:1] * ybuf[0:tc, :]
    for k in range(1, TOP_K):
        ffn = ffn + gate_ref[:, k:k + 1] * ybuf[k * tc:(k + 1) * tc, :]
    h2 = h_ref[...] + mod_ref[0, 5:6, :] * ffn
    ms = jnp.mean(h2 * h2, axis=-1, keepdims=True)
    o_ref[...] = (h2 * lax.rsqrt(ms + EPS) * fw_ref[...]).astype(o_ref.dtype)


def _final(h1, gates_tk, slot_kt, mod3, final_norm_w, y_slots, seq, out_dtype):
    t, d = h1.shape
    tc = min(FINAL_TC, seq)
    tiles_per_seq = seq // tc
    slots3 = slot_kt.reshape(TOP_K, t // tc, tc).transpose(1, 0, 2).reshape(t // tc, 1, TOP_K * tc)
    return pl.pallas_call(
        _final_kernel,
        grid=(t // tc,),
        in_specs=[pl.BlockSpec((1, 1, TOP_K * tc), lambda i: (i, 0, 0), memory_space=pltpu.SMEM),
                  pl.BlockSpec((tc, d), lambda i: (i, 0)),
                  pl.BlockSpec((tc, gates_tk.shape[1]), lambda i: (i, 0)),
                  pl.BlockSpec((1, N_MOD, d), lambda i: (i // tiles_per_seq, 0, 0)),
                  pl.BlockSpec((1, d), lambda i: (0, 0)),
                  pl.BlockSpec(memory_space=pl.ANY)],
        out_specs=pl.BlockSpec((tc, d), lambda i: (i, 0)),
        out_shape=jax.ShapeDtypeStruct((t, d), out_dtype),
        scratch_shapes=[pltpu.VMEM((TOP_K * tc, d), F32), pltpu.SemaphoreType.DMA(())],
        compiler_params=_params("arbitrary"),
        name="final",
    )(slots3, h1, gates_tk, mod3, final_norm_w.astype(F32).reshape(1, d), y_slots)


def kernel(x, c, w_ada, b_ada, w_in, pool_w, pool_scale, conv_w, conv_b, dt_bias, a_log, d_skip, ssd_norm_w,
           w_branch_pool, w_branch_ssd, w_out, w_router, b_router, w_gate_up, b_gate_up, w_down, b_down,
           final_norm_w):
    bsz, seq, d = x.shape
    depth = w_ada.shape[0]
    t = bsz * seq
    inner = ssd_norm_w.shape[1]
    heads = dt_bias.shape[1]
    bc = SSD_GROUPS * SSD_STATE
    ne = w_router.shape[2]
    assert depth == 1, "the final RMSNorm is fused into the last kernel of the single layer"
    assert seq % SEQ_TILE == 0 and heads <= LANES and (7 * d) % bc == 0 and (4 * d) % inner == 0
    assert inner // SSD_GROUPS % LANES == 0 and d % (len(POOL_WINDOWS) * LANES) == 0

    s1 = d
    s2 = s1 + inner
    s3 = s2 + inner + 2 * bc
    s4 = s3 + heads
    s5 = s4 + d
    blk = {"z": 0, "xs": 1, "p": (2 * inner) // d, "B": (2 * inner + 3 * d) // bc, "C": (2 * inner + 3 * d) // bc + 1}
    gp_off, gs_off = 2 * inner + d, 2 * inner + 2 * d

    h = x.astype(F32).reshape(t, d)
    for layer in range(depth):
        wl = w_in[layer]
        w_main = jnp.concatenate([wl[:, s1:s2], wl[:, s2:s2 + inner], wl[:, :s1], wl[:, s4:s5], wl[:, s5:],
                                  wl[:, s2 + inner:s2 + inner + bc], wl[:, s2 + inner + bc:s3]], axis=1).astype(BF16)
        w_dt = jnp.pad(wl[:, s3:s4].astype(F32), ((0, 0), (0, LANES - heads)))

        mod3 = _ada(c, w_ada[layer], b_ada[layer]).reshape(bsz, N_MOD, d)
        proj, dt_raw = _inproj(h, mod3, w_main, w_dt, seq)
        ypool = _pool(proj, blk["p"], pool_w[layer].astype(BF16), pool_scale[layer], seq, d)
        yssd = _ssd(proj, dt_raw, blk, conv_w[layer], conv_b[layer], dt_bias[layer], a_log[layer], d_skip[layer],
                    ssd_norm_w[layer], seq)
        merged = _merge(ypool, yssd, proj, gp_off, gs_off, w_branch_pool[layer].astype(BF16),
                        w_branch_ssd[layer].astype(BF16))
        h1, u2, idx_kt, gate_kt, rank_kt, counts = _outproj(merged, w_out[layer].astype(BF16), h, mod3,
                                                            w_router[layer], b_router[layer], seq)

        counts = counts[:, 0]
        padded = (counts + MOE_BM - 1) // MOE_BM * MOE_BM
        pend = jnp.cumsum(padded)
        pstart = pend - padded
        n_blocks = (t * TOP_K) // MOE_BM + ne
        n_active = (pend[-1] // MOE_BM).astype(jnp.int32).reshape(1)
        blocks = jnp.minimum(jnp.arange(n_blocks, dtype=jnp.int32), n_active[0] - 1)
        block_expert = jnp.minimum(jnp.sum(pend[None, :] <= (blocks * MOE_BM)[:, None], axis=1),
                                   ne - 1).astype(jnp.int32)
        eids = jnp.arange(ne, dtype=jnp.int32)[:, None, None]
        slot_kt = (jnp.sum(jnp.where(idx_kt[:TOP_K][None] == eids, pstart[:, None, None], 0), axis=0)
                   + rank_kt[:TOP_K]).astype(jnp.int32)

        xs = _dispatch(u2, slot_kt.T, n_blocks * MOE_BM)
        y_slots = _moe(xs, block_expert, n_active, w_gate_up[layer].astype(BF16),
                       b_gate_up[layer].astype(F32)[:, None, :], w_down[layer].astype(BF16),
                       b_down[layer].astype(F32)[:, None, :])
        h = _final(h1, gate_kt.T, slot_kt, mod3, final_norm_w, y_slots, seq, x.dtype)
    return h.reshape(bsz, seq, d)
```

```python
import functools

import jax
import jax.numpy as jnp
from jax import lax
from jax.experimental import pallas as pl
from jax.experimental.pallas import tpu as pltpu

F32 = jnp.float32
BF16 = jnp.bfloat16
HIGHEST = lax.Precision.HIGHEST

EPS = 1e-6
POOL_WINDOWS = (2, 4, 8, 16)
SSD_GROUPS = 8
SSD_STATE = 128
SSD_CONV = 4
SSD_HEAD_DIM = 64
TOP_K = 4
SWIGLU_LIMIT = 7.0
SWIGLU_ALPHA = 1.702
N_MOD = 6

LOG2_E = 1.4426950408889634
LANES = 128
CONV_HALO = 8
V7X_VMEM_LIMIT = 56 * 1024 * 1024

SEQ_TILE = 256
INPROJ_TM, INPROJ_TN = 1024, 1024
MERGE_TM, MERGE_TN = 512, 512
OUT_TM = 512
MOE_BM, MOE_TF = 512, 512
DISPATCH_TS = 256
FINAL_TC = 256


def _params(*sem):
    return pltpu.CompilerParams(dimension_semantics=sem, vmem_limit_bytes=V7X_VMEM_LIMIT)


def _silu(v):
    return v * jax.nn.sigmoid(v)


def _ada_kernel(cb_ref, w_ref, b_ref, o_ref):
    nb, tn = cb_ref.shape[0], w_ref.shape[1]
    for b in range(nb):
        ca = _silu(cb_ref[b])
        cols = [jnp.sum(w_ref[:, j * LANES:(j + 1) * LANES] * ca, axis=0, keepdims=True)
                for j in range(tn // LANES)]
        o_ref[b:b + 1, :] = jnp.concatenate(cols, axis=1) + b_ref[...]


def _ada(c, w_ada, b_ada):
    nb, k = c.shape
    n = w_ada.shape[1]
    tn = 1024 if n % 1024 == 0 else n
    cb = jnp.broadcast_to(c.astype(F32)[:, :, None], (nb, k, LANES))
    return pl.pallas_call(
        _ada_kernel,
        grid=(n // tn,),
        in_specs=[pl.BlockSpec((nb, k, LANES), lambda j: (0, 0, 0)),
                  pl.BlockSpec((k, tn), lambda j: (0, j)),
                  pl.BlockSpec((1, tn), lambda j: (0, j))],
        out_specs=pl.BlockSpec((nb, tn), lambda j: (0, j)),
        out_shape=jax.ShapeDtypeStruct((nb, n), F32),
        compiler_params=_params("arbitrary"),
        name="ada",
    )(cb, w_ada, b_ada.reshape(1, n))


def _inproj_kernel(x_ref, mod_ref, w_ref, wdt_ref, o_ref, dt_ref, u_s, *, rows_per_chunk):
    j = pl.program_id(1)
    tm = x_ref.shape[0]

    @pl.when(j == 0)
    def _():
        sh = mod_ref[0, 0:1, :]
        sc = mod_ref[0, 1:2, :]

        def body(r, carry):
            rows = pl.ds(pl.multiple_of(r * rows_per_chunk, rows_per_chunk), rows_per_chunk)
            xv = x_ref[rows, :]
            ms = jnp.mean(xv * xv, axis=-1, keepdims=True)
            u = xv * lax.rsqrt(ms + EPS) * (1.0 + sc) + sh
            u_s[rows, :] = u.astype(BF16)
            dt_ref[rows, :] = jnp.dot(u, wdt_ref[...], precision=HIGHEST, preferred_element_type=F32)
            return carry

        lax.fori_loop(0, tm // rows_per_chunk, body, 0)

    o_ref[...] = jnp.dot(u_s[...], w_ref[...], preferred_element_type=F32).astype(o_ref.dtype)


def _inproj(x2, mod3, w_main, w_dt, seq):
    t, d = x2.shape
    n = w_main.shape[1]
    tm = min(INPROJ_TM, seq)
    tn = INPROJ_TN
    tiles_per_seq = seq // tm
    return pl.pallas_call(
        functools.partial(_inproj_kernel, rows_per_chunk=min(128, tm)),
        grid=(t // tm, n // tn),
        in_specs=[pl.BlockSpec((tm, d), lambda i, j: (i, 0)),
                  pl.BlockSpec((1, N_MOD, d), lambda i, j: (i // tiles_per_seq, 0, 0)),
                  pl.BlockSpec((d, tn), lambda i, j: (0, j)),
                  pl.BlockSpec((d, LANES), lambda i, j: (0, 0))],
        out_specs=[pl.BlockSpec((tm, tn), lambda i, j: (i, j)),
                   pl.BlockSpec((tm, LANES), lambda i, j: (i, 0))],
        out_shape=[jax.ShapeDtypeStruct((t, n), BF16), jax.ShapeDtypeStruct((t, LANES), F32)],
        scratch_shapes=[pltpu.VMEM((tm, d), BF16)],
        compiler_params=_params("arbitrary", "arbitrary"),
        name="inproj",
    )(x2, mod3, w_main, w_dt)


def _pool_kernel(p_ref, pw_ref, ps_ref, o_ref, prev_s, *, tiles_per_seq):
    i = pl.program_id(0)
    tl, d = p_ref.shape
    gd = d // len(POOL_WINDOWS)
    it = i % tiles_per_seq

    @pl.when(it == 0)
    def _():
        prev_s[...] = jnp.zeros_like(prev_s)

    row = lax.broadcasted_iota(jnp.int32, (tl, 2 * tl), 0)
    col = lax.broadcasted_iota(jnp.int32, (tl, 2 * tl), 1)
    pos = (it * tl + lax.broadcasted_iota(jnp.int32, (tl, 1), 0) + 1).astype(F32)
    for g, w in enumerate(POOL_WINDOWS):
        sl = slice(g * gd, (g + 1) * gd)
        cur = p_ref[:, sl]
        ext = jnp.concatenate([prev_s[:, sl], cur], axis=0)
        band = ((col <= row + tl) & (col > row + tl - w)).astype(BF16)
        win_sum = jnp.dot(band, ext, preferred_element_type=F32)
        mean = win_sum / jnp.minimum(pos, float(w))
        dlt = (mean - cur.astype(F32)).astype(BF16)
        mixed = jnp.dot(dlt, pw_ref[g], preferred_element_type=F32)
        o_ref[:, sl] = (mixed * ps_ref[:, sl]).astype(o_ref.dtype)
    prev_s[...] = p_ref[...]


def _pool(proj, p_blk, pool_w_bf, pool_scale, seq, d):
    t = proj.shape[0]
    tl = SEQ_TILE
    g, gd = pool_w_bf.shape[0], pool_w_bf.shape[1]
    return pl.pallas_call(
        functools.partial(_pool_kernel, tiles_per_seq=seq // tl),
        grid=(t // tl,),
        in_specs=[pl.BlockSpec((tl, d), lambda i: (i, p_blk)),
                  pl.BlockSpec((g, gd, gd), lambda i: (0, 0, 0)),
                  pl.BlockSpec((1, d), lambda i: (0, 0))],
        out_specs=pl.BlockSpec((tl, d), lambda i: (i, 0)),
        out_shape=jax.ShapeDtypeStruct((t, d), BF16),
        scratch_shapes=[pltpu.VMEM((tl, d), BF16)],
        compiler_params=_params("arbitrary"),
        name="pool",
    )(proj, pool_w_bf, pool_scale.reshape(1, d))


def _ssd_kernel(z_ref, xs_ref, bm_ref, cm_ref, dt_ref,
                cwx_ref, cwb_ref, cwc_ref, cbx_ref, cbb_ref, cbc_ref,
                dtb_ref, alog_ref, dsk_ref, nw_ref, exp_ref,
                o_ref,
                extx_s, extb_s, extc_s, state_s, xdt_s, y_s, cb_s, cs_s, cst_s, xc_s, bc_s, cc_s,
                *, tiles_per_seq, heads):
    i = pl.program_id(0)
    tl, inner = xs_ref.shape
    gw = inner // SSD_GROUPS
    pairs_per_group = gw // LANES
    n_pairs = inner // LANES

    @pl.when(i % tiles_per_seq == 0)
    def _():
        extx_s[0:CONV_HALO, :] = jnp.zeros((CONV_HALO, inner), F32)
        extb_s[0:CONV_HALO, :] = jnp.zeros((CONV_HALO, extb_s.shape[1]), F32)
        extc_s[0:CONV_HALO, :] = jnp.zeros((CONV_HALO, extc_s.shape[1]), F32)
        state_s[...] = jnp.zeros_like(state_s)

    def conv_silu(ext_ref, src_ref, w_ref, b_ref, dst_ref, width, cw):
        for c0 in range(0, width, cw):
            cs = slice(c0, c0 + cw)
            ext_ref[CONV_HALO:CONV_HALO + tl, cs] = src_ref[:, cs].astype(F32)
            acc = b_ref[:, cs] + w_ref[SSD_CONV - 1:SSD_CONV, cs] * ext_ref[CONV_HALO:CONV_HALO + tl, cs]
            for k in range(1, SSD_CONV):
                acc = acc + w_ref[SSD_CONV - 1 - k:SSD_CONV - k, cs] * ext_ref[pl.ds(CONV_HALO - k, tl), cs]
            dst_ref[:, cs] = _silu(acc)
            ext_ref[0:CONV_HALO, cs] = ext_ref[tl:tl + CONV_HALO, cs]

    conv_silu(extx_s, xs_ref, cwx_ref, cbx_ref, xc_s, inner, LANES)
    conv_silu(extb_s, bm_ref, cwb_ref, cbb_ref, bc_s, bm_ref.shape[1], LANES)
    conv_silu(extc_s, cm_ref, cwc_ref, cbc_ref, cc_s, cm_ref.shape[1], LANES)

    dtv = dt_ref[...] + dtb_ref[...]
    dt = jnp.maximum(dtv, 0.0) + jnp.log1p(jnp.exp(-jnp.abs(dtv)))
    a = -jnp.exp(alog_ref[...])
    da = dt * a
    ri = lax.broadcasted_iota(jnp.int32, (tl, tl), 0)
    ci = lax.broadcasted_iota(jnp.int32, (tl, tl), 1)
    causal = ri >= ci
    cs = jnp.dot(causal.astype(F32), da, precision=HIGHEST, preferred_element_type=F32)
    cs2 = cs * LOG2_E
    cs_s[...] = cs2
    cst_s[...] = cs2.T
    last = cs[tl - 1:tl, :]
    exp_m = exp_ref[...]
    dt_x = jnp.dot(dt.astype(BF16), exp_m, preferred_element_type=F32)
    ecs_x = jnp.dot(jnp.exp(cs).astype(BF16), exp_m, preferred_element_type=F32)
    dte_x = jnp.dot(jnp.exp(last - cs).astype(BF16), exp_m, preferred_element_type=F32)

    for g in range(SSD_GROUPS):
        gs = slice(g * gw, (g + 1) * gw)
        ns = slice(g * SSD_STATE, (g + 1) * SSD_STATE)
        xg = xc_s[:, gs]
        xdt = xg * dt_x[:, gs]
        xdt_bf = xdt.astype(BF16)
        xd_bf = (xdt * dte_x[:, gs]).astype(BF16)
        bg = bc_s[:, ns]
        cg = cc_s[:, ns].astype(BF16)
        cb_s[g] = lax.dot_general(cg, bg.astype(BF16), (((1,), (1,)), ((), ())), preferred_element_type=F32)
        s_old = state_s[g]
        y_off = jnp.dot(cg, s_old.astype(BF16), preferred_element_type=F32) * ecs_x[:, gs]
        state_s[g] = (s_old * ecs_x[tl - 1:tl, gs]
                      + jnp.dot(bg.T.astype(BF16), xd_bf, preferred_element_type=F32))
        y0 = y_off + dsk_ref[:, gs] * xg
        for q in range(pairs_per_group):
            qs = slice(q * LANES, (q + 1) * LANES)
            xdt_s[g * pairs_per_group + q] = xdt_bf[:, qs]
            y_s[g * pairs_per_group + q] = y0[:, qs]

    lane_h = lax.broadcasted_iota(jnp.int32, (tl, LANES), 1)
    low_half = lane_h < SSD_HEAD_DIM

    half = tl // 2
    causal_top = (lax.broadcasted_iota(jnp.int32, (half, half), 0)
                  >= lax.broadcasted_iota(jnp.int32, (half, half), 1))
    causal_bot = (lax.broadcasted_iota(jnp.int32, (half, tl), 0) + half
                  >= lax.broadcasted_iota(jnp.int32, (half, tl), 1))
    lane_hh = lax.broadcasted_iota(jnp.int32, (half, LANES), 1)
    low_hh = lane_hh < SSD_HEAD_DIM

    def pair_body(hp, carry):
        g = hp // pairs_per_group
        cb_top = cb_s[g, 0:half, 0:half]
        cb_bot = cb_s[g, half:tl, :]
        x_top = xdt_s[hp, 0:half, :]
        x_all = xdt_s[hp]
        cs_top = cs_s[0:half, :]
        cs_bot = cs_s[half:tl, :]
        tops, bots = [], []
        for e in range(2):
            h = 2 * hp + e
            head_lane = jnp.full((half, LANES), h, jnp.int32)
            col_top = jnp.take_along_axis(cs_top, head_lane, axis=1)
            col_bot = jnp.take_along_axis(cs_bot, head_lane, axis=1)
            col_bot = jnp.concatenate([col_bot] * (tl // LANES), axis=1)
            row_all = cst_s[pl.ds(h, 1), :]
            row_top = row_all[:, 0:half]
            m_top = jnp.exp2(jnp.where(causal_top, col_top - row_top, -jnp.inf)) * cb_top
            m_bot = jnp.exp2(jnp.where(causal_bot, col_bot - row_all, -jnp.inf)) * cb_bot
            tops.append(jnp.dot(m_top.astype(BF16), x_top, preferred_element_type=F32))
            bots.append(jnp.dot(m_bot.astype(BF16), x_all, preferred_element_type=F32))
        y_s[hp, 0:half, :] = y_s[hp, 0:half, :] + jnp.where(low_hh, tops[0], tops[1])
        y_s[hp, half:tl, :] = y_s[hp, half:tl, :] + jnp.where(low_hh, bots[0], bots[1])
        return carry

    lax.fori_loop(0, n_pairs, pair_body, 0, unroll=16)

    for g in range(SSD_GROUPS):
        gs = slice(g * gw, (g + 1) * gw)
        yg = jnp.concatenate([y_s[g * pairs_per_group + q] for q in range(pairs_per_group)], axis=1)
        yg = yg * _silu(z_ref[:, gs].astype(F32))
        ms = jnp.mean(yg * yg, axis=-1, keepdims=True)
        o_ref[:, gs] = (yg * lax.rsqrt(ms + EPS) * nw_ref[:, gs]).astype(o_ref.dtype)


def _ssd(proj, dt_raw, blk, conv_w, conv_b, dt_bias, a_log, d_skip, ssd_norm_w, seq):
    t = proj.shape[0]
    tl = SEQ_TILE
    heads = dt_bias.shape[0]
    inner = ssd_norm_w.shape[0]
    bc = SSD_GROUPS * SSD_STATE
    gw = inner // SSD_GROUPS
    pad = LANES - heads
    row = lambda v: v.astype(F32).reshape(1, -1)
    dtb = jnp.pad(row(dt_bias), ((0, 0), (0, pad)))
    alog = jnp.pad(row(a_log), ((0, 0), (0, pad)))
    dsk = jnp.repeat(d_skip.astype(F32), SSD_HEAD_DIM).reshape(1, inner)
    expand = (lax.broadcasted_iota(jnp.int32, (LANES, inner), 1) // SSD_HEAD_DIM
              == lax.broadcasted_iota(jnp.int32, (LANES, inner), 0)).astype(BF16)
    cw = conv_w.astype(F32)
    cbias = row(conv_b)
    full = lambda shape: pl.BlockSpec(shape, lambda i: tuple(0 for _ in shape))
    return pl.pallas_call(
        functools.partial(_ssd_kernel, tiles_per_seq=seq // tl, heads=heads),
        grid=(t // tl,),
        in_specs=[pl.BlockSpec((tl, inner), lambda i: (i, blk["z"])),
                  pl.BlockSpec((tl, inner), lambda i: (i, blk["xs"])),
                  pl.BlockSpec((tl, bc), lambda i: (i, blk["B"])),
                  pl.BlockSpec((tl, bc), lambda i: (i, blk["C"])),
                  pl.BlockSpec((tl, LANES), lambda i: (i, 0)),
                  full((SSD_CONV, inner)), full((SSD_CONV, bc)), full((SSD_CONV, bc)),
                  full((1, inner)), full((1, bc)), full((1, bc)),
                  full((1, LANES)), full((1, LANES)), full((1, inner)), full((1, inner)),
                  full((LANES, inner))],
        out_specs=pl.BlockSpec((tl, inner), lambda i: (i, 0)),
        out_shape=jax.ShapeDtypeStruct((t, inner), BF16),
        scratch_shapes=[pltpu.VMEM((tl + CONV_HALO, inner), F32),
                        pltpu.VMEM((tl + CONV_HALO, bc), F32),
                        pltpu.VMEM((tl + CONV_HALO, bc), F32),
                        pltpu.VMEM((SSD_GROUPS, SSD_STATE, gw), F32),
                        pltpu.VMEM((inner // LANES, tl, LANES), BF16),
                        pltpu.VMEM((inner // LANES, tl, LANES), F32),
                        pltpu.VMEM((SSD_GROUPS, tl, tl), F32),
                        pltpu.VMEM((tl, LANES), F32),
                        pltpu.VMEM((LANES, tl), F32),
                        pltpu.VMEM((tl, inner), F32),
                        pltpu.VMEM((tl, bc), F32),
                        pltpu.VMEM((tl, bc), F32)],
        compiler_params=_params("arbitrary"),
        name="ssd",
    )(proj, proj, proj, proj, dt_raw,
      cw[:, :inner], cw[:, inner:inner + bc], cw[:, inner + bc:],
      cbias[:, :inner], cbias[:, inner:inner + bc], cbias[:, inner + bc:],
      dtb, alog, dsk, row(ssd_norm_w), expand)


def _merge_kernel(yp_ref, ys_ref, gp_ref, gs_ref, wp_ref, ws_ref, o_ref):
    a = jnp.dot(yp_ref[...], wp_ref[...], preferred_element_type=F32)
    b = jnp.dot(ys_ref[...], ws_ref[...], preferred_element_type=F32)
    o_ref[...] = (jax.nn.sigmoid(gp_ref[...].astype(F32)) * a
                  + jax.nn.sigmoid(gs_ref[...].astype(F32)) * b).astype(o_ref.dtype)


def _merge(ypool, yssd, proj, gp_off, gs_off, wbp, wbs):
    t, d = ypool.shape
    inner = yssd.shape[1]
    tm, tn = min(MERGE_TM, t), MERGE_TN
    return pl.pallas_call(
        _merge_kernel,
        grid=(t // tm, d // tn),
        in_specs=[pl.BlockSpec((tm, d), lambda i, j: (i, 0)),
                  pl.BlockSpec((tm, inner), lambda i, j: (i, 0)),
                  pl.BlockSpec((tm, tn), lambda i, j: (i, gp_off // tn + j)),
                  pl.BlockSpec((tm, tn), lambda i, j: (i, gs_off // tn + j)),
                  pl.BlockSpec((d, tn), lambda i, j: (0, j)),
                  pl.BlockSpec((inner, tn), lambda i, j: (0, j))],
        out_specs=pl.BlockSpec((tm, tn), lambda i, j: (i, j)),
        out_shape=jax.ShapeDtypeStruct((t, d), BF16),
        compiler_params=_params("arbitrary", "arbitrary"),
        name="merge",
    )(ypool, yssd, proj, proj, wbp, wbs)


def _out_kernel(m_ref, wo_ref, x_ref, mod_ref, wr_ref, br_ref,
                h_ref, u_ref, idx_ref, gate_ref, rank_ref, cnt_ref, carry_s):
    i = pl.program_id(0)
    tm = m_ref.shape[0]
    ne = wr_ref.shape[0]

    @pl.when(i == 0)
    def _():
        carry_s[...] = jnp.zeros_like(carry_s)

    mix = jnp.dot(m_ref[...], wo_ref[...], preferred_element_type=F32)
    h1 = x_ref[...] + mod_ref[0, 2:3, :] * mix
    h_ref[...] = h1
    ms = jnp.mean(h1 * h1, axis=-1, keepdims=True)
    u = h1 * lax.rsqrt(ms + EPS) * (1.0 + mod_ref[0, 4:5, :]) + mod_ref[0, 3:4, :]
    u_ref[...] = u
    logits = lax.dot_general(wr_ref[...], u, (((1,), (1,)), ((), ())), precision=HIGHEST,
                             preferred_element_type=F32) + br_ref[...]
    eidx = lax.broadcasted_iota(jnp.int32, (ne, tm), 0)
    work = logits
    vals, idxs, hots = [], [], []
    for _ in range(TOP_K):
        mx = jnp.max(work, axis=0, keepdims=True)
        sel = jnp.min(jnp.where(work == mx, eidx, ne), axis=0, keepdims=True)
        hot = eidx == sel
        vals.append(mx)
        idxs.append(sel)
        hots.append(hot)
        work = jnp.where(hot, -jnp.inf, work)
    exps = [jnp.exp(v - vals[0]) for v in vals]
    den = exps[0]
    for e in exps[1:]:
        den = den + e
    cnt = hots[0].astype(F32)
    for hot in hots[1:]:
        cnt = cnt + hot.astype(F32)
    ti = lax.broadcasted_iota(jnp.int32, (tm, tm), 0)
    tj = lax.broadcasted_iota(jnp.int32, (tm, tm), 1)
    before = (ti < tj).astype(BF16)
    prefix = jnp.dot(cnt.astype(BF16), before, preferred_element_type=F32)
    base = carry_s[:, 0:1] + prefix
    pad_rows = idx_ref.shape[0] - TOP_K
    ranks = [jnp.sum(jnp.where(hot, base, 0.0), axis=0, keepdims=True) for hot in hots]
    idx_ref[...] = jnp.concatenate(idxs + [jnp.zeros((pad_rows, tm), jnp.int32)], axis=0)
    gate_ref[...] = jnp.concatenate([e / den for e in exps] + [jnp.zeros((pad_rows, tm), F32)], axis=0)
    rank_ref[...] = jnp.concatenate([r.astype(jnp.int32) for r in ranks]
                                    + [jnp.zeros((pad_rows, tm), jnp.int32)], axis=0)
    carry_s[...] = carry_s[...] + jnp.sum(cnt, axis=1, keepdims=True)
    cnt_ref[...] = carry_s[...].astype(jnp.int32)


def _outproj(merged, wo, x2, mod3, w_router, b_router, seq):
    t, d = x2.shape
    ne = w_router.shape[1]
    tm = min(OUT_TM, seq)
    tiles_per_seq = seq // tm
    rows = 8
    return pl.pallas_call(
        _out_kernel,
        grid=(t // tm,),
        in_specs=[pl.BlockSpec((tm, d), lambda i: (i, 0)),
                  pl.BlockSpec((d, d), lambda i: (0, 0), pipeline_mode=pl.Buffered(1)),
                  pl.BlockSpec((tm, d), lambda i: (i, 0)),
                  pl.BlockSpec((1, N_MOD, d), lambda i: (i // tiles_per_seq, 0, 0)),
                  pl.BlockSpec((ne, d), lambda i: (0, 0)),
                  pl.BlockSpec((ne, 1), lambda i: (0, 0))],
        out_specs=[pl.BlockSpec((tm, d), lambda i: (i, 0)),
                   pl.BlockSpec((tm, d), lambda i: (i, 0)),
                   pl.BlockSpec((rows, tm), lambda i: (0, i)),
                   pl.BlockSpec((rows, tm), lambda i: (0, i)),
                   pl.BlockSpec((rows, tm), lambda i: (0, i)),
                   pl.BlockSpec((ne, LANES), lambda i: (0, 0))],
        out_shape=[jax.ShapeDtypeStruct((t, d), F32), jax.ShapeDtypeStruct((t, d), F32),
                   jax.ShapeDtypeStruct((rows, t), jnp.int32), jax.ShapeDtypeStruct((rows, t), F32),
                   jax.ShapeDtypeStruct((rows, t), jnp.int32), jax.ShapeDtypeStruct((ne, LANES), jnp.int32)],
        scratch_shapes=[pltpu.VMEM((ne, LANES), F32)],
        compiler_params=_params("arbitrary"),
        name="outproj",
    )(merged, wo, x2, mod3, w_router.astype(F32).T, b_router.astype(F32).reshape(ne, 1))


def _row_copy(src_hbm, src_row, dst_ref, dst_row, sem):
    return pltpu.make_async_copy(src_hbm.at[pl.ds(src_row, 1)], dst_ref.at[pl.ds(dst_row, 1)], sem)


def _dispatch_kernel(slot_ref, u_ref, init_hbm, xs_hbm, sem):
    del init_hbm
    ts = u_ref.shape[0]

    def issue(tok, carry):
        for k in range(TOP_K):
            _row_copy(u_ref, tok, xs_hbm, slot_ref[0, 0, tok * TOP_K + k], sem).start(priority=k % 2)
        return carry

    lax.fori_loop(0, ts, issue, 0)

    def drain(tk, carry):
        _row_copy(u_ref, 0, xs_hbm, 0, sem).wait()
        return carry

    lax.fori_loop(0, ts * TOP_K, drain, 0)


def _dispatch(u2, slot_tk, n_slots):
    t, d = u2.shape
    ts = min(DISPATCH_TS, t)
    slots3 = slot_tk.reshape(t // ts, 1, ts * TOP_K)
    init = jnp.zeros((n_slots, d), u2.dtype)
    return pl.pallas_call(
        _dispatch_kernel,
        grid=(t // ts,),
        in_specs=[pl.BlockSpec((1, 1, ts * TOP_K), lambda i: (i, 0, 0), memory_space=pltpu.SMEM),
                  pl.BlockSpec((ts, d), lambda i: (i, 0)),
                  pl.BlockSpec(memory_space=pl.ANY)],
        out_specs=pl.BlockSpec(memory_space=pl.ANY),
        out_shape=jax.ShapeDtypeStruct((n_slots, d), u2.dtype),
        scratch_shapes=[pltpu.SemaphoreType.DMA(())],
        input_output_aliases={2: 0},
        compiler_params=_params("arbitrary"),
        name="dispatch",
    )(slots3, u2, init)


def _moe_kernel(be_ref, na_ref, x_ref, wg_ref, wu_ref, bg_ref, bu_ref, wd_ref, bd_ref, o_ref):
    del be_ref
    b, j = pl.program_id(0), pl.program_id(1)

    @pl.when(b < na_ref[0])
    def _():
        xb = x_ref[...].astype(BF16)
        gate = jnp.dot(xb, wg_ref[...], preferred_element_type=F32) + bg_ref[...]
        up = jnp.dot(xb, wu_ref[...], preferred_element_type=F32) + bu_ref[...]
        gate = jnp.minimum(gate, SWIGLU_LIMIT)
        up = jnp.clip(up, -SWIGLU_LIMIT, SWIGLU_LIMIT)
        act = (up + 1.0) * gate * jax.nn.sigmoid(SWIGLU_ALPHA * gate)
        part = jnp.dot(act.astype(BF16), wd_ref[...], preferred_element_type=F32)

        @pl.when(j == 0)
        def _():
            o_ref[...] = part + bd_ref[...]

        @pl.when(j > 0)
        def _():
            o_ref[...] = o_ref[...] + part

    @pl.when((b >= na_ref[0]) & (j == 0))
    def _():
        o_ref[...] = jnp.zeros_like(o_ref)


def _moe(xs, block_expert, n_active, wgu, bgu, wd, bd):
    n_slots, d = xs.shape
    ne, _, f2 = wgu.shape
    f = f2 // 2
    bm, tf = MOE_BM, min(MOE_TF, f)
    nb, ft = n_slots // bm, f // tf

    def blk(b, na):
        return jnp.maximum(jnp.minimum(b, na[0] - 1), 0)

    def jj(b, j, na):
        return jnp.where(b < na[0], j, ft - 1)

    grid_spec = pltpu.PrefetchScalarGridSpec(
        num_scalar_prefetch=2,
        grid=(nb, ft),
        in_specs=[pl.BlockSpec((bm, d), lambda b, j, be, na: (blk(b, na), 0)),
                  pl.BlockSpec((None, d, tf), lambda b, j, be, na: (be[b], 0, jj(b, j, na))),
                  pl.BlockSpec((None, d, tf), lambda b, j, be, na: (be[b], 0, ft + jj(b, j, na))),
                  pl.BlockSpec((None, 1, tf), lambda b, j, be, na: (be[b], 0, jj(b, j, na))),
                  pl.BlockSpec((None, 1, tf), lambda b, j, be, na: (be[b], 0, ft + jj(b, j, na))),
                  pl.BlockSpec((None, tf, d), lambda b, j, be, na: (be[b], jj(b, j, na), 0)),
                  pl.BlockSpec((None, 1, d), lambda b, j, be, na: (be[b], 0, 0))],
        out_specs=pl.BlockSpec((bm, d), lambda b, j, be, na: (b, 0)),
    )
    return pl.pallas_call(
        _moe_kernel,
        grid_spec=grid_spec,
        out_shape=jax.ShapeDtypeStruct((n_slots, d), F32),
        compiler_params=_params("arbitrary", "arbitrary"),
        name="moe",
    )(block_expert, n_active, xs, wgu, wgu, bgu, bgu, wd, bd)


def _final_kernel(slot_ref, h_ref, gate_ref, mod_ref, fw_ref, y_hbm, o_ref, ybuf, sem):
    tc = h_ref.shape[0]

    def issue(pair, carry):
        for q in range(2):
            tk = 2 * pair + q
            _row_copy(y_hbm, slot_ref[0, 0, tk], ybuf, tk, sem).start(priority=q)
        return carry

    lax.fori_loop(0, tc * TOP_K // 2, issue, 0)

    def drain(tk, carry):
        _row_copy(y_hbm, 0, ybuf, 0, sem).wait()
        return carry

    lax.fori_loop(0, tc * TOP_K, drain, 0)

    ffn = gate_ref[:, 0:1] * ybuf[0:tc, :]
    for k in range(1, TOP_K):
        ffn = ffn + gate_ref[:, k:k + 1] * ybuf[k * tc:(k + 1) * tc, :]
    h2 = h_ref[...] + mod_ref[0, 5:6, :] * ffn
    ms = jnp.mean(h2 * h2, axis=-1, keepdims=True)
    o_ref[...] = (h2 * lax.rsqrt(ms + EPS) * fw_ref[...]).astype(o_ref.dtype)


def _final(h1, gates_tk, slot_kt, mod3, final_norm_w, y_slots, seq, out_dtype):
    t, d = h1.shape
    tc = min(FINAL_TC, seq)
    tiles_per_seq = seq // tc
    slots3 = slot_kt.reshape(TOP_K, t // tc, tc).transpose(1, 0, 2).reshape(t // tc, 1, TOP_K * tc)
    return pl.pallas_call(
        _final_kernel,
        grid=(t // tc,),
        in_specs=[pl.BlockSpec((1, 1, TOP_K * tc), lambda i: (i, 0, 0), memory_space=pltpu.SMEM),
                  pl.BlockSpec((tc, d), lambda i: (i, 0)),
                  pl.BlockSpec((tc, gates_tk.shape[1]), lambda i: (i, 0)),
                  pl.BlockSpec((1, N_MOD, d), lambda i: (i // tiles_per_seq, 0, 0)),
                  pl.BlockSpec((1, d), lambda i: (0, 0)),
                  pl.BlockSpec(memory_space=pl.ANY)],
        out_specs=pl.BlockSpec((tc, d), lambda i: (i, 0)),
        out_shape=jax.ShapeDtypeStruct((t, d), out_dtype),
        scratch_shapes=[pltpu.VMEM((TOP_K * tc, d), F32), pltpu.SemaphoreType.DMA(())],
        compiler_params=_params("arbitrary"),
        name="final",
    )(slots3, h1, gates_tk, mod3, final_norm_w.astype(F32).reshape(1, d), y_slots)


def kernel(x, c, w_ada, b_ada, w_in, pool_w, pool_scale, conv_w, conv_b, dt_bias, a_log, d_skip, ssd_norm_w,
           w_branch_pool, w_branch_ssd, w_out, w_router, b_router, w_gate_up, b_gate_up, w_down, b_down,
           final_norm_w):
    bsz, seq, d = x.shape
    depth = w_ada.shape[0]
    t = bsz * seq
    inner = ssd_norm_w.shape[1]
    heads = dt_bias.shape[1]
    bc = SSD_GROUPS * SSD_STATE
    ne = w_router.shape[2]
    assert depth == 1, "the final RMSNorm is fused into the last kernel of the single layer"
    assert SEQ_TILE == 2 * LANES
    assert seq % SEQ_TILE == 0 and heads <= LANES and (7 * d) % bc == 0 and (4 * d) % inner == 0
    assert inner // SSD_GROUPS % LANES == 0 and d % (len(POOL_WINDOWS) * LANES) == 0

    s1 = d
    s2 = s1 + inner
    s3 = s2 + inner + 2 * bc
    s4 = s3 + heads
    s5 = s4 + d
    blk = {"z": 0, "xs": 1, "p": (2 * inner) // d, "B": (2 * inner + 3 * d) // bc, "C": (2 * inner + 3 * d) // bc + 1}
    gp_off, gs_off = 2 * inner + d, 2 * inner + 2 * d

    h = x.astype(F32).reshape(t, d)
    for layer in range(depth):
        wl = w_in[layer]
        w_main = jnp.concatenate([wl[:, s1:s2], wl[:, s2:s2 + inner], wl[:, :s1], wl[:, s4:s5], wl[:, s5:],
                                  wl[:, s2 + inner:s2 + inner + bc], wl[:, s2 + inner + bc:s3]], axis=1).astype(BF16)
        w_dt = jnp.pad(wl[:, s3:s4].astype(F32), ((0, 0), (0, LANES - heads)))

        mod3 = _ada(c, w_ada[layer], b_ada[layer]).reshape(bsz, N_MOD, d)
        proj, dt_raw = _inproj(h, mod3, w_main, w_dt, seq)
        ypool = _pool(proj, blk["p"], pool_w[layer].astype(BF16), pool_scale[layer], seq, d)
        yssd = _ssd(proj, dt_raw, blk, conv_w[layer], conv_b[layer], dt_bias[layer], a_log[layer], d_skip[layer],
                    ssd_norm_w[layer], seq)
        merged = _merge(ypool, yssd, proj, gp_off, gs_off, w_branch_pool[layer].astype(BF16),
                        w_branch_ssd[layer].astype(BF16))
        h1, u2, idx_kt, gate_kt, rank_kt, counts = _outproj(merged, w_out[layer].astype(BF16), h, mod3,
                                                            w_router[layer], b_router[layer], seq)

        counts = counts[:, 0]
        padded = (counts + MOE_BM - 1) // MOE_BM * MOE_BM
        pend = jnp.cumsum(padded)
        pstart = pend - padded
        n_blocks = (t * TOP_K) // MOE_BM + ne
        n_active = (pend[-1] // MOE_BM).astype(jnp.int32).reshape(1)
        blocks = jnp.minimum(jnp.arange(n_blocks, dtype=jnp.int32), n_active[0] - 1)
        block_expert = jnp.minimum(jnp.sum(pend[None, :] <= (blocks * MOE_BM)[:, None], axis=1),
                                   ne - 1).astype(jnp.int32)
        eids = jnp.arange(ne, dtype=jnp.int32)[:, None, None]
        slot_kt = (jnp.sum(jnp.where(idx_kt[:TOP_K][None] == eids, pstart[:, None, None], 0), axis=0)
                   + rank_kt[:TOP_K]).astype(jnp.int32)

        xs = _dispatch(u2, slot_kt.T, n_blocks * MOE_BM)
        y_slots = _moe(xs, block_expert, n_active, w_gate_up[layer].astype(BF16),
                       b_gate_up[layer].astype(F32)[:, None, :], w_down[layer].astype(BF16),
                       b_down[layer].astype(F32)[:, None, :])
        h = _final(h1, gate_kt.T, slot_kt, mod3, final_norm_w, y_slots, seq, x.dtype)
    return h.reshape(bsz, seq, d)
```

```python
import functools

import jax
import jax.numpy as jnp
from jax import lax
from jax.experimental import pallas as pl
from jax.experimental.pallas import tpu as pltpu

F32 = jnp.float32
BF16 = jnp.bfloat16
HIGHEST = lax.Precision.HIGHEST

EPS = 1e-6
POOL_WINDOWS = (2, 4, 8, 16)
SSD_GROUPS = 8
SSD_STATE = 128
SSD_CONV = 4
SSD_HEAD_DIM = 64
TOP_K = 4
SWIGLU_LIMIT = 7.0
SWIGLU_ALPHA = 1.702
N_MOD = 6

LOG2_E = 1.4426950408889634
LANES = 128
CONV_HALO = 8
V7X_VMEM_LIMIT = 56 * 1024 * 1024

SEQ_TILE = 256
INPROJ_TM, INPROJ_TN = 1024, 1024
MERGE_TM, MERGE_TN = 512, 512
OUT_TM = 512
MOE_BM, MOE_TF = 512, 1024
DISPATCH_TS = 256
FINAL_TC = 256
FINAL_ISSUE_UNROLL = 8


def _params(*sem):
    return pltpu.CompilerParams(dimension_semantics=sem, vmem_limit_bytes=V7X_VMEM_LIMIT)


def _silu(v):
    return v * jax.nn.sigmoid(v)


def _pack_bf16_pairs(v):
    half = v.shape[1] // 2
    lo = lax.bitcast_convert_type(v[:, :half].astype(BF16).astype(F32), jnp.uint32)
    hi = lax.bitcast_convert_type(v[:, half:].astype(BF16).astype(F32), jnp.uint32)
    return (hi & jnp.uint32(0xFFFF0000)) | (lo >> 16)


def _unpack_bf16_pairs(w):
    lo = lax.bitcast_convert_type(w << 16, F32).astype(BF16)
    hi = lax.bitcast_convert_type(w & jnp.uint32(0xFFFF0000), F32).astype(BF16)
    return lo, hi


def _ada_kernel(cb_ref, w_ref, b_ref, o_ref):
    nb, tn = cb_ref.shape[0], w_ref.shape[1]
    for b in range(nb):
        ca = _silu(cb_ref[b])
        cols = [jnp.sum(w_ref[:, j * LANES:(j + 1) * LANES] * ca, axis=0, keepdims=True)
                for j in range(tn // LANES)]
        o_ref[b:b + 1, :] = jnp.concatenate(cols, axis=1) + b_ref[...]


def _ada(c, w_ada, b_ada):
    nb, k = c.shape
    n = w_ada.shape[1]
    tn = 1024 if n % 1024 == 0 else n
    cb = jnp.broadcast_to(c.astype(F32)[:, :, None], (nb, k, LANES))
    return pl.pallas_call(
        _ada_kernel,
        grid=(n // tn,),
        in_specs=[pl.BlockSpec((nb, k, LANES), lambda j: (0, 0, 0)),
                  pl.BlockSpec((k, tn), lambda j: (0, j)),
                  pl.BlockSpec((1, tn), lambda j: (0, j))],
        out_specs=pl.BlockSpec((nb, tn), lambda j: (0, j)),
        out_shape=jax.ShapeDtypeStruct((nb, n), F32),
        compiler_params=_params("arbitrary"),
        name="ada",
    )(cb, w_ada, b_ada.reshape(1, n))


def _inproj_kernel(x_ref, mod_ref, w_ref, wdt_ref, o_ref, dt_ref, u_s, *, rows_per_chunk):
    j = pl.program_id(1)
    tm = x_ref.shape[0]

    @pl.when(j == 0)
    def _():
        sh = mod_ref[0, 0:1, :]
        sc = mod_ref[0, 1:2, :]

        def body(r, carry):
            rows = pl.ds(pl.multiple_of(r * rows_per_chunk, rows_per_chunk), rows_per_chunk)
            xv = x_ref[rows, :]
            ms = jnp.mean(xv * xv, axis=-1, keepdims=True)
            u = xv * lax.rsqrt(ms + EPS) * (1.0 + sc) + sh
            u_s[rows, :] = u.astype(BF16)
            dt_ref[rows, :] = jnp.dot(u, wdt_ref[...], precision=HIGHEST, preferred_element_type=F32)
            return carry

        lax.fori_loop(0, tm // rows_per_chunk, body, 0)

    o_ref[...] = jnp.dot(u_s[...], w_ref[...], preferred_element_type=F32).astype(o_ref.dtype)


def _inproj(x2, mod3, w_main, w_dt, seq):
    t, d = x2.shape
    n = w_main.shape[1]
    tm = min(INPROJ_TM, seq)
    tn = INPROJ_TN
    tiles_per_seq = seq // tm
    return pl.pallas_call(
        functools.partial(_inproj_kernel, rows_per_chunk=min(128, tm)),
        grid=(t // tm, n // tn),
        in_specs=[pl.BlockSpec((tm, d), lambda i, j: (i, 0)),
                  pl.BlockSpec((1, N_MOD, d), lambda i, j: (i // tiles_per_seq, 0, 0)),
                  pl.BlockSpec((d, tn), lambda i, j: (0, j)),
                  pl.BlockSpec((d, LANES), lambda i, j: (0, 0))],
        out_specs=[pl.BlockSpec((tm, tn), lambda i, j: (i, j)),
                   pl.BlockSpec((tm, LANES), lambda i, j: (i, 0))],
        out_shape=[jax.ShapeDtypeStruct((t, n), BF16), jax.ShapeDtypeStruct((t, LANES), F32)],
        scratch_shapes=[pltpu.VMEM((tm, d), BF16)],
        compiler_params=_params("arbitrary", "arbitrary"),
        name="inproj",
    )(x2, mod3, w_main, w_dt)


def _pool_kernel(p_ref, pw_ref, ps_ref, o_ref, prev_s, *, tiles_per_seq):
    i = pl.program_id(0)
    tl, d = p_ref.shape
    gd = d // len(POOL_WINDOWS)
    it = i % tiles_per_seq

    @pl.when(it == 0)
    def _():
        prev_s[...] = jnp.zeros_like(prev_s)

    row = lax.broadcasted_iota(jnp.int32, (tl, 2 * tl), 0)
    col = lax.broadcasted_iota(jnp.int32, (tl, 2 * tl), 1)
    pos = (it * tl + lax.broadcasted_iota(jnp.int32, (tl, 1), 0) + 1).astype(F32)
    for g, w in enumerate(POOL_WINDOWS):
        sl = slice(g * gd, (g + 1) * gd)
        cur = p_ref[:, sl]
        ext = jnp.concatenate([prev_s[:, sl], cur], axis=0)
        band = ((col <= row + tl) & (col > row + tl - w)).astype(BF16)
        win_sum = jnp.dot(band, ext, preferred_element_type=F32)
        mean = win_sum / jnp.minimum(pos, float(w))
        dlt = (mean - cur.astype(F32)).astype(BF16)
        mixed = jnp.dot(dlt, pw_ref[g], preferred_element_type=F32)
        o_ref[:, sl] = (mixed * ps_ref[:, sl]).astype(o_ref.dtype)
    prev_s[...] = p_ref[...]


def _pool(proj, p_blk, pool_w_bf, pool_scale, seq, d):
    t = proj.shape[0]
    tl = SEQ_TILE
    g, gd = pool_w_bf.shape[0], pool_w_bf.shape[1]
    return pl.pallas_call(
        functools.partial(_pool_kernel, tiles_per_seq=seq // tl),
        grid=(t // tl,),
        in_specs=[pl.BlockSpec((tl, d), lambda i: (i, p_blk)),
                  pl.BlockSpec((g, gd, gd), lambda i: (0, 0, 0)),
                  pl.BlockSpec((1, d), lambda i: (0, 0))],
        out_specs=pl.BlockSpec((tl, d), lambda i: (i, 0)),
        out_shape=jax.ShapeDtypeStruct((t, d), BF16),
        scratch_shapes=[pltpu.VMEM((tl, d), BF16)],
        compiler_params=_params("arbitrary"),
        name="pool",
    )(proj, pool_w_bf, pool_scale.reshape(1, d))


def _ssd_kernel(z_ref, xs_ref, bm_ref, cm_ref, dt_ref,
                cwx_ref, cwb_ref, cwc_ref, cbx_ref, cbb_ref, cbc_ref,
                dtb_ref, alog_ref, dsk_ref, nw_ref, exp_ref,
                o_ref,
                extx_s, extb_s, extc_s, state_s, xdt_s, y_s, cb_s, cs_s, cst_s, xc_s, bc_s, cc_s,
                *, tiles_per_seq, heads):
    i = pl.program_id(0)
    tl, inner = xs_ref.shape
    gw = inner // SSD_GROUPS
    pairs_per_group = gw // LANES
    n_pairs = inner // LANES

    @pl.when(i % tiles_per_seq == 0)
    def _():
        extx_s[0:CONV_HALO, :] = jnp.zeros((CONV_HALO, inner), F32)
        extb_s[0:CONV_HALO, :] = jnp.zeros((CONV_HALO, extb_s.shape[1]), F32)
        extc_s[0:CONV_HALO, :] = jnp.zeros((CONV_HALO, extc_s.shape[1]), F32)
        state_s[...] = jnp.zeros_like(state_s)

    def conv_silu(ext_ref, src_ref, w_ref, b_ref, dst_ref, width, cw):
        for c0 in range(0, width, cw):
            cs = slice(c0, c0 + cw)
            ext_ref[CONV_HALO:CONV_HALO + tl, cs] = src_ref[:, cs].astype(F32)
            acc = b_ref[:, cs] + w_ref[SSD_CONV - 1:SSD_CONV, cs] * ext_ref[CONV_HALO:CONV_HALO + tl, cs]
            for k in range(1, SSD_CONV):
                acc = acc + w_ref[SSD_CONV - 1 - k:SSD_CONV - k, cs] * ext_ref[pl.ds(CONV_HALO - k, tl), cs]
            dst_ref[:, cs] = _silu(acc)
            ext_ref[0:CONV_HALO, cs] = ext_ref[tl:tl + CONV_HALO, cs]

    conv_silu(extx_s, xs_ref, cwx_ref, cbx_ref, xc_s, inner, LANES)
    conv_silu(extb_s, bm_ref, cwb_ref, cbb_ref, bc_s, bm_ref.shape[1], LANES)
    conv_silu(extc_s, cm_ref, cwc_ref, cbc_ref, cc_s, cm_ref.shape[1], LANES)

    dtv = dt_ref[...] + dtb_ref[...]
    dt = jnp.maximum(dtv, 0.0) + jnp.log1p(jnp.exp(-jnp.abs(dtv)))
    a = -jnp.exp(alog_ref[...])
    da = dt * a
    ri = lax.broadcasted_iota(jnp.int32, (tl, tl), 0)
    ci = lax.broadcasted_iota(jnp.int32, (tl, tl), 1)
    causal = ri >= ci
    cs = jnp.dot(causal.astype(F32), da, precision=HIGHEST, preferred_element_type=F32)
    cs2 = cs * LOG2_E
    cs_s[...] = cs2
    cst_s[...] = cs2.T
    last = cs[tl - 1:tl, :]
    exp_m = exp_ref[...]
    dt_x = jnp.dot(dt.astype(BF16), exp_m, preferred_element_type=F32)
    ecs_x = jnp.dot(jnp.exp(cs).astype(BF16), exp_m, preferred_element_type=F32)
    dte_x = jnp.dot(jnp.exp(last - cs).astype(BF16), exp_m, preferred_element_type=F32)

    for g in range(SSD_GROUPS):
        gs = slice(g * gw, (g + 1) * gw)
        ns = slice(g * SSD_STATE, (g + 1) * SSD_STATE)
        xg = xc_s[:, gs]
        xdt = xg * dt_x[:, gs]
        xdt_bf = xdt.astype(BF16)
        xd_bf = (xdt * dte_x[:, gs]).astype(BF16)
        bg = bc_s[:, ns]
        cg = cc_s[:, ns].astype(BF16)
        cb_s[g] = lax.dot_general(cg, bg.astype(BF16), (((1,), (1,)), ((), ())), preferred_element_type=F32)
        s_old = state_s[g]
        y_off = jnp.dot(cg, s_old.astype(BF16), preferred_element_type=F32) * ecs_x[:, gs]
        state_s[g] = (s_old * ecs_x[tl - 1:tl, gs]
                      + jnp.dot(bg.T.astype(BF16), xd_bf, preferred_element_type=F32))
        y0 = y_off + dsk_ref[:, gs] * xg
        for q in range(pairs_per_group):
            qs = slice(q * LANES, (q + 1) * LANES)
            xdt_s[g * pairs_per_group + q] = xdt_bf[:, qs]
            y_s[g * pairs_per_group + q] = y0[:, qs]

    lane_h = lax.broadcasted_iota(jnp.int32, (tl, LANES), 1)
    low_half = lane_h < SSD_HEAD_DIM

    half = tl // 2
    causal_top = (lax.broadcasted_iota(jnp.int32, (half, half), 0)
                  >= lax.broadcasted_iota(jnp.int32, (half, half), 1))
    causal_bot = (lax.broadcasted_iota(jnp.int32, (half, tl), 0) + half
                  >= lax.broadcasted_iota(jnp.int32, (half, tl), 1))
    lane_hh = lax.broadcasted_iota(jnp.int32, (half, LANES), 1)
    low_hh = lane_hh < SSD_HEAD_DIM

    def pair_body(hp, carry):
        g = hp // pairs_per_group
        cb_top = cb_s[g, 0:half, 0:half]
        cb_bot = cb_s[g, half:tl, :]
        x_top = xdt_s[hp, 0:half, :]
        x_all = xdt_s[hp]
        cs_top = cs_s[0:half, :]
        cs_bot = cs_s[half:tl, :]
        tops, bots = [], []
        for e in range(2):
            h = 2 * hp + e
            head_lane = jnp.full((half, LANES), h, jnp.int32)
            col_top = jnp.take_along_axis(cs_top, head_lane, axis=1)
            col_bot = jnp.take_along_axis(cs_bot, head_lane, axis=1)
            col_bot = jnp.concatenate([col_bot] * (tl // LANES), axis=1)
            row_all = cst_s[pl.ds(h, 1), :]
            row_top = row_all[:, 0:half]
            m_top = jnp.exp2(jnp.where(causal_top, col_top - row_top, -jnp.inf)) * cb_top
            m_bot = jnp.exp2(jnp.where(causal_bot, col_bot - row_all, -jnp.inf)) * cb_bot
            tops.append(jnp.dot(m_top.astype(BF16), x_top, preferred_element_type=F32))
            bots.append(jnp.dot(m_bot.astype(BF16), x_all, preferred_element_type=F32))
        y_s[hp, 0:half, :] = y_s[hp, 0:half, :] + jnp.where(low_hh, tops[0], tops[1])
        y_s[hp, half:tl, :] = y_s[hp, half:tl, :] + jnp.where(low_hh, bots[0], bots[1])
        return carry

    lax.fori_loop(0, n_pairs, pair_body, 0, unroll=16)

    for g in range(SSD_GROUPS):
        gs = slice(g * gw, (g + 1) * gw)
        yg = jnp.concatenate([y_s[g * pairs_per_group + q] for q in range(pairs_per_group)], axis=1)
        yg = yg * _silu(z_ref[:, gs].astype(F32))
        ms = jnp.mean(yg * yg, axis=-1, keepdims=True)
        o_ref[:, gs] = (yg * lax.rsqrt(ms + EPS) * nw_ref[:, gs]).astype(o_ref.dtype)


def _ssd(proj, dt_raw, blk, conv_w, conv_b, dt_bias, a_log, d_skip, ssd_norm_w, seq):
    t = proj.shape[0]
    tl = SEQ_TILE
    heads = dt_bias.shape[0]
    inner = ssd_norm_w.shape[0]
    bc = SSD_GROUPS * SSD_STATE
    gw = inner // SSD_GROUPS
    pad = LANES - heads
    row = lambda v: v.astype(F32).reshape(1, -1)
    dtb = jnp.pad(row(dt_bias), ((0, 0), (0, pad)))
    alog = jnp.pad(row(a_log), ((0, 0), (0, pad)))
    dsk = jnp.repeat(d_skip.astype(F32), SSD_HEAD_DIM).reshape(1, inner)
    expand = (lax.broadcasted_iota(jnp.int32, (LANES, inner), 1) // SSD_HEAD_DIM
              == lax.broadcasted_iota(jnp.int32, (LANES, inner), 0)).astype(BF16)
    cw = conv_w.astype(F32)
    cbias = row(conv_b)
    full = lambda shape: pl.BlockSpec(shape, lambda i: tuple(0 for _ in shape))
    return pl.pallas_call(
        functools.partial(_ssd_kernel, tiles_per_seq=seq // tl, heads=heads),
        grid=(t // tl,),
        in_specs=[pl.BlockSpec((tl, inner), lambda i: (i, blk["z"])),
                  pl.BlockSpec((tl, inner), lambda i: (i, blk["xs"])),
                  pl.BlockSpec((tl, bc), lambda i: (i, blk["B"])),
                  pl.BlockSpec((tl, bc), lambda i: (i, blk["C"])),
                  pl.BlockSpec((tl, LANES), lambda i: (i, 0)),
                  full((SSD_CONV, inner)), full((SSD_CONV, bc)), full((SSD_CONV, bc)),
                  full((1, inner)), full((1, bc)), full((1, bc)),
                  full((1, LANES)), full((1, LANES)), full((1, inner)), full((1, inner)),
                  full((LANES, inner))],
        out_specs=pl.BlockSpec((tl, inner), lambda i: (i, 0)),
        out_shape=jax.ShapeDtypeStruct((t, inner), BF16),
        scratch_shapes=[pltpu.VMEM((tl + CONV_HALO, inner), F32),
                        pltpu.VMEM((tl + CONV_HALO, bc), F32),
                        pltpu.VMEM((tl + CONV_HALO, bc), F32),
                        pltpu.VMEM((SSD_GROUPS, SSD_STATE, gw), F32),
                        pltpu.VMEM((inner // LANES, tl, LANES), BF16),
                        pltpu.VMEM((inner // LANES, tl, LANES), F32),
                        pltpu.VMEM((SSD_GROUPS, tl, tl), F32),
                        pltpu.VMEM((tl, LANES), F32),
                        pltpu.VMEM((LANES, tl), F32),
                        pltpu.VMEM((tl, inner), F32),
                        pltpu.VMEM((tl, bc), F32),
                        pltpu.VMEM((tl, bc), F32)],
        compiler_params=_params("arbitrary"),
        name="ssd",
    )(proj, proj, proj, proj, dt_raw,
      cw[:, :inner], cw[:, inner:inner + bc], cw[:, inner + bc:],
      cbias[:, :inner], cbias[:, inner:inner + bc], cbias[:, inner + bc:],
      dtb, alog, dsk, row(ssd_norm_w), expand)


def _merge_kernel(yp_ref, ys_ref, gp_ref, gs_ref, wp_ref, ws_ref, o_ref):
    a = jnp.dot(yp_ref[...], wp_ref[...], preferred_element_type=F32)
    b = jnp.dot(ys_ref[...], ws_ref[...], preferred_element_type=F32)
    o_ref[...] = (jax.nn.sigmoid(gp_ref[...].astype(F32)) * a
                  + jax.nn.sigmoid(gs_ref[...].astype(F32)) * b).astype(o_ref.dtype)


def _merge(ypool, yssd, proj, gp_off, gs_off, wbp, wbs):
    t, d = ypool.shape
    inner = yssd.shape[1]
    tm, tn = min(MERGE_TM, t), MERGE_TN
    return pl.pallas_call(
        _merge_kernel,
        grid=(t // tm, d // tn),
        in_specs=[pl.BlockSpec((tm, d), lambda i, j: (i, 0)),
                  pl.BlockSpec((tm, inner), lambda i, j: (i, 0)),
                  pl.BlockSpec((tm, tn), lambda i, j: (i, gp_off // tn + j)),
                  pl.BlockSpec((tm, tn), lambda i, j: (i, gs_off // tn + j)),
                  pl.BlockSpec((d, tn), lambda i, j: (0, j)),
                  pl.BlockSpec((inner, tn), lambda i, j: (0, j))],
        out_specs=pl.BlockSpec((tm, tn), lambda i, j: (i, j)),
        out_shape=jax.ShapeDtypeStruct((t, d), BF16),
        compiler_params=_params("arbitrary", "arbitrary"),
        name="merge",
    )(ypool, yssd, proj, proj, wbp, wbs)


def _out_kernel(m_ref, wo_ref, x_ref, mod_ref, wr_ref, br_ref,
                h_ref, u_ref, idx_ref, gate_ref, rank_ref, cnt_ref, carry_s):
    i = pl.program_id(0)
    tm = m_ref.shape[0]
    ne = wr_ref.shape[0]

    @pl.when(i == 0)
    def _():
        carry_s[...] = jnp.zeros_like(carry_s)

    mix = jnp.dot(m_ref[...], wo_ref[...], preferred_element_type=F32)
    h1 = x_ref[...] + mod_ref[0, 2:3, :] * mix
    h_ref[...] = h1
    ms = jnp.mean(h1 * h1, axis=-1, keepdims=True)
    u = h1 * lax.rsqrt(ms + EPS) * (1.0 + mod_ref[0, 4:5, :]) + mod_ref[0, 3:4, :]
    u_ref[...] = _pack_bf16_pairs(u)
    logits = lax.dot_general(wr_ref[...], u, (((1,), (1,)), ((), ())), precision=HIGHEST,
                             preferred_element_type=F32) + br_ref[...]
    eidx = lax.broadcasted_iota(jnp.int32, (ne, tm), 0)
    work = logits
    vals, idxs, hots = [], [], []
    for _ in range(TOP_K):
        mx = jnp.max(work, axis=0, keepdims=True)
        sel = jnp.min(jnp.where(work == mx, eidx, ne), axis=0, keepdims=True)
        hot = eidx == sel
        vals.append(mx)
        idxs.append(sel)
        hots.append(hot)
        work = jnp.where(hot, -jnp.inf, work)
    exps = [jnp.exp(v - vals[0]) for v in vals]
    den = exps[0]
    for e in exps[1:]:
        den = den + e
    cnt = hots[0].astype(F32)
    for hot in hots[1:]:
        cnt = cnt + hot.astype(F32)
    ti = lax.broadcasted_iota(jnp.int32, (tm, tm), 0)
    tj = lax.broadcasted_iota(jnp.int32, (tm, tm), 1)
    before = (ti < tj).astype(BF16)
    prefix = jnp.dot(cnt.astype(BF16), before, preferred_element_type=F32)
    base = carry_s[:, 0:1] + prefix
    pad_rows = idx_ref.shape[0] - TOP_K
    ranks = [jnp.sum(jnp.where(hot, base, 0.0), axis=0, keepdims=True) for hot in hots]
    idx_ref[...] = jnp.concatenate(idxs + [jnp.zeros((pad_rows, tm), jnp.int32)], axis=0)
    gate_ref[...] = jnp.concatenate([e / den for e in exps] + [jnp.zeros((pad_rows, tm), F32)], axis=0)
    rank_ref[...] = jnp.concatenate([r.astype(jnp.int32) for r in ranks]
                                    + [jnp.zeros((pad_rows, tm), jnp.int32)], axis=0)
    carry_s[...] = carry_s[...] + jnp.sum(cnt, axis=1, keepdims=True)
    cnt_ref[...] = carry_s[...].astype(jnp.int32)


def _outproj(merged, wo, x2, mod3, w_router, b_router, seq):
    t, d = x2.shape
    ne = w_router.shape[1]
    tm = min(OUT_TM, seq)
    tiles_per_seq = seq // tm
    rows = 8
    return pl.pallas_call(
        _out_kernel,
        grid=(t // tm,),
        in_specs=[pl.BlockSpec((tm, d), lambda i: (i, 0)),
                  pl.BlockSpec((d, d), lambda i: (0, 0), pipeline_mode=pl.Buffered(1)),
                  pl.BlockSpec((tm, d), lambda i: (i, 0)),
                  pl.BlockSpec((1, N_MOD, d), lambda i: (i // tiles_per_seq, 0, 0)),
                  pl.BlockSpec((ne, d), lambda i: (0, 0)),
                  pl.BlockSpec((ne, 1), lambda i: (0, 0))],
        out_specs=[pl.BlockSpec((tm, d), lambda i: (i, 0)),
                   pl.BlockSpec((tm, d // 2), lambda i: (i, 0)),
                   pl.BlockSpec((rows, tm), lambda i: (0, i)),
                   pl.BlockSpec((rows, tm), lambda i: (0, i)),
                   pl.BlockSpec((rows, tm), lambda i: (0, i)),
                   pl.BlockSpec((ne, LANES), lambda i: (0, 0))],
        out_shape=[jax.ShapeDtypeStruct((t, d), F32), jax.ShapeDtypeStruct((t, d // 2), jnp.uint32),
                   jax.ShapeDtypeStruct((rows, t), jnp.int32), jax.ShapeDtypeStruct((rows, t), F32),
                   jax.ShapeDtypeStruct((rows, t), jnp.int32), jax.ShapeDtypeStruct((ne, LANES), jnp.int32)],
        scratch_shapes=[pltpu.VMEM((ne, LANES), F32)],
        compiler_params=_params("arbitrary"),
        name="outproj",
    )(merged, wo, x2, mod3, w_router.astype(F32).T, b_router.astype(F32).reshape(ne, 1))


def _row_copy(src_hbm, src_row, dst_ref, dst_row, sem):
    return pltpu.make_async_copy(src_hbm.at[pl.ds(src_row, 1)], dst_ref.at[pl.ds(dst_row, 1)], sem)


def _dispatch_kernel(slot_ref, u_ref, init_hbm, xs_hbm, sem):
    del init_hbm
    ts = u_ref.shape[0]

    def issue(tok, carry):
        for k in range(TOP_K):
            _row_copy(u_ref, tok, xs_hbm, slot_ref[0, 0, tok * TOP_K + k], sem).start(priority=k % 2)
        return carry

    lax.fori_loop(0, ts, issue, 0)
    rows = pl.ds(0, ts * TOP_K)
    pltpu.make_async_copy(xs_hbm.at[rows], xs_hbm.at[rows], sem).wait()


def _dispatch(u2, slot_tk, n_slots):
    t, d = u2.shape
    ts = min(DISPATCH_TS, t)
    slots3 = slot_tk.reshape(t // ts, 1, ts * TOP_K)
    init = jnp.zeros((n_slots, d), u2.dtype)
    return pl.pallas_call(
        _dispatch_kernel,
        grid=(t // ts,),
        in_specs=[pl.BlockSpec((1, 1, ts * TOP_K), lambda i: (i, 0, 0), memory_space=pltpu.SMEM),
                  pl.BlockSpec((ts, d), lambda i: (i, 0)),
                  pl.BlockSpec(memory_space=pl.ANY)],
        out_specs=pl.BlockSpec(memory_space=pl.ANY),
        out_shape=jax.ShapeDtypeStruct((n_slots, d), u2.dtype),
        scratch_shapes=[pltpu.SemaphoreType.DMA(())],
        input_output_aliases={2: 0},
        compiler_params=_params("arbitrary"),
        name="dispatch",
    )(slots3, u2, init)


def _moe_kernel(be_ref, na_ref, x_ref, wg_ref, wu_ref, bg_ref, bu_ref, wd_ref, bd_ref, o_ref, xb_s):
    del be_ref
    b, j = pl.program_id(0), pl.program_id(1)
    half = x_ref.shape[1]

    @pl.when((b < na_ref[0]) & (j == 0))
    def _():
        lo, hi = _unpack_bf16_pairs(x_ref[...])
        xb_s[:, :half] = lo
        xb_s[:, half:] = hi

    @pl.when(b < na_ref[0])
    def _():
        xb = xb_s[...]
        gate = jnp.dot(xb, wg_ref[...], preferred_element_type=F32) + bg_ref[...]
        up = jnp.dot(xb, wu_ref[...], preferred_element_type=F32) + bu_ref[...]
        gate = jnp.minimum(gate, SWIGLU_LIMIT)
        up = jnp.clip(up, -SWIGLU_LIMIT, SWIGLU_LIMIT)
        act = (up + 1.0) * gate * jax.nn.sigmoid(SWIGLU_ALPHA * gate)
        part = jnp.dot(act.astype(BF16), wd_ref[...], preferred_element_type=F32)

        @pl.when(j == 0)
        def _():
            o_ref[...] = part + bd_ref[...]

        @pl.when(j > 0)
        def _():
            o_ref[...] = o_ref[...] + part

    @pl.when((b >= na_ref[0]) & (j == 0))
    def _():
        o_ref[...] = jnp.zeros_like(o_ref)


def _moe(xs, block_expert, n_active, wgu, bgu, wd, bd):
    n_slots = xs.shape[0]
    ne, d, f2 = wgu.shape
    f = f2 // 2
    bm, tf = MOE_BM, min(MOE_TF, f)
    nb, ft = n_slots // bm, f // tf

    def blk(b, na):
        return jnp.maximum(jnp.minimum(b, na[0] - 1), 0)

    def jj(b, j, na):
        return jnp.where(b < na[0], j, ft - 1)

    grid_spec = pltpu.PrefetchScalarGridSpec(
        num_scalar_prefetch=2,
        grid=(nb, ft),
        in_specs=[pl.BlockSpec((bm, d // 2), lambda b, j, be, na: (blk(b, na), 0)),
                  pl.BlockSpec((None, d, tf), lambda b, j, be, na: (be[b], 0, jj(b, j, na))),
                  pl.BlockSpec((None, d, tf), lambda b, j, be, na: (be[b], 0, ft + jj(b, j, na))),
                  pl.BlockSpec((None, 1, tf), lambda b, j, be, na: (be[b], 0, jj(b, j, na))),
                  pl.BlockSpec((None, 1, tf), lambda b, j, be, na: (be[b], 0, ft + jj(b, j, na))),
                  pl.BlockSpec((None, tf, d), lambda b, j, be, na: (be[b], jj(b, j, na), 0)),
                  pl.BlockSpec((None, 1, d), lambda b, j, be, na: (be[b], 0, 0))],
        out_specs=pl.BlockSpec((bm, d), lambda b, j, be, na: (b, 0)),
        scratch_shapes=[pltpu.VMEM((bm, d), BF16)],
    )
    return pl.pallas_call(
        _moe_kernel,
        grid_spec=grid_spec,
        out_shape=jax.ShapeDtypeStruct((n_slots, d), F32),
        compiler_params=_params("arbitrary", "arbitrary"),
        name="moe",
    )(block_expert, n_active, xs, wgu, wgu, bgu, bgu, wd, bd)


def _final_kernel(slot_ref, h_ref, gate_ref, mod_ref, fw_ref, y_hbm, o_ref, ybuf, sem):
    tc = h_ref.shape[0]

    def issue(grp, carry):
        for q in range(FINAL_ISSUE_UNROLL):
            tk = FINAL_ISSUE_UNROLL * grp + q
            _row_copy(y_hbm, slot_ref[0, 0, tk], ybuf, tk, sem).start(priority=q % 2)
        return carry

    lax.fori_loop(0, tc * TOP_K // FINAL_ISSUE_UNROLL, issue, 0)
    pltpu.make_async_copy(y_hbm.at[pl.ds(0, tc * TOP_K)], ybuf, sem).wait()

    ffn = gate_ref[:, 0:1] * ybuf[0:tc, :]
    for k in range(1, TOP_K):
        ffn = ffn + gate_ref[:, k:k + 1] * ybuf[k * tc:(k + 1) * tc, :]
    h2 = h_ref[...] + mod_ref[0, 5:6, :] * ffn
    ms = jnp.mean(h2 * h2, axis=-1, keepdims=True)
    o_ref[...] = (h2 * lax.rsqrt(ms + EPS) * fw_ref[...]).astype(o_ref.dtype)


def _final(h1, gates_tk, slot_kt, mod3, final_norm_w, y_slots, seq, out_dtype):
    t, d = h1.shape
    tc = min(FINAL_TC, seq)
    tiles_per_seq = seq // tc
    slots3 = slot_kt.reshape(TOP_K, t // tc, tc).transpose(1, 0, 2).reshape(t // tc, 1, TOP_K * tc)
    return pl.pallas_call(
        _final_kernel,
        grid=(t // tc,),
        in_specs=[pl.BlockSpec((1, 1, TOP_K * tc), lambda i: (i, 0, 0), memory_space=pltpu.SMEM),
                  pl.BlockSpec((tc, d), lambda i: (i, 0)),
                  pl.BlockSpec((tc, gates_tk.shape[1]), lambda i: (i, 0)),
                  pl.BlockSpec((1, N_MOD, d), lambda i: (i // tiles_per_seq, 0, 0)),
                  pl.BlockSpec((1, d), lambda i: (0, 0)),
                  pl.BlockSpec(memory_space=pl.ANY)],
        out_specs=pl.BlockSpec((tc, d), lambda i: (i, 0)),
        out_shape=jax.ShapeDtypeStruct((t, d), out_dtype),
        scratch_shapes=[pltpu.VMEM((TOP_K * tc, d), F32), pltpu.SemaphoreType.DMA(())],
        compiler_params=_params("arbitrary"),
        name="final",
    )(slots3, h1, gates_tk, mod3, final_norm_w.astype(F32).reshape(1, d), y_slots)


def kernel(x, c, w_ada, b_ada, w_in, pool_w, pool_scale, conv_w, conv_b, dt_bias, a_log, d_skip, ssd_norm_w,
           w_branch_pool, w_branch_ssd, w_out, w_router, b_router, w_gate_up, b_gate_up, w_down, b_down,
           final_norm_w):
    bsz, seq, d = x.shape
    depth = w_ada.shape[0]
    t = bsz * seq
    inner = ssd_norm_w.shape[1]
    heads = dt_bias.shape[1]
    bc = SSD_GROUPS * SSD_STATE
    ne = w_router.shape[2]
    assert depth == 1, "the final RMSNorm is fused into the last kernel of the single layer"
    assert SEQ_TILE == 2 * LANES
    assert seq % SEQ_TILE == 0 and heads <= LANES and (7 * d) % bc == 0 and (4 * d) % inner == 0
    assert inner // SSD_GROUPS % LANES == 0 and d % (len(POOL_WINDOWS) * LANES) == 0

    s1 = d
    s2 = s1 + inner
    s3 = s2 + inner + 2 * bc
    s4 = s3 + heads
    s5 = s4 + d
    blk = {"z": 0, "xs": 1, "p": (2 * inner) // d, "B": (2 * inner + 3 * d) // bc, "C": (2 * inner + 3 * d) // bc + 1}
    gp_off, gs_off = 2 * inner + d, 2 * inner + 2 * d

    h = x.astype(F32).reshape(t, d)
    for layer in range(depth):
        wl = w_in[layer]
        w_main = jnp.concatenate([wl[:, s1:s2], wl[:, s2:s2 + inner], wl[:, :s1], wl[:, s4:s5], wl[:, s5:],
                                  wl[:, s2 + inner:s2 + inner + bc], wl[:, s2 + inner + bc:s3]], axis=1).astype(BF16)
        w_dt = jnp.pad(wl[:, s3:s4].astype(F32), ((0, 0), (0, LANES - heads)))

        mod3 = _ada(c, w_ada[layer], b_ada[layer]).reshape(bsz, N_MOD, d)
        proj, dt_raw = _inproj(h, mod3, w_main, w_dt, seq)
        ypool = _pool(proj, blk["p"], pool_w[layer].astype(BF16), pool_scale[layer], seq, d)
        yssd = _ssd(proj, dt_raw, blk, conv_w[layer], conv_b[layer], dt_bias[layer], a_log[layer], d_skip[layer],
                    ssd_norm_w[layer], seq)
        merged = _merge(ypool, yssd, proj, gp_off, gs_off, w_branch_pool[layer].astype(BF16),
                        w_branch_ssd[layer].astype(BF16))
        h1, u2, idx_kt, gate_kt, rank_kt, counts = _outproj(merged, w_out[layer].astype(BF16), h, mod3,
                                                            w_router[layer], b_router[layer], seq)

        counts = counts[:, 0]
        padded = (counts + MOE_BM - 1) // MOE_BM * MOE_BM
        pend = jnp.cumsum(padded)
        pstart = pend - padded
        n_blocks = (t * TOP_K) // MOE_BM + ne
        n_active = (pend[-1] // MOE_BM).astype(jnp.int32).reshape(1)
        blocks = jnp.minimum(jnp.arange(n_blocks, dtype=jnp.int32), n_active[0] - 1)
        block_expert = jnp.minimum(jnp.sum(pend[None, :] <= (blocks * MOE_BM)[:, None], axis=1),
                                   ne - 1).astype(jnp.int32)
        eids = jnp.arange(ne, dtype=jnp.int32)[:, None, None]
        slot_kt = (jnp.sum(jnp.where(idx_kt[:TOP_K][None] == eids, pstart[:, None, None], 0), axis=0)
                   + rank_kt[:TOP_K]).astype(jnp.int32)

        xs = _dispatch(u2, slot_kt.T, n_blocks * MOE_BM)
        y_slots = _moe(xs, block_expert, n_active, w_gate_up[layer].astype(BF16),
                       b_gate_up[layer].astype(F32)[:, None, :], w_down[layer].astype(BF16),
                       b_down[layer].astype(F32)[:, None, :])
        h = _final(h1, gate_kt.T, slot_kt, mod3, final_norm_w, y_slots, seq, x.dtype)
    return h.reshape(bsz, seq, d)
```

```python
import functools

import jax
import jax.numpy as jnp
from jax import lax
from jax.experimental import pallas as pl
from jax.experimental.pallas import tpu as pltpu

F32 = jnp.float32
BF16 = jnp.bfloat16
HIGHEST = lax.Precision.HIGHEST

EPS = 1e-6
POOL_WINDOWS = (2, 4, 8, 16)
SSD_GROUPS = 8
SSD_STATE = 128
SSD_CONV = 4
SSD_HEAD_DIM = 64
TOP_K = 4
SWIGLU_LIMIT = 7.0
SWIGLU_ALPHA = 1.702
N_MOD = 6

LOG2_E = 1.4426950408889634
LANES = 128
CONV_HALO = 8
V7X_VMEM_LIMIT = 56 * 1024 * 1024

SEQ_TILE = 256
INPROJ_TM, INPROJ_TN = 1024, 1024
MERGE_TM, MERGE_TN = 512, 512
OUT_TM = 512
MOE_BM, MOE_TF = 512, 512
MOE_CHUNK_BLOCKS = 2
DISPATCH_TS = 256
FINAL_TC = 256
FINAL_ISSUE_UNROLL = 8


def _params(*sem):
    return pltpu.CompilerParams(dimension_semantics=sem, vmem_limit_bytes=V7X_VMEM_LIMIT)


def _silu(v):
    return v * jax.nn.sigmoid(v)


def _pack_bf16_pairs(v):
    half = v.shape[1] // 2
    lo = lax.bitcast_convert_type(v[:, :half].astype(BF16).astype(F32), jnp.uint32)
    hi = lax.bitcast_convert_type(v[:, half:].astype(BF16).astype(F32), jnp.uint32)
    return (hi & jnp.uint32(0xFFFF0000)) | (lo >> 16)


def _unpack_bf16_pairs(w):
    lo = lax.bitcast_convert_type(w << 16, F32).astype(BF16)
    hi = lax.bitcast_convert_type(w & jnp.uint32(0xFFFF0000), F32).astype(BF16)
    return lo, hi


def _ada_kernel(cb_ref, w_ref, b_ref, o_ref):
    nb, tn = cb_ref.shape[0], w_ref.shape[1]
    for b in range(nb):
        ca = _silu(cb_ref[b])
        cols = [jnp.sum(w_ref[:, j * LANES:(j + 1) * LANES] * ca, axis=0, keepdims=True)
                for j in range(tn // LANES)]
        o_ref[b:b + 1, :] = jnp.concatenate(cols, axis=1) + b_ref[...]


def _ada(c, w_ada, b_ada):
    nb, k = c.shape
    n = w_ada.shape[1]
    tn = 1024 if n % 1024 == 0 else n
    cb = jnp.broadcast_to(c.astype(F32)[:, :, None], (nb, k, LANES))
    return pl.pallas_call(
        _ada_kernel,
        grid=(n // tn,),
        in_specs=[pl.BlockSpec((nb, k, LANES), lambda j: (0, 0, 0)),
                  pl.BlockSpec((k, tn), lambda j: (0, j)),
                  pl.BlockSpec((1, tn), lambda j: (0, j))],
        out_specs=pl.BlockSpec((nb, tn), lambda j: (0, j)),
        out_shape=jax.ShapeDtypeStruct((nb, n), F32),
        compiler_params=_params("arbitrary"),
        name="ada",
    )(cb, w_ada, b_ada.reshape(1, n))


def _inproj_kernel(x_ref, mod_ref, w_ref, wdt_ref, o_ref, dt_ref, u_s, *, rows_per_chunk):
    j = pl.program_id(1)
    tm = x_ref.shape[0]

    @pl.when(j == 0)
    def _():
        sh = mod_ref[0, 0:1, :]
        sc = mod_ref[0, 1:2, :]

        def body(r, carry):
            rows = pl.ds(pl.multiple_of(r * rows_per_chunk, rows_per_chunk), rows_per_chunk)
            xv = x_ref[rows, :]
            ms = jnp.mean(xv * xv, axis=-1, keepdims=True)
            u = xv * lax.rsqrt(ms + EPS) * (1.0 + sc) + sh
            u_s[rows, :] = u.astype(BF16)
            dt_ref[rows, :] = jnp.dot(u, wdt_ref[...], precision=HIGHEST, preferred_element_type=F32)
            return carry

        lax.fori_loop(0, tm // rows_per_chunk, body, 0)

    o_ref[...] = jnp.dot(u_s[...], w_ref[...], preferred_element_type=F32).astype(o_ref.dtype)


def _inproj(x2, mod3, w_main, w_dt, seq):
    t, d = x2.shape
    n = w_main.shape[1]
    tm = min(INPROJ_TM, seq)
    tn = INPROJ_TN
    tiles_per_seq = seq // tm
    return pl.pallas_call(
        functools.partial(_inproj_kernel, rows_per_chunk=min(128, tm)),
        grid=(t // tm, n // tn),
        in_specs=[pl.BlockSpec((tm, d), lambda i, j: (i, 0)),
                  pl.BlockSpec((1, N_MOD, d), lambda i, j: (i // tiles_per_seq, 0, 0)),
                  pl.BlockSpec((d, tn), lambda i, j: (0, j)),
                  pl.BlockSpec((d, LANES), lambda i, j: (0, 0))],
        out_specs=[pl.BlockSpec((tm, tn), lambda i, j: (i, j)),
                   pl.BlockSpec((tm, LANES), lambda i, j: (i, 0))],
        out_shape=[jax.ShapeDtypeStruct((t, n), BF16), jax.ShapeDtypeStruct((t, LANES), F32)],
        scratch_shapes=[pltpu.VMEM((tm, d), BF16)],
        compiler_params=_params("arbitrary", "arbitrary"),
        name="inproj",
    )(x2, mod3, w_main, w_dt)


def _pool_kernel(p_ref, pw_ref, ps_ref, o_ref, prev_s, *, tiles_per_seq):
    i = pl.program_id(0)
    tl, d = p_ref.shape
    gd = d // len(POOL_WINDOWS)
    it = i % tiles_per_seq

    @pl.when(it == 0)
    def _():
        prev_s[...] = jnp.zeros_like(prev_s)

    row = lax.broadcasted_iota(jnp.int32, (tl, 2 * tl), 0)
    col = lax.broadcasted_iota(jnp.int32, (tl, 2 * tl), 1)
    pos = (it * tl + lax.broadcasted_iota(jnp.int32, (tl, 1), 0) + 1).astype(F32)
    for g, w in enumerate(POOL_WINDOWS):
        sl = slice(g * gd, (g + 1) * gd)
        cur = p_ref[:, sl]
        ext = jnp.concatenate([prev_s[:, sl], cur], axis=0)
        band = ((col <= row + tl) & (col > row + tl - w)).astype(BF16)
        win_sum = jnp.dot(band, ext, preferred_element_type=F32)
        mean = win_sum / jnp.minimum(pos, float(w))
        dlt = (mean - cur.astype(F32)).astype(BF16)
        mixed = jnp.dot(dlt, pw_ref[g], preferred_element_type=F32)
        o_ref[:, sl] = (mixed * ps_ref[:, sl]).astype(o_ref.dtype)
    prev_s[...] = p_ref[...]


def _pool(proj, p_blk, pool_w_bf, pool_scale, seq, d):
    t = proj.shape[0]
    tl = SEQ_TILE
    g, gd = pool_w_bf.shape[0], pool_w_bf.shape[1]
    return pl.pallas_call(
        functools.partial(_pool_kernel, tiles_per_seq=seq // tl),
        grid=(t // tl,),
        in_specs=[pl.BlockSpec((tl, d), lambda i: (i, p_blk)),
                  pl.BlockSpec((g, gd, gd), lambda i: (0, 0, 0)),
                  pl.BlockSpec((1, d), lambda i: (0, 0))],
        out_specs=pl.BlockSpec((tl, d), lambda i: (i, 0)),
        out_shape=jax.ShapeDtypeStruct((t, d), BF16),
        scratch_shapes=[pltpu.VMEM((tl, d), BF16)],
        compiler_params=_params("arbitrary"),
        name="pool",
    )(proj, pool_w_bf, pool_scale.reshape(1, d))


def _ssd_kernel(z_ref, xs_ref, bm_ref, cm_ref, dt_ref,
                cwx_ref, cwb_ref, cwc_ref, cbx_ref, cbb_ref, cbc_ref,
                dtb_ref, alog_ref, dsk_ref, nw_ref, exp_ref,
                o_ref,
                extx_s, extb_s, extc_s, state_s, xdt_s, y_s, cb_s, cs_s, cst_s, xc_s, bc_s, cc_s,
                *, tiles_per_seq, heads):
    i = pl.program_id(0)
    tl, inner = xs_ref.shape
    gw = inner // SSD_GROUPS
    pairs_per_group = gw // LANES
    n_pairs = inner // LANES

    @pl.when(i % tiles_per_seq == 0)
    def _():
        extx_s[0:CONV_HALO, :] = jnp.zeros((CONV_HALO, inner), F32)
        extb_s[0:CONV_HALO, :] = jnp.zeros((CONV_HALO, extb_s.shape[1]), F32)
        extc_s[0:CONV_HALO, :] = jnp.zeros((CONV_HALO, extc_s.shape[1]), F32)
        state_s[...] = jnp.zeros_like(state_s)

    def conv_silu(ext_ref, src_ref, w_ref, b_ref, dst_ref, width, cw):
        for c0 in range(0, width, cw):
            cs = slice(c0, c0 + cw)
            ext_ref[CONV_HALO:CONV_HALO + tl, cs] = src_ref[:, cs].astype(F32)
            acc = b_ref[:, cs] + w_ref[SSD_CONV - 1:SSD_CONV, cs] * ext_ref[CONV_HALO:CONV_HALO + tl, cs]
            for k in range(1, SSD_CONV):
                acc = acc + w_ref[SSD_CONV - 1 - k:SSD_CONV - k, cs] * ext_ref[pl.ds(CONV_HALO - k, tl), cs]
            dst_ref[:, cs] = _silu(acc)
            ext_ref[0:CONV_HALO, cs] = ext_ref[tl:tl + CONV_HALO, cs]

    conv_silu(extx_s, xs_ref, cwx_ref, cbx_ref, xc_s, inner, LANES)
    conv_silu(extb_s, bm_ref, cwb_ref, cbb_ref, bc_s, bm_ref.shape[1], LANES)
    conv_silu(extc_s, cm_ref, cwc_ref, cbc_ref, cc_s, cm_ref.shape[1], LANES)

    dtv = dt_ref[...] + dtb_ref[...]
    dt = jnp.maximum(dtv, 0.0) + jnp.log1p(jnp.exp(-jnp.abs(dtv)))
    a = -jnp.exp(alog_ref[...])
    da = dt * a
    ri = lax.broadcasted_iota(jnp.int32, (tl, tl), 0)
    ci = lax.broadcasted_iota(jnp.int32, (tl, tl), 1)
    causal = ri >= ci
    cs = jnp.dot(causal.astype(F32), da, precision=HIGHEST, preferred_element_type=F32)
    cs2 = cs * LOG2_E
    cs_s[...] = cs2
    cst_s[...] = cs2.T
    last = cs[tl - 1:tl, :]
    exp_m = exp_ref[...]
    dt_x = jnp.dot(dt.astype(BF16), exp_m, preferred_element_type=F32)
    ecs_x = jnp.dot(jnp.exp(cs).astype(BF16), exp_m, preferred_element_type=F32)
    dte_x = jnp.dot(jnp.exp(last - cs).astype(BF16), exp_m, preferred_element_type=F32)

    for g in range(SSD_GROUPS):
        gs = slice(g * gw, (g + 1) * gw)
        ns = slice(g * SSD_STATE, (g + 1) * SSD_STATE)
        xg = xc_s[:, gs]
        xdt = xg * dt_x[:, gs]
        xdt_bf = xdt.astype(BF16)
        xd_bf = (xdt * dte_x[:, gs]).astype(BF16)
        bg = bc_s[:, ns]
        cg = cc_s[:, ns].astype(BF16)
        cb_s[g] = lax.dot_general(cg, bg.astype(BF16), (((1,), (1,)), ((), ())), preferred_element_type=F32)
        s_old = state_s[g]
        y_off = jnp.dot(cg, s_old.astype(BF16), preferred_element_type=F32) * ecs_x[:, gs]
        state_s[g] = (s_old * ecs_x[tl - 1:tl, gs]
                      + jnp.dot(bg.T.astype(BF16), xd_bf, preferred_element_type=F32))
        y0 = y_off + dsk_ref[:, gs] * xg
        for q in range(pairs_per_group):
            qs = slice(q * LANES, (q + 1) * LANES)
            xdt_s[g * pairs_per_group + q] = xdt_bf[:, qs]
            y_s[g * pairs_per_group + q] = y0[:, qs]

    lane_h = lax.broadcasted_iota(jnp.int32, (tl, LANES), 1)
    low_half = lane_h < SSD_HEAD_DIM

    half = tl // 2
    causal_top = (lax.broadcasted_iota(jnp.int32, (half, half), 0)
                  >= lax.broadcasted_iota(jnp.int32, (half, half), 1))
    causal_bot = (lax.broadcasted_iota(jnp.int32, (half, tl), 0) + half
                  >= lax.broadcasted_iota(jnp.int32, (half, tl), 1))
    lane_hh = lax.broadcasted_iota(jnp.int32, (half, LANES), 1)
    low_hh = lane_hh < SSD_HEAD_DIM

    def pair_body(hp, carry):
        g = hp // pairs_per_group
        cb_top = cb_s[g, 0:half, 0:half]
        cb_bot = cb_s[g, half:tl, :]
        x_top = xdt_s[hp, 0:half, :]
        x_all = xdt_s[hp]
        cs_top = cs_s[0:half, :]
        cs_bot = cs_s[half:tl, :]
        tops, bots = [], []
        for e in range(2):
            h = 2 * hp + e
            head_lane = jnp.full((half, LANES), h, jnp.int32)
            col_top = jnp.take_along_axis(cs_top, head_lane, axis=1)
            col_bot = jnp.take_along_axis(cs_bot, head_lane, axis=1)
            col_bot = jnp.concatenate([col_bot] * (tl // LANES), axis=1)
            row_all = cst_s[pl.ds(h, 1), :]
            row_top = row_all[:, 0:half]
            m_top = jnp.exp2(jnp.where(causal_top, col_top - row_top, -jnp.inf)) * cb_top
            m_bot = jnp.exp2(jnp.where(causal_bot, col_bot - row_all, -jnp.inf)) * cb_bot
            tops.append(jnp.dot(m_top.astype(BF16), x_top, preferred_element_type=F32))
            bots.append(jnp.dot(m_bot.astype(BF16), x_all, preferred_element_type=F32))
        y_s[hp, 0:half, :] = y_s[hp, 0:half, :] + jnp.where(low_hh, tops[0], tops[1])
        y_s[hp, half:tl, :] = y_s[hp, half:tl, :] + jnp.where(low_hh, bots[0], bots[1])
        return carry

    lax.fori_loop(0, n_pairs, pair_body, 0, unroll=16)

    for g in range(SSD_GROUPS):
        gs = slice(g * gw, (g + 1) * gw)
        yg = jnp.concatenate([y_s[g * pairs_per_group + q] for q in range(pairs_per_group)], axis=1)
        yg = yg * _silu(z_ref[:, gs].astype(F32))
        ms = jnp.mean(yg * yg, axis=-1, keepdims=True)
        o_ref[:, gs] = (yg * lax.rsqrt(ms + EPS) * nw_ref[:, gs]).astype(o_ref.dtype)


def _ssd(proj, dt_raw, blk, conv_w, conv_b, dt_bias, a_log, d_skip, ssd_norm_w, seq):
    t = proj.shape[0]
    tl = SEQ_TILE
    heads = dt_bias.shape[0]
    inner = ssd_norm_w.shape[0]
    bc = SSD_GROUPS * SSD_STATE
    gw = inner // SSD_GROUPS
    pad = LANES - heads
    row = lambda v: v.astype(F32).reshape(1, -1)
    dtb = jnp.pad(row(dt_bias), ((0, 0), (0, pad)))
    alog = jnp.pad(row(a_log), ((0, 0), (0, pad)))
    dsk = jnp.repeat(d_skip.astype(F32), SSD_HEAD_DIM).reshape(1, inner)
    expand = (lax.broadcasted_iota(jnp.int32, (LANES, inner), 1) // SSD_HEAD_DIM
              == lax.broadcasted_iota(jnp.int32, (LANES, inner), 0)).astype(BF16)
    cw = conv_w.astype(F32)
    cbias = row(conv_b)
    full = lambda shape: pl.BlockSpec(shape, lambda i: tuple(0 for _ in shape))
    return pl.pallas_call(
        functools.partial(_ssd_kernel, tiles_per_seq=seq // tl, heads=heads),
        grid=(t // tl,),
        in_specs=[pl.BlockSpec((tl, inner), lambda i: (i, blk["z"])),
                  pl.BlockSpec((tl, inner), lambda i: (i, blk["xs"])),
                  pl.BlockSpec((tl, bc), lambda i: (i, blk["B"])),
                  pl.BlockSpec((tl, bc), lambda i: (i, blk["C"])),
                  pl.BlockSpec((tl, LANES), lambda i: (i, 0)),
                  full((SSD_CONV, inner)), full((SSD_CONV, bc)), full((SSD_CONV, bc)),
                  full((1, inner)), full((1, bc)), full((1, bc)),
                  full((1, LANES)), full((1, LANES)), full((1, inner)), full((1, inner)),
                  full((LANES, inner))],
        out_specs=pl.BlockSpec((tl, inner), lambda i: (i, 0)),
        out_shape=jax.ShapeDtypeStruct((t, inner), BF16),
        scratch_shapes=[pltpu.VMEM((tl + CONV_HALO, inner), F32),
                        pltpu.VMEM((tl + CONV_HALO, bc), F32),
                        pltpu.VMEM((tl + CONV_HALO, bc), F32),
                        pltpu.VMEM((SSD_GROUPS, SSD_STATE, gw), F32),
                        pltpu.VMEM((inner // LANES, tl, LANES), BF16),
                        pltpu.VMEM((inner // LANES, tl, LANES), F32),
                        pltpu.VMEM((SSD_GROUPS, tl, tl), F32),
                        pltpu.VMEM((tl, LANES), F32),
                        pltpu.VMEM((LANES, tl), F32),
                        pltpu.VMEM((tl, inner), F32),
                        pltpu.VMEM((tl, bc), F32),
                        pltpu.VMEM((tl, bc), F32)],
        compiler_params=_params("arbitrary"),
        name="ssd",
    )(proj, proj, proj, proj, dt_raw,
      cw[:, :inner], cw[:, inner:inner + bc], cw[:, inner + bc:],
      cbias[:, :inner], cbias[:, inner:inner + bc], cbias[:, inner + bc:],
      dtb, alog, dsk, row(ssd_norm_w), expand)


def _merge_kernel(yp_ref, ys_ref, gp_ref, gs_ref, wp_ref, ws_ref, o_ref):
    a = jnp.dot(yp_ref[...], wp_ref[...], preferred_element_type=F32)
    b = jnp.dot(ys_ref[...], ws_ref[...], preferred_element_type=F32)
    o_ref[...] = (jax.nn.sigmoid(gp_ref[...].astype(F32)) * a
                  + jax.nn.sigmoid(gs_ref[...].astype(F32)) * b).astype(o_ref.dtype)


def _merge(ypool, yssd, proj, gp_off, gs_off, wbp, wbs):
    t, d = ypool.shape
    inner = yssd.shape[1]
    tm, tn = min(MERGE_TM, t), MERGE_TN
    return pl.pallas_call(
        _merge_kernel,
        grid=(t // tm, d // tn),
        in_specs=[pl.BlockSpec((tm, d), lambda i, j: (i, 0)),
                  pl.BlockSpec((tm, inner), lambda i, j: (i, 0)),
                  pl.BlockSpec((tm, tn), lambda i, j: (i, gp_off // tn + j)),
                  pl.BlockSpec((tm, tn), lambda i, j: (i, gs_off // tn + j)),
                  pl.BlockSpec((d, tn), lambda i, j: (0, j)),
                  pl.BlockSpec((inner, tn), lambda i, j: (0, j))],
        out_specs=pl.BlockSpec((tm, tn), lambda i, j: (i, j)),
        out_shape=jax.ShapeDtypeStruct((t, d), BF16),
        compiler_params=_params("arbitrary", "arbitrary"),
        name="merge",
    )(ypool, yssd, proj, proj, wbp, wbs)


def _out_kernel(m_ref, wo_ref, x_ref, mod_ref, wr_ref, br_ref,
                h_ref, u_ref, idx_ref, gate_ref, rank_ref, cnt_ref, carry_s):
    i = pl.program_id(0)
    tm = m_ref.shape[0]
    ne = wr_ref.shape[0]

    @pl.when(i == 0)
    def _():
        carry_s[...] = jnp.zeros_like(carry_s)

    mix = jnp.dot(m_ref[...], wo_ref[...], preferred_element_type=F32)
    h1 = x_ref[...] + mod_ref[0, 2:3, :] * mix
    h_ref[...] = h1
    ms = jnp.mean(h1 * h1, axis=-1, keepdims=True)
    u = h1 * lax.rsqrt(ms + EPS) * (1.0 + mod_ref[0, 4:5, :]) + mod_ref[0, 3:4, :]
    u_ref[...] = _pack_bf16_pairs(u)
    logits = lax.dot_general(wr_ref[...], u, (((1,), (1,)), ((), ())), precision=HIGHEST,
                             preferred_element_type=F32) + br_ref[...]
    eidx = lax.broadcasted_iota(jnp.int32, (ne, tm), 0)
    work = logits
    vals, idxs, hots = [], [], []
    for _ in range(TOP_K):
        mx = jnp.max(work, axis=0, keepdims=True)
        sel = jnp.min(jnp.where(work == mx, eidx, ne), axis=0, keepdims=True)
        hot = eidx == sel
        vals.append(mx)
        idxs.append(sel)
        hots.append(hot)
        work = jnp.where(hot, -jnp.inf, work)
    exps = [jnp.exp(v - vals[0]) for v in vals]
    den = exps[0]
    for e in exps[1:]:
        den = den + e
    cnt = hots[0].astype(F32)
    for hot in hots[1:]:
        cnt = cnt + hot.astype(F32)
    ti = lax.broadcasted_iota(jnp.int32, (tm, tm), 0)
    tj = lax.broadcasted_iota(jnp.int32, (tm, tm), 1)
    before = (ti < tj).astype(BF16)
    prefix = jnp.dot(cnt.astype(BF16), before, preferred_element_type=F32)
    base = carry_s[:, 0:1] + prefix
    pad_rows = idx_ref.shape[0] - TOP_K
    ranks = [jnp.sum(jnp.where(hot, base, 0.0), axis=0, keepdims=True) for hot in hots]
    idx_ref[...] = jnp.concatenate(idxs + [jnp.zeros((pad_rows, tm), jnp.int32)], axis=0)
    gate_ref[...] = jnp.concatenate([e / den for e in exps] + [jnp.zeros((pad_rows, tm), F32)], axis=0)
    rank_ref[...] = jnp.concatenate([r.astype(jnp.int32) for r in ranks]
                                    + [jnp.zeros((pad_rows, tm), jnp.int32)], axis=0)
    carry_s[...] = carry_s[...] + jnp.sum(cnt, axis=1, keepdims=True)
    cnt_ref[...] = carry_s[...].astype(jnp.int32)


def _outproj(merged, wo, x2, mod3, w_router, b_router, seq):
    t, d = x2.shape
    ne = w_router.shape[1]
    tm = min(OUT_TM, seq)
    tiles_per_seq = seq // tm
    rows = 8
    return pl.pallas_call(
        _out_kernel,
        grid=(t // tm,),
        in_specs=[pl.BlockSpec((tm, d), lambda i: (i, 0)),
                  pl.BlockSpec((d, d), lambda i: (0, 0), pipeline_mode=pl.Buffered(1)),
                  pl.BlockSpec((tm, d), lambda i: (i, 0)),
                  pl.BlockSpec((1, N_MOD, d), lambda i: (i // tiles_per_seq, 0, 0)),
                  pl.BlockSpec((ne, d), lambda i: (0, 0)),
                  pl.BlockSpec((ne, 1), lambda i: (0, 0))],
        out_specs=[pl.BlockSpec((tm, d), lambda i: (i, 0)),
                   pl.BlockSpec((tm, d // 2), lambda i: (i, 0)),
                   pl.BlockSpec((rows, tm), lambda i: (0, i)),
                   pl.BlockSpec((rows, tm), lambda i: (0, i)),
                   pl.BlockSpec((rows, tm), lambda i: (0, i)),
                   pl.BlockSpec((ne, LANES), lambda i: (0, 0))],
        out_shape=[jax.ShapeDtypeStruct((t, d), F32), jax.ShapeDtypeStruct((t, d // 2), jnp.uint32),
                   jax.ShapeDtypeStruct((rows, t), jnp.int32), jax.ShapeDtypeStruct((rows, t), F32),
                   jax.ShapeDtypeStruct((rows, t), jnp.int32), jax.ShapeDtypeStruct((ne, LANES), jnp.int32)],
        scratch_shapes=[pltpu.VMEM((ne, LANES), F32)],
        compiler_params=_params("arbitrary"),
        name="outproj",
    )(merged, wo, x2, mod3, w_router.astype(F32).T, b_router.astype(F32).reshape(ne, 1))


def _row_copy(src_hbm, src_row, dst_ref, dst_row, sem):
    return pltpu.make_async_copy(src_hbm.at[pl.ds(src_row, 1)], dst_ref.at[pl.ds(dst_row, 1)], sem)


def _dispatch_kernel(slot_ref, u_ref, init_hbm, xs_hbm, sem):
    del init_hbm
    ts = u_ref.shape[0]

    def issue(tok, carry):
        for k in range(TOP_K):
            _row_copy(u_ref, tok, xs_hbm, slot_ref[0, 0, tok * TOP_K + k], sem).start(priority=k % 2)
        return carry

    lax.fori_loop(0, ts, issue, 0)
    rows = pl.ds(0, ts * TOP_K)
    pltpu.make_async_copy(xs_hbm.at[rows], xs_hbm.at[rows], sem).wait()


def _dispatch(u2, slot_tk, n_slots):
    t, d = u2.shape
    ts = min(DISPATCH_TS, t)
    slots3 = slot_tk.reshape(t // ts, 1, ts * TOP_K)
    init = jnp.zeros((n_slots, d), u2.dtype)
    return pl.pallas_call(
        _dispatch_kernel,
        grid=(t // ts,),
        in_specs=[pl.BlockSpec((1, 1, ts * TOP_K), lambda i: (i, 0, 0), memory_space=pltpu.SMEM),
                  pl.BlockSpec((ts, d), lambda i: (i, 0)),
                  pl.BlockSpec(memory_space=pl.ANY)],
        out_specs=pl.BlockSpec(memory_space=pl.ANY),
        out_shape=jax.ShapeDtypeStruct((n_slots, d), u2.dtype),
        scratch_shapes=[pltpu.SemaphoreType.DMA(())],
        input_output_aliases={2: 0},
        compiler_params=_params("arbitrary"),
        name="dispatch",
    )(slots3, u2, init)


ITEM_IDLE, ITEM_ACTIVE, ITEM_ZERO_FILL = 0, 1, 2


def _moe_kernel(tb_ref, tj_ref, te_ref, tr_ref, tflag_ref, tout_ref,
                x_ref, wg_ref, wu_ref, bg_ref, bu_ref, wd_ref, bd_ref, o_ref,
                wg_s, wu_s, wd_s, xb_s, acc_s, *, last_j):
    del tb_ref, te_ref, tout_ref
    i = pl.program_id(0)
    flag = tflag_ref[i]
    r = tr_ref[i]
    j = tj_ref[i]
    half = x_ref.shape[1]

    @pl.when((flag == ITEM_ACTIVE) & (r == 0))
    def _():
        wg_s[...] = wg_ref[...].astype(BF16)
        wu_s[...] = wu_ref[...].astype(BF16)
        wd_s[...] = wd_ref[...].astype(BF16)

    @pl.when(flag == ITEM_ACTIVE)
    def _():
        lo, hi = _unpack_bf16_pairs(x_ref[...])
        xb_s[:, :half] = lo
        xb_s[:, half:] = hi
        xb = xb_s[...]
        gate = jnp.dot(xb, wg_s[...], preferred_element_type=F32) + bg_ref[...]
        up = jnp.dot(xb, wu_s[...], preferred_element_type=F32) + bu_ref[...]
        gate = jnp.minimum(gate, SWIGLU_LIMIT)
        up = jnp.clip(up, -SWIGLU_LIMIT, SWIGLU_LIMIT)
        act = (up + 1.0) * gate * jax.nn.sigmoid(SWIGLU_ALPHA * gate)
        part = jnp.dot(act.astype(BF16), wd_s[...], preferred_element_type=F32)

        @pl.when(j == 0)
        def _():
            acc_s[r] = part + bd_ref[...]

        @pl.when(j > 0)
        def _():
            acc_s[r] = acc_s[r] + part

        @pl.when(j == last_j)
        def _():
            o_ref[...] = _pack_bf16_pairs(acc_s[r])

    @pl.when(flag == ITEM_ZERO_FILL)
    def _():
        o_ref[...] = jnp.zeros_like(o_ref)


def _lookup(table, idx):
    hot = idx[:, None] == jnp.arange(table.shape[0], dtype=jnp.int32)[None, :]
    return jnp.sum(jnp.where(hot, table[None, :].astype(jnp.int32), 0), axis=1).astype(jnp.int32)


def _moe_tables(counts, n_blocks, ft):
    ne = counts.shape[0]
    padded = (counts + MOE_BM - 1) // MOE_BM * MOE_BM
    pend = jnp.cumsum(padded)
    pstart = pend - padded
    blocks_per_expert = padded // MOE_BM
    first_block = pstart // MOE_BM
    n_active = (pend[-1] // MOE_BM).astype(jnp.int32)

    blocks = jnp.arange(n_blocks, dtype=jnp.int32)
    block_expert = jnp.minimum(jnp.sum(pend[None, :] <= (blocks * MOE_BM)[:, None], axis=1), ne - 1).astype(jnp.int32)
    off = blocks - _lookup(first_block, block_expert)
    is_start = (off % MOE_CHUNK_BLOCKS == 0) & (blocks < n_active)
    chunk_size = jnp.minimum(MOE_CHUNK_BLOCKS,
                             _lookup(blocks_per_expert, block_expert) - off // MOE_CHUNK_BLOCKS * MOE_CHUNK_BLOCKS)

    items = jnp.arange(n_blocks * ft, dtype=jnp.int32)
    active = items < ft * n_active
    src = jnp.minimum(items, ft * n_active - 1)
    cand = jnp.where(is_start[None, :] & (ft * blocks[None, :] <= src[:, None]), blocks[None, :], 0)
    cstart = jnp.max(cand, axis=1).astype(jnp.int32)
    m = jnp.maximum(_lookup(chunk_size, cstart), 1)
    local = src - ft * cstart
    tj = local // m
    tr = local % m
    tb = cstart + tr
    te = _lookup(block_expert, cstart)
    k = items - ft * n_active
    zero_fill = (~active) & (n_active + k < n_blocks)
    tout = jnp.where(active, jnp.where(tj == ft - 1, tb, cstart), jnp.minimum(n_active + k, n_blocks - 1))
    tflag = jnp.where(active, ITEM_ACTIVE, jnp.where(zero_fill, ITEM_ZERO_FILL, ITEM_IDLE))
    as_i32 = lambda v: v.astype(jnp.int32)
    return pstart, tuple(map(as_i32, (tb, tj, te, tr, tflag, tout)))


def _moe(xs, tables, wgu, bgu, wd, bd):
    n_slots = xs.shape[0]
    ne, d, f2 = wgu.shape
    f = f2 // 2
    bm, tf = MOE_BM, min(MOE_TF, f)
    nb, ft = n_slots // bm, f // tf
    grid_spec = pltpu.PrefetchScalarGridSpec(
        num_scalar_prefetch=6,
        grid=(nb * ft,),
        in_specs=[pl.BlockSpec((bm, d // 2), lambda i, tb, tj, te, tr, tg, to: (tb[i], 0)),
                  pl.BlockSpec((None, d, tf), lambda i, tb, tj, te, tr, tg, to: (te[i], 0, tj[i])),
                  pl.BlockSpec((None, d, tf), lambda i, tb, tj, te, tr, tg, to: (te[i], 0, ft + tj[i])),
                  pl.BlockSpec((None, 1, tf), lambda i, tb, tj, te, tr, tg, to: (te[i], 0, tj[i])),
                  pl.BlockSpec((None, 1, tf), lambda i, tb, tj, te, tr, tg, to: (te[i], 0, ft + tj[i])),
                  pl.BlockSpec((None, tf, d), lambda i, tb, tj, te, tr, tg, to: (te[i], tj[i], 0)),
                  pl.BlockSpec((None, 1, d), lambda i, tb, tj, te, tr, tg, to: (te[i], 0, 0))],
        out_specs=pl.BlockSpec((bm, d // 2), lambda i, tb, tj, te, tr, tg, to: (to[i], 0)),
        scratch_shapes=[pltpu.VMEM((d, tf), BF16), pltpu.VMEM((d, tf), BF16), pltpu.VMEM((tf, d), BF16),
                        pltpu.VMEM((bm, d), BF16), pltpu.VMEM((MOE_CHUNK_BLOCKS, bm, d), F32)],
    )
    return pl.pallas_call(
        functools.partial(_moe_kernel, last_j=ft - 1),
        grid_spec=grid_spec,
        out_shape=jax.ShapeDtypeStruct((n_slots, d // 2), jnp.uint32),
        compiler_params=_params("arbitrary"),
        name="moe",
    )(*tables, xs, wgu, wgu, bgu, bgu, wd, bd)


def _final_kernel(slot_ref, h_ref, gate_ref, mod_ref, fw_ref, y_hbm, o_ref, ybuf, sem):
    tc, d = h_ref.shape
    half = d // 2

    def issue(grp, carry):
        for q in range(FINAL_ISSUE_UNROLL):
            tk = FINAL_ISSUE_UNROLL * grp + q
            _row_copy(y_hbm, slot_ref[0, 0, tk], ybuf, tk, sem).start(priority=q % 2)
        return carry

    lax.fori_loop(0, tc * TOP_K // FINAL_ISSUE_UNROLL, issue, 0)
    pltpu.make_async_copy(y_hbm.at[pl.ds(0, tc * TOP_K)], ybuf, sem).wait()

    ffn_lo = jnp.zeros((tc, half), F32)
    ffn_hi = jnp.zeros((tc, half), F32)
    for k in range(TOP_K):
        w = ybuf[k * tc:(k + 1) * tc, :]
        g = gate_ref[:, k:k + 1]
        ffn_lo = ffn_lo + g * lax.bitcast_convert_type(w << 16, F32)
        ffn_hi = ffn_hi + g * lax.bitcast_convert_type(w & jnp.uint32(0xFFFF0000), F32)
    h2_lo = h_ref[:, :half] + mod_ref[0, 5:6, :half] * ffn_lo
    h2_hi = h_ref[:, half:] + mod_ref[0, 5:6, half:] * ffn_hi
    ms = (jnp.sum(h2_lo * h2_lo, axis=-1, keepdims=True) + jnp.sum(h2_hi * h2_hi, axis=-1, keepdims=True)) / d
    inv = lax.rsqrt(ms + EPS)
    o_ref[:, :half] = (h2_lo * inv * fw_ref[:, :half]).astype(o_ref.dtype)
    o_ref[:, half:] = (h2_hi * inv * fw_ref[:, half:]).astype(o_ref.dtype)


def _final(h1, gates_tk, slot_kt, mod3, final_norm_w, y_slots, seq, out_dtype):
    t, d = h1.shape
    tc = min(FINAL_TC, seq)
    tiles_per_seq = seq // tc
    slots3 = slot_kt.reshape(TOP_K, t // tc, tc).transpose(1, 0, 2).reshape(t // tc, 1, TOP_K * tc)
    return pl.pallas_call(
        _final_kernel,
        grid=(t // tc,),
        in_specs=[pl.BlockSpec((1, 1, TOP_K * tc), lambda i: (i, 0, 0), memory_space=pltpu.SMEM),
                  pl.BlockSpec((tc, d), lambda i: (i, 0)),
                  pl.BlockSpec((tc, gates_tk.shape[1]), lambda i: (i, 0)),
                  pl.BlockSpec((1, N_MOD, d), lambda i: (i // tiles_per_seq, 0, 0)),
                  pl.BlockSpec((1, d), lambda i: (0, 0)),
                  pl.BlockSpec(memory_space=pl.ANY)],
        out_specs=pl.BlockSpec((tc, d), lambda i: (i, 0)),
        out_shape=jax.ShapeDtypeStruct((t, d), out_dtype),
        scratch_shapes=[pltpu.VMEM((TOP_K * tc, d // 2), jnp.uint32), pltpu.SemaphoreType.DMA(())],
        compiler_params=_params("arbitrary"),
        name="final",
    )(slots3, h1, gates_tk, mod3, final_norm_w.astype(F32).reshape(1, d), y_slots)


def kernel(x, c, w_ada, b_ada, w_in, pool_w, pool_scale, conv_w, conv_b, dt_bias, a_log, d_skip, ssd_norm_w,
           w_branch_pool, w_branch_ssd, w_out, w_router, b_router, w_gate_up, b_gate_up, w_down, b_down,
           final_norm_w):
    bsz, seq, d = x.shape
    depth = w_ada.shape[0]
    t = bsz * seq
    inner = ssd_norm_w.shape[1]
    heads = dt_bias.shape[1]
    bc = SSD_GROUPS * SSD_STATE
    ne = w_router.shape[2]
    assert depth == 1, "the final RMSNorm is fused into the last kernel of the single layer"
    assert SEQ_TILE == 2 * LANES
    assert seq % SEQ_TILE == 0 and heads <= LANES and (7 * d) % bc == 0 and (4 * d) % inner == 0
    assert inner // SSD_GROUPS % LANES == 0 and d % (len(POOL_WINDOWS) * LANES) == 0

    s1 = d
    s2 = s1 + inner
    s3 = s2 + inner + 2 * bc
    s4 = s3 + heads
    s5 = s4 + d
    blk = {"z": 0, "xs": 1, "p": (2 * inner) // d, "B": (2 * inner + 3 * d) // bc, "C": (2 * inner + 3 * d) // bc + 1}
    gp_off, gs_off = 2 * inner + d, 2 * inner + 2 * d

    h = x.astype(F32).reshape(t, d)
    for layer in range(depth):
        wl = w_in[layer]
        w_main = jnp.concatenate([wl[:, s1:s2], wl[:, s2:s2 + inner], wl[:, :s1], wl[:, s4:s5], wl[:, s5:],
                                  wl[:, s2 + inner:s2 + inner + bc], wl[:, s2 + inner + bc:s3]], axis=1).astype(BF16)
        w_dt = jnp.pad(wl[:, s3:s4].astype(F32), ((0, 0), (0, LANES - heads)))

        mod3 = _ada(c, w_ada[layer], b_ada[layer]).reshape(bsz, N_MOD, d)
        proj, dt_raw = _inproj(h, mod3, w_main, w_dt, seq)
        ypool = _pool(proj, blk["p"], pool_w[layer].astype(BF16), pool_scale[layer], seq, d)
        yssd = _ssd(proj, dt_raw, blk, conv_w[layer], conv_b[layer], dt_bias[layer], a_log[layer], d_skip[layer],
                    ssd_norm_w[layer], seq)
        merged = _merge(ypool, yssd, proj, gp_off, gs_off, w_branch_pool[layer].astype(BF16),
                        w_branch_ssd[layer].astype(BF16))
        h1, u2, idx_kt, gate_kt, rank_kt, counts = _outproj(merged, w_out[layer].astype(BF16), h, mod3,
                                                            w_router[layer], b_router[layer], seq)

        n_blocks = (t * TOP_K) // MOE_BM + ne
        ft = w_down.shape[2] // min(MOE_TF, w_down.shape[2])
        pstart, tables = _moe_tables(counts[:, 0], n_blocks, ft)
        eids = jnp.arange(ne, dtype=jnp.int32)[:, None, None]
        slot_kt = (jnp.sum(jnp.where(idx_kt[:TOP_K][None] == eids, pstart[:, None, None], 0), axis=0)
                   + rank_kt[:TOP_K]).astype(jnp.int32)

        xs = _dispatch(u2, slot_kt.T, n_blocks * MOE_BM)
        y_slots = _moe(xs, tables, w_gate_up[layer].astype(F32), b_gate_up[layer].astype(F32)[:, None, :],
                       w_down[layer].astype(F32), b_down[layer].astype(F32)[:, None, :])
        h = _final(h1, gate_kt.T, slot_kt, mod3, final_norm_w, y_slots, seq, x.dtype)
    return h.reshape(bsz, seq, d)
```

```python
import functools

import jax
import jax.numpy as jnp
from jax import lax
from jax.experimental import pallas as pl
from jax.experimental.pallas import tpu as pltpu

F32 = jnp.float32
BF16 = jnp.bfloat16
HIGHEST = lax.Precision.HIGHEST

EPS = 1e-6
POOL_WINDOWS = (2, 4, 8, 16)
SSD_GROUPS = 8
SSD_STATE = 128
SSD_CONV = 4
SSD_HEAD_DIM = 64
TOP_K = 4
SWIGLU_LIMIT = 7.0
SWIGLU_ALPHA = 1.702
N_MOD = 6

LOG2_E = 1.4426950408889634
LANES = 128
CONV_HALO = 8
V7X_VMEM_LIMIT = 56 * 1024 * 1024

SEQ_TILE = 256
INPROJ_TM, INPROJ_TN = 1024, 1024
MERGE_TM, MERGE_TN = 512, 512
OUT_TM = 512
MOE_BM, MOE_TF = 512, 512
MOE_CHUNK_BLOCKS = 2
DISPATCH_TS = 256
FINAL_TC = 256
FINAL_ISSUE_UNROLL = 8


def _params(*sem):
    return pltpu.CompilerParams(dimension_semantics=sem, vmem_limit_bytes=V7X_VMEM_LIMIT)


def _silu(v):
    return v * jax.nn.sigmoid(v)


def _pack_bf16_pairs(v):
    half = v.shape[1] // 2
    lo = lax.bitcast_convert_type(v[:, :half].astype(BF16).astype(F32), jnp.uint32)
    hi = lax.bitcast_convert_type(v[:, half:].astype(BF16).astype(F32), jnp.uint32)
    return (hi & jnp.uint32(0xFFFF0000)) | (lo >> 16)


def _unpack_bf16_pairs(w):
    lo = lax.bitcast_convert_type(w << 16, F32).astype(BF16)
    hi = lax.bitcast_convert_type(w & jnp.uint32(0xFFFF0000), F32).astype(BF16)
    return lo, hi


def _ada_kernel(cb_ref, w_ref, b_ref, o_ref):
    nb, tn = cb_ref.shape[0], w_ref.shape[1]
    for b in range(nb):
        ca = _silu(cb_ref[b])
        cols = [jnp.sum(w_ref[:, j * LANES:(j + 1) * LANES] * ca, axis=0, keepdims=True)
                for j in range(tn // LANES)]
        o_ref[b:b + 1, :] = jnp.concatenate(cols, axis=1) + b_ref[...]


def _ada(c, w_ada, b_ada):
    nb, k = c.shape
    n = w_ada.shape[1]
    tn = 1024 if n % 1024 == 0 else n
    cb = jnp.broadcast_to(c.astype(F32)[:, :, None], (nb, k, LANES))
    return pl.pallas_call(
        _ada_kernel,
        grid=(n // tn,),
        in_specs=[pl.BlockSpec((nb, k, LANES), lambda j: (0, 0, 0)),
                  pl.BlockSpec((k, tn), lambda j: (0, j)),
                  pl.BlockSpec((1, tn), lambda j: (0, j))],
        out_specs=pl.BlockSpec((nb, tn), lambda j: (0, j)),
        out_shape=jax.ShapeDtypeStruct((nb, n), F32),
        compiler_params=_params("arbitrary"),
        name="ada",
    )(cb, w_ada, b_ada.reshape(1, n))


def _inproj_kernel(x_ref, mod_ref, w_ref, wdt_ref, o_ref, dt_ref, u_s, *, rows_per_chunk):
    j = pl.program_id(1)
    tm = x_ref.shape[0]

    @pl.when(j == 0)
    def _():
        sh = mod_ref[0, 0:1, :]
        sc = mod_ref[0, 1:2, :]

        def body(r, carry):
            rows = pl.ds(pl.multiple_of(r * rows_per_chunk, rows_per_chunk), rows_per_chunk)
            xv = x_ref[rows, :]
            ms = jnp.mean(xv * xv, axis=-1, keepdims=True)
            u = xv * lax.rsqrt(ms + EPS) * (1.0 + sc) + sh
            u_s[rows, :] = u.astype(BF16)
            dt_ref[rows, :] = jnp.dot(u, wdt_ref[...], precision=HIGHEST, preferred_element_type=F32)
            return carry

        lax.fori_loop(0, tm // rows_per_chunk, body, 0)

    o_ref[...] = jnp.dot(u_s[...], w_ref[...], preferred_element_type=F32).astype(o_ref.dtype)


def _inproj(x2, mod3, w_main, w_dt, seq):
    t, d = x2.shape
    n = w_main.shape[1]
    tm = min(INPROJ_TM, seq)
    tn = INPROJ_TN
    tiles_per_seq = seq // tm
    return pl.pallas_call(
        functools.partial(_inproj_kernel, rows_per_chunk=min(128, tm)),
        grid=(t // tm, n // tn),
        in_specs=[pl.BlockSpec((tm, d), lambda i, j: (i, 0)),
                  pl.BlockSpec((1, N_MOD, d), lambda i, j: (i // tiles_per_seq, 0, 0)),
                  pl.BlockSpec((d, tn), lambda i, j: (0, j)),
                  pl.BlockSpec((d, LANES), lambda i, j: (0, 0))],
        out_specs=[pl.BlockSpec((tm, tn), lambda i, j: (i, j)),
                   pl.BlockSpec((tm, LANES), lambda i, j: (i, 0))],
        out_shape=[jax.ShapeDtypeStruct((t, n), BF16), jax.ShapeDtypeStruct((t, LANES), F32)],
        scratch_shapes=[pltpu.VMEM((tm, d), BF16)],
        compiler_params=_params("arbitrary", "arbitrary"),
        name="inproj",
    )(x2, mod3, w_main, w_dt)


def _pool_kernel(p_ref, pw_ref, ps_ref, o_ref, prev_s, *, tiles_per_seq):
    i = pl.program_id(0)
    tl, d = p_ref.shape
    gd = d // len(POOL_WINDOWS)
    it = i % tiles_per_seq

    @pl.when(it == 0)
    def _():
        prev_s[...] = jnp.zeros_like(prev_s)

    row = lax.broadcasted_iota(jnp.int32, (tl, 2 * tl), 0)
    col = lax.broadcasted_iota(jnp.int32, (tl, 2 * tl), 1)
    pos = (it * tl + lax.broadcasted_iota(jnp.int32, (tl, 1), 0) + 1).astype(F32)
    for g, w in enumerate(POOL_WINDOWS):
        sl = slice(g * gd, (g + 1) * gd)
        cur = p_ref[:, sl]
        ext = jnp.concatenate([prev_s[:, sl], cur], axis=0)
        band = ((col <= row + tl) & (col > row + tl - w)).astype(BF16)
        win_sum = jnp.dot(band, ext, preferred_element_type=F32)
        mean = win_sum / jnp.minimum(pos, float(w))
        dlt = (mean - cur.astype(F32)).astype(BF16)
        mixed = jnp.dot(dlt, pw_ref[g], preferred_element_type=F32)
        o_ref[:, sl] = (mixed * ps_ref[:, sl]).astype(o_ref.dtype)
    prev_s[...] = p_ref[...]


def _pool(proj, p_blk, pool_w_bf, pool_scale, seq, d):
    t = proj.shape[0]
    tl = SEQ_TILE
    g, gd = pool_w_bf.shape[0], pool_w_bf.shape[1]
    return pl.pallas_call(
        functools.partial(_pool_kernel, tiles_per_seq=seq // tl),
        grid=(t // tl,),
        in_specs=[pl.BlockSpec((tl, d), lambda i: (i, p_blk)),
                  pl.BlockSpec((g, gd, gd), lambda i: (0, 0, 0)),
                  pl.BlockSpec((1, d), lambda i: (0, 0))],
        out_specs=pl.BlockSpec((tl, d), lambda i: (i, 0)),
        out_shape=jax.ShapeDtypeStruct((t, d), BF16),
        scratch_shapes=[pltpu.VMEM((tl, d), BF16)],
        compiler_params=_params("arbitrary"),
        name="pool",
    )(proj, pool_w_bf, pool_scale.reshape(1, d))


def _ssd_kernel(z_ref, xs_ref, bm_ref, cm_ref, dt_ref,
                cwx_ref, cwb_ref, cwc_ref, cbx_ref, cbb_ref, cbc_ref,
                dtb_ref, alog_ref, dsk_ref, nw_ref, exp_ref,
                o_ref,
                extx_s, extb_s, extc_s, state_s, xdt_s, y_s, cb_s, cs_s, cst_s, xc_s, bc_s, cc_s,
                *, tiles_per_seq, heads):
    i = pl.program_id(0)
    tl, inner = xs_ref.shape
    gw = inner // SSD_GROUPS
    pairs_per_group = gw // LANES
    n_pairs = inner // LANES

    @pl.when(i % tiles_per_seq == 0)
    def _():
        extx_s[0:CONV_HALO, :] = jnp.zeros((CONV_HALO, inner), F32)
        extb_s[0:CONV_HALO, :] = jnp.zeros((CONV_HALO, extb_s.shape[1]), F32)
        extc_s[0:CONV_HALO, :] = jnp.zeros((CONV_HALO, extc_s.shape[1]), F32)
        state_s[...] = jnp.zeros_like(state_s)

    def conv_silu(ext_ref, src_ref, w_ref, b_ref, dst_ref, width, cw):
        for c0 in range(0, width, cw):
            cs = slice(c0, c0 + cw)
            ext_ref[CONV_HALO:CONV_HALO + tl, cs] = src_ref[:, cs].astype(F32)
            acc = b_ref[:, cs] + w_ref[SSD_CONV - 1:SSD_CONV, cs] * ext_ref[CONV_HALO:CONV_HALO + tl, cs]
            for k in range(1, SSD_CONV):
                acc = acc + w_ref[SSD_CONV - 1 - k:SSD_CONV - k, cs] * ext_ref[pl.ds(CONV_HALO - k, tl), cs]
            dst_ref[:, cs] = _silu(acc)
            ext_ref[0:CONV_HALO, cs] = ext_ref[tl:tl + CONV_HALO, cs]

    conv_silu(extx_s, xs_ref, cwx_ref, cbx_ref, xc_s, inner, LANES)
    conv_silu(extb_s, bm_ref, cwb_ref, cbb_ref, bc_s, bm_ref.shape[1], LANES)
    conv_silu(extc_s, cm_ref, cwc_ref, cbc_ref, cc_s, cm_ref.shape[1], LANES)

    dtv = dt_ref[...] + dtb_ref[...]
    dt = jnp.maximum(dtv, 0.0) + jnp.log1p(jnp.exp(-jnp.abs(dtv)))
    a = -jnp.exp(alog_ref[...])
    da = dt * a
    ri = lax.broadcasted_iota(jnp.int32, (tl, tl), 0)
    ci = lax.broadcasted_iota(jnp.int32, (tl, tl), 1)
    causal = ri >= ci
    cs = jnp.dot(causal.astype(F32), da, precision=HIGHEST, preferred_element_type=F32)
    cs2 = cs * LOG2_E
    cs_s[...] = cs2
    cst_s[...] = cs2.T
    last = cs[tl - 1:tl, :]
    exp_m = exp_ref[...]
    dt_x = jnp.dot(dt.astype(BF16), exp_m, preferred_element_type=F32)
    ecs_x = jnp.dot(jnp.exp(cs).astype(BF16), exp_m, preferred_element_type=F32)
    dte_x = jnp.dot(jnp.exp(last - cs).astype(BF16), exp_m, preferred_element_type=F32)

    for g in range(SSD_GROUPS):
        gs = slice(g * gw, (g + 1) * gw)
        ns = slice(g * SSD_STATE, (g + 1) * SSD_STATE)
        xg = xc_s[:, gs]
        xdt = xg * dt_x[:, gs]
        xdt_bf = xdt.astype(BF16)
        xd_bf = (xdt * dte_x[:, gs]).astype(BF16)
        bg = bc_s[:, ns]
        cg = cc_s[:, ns].astype(BF16)
        cb_s[g] = lax.dot_general(cg, bg.astype(BF16), (((1,), (1,)), ((), ())), preferred_element_type=F32)
        s_old = state_s[g]
        y_off = jnp.dot(cg, s_old.astype(BF16), preferred_element_type=F32) * ecs_x[:, gs]
        state_s[g] = (s_old * ecs_x[tl - 1:tl, gs]
                      + jnp.dot(bg.T.astype(BF16), xd_bf, preferred_element_type=F32))
        y0 = y_off + dsk_ref[:, gs] * xg
        for q in range(pairs_per_group):
            qs = slice(q * LANES, (q + 1) * LANES)
            xdt_s[g * pairs_per_group + q] = xdt_bf[:, qs]
            y_s[g * pairs_per_group + q] = y0[:, qs]

    lane_h = lax.broadcasted_iota(jnp.int32, (tl, LANES), 1)
    low_half = lane_h < SSD_HEAD_DIM

    half = tl // 2
    causal_top = (lax.broadcasted_iota(jnp.int32, (half, half), 0)
                  >= lax.broadcasted_iota(jnp.int32, (half, half), 1))
    causal_bot = (lax.broadcasted_iota(jnp.int32, (half, tl), 0) + half
                  >= lax.broadcasted_iota(jnp.int32, (half, tl), 1))
    lane_hh = lax.broadcasted_iota(jnp.int32, (half, LANES), 1)
    low_hh = lane_hh < SSD_HEAD_DIM

    def pair_body(hp, carry):
        g = hp // pairs_per_group
        cb_top = cb_s[g, 0:half, 0:half]
        cb_bot = cb_s[g, half:tl, :]
        x_top = xdt_s[hp, 0:half, :]
        x_all = xdt_s[hp]
        cs_top = cs_s[0:half, :]
        cs_bot = cs_s[half:tl, :]
        tops, bots = [], []
        for e in range(2):
            h = 2 * hp + e
            head_lane = jnp.full((half, LANES), h, jnp.int32)
            col_top = jnp.take_along_axis(cs_top, head_lane, axis=1)
            col_bot = jnp.take_along_axis(cs_bot, head_lane, axis=1)
            col_bot = jnp.concatenate([col_bot] * (tl // LANES), axis=1)
            row_all = cst_s[pl.ds(h, 1), :]
            row_top = row_all[:, 0:half]
            m_top = jnp.exp2(jnp.where(causal_top, col_top - row_top, -jnp.inf)) * cb_top
            m_bot = jnp.exp2(jnp.where(causal_bot, col_bot - row_all, -jnp.inf)) * cb_bot
            tops.append(jnp.dot(m_top.astype(BF16), x_top, preferred_element_type=F32))
            bots.append(jnp.dot(m_bot.astype(BF16), x_all, preferred_element_type=F32))
        y_s[hp, 0:half, :] = y_s[hp, 0:half, :] + jnp.where(low_hh, tops[0], tops[1])
        y_s[hp, half:tl, :] = y_s[hp, half:tl, :] + jnp.where(low_hh, bots[0], bots[1])
        return carry

    lax.fori_loop(0, n_pairs, pair_body, 0, unroll=16)

    for g in range(SSD_GROUPS):
        gs = slice(g * gw, (g + 1) * gw)
        yg = jnp.concatenate([y_s[g * pairs_per_group + q] for q in range(pairs_per_group)], axis=1)
        yg = yg * _silu(z_ref[:, gs].astype(F32))
        ms = jnp.mean(yg * yg, axis=-1, keepdims=True)
        o_ref[:, gs] = (yg * lax.rsqrt(ms + EPS) * nw_ref[:, gs]).astype(o_ref.dtype)


def _ssd(proj, dt_raw, blk, conv_w, conv_b, dt_bias, a_log, d_skip, ssd_norm_w, seq):
    t = proj.shape[0]
    tl = SEQ_TILE
    heads = dt_bias.shape[0]
    inner = ssd_norm_w.shape[0]
    bc = SSD_GROUPS * SSD_STATE
    gw = inner // SSD_GROUPS
    pad = LANES - heads
    row = lambda v: v.astype(F32).reshape(1, -1)
    dtb = jnp.pad(row(dt_bias), ((0, 0), (0, pad)))
    alog = jnp.pad(row(a_log), ((0, 0), (0, pad)))
    dsk = jnp.repeat(d_skip.astype(F32), SSD_HEAD_DIM).reshape(1, inner)
    expand = (lax.broadcasted_iota(jnp.int32, (LANES, inner), 1) // SSD_HEAD_DIM
              == lax.broadcasted_iota(jnp.int32, (LANES, inner), 0)).astype(BF16)
    cw = conv_w.astype(F32)
    cbias = row(conv_b)
    full = lambda shape: pl.BlockSpec(shape, lambda i: tuple(0 for _ in shape))
    return pl.pallas_call(
        functools.partial(_ssd_kernel, tiles_per_seq=seq // tl, heads=heads),
        grid=(t // tl,),
        in_specs=[pl.BlockSpec((tl, inner), lambda i: (i, blk["z"])),
                  pl.BlockSpec((tl, inner), lambda i: (i, blk["xs"])),
                  pl.BlockSpec((tl, bc), lambda i: (i, blk["B"])),
                  pl.BlockSpec((tl, bc), lambda i: (i, blk["C"])),
                  pl.BlockSpec((tl, LANES), lambda i: (i, 0)),
                  full((SSD_CONV, inner)), full((SSD_CONV, bc)), full((SSD_CONV, bc)),
                  full((1, inner)), full((1, bc)), full((1, bc)),
                  full((1, LANES)), full((1, LANES)), full((1, inner)), full((1, inner)),
                  full((LANES, inner))],
        out_specs=pl.BlockSpec((tl, inner), lambda i: (i, 0)),
        out_shape=jax.ShapeDtypeStruct((t, inner), BF16),
        scratch_shapes=[pltpu.VMEM((tl + CONV_HALO, inner), F32),
                        pltpu.VMEM((tl + CONV_HALO, bc), F32),
                        pltpu.VMEM((tl + CONV_HALO, bc), F32),
                        pltpu.VMEM((SSD_GROUPS, SSD_STATE, gw), F32),
                        pltpu.VMEM((inner // LANES, tl, LANES), BF16),
                        pltpu.VMEM((inner // LANES, tl, LANES), F32),
                        pltpu.VMEM((SSD_GROUPS, tl, tl), F32),
                        pltpu.VMEM((tl, LANES), F32),
                        pltpu.VMEM((LANES, tl), F32),
                        pltpu.VMEM((tl, inner), F32),
                        pltpu.VMEM((tl, bc), F32),
                        pltpu.VMEM((tl, bc), F32)],
        compiler_params=_params("arbitrary"),
        name="ssd",
    )(proj, proj, proj, proj, dt_raw,
      cw[:, :inner], cw[:, inner:inner + bc], cw[:, inner + bc:],
      cbias[:, :inner], cbias[:, inner:inner + bc], cbias[:, inner + bc:],
      dtb, alog, dsk, row(ssd_norm_w), expand)


def _merge_kernel(yp_ref, ys_ref, gp_ref, gs_ref, wp_ref, ws_ref, o_ref):
    a = jnp.dot(yp_ref[...], wp_ref[...], preferred_element_type=F32)
    b = jnp.dot(ys_ref[...], ws_ref[...], preferred_element_type=F32)
    o_ref[...] = (jax.nn.sigmoid(gp_ref[...].astype(F32)) * a
                  + jax.nn.sigmoid(gs_ref[...].astype(F32)) * b).astype(o_ref.dtype)


def _merge(ypool, yssd, proj, gp_off, gs_off, wbp, wbs):
    t, d = ypool.shape
    inner = yssd.shape[1]
    tm, tn = min(MERGE_TM, t), MERGE_TN
    return pl.pallas_call(
        _merge_kernel,
        grid=(t // tm, d // tn),
        in_specs=[pl.BlockSpec((tm, d), lambda i, j: (i, 0)),
                  pl.BlockSpec((tm, inner), lambda i, j: (i, 0)),
                  pl.BlockSpec((tm, tn), lambda i, j: (i, gp_off // tn + j)),
                  pl.BlockSpec((tm, tn), lambda i, j: (i, gs_off // tn + j)),
                  pl.BlockSpec((d, tn), lambda i, j: (0, j)),
                  pl.BlockSpec((inner, tn), lambda i, j: (0, j))],
        out_specs=pl.BlockSpec((tm, tn), lambda i, j: (i, j)),
        out_shape=jax.ShapeDtypeStruct((t, d), BF16),
        compiler_params=_params("arbitrary", "arbitrary"),
        name="merge",
    )(ypool, yssd, proj, proj, wbp, wbs)


def _out_kernel(m_ref, wo_ref, x_ref, mod_ref, wr_ref, br_ref,
                h_ref, u_ref, idx_ref, gate_ref, rank_ref, cnt_ref, carry_s):
    i = pl.program_id(0)
    tm = m_ref.shape[0]
    ne = wr_ref.shape[0]

    @pl.when(i == 0)
    def _():
        carry_s[...] = jnp.zeros_like(carry_s)

    mix = jnp.dot(m_ref[...], wo_ref[...], preferred_element_type=F32)
    h1 = x_ref[...] + mod_ref[0, 2:3, :] * mix
    h_ref[...] = h1
    ms = jnp.mean(h1 * h1, axis=-1, keepdims=True)
    u = h1 * lax.rsqrt(ms + EPS) * (1.0 + mod_ref[0, 4:5, :]) + mod_ref[0, 3:4, :]
    u_ref[...] = _pack_bf16_pairs(u)
    logits = lax.dot_general(wr_ref[...], u, (((1,), (1,)), ((), ())), precision=HIGHEST,
                             preferred_element_type=F32) + br_ref[...]
    eidx = lax.broadcasted_iota(jnp.int32, (ne, tm), 0)
    work = logits
    vals, idxs, hots = [], [], []
    for _ in range(TOP_K):
        mx = jnp.max(work, axis=0, keepdims=True)
        sel = jnp.min(jnp.where(work == mx, eidx, ne), axis=0, keepdims=True)
        hot = eidx == sel
        vals.append(mx)
        idxs.append(sel)
        hots.append(hot)
        work = jnp.where(hot, -jnp.inf, work)
    exps = [jnp.exp(v - vals[0]) for v in vals]
    den = exps[0]
    for e in exps[1:]:
        den = den + e
    cnt = hots[0].astype(F32)
    for hot in hots[1:]:
        cnt = cnt + hot.astype(F32)
    ti = lax.broadcasted_iota(jnp.int32, (tm, tm), 0)
    tj = lax.broadcasted_iota(jnp.int32, (tm, tm), 1)
    before = (ti < tj).astype(BF16)
    prefix = jnp.dot(cnt.astype(BF16), before, preferred_element_type=F32)
    base = carry_s[:, 0:1] + prefix
    pad_rows = idx_ref.shape[0] - TOP_K
    ranks = [jnp.sum(jnp.where(hot, base, 0.0), axis=0, keepdims=True) for hot in hots]
    idx_ref[...] = jnp.concatenate(idxs + [jnp.zeros((pad_rows, tm), jnp.int32)], axis=0)
    gate_ref[...] = jnp.concatenate([e / den for e in exps] + [jnp.zeros((pad_rows, tm), F32)], axis=0)
    rank_ref[...] = jnp.concatenate([r.astype(jnp.int32) for r in ranks]
                                    + [jnp.zeros((pad_rows, tm), jnp.int32)], axis=0)
    carry_s[...] = carry_s[...] + jnp.sum(cnt, axis=1, keepdims=True)
    cnt_ref[...] = carry_s[...].astype(jnp.int32)


def _outproj(merged, wo, x2, mod3, w_router, b_router, seq):
    t, d = x2.shape
    ne = w_router.shape[1]
    tm = min(OUT_TM, seq)
    tiles_per_seq = seq // tm
    rows = 8
    return pl.pallas_call(
        _out_kernel,
        grid=(t // tm,),
        in_specs=[pl.BlockSpec((tm, d), lambda i: (i, 0)),
                  pl.BlockSpec((d, d), lambda i: (0, 0), pipeline_mode=pl.Buffered(1)),
                  pl.BlockSpec((tm, d), lambda i: (i, 0)),
                  pl.BlockSpec((1, N_MOD, d), lambda i: (i // tiles_per_seq, 0, 0)),
                  pl.BlockSpec((ne, d), lambda i: (0, 0)),
                  pl.BlockSpec((ne, 1), lambda i: (0, 0))],
        out_specs=[pl.BlockSpec((tm, d), lambda i: (i, 0)),
                   pl.BlockSpec((tm, d // 2), lambda i: (i, 0)),
                   pl.BlockSpec((rows, tm), lambda i: (0, i)),
                   pl.BlockSpec((rows, tm), lambda i: (0, i)),
                   pl.BlockSpec((rows, tm), lambda i: (0, i)),
                   pl.BlockSpec((ne, LANES), lambda i: (0, 0))],
        out_shape=[jax.ShapeDtypeStruct((t, d), F32), jax.ShapeDtypeStruct((t, d // 2), jnp.uint32),
                   jax.ShapeDtypeStruct((rows, t), jnp.int32), jax.ShapeDtypeStruct((rows, t), F32),
                   jax.ShapeDtypeStruct((rows, t), jnp.int32), jax.ShapeDtypeStruct((ne, LANES), jnp.int32)],
        scratch_shapes=[pltpu.VMEM((ne, LANES), F32)],
        compiler_params=_params("arbitrary"),
        name="outproj",
    )(merged, wo, x2, mod3, w_router.astype(F32).T, b_router.astype(F32).reshape(ne, 1))


def _row_copy(src_hbm, src_row, dst_ref, dst_row, sem):
    return pltpu.make_async_copy(src_hbm.at[pl.ds(src_row, 1)], dst_ref.at[pl.ds(dst_row, 1)], sem)


def _dispatch_kernel(slot_ref, u_ref, init_hbm, xs_hbm, sem):
    del init_hbm
    ts = u_ref.shape[0]

    def issue(tok, carry):
        for k in range(TOP_K):
            _row_copy(u_ref, tok, xs_hbm, slot_ref[0, 0, tok * TOP_K + k], sem).start(priority=k % 2)
        return carry

    lax.fori_loop(0, ts, issue, 0)
    rows = pl.ds(0, ts * TOP_K)
    pltpu.make_async_copy(xs_hbm.at[rows], xs_hbm.at[rows], sem).wait()


def _dispatch(u2, slot_tk, n_slots):
    t, d = u2.shape
    ts = min(DISPATCH_TS, t)
    slots3 = slot_tk.reshape(t // ts, 1, ts * TOP_K)
    init = jnp.zeros((n_slots, d), u2.dtype)
    return pl.pallas_call(
        _dispatch_kernel,
        grid=(t // ts,),
        in_specs=[pl.BlockSpec((1, 1, ts * TOP_K), lambda i: (i, 0, 0), memory_space=pltpu.SMEM),
                  pl.BlockSpec((ts, d), lambda i: (i, 0)),
                  pl.BlockSpec(memory_space=pl.ANY)],
        out_specs=pl.BlockSpec(memory_space=pl.ANY),
        out_shape=jax.ShapeDtypeStruct((n_slots, d), u2.dtype),
        scratch_shapes=[pltpu.SemaphoreType.DMA(())],
        input_output_aliases={2: 0},
        compiler_params=_params("arbitrary"),
        name="dispatch",
    )(slots3, u2, init)


ITEM_IDLE, ITEM_ACTIVE, ITEM_ZERO_FILL = 0, 1, 2


def _moe_kernel(tb_ref, tj_ref, te_ref, tr_ref, tflag_ref, tout_ref, tslot_ref, tne_ref, tnj_ref, thn_ref,
                x_ref, wgu_hbm, bg_ref, bu_ref, wd_hbm, bd_ref, o_ref,
                wg_s, wu_s, wd_s, xb_s, acc_s, wg_buf, wu_buf, wd_buf, sem, *, last_j):
    del tb_ref, tout_ref
    i = pl.program_id(0)
    flag = tflag_ref[i]
    r = tr_ref[i]
    j = tj_ref[i]
    half = x_ref.shape[1]
    tf = wg_s.shape[1]
    up_col0 = wgu_hbm.shape[2] // 2

    def weight_copies(e, jt, slot):
        c0 = pl.multiple_of(jt * tf, tf)
        return (pltpu.make_async_copy(wgu_hbm.at[e, :, pl.ds(c0, tf)], wg_buf.at[slot], sem.at[slot]),
                pltpu.make_async_copy(wgu_hbm.at[e, :, pl.ds(up_col0 + c0, tf)], wu_buf.at[slot], sem.at[slot]),
                pltpu.make_async_copy(wd_hbm.at[e, pl.ds(c0, tf), :], wd_buf.at[slot], sem.at[slot]))

    @pl.when((flag == ITEM_ACTIVE) & (r == 0))
    def _():
        slot = tslot_ref[i]

        @pl.when(i == 0)
        def _():
            for cp in weight_copies(te_ref[i], j, slot):
                cp.start()

        for cp in weight_copies(te_ref[i], j, slot):
            cp.wait()
        wg_s[...] = wg_buf[slot].astype(BF16)
        wu_s[...] = wu_buf[slot].astype(BF16)
        wd_s[...] = wd_buf[slot].astype(BF16)

        @pl.when(thn_ref[i] == 1)
        def _():
            for cp in weight_copies(tne_ref[i], tnj_ref[i], 1 - slot):
                cp.start()

    @pl.when(flag == ITEM_ACTIVE)
    def _():
        lo, hi = _unpack_bf16_pairs(x_ref[...])
        xb_s[:, :half] = lo
        xb_s[:, half:] = hi
        xb = xb_s[...]
        gate = jnp.dot(xb, wg_s[...], preferred_element_type=F32) + bg_ref[...]
        up = jnp.dot(xb, wu_s[...], preferred_element_type=F32) + bu_ref[...]
        gate = jnp.minimum(gate, SWIGLU_LIMIT)
        up = jnp.clip(up, -SWIGLU_LIMIT, SWIGLU_LIMIT)
        act = (up + 1.0) * gate * jax.nn.sigmoid(SWIGLU_ALPHA * gate)
        part = jnp.dot(act.astype(BF16), wd_s[...], preferred_element_type=F32)

        @pl.when(j == 0)
        def _():
            acc_s[r] = part + bd_ref[...]

        @pl.when(j > 0)
        def _():
            acc_s[r] = acc_s[r] + part

        @pl.when(j == last_j)
        def _():
            o_ref[...] = _pack_bf16_pairs(acc_s[r])

    @pl.when(flag == ITEM_ZERO_FILL)
    def _():
        o_ref[...] = jnp.zeros_like(o_ref)


def _lookup(table, idx):
    hot = idx[:, None] == jnp.arange(table.shape[0], dtype=jnp.int32)[None, :]
    return jnp.sum(jnp.where(hot, table[None, :].astype(jnp.int32), 0), axis=1).astype(jnp.int32)


def _moe_tables(counts, n_blocks, ft):
    ne = counts.shape[0]
    padded = (counts + MOE_BM - 1) // MOE_BM * MOE_BM
    pend = jnp.cumsum(padded)
    pstart = pend - padded
    blocks_per_expert = padded // MOE_BM
    first_block = pstart // MOE_BM
    n_active = (pend[-1] // MOE_BM).astype(jnp.int32)

    blocks = jnp.arange(n_blocks, dtype=jnp.int32)
    block_expert = jnp.minimum(jnp.sum(pend[None, :] <= (blocks * MOE_BM)[:, None], axis=1), ne - 1).astype(jnp.int32)
    off = blocks - _lookup(first_block, block_expert)
    is_start = (off % MOE_CHUNK_BLOCKS == 0) & (blocks < n_active)
    chunk_size = jnp.minimum(MOE_CHUNK_BLOCKS,
                             _lookup(blocks_per_expert, block_expert) - off // MOE_CHUNK_BLOCKS * MOE_CHUNK_BLOCKS)

    items = jnp.arange(n_blocks * ft, dtype=jnp.int32)
    active = items < ft * n_active
    src = jnp.minimum(items, ft * n_active - 1)
    cand = jnp.where(is_start[None, :] & (ft * blocks[None, :] <= src[:, None]), blocks[None, :], 0)
    cstart = jnp.max(cand, axis=1).astype(jnp.int32)
    m = jnp.maximum(_lookup(chunk_size, cstart), 1)
    local = src - ft * cstart
    tj = local // m
    tr = local % m
    tb = cstart + tr
    te = _lookup(block_expert, cstart)
    k = items - ft * n_active
    zero_fill = (~active) & (n_active + k < n_blocks)
    tout = jnp.where(active, jnp.where(tj == ft - 1, tb, cstart), jnp.minimum(n_active + k, n_blocks - 1))
    tflag = jnp.where(active, ITEM_ACTIVE, jnp.where(zero_fill, ITEM_ZERO_FILL, ITEM_IDLE))
    group_start = active & (tr == 0)
    tslot = (jnp.cumsum(group_start.astype(jnp.int32)) - 1) % 2
    nxt = items + m
    has_next = group_start & (nxt < ft * n_active)
    nxt = jnp.minimum(nxt, n_blocks * ft - 1)
    tne, tnj = jnp.take(te, nxt), jnp.take(tj, nxt)
    as_i32 = lambda v: v.astype(jnp.int32)
    return pstart, tuple(map(as_i32, (tb, tj, te, tr, tflag, tout, tslot, tne, tnj, has_next)))


def _moe(xs, tables, wgu, bgu, wd, bd):
    n_slots = xs.shape[0]
    ne, d, f2 = wgu.shape
    f = f2 // 2
    bm, tf = MOE_BM, min(MOE_TF, f)
    nb, ft = n_slots // bm, f // tf
    grid_spec = pltpu.PrefetchScalarGridSpec(
        num_scalar_prefetch=len(tables),
        grid=(nb * ft,),
        in_specs=[pl.BlockSpec((bm, d // 2), lambda i, tb, tj, te, *_: (tb[i], 0)),
                  pl.BlockSpec(memory_space=pl.ANY),
                  pl.BlockSpec((None, 1, tf), lambda i, tb, tj, te, *_: (te[i], 0, tj[i])),
                  pl.BlockSpec((None, 1, tf), lambda i, tb, tj, te, *_: (te[i], 0, ft + tj[i])),
                  pl.BlockSpec(memory_space=pl.ANY),
                  pl.BlockSpec((None, 1, d), lambda i, tb, tj, te, *_: (te[i], 0, 0))],
        out_specs=pl.BlockSpec((bm, d // 2), lambda i, tb, tj, te, tr, tg, to, *_: (to[i], 0)),
        scratch_shapes=[pltpu.VMEM((d, tf), BF16), pltpu.VMEM((d, tf), BF16), pltpu.VMEM((tf, d), BF16),
                        pltpu.VMEM((bm, d), BF16), pltpu.VMEM((MOE_CHUNK_BLOCKS, bm, d), F32),
                        pltpu.VMEM((2, d, tf), F32), pltpu.VMEM((2, d, tf), F32), pltpu.VMEM((2, tf, d), F32),
                        pltpu.SemaphoreType.DMA((2,))],
    )
    return pl.pallas_call(
        functools.partial(_moe_kernel, last_j=ft - 1),
        grid_spec=grid_spec,
        out_shape=jax.ShapeDtypeStruct((n_slots, d // 2), jnp.uint32),
        compiler_params=_params("arbitrary"),
        name="moe",
    )(*tables, xs, wgu, bgu, bgu, wd, bd)


def _final_kernel(slot_ref, h_ref, gate_ref, mod_ref, fw_ref, y_hbm, o_ref, ybuf, sem):
    tc, d = h_ref.shape
    half = d // 2

    def issue(grp, carry):
        for q in range(FINAL_ISSUE_UNROLL):
            tk = FINAL_ISSUE_UNROLL * grp + q
            _row_copy(y_hbm, slot_ref[0, 0, tk], ybuf, tk, sem).start(priority=q % 2)
        return carry

    lax.fori_loop(0, tc * TOP_K // FINAL_ISSUE_UNROLL, issue, 0)
    pltpu.make_async_copy(y_hbm.at[pl.ds(0, tc * TOP_K)], ybuf, sem).wait()

    ffn_lo = jnp.zeros((tc, half), F32)
    ffn_hi = jnp.zeros((tc, half), F32)
    for k in range(TOP_K):
        w = ybuf[k * tc:(k + 1) * tc, :]
        g = gate_ref[:, k:k + 1]
        ffn_lo = ffn_lo + g * lax.bitcast_convert_type(w << 16, F32)
        ffn_hi = ffn_hi + g * lax.bitcast_convert_type(w & jnp.uint32(0xFFFF0000), F32)
    h2_lo = h_ref[:, :half] + mod_ref[0, 5:6, :half] * ffn_lo
    h2_hi = h_ref[:, half:] + mod_ref[0, 5:6, half:] * ffn_hi
    ms = (jnp.sum(h2_lo * h2_lo, axis=-1, keepdims=True) + jnp.sum(h2_hi * h2_hi, axis=-1, keepdims=True)) / d
    inv = lax.rsqrt(ms + EPS)
    o_ref[:, :half] = (h2_lo * inv * fw_ref[:, :half]).astype(o_ref.dtype)
    o_ref[:, half:] = (h2_hi * inv * fw_ref[:, half:]).astype(o_ref.dtype)


def _final(h1, gates_tk, slot_kt, mod3, final_norm_w, y_slots, seq, out_dtype):
    t, d = h1.shape
    tc = min(FINAL_TC, seq)
    tiles_per_seq = seq // tc
    slots3 = slot_kt.reshape(TOP_K, t // tc, tc).transpose(1, 0, 2).reshape(t // tc, 1, TOP_K * tc)
    return pl.pallas_call(
        _final_kernel,
        grid=(t // tc,),
        in_specs=[pl.BlockSpec((1, 1, TOP_K * tc), lambda i: (i, 0, 0), memory_space=pltpu.SMEM),
                  pl.BlockSpec((tc, d), lambda i: (i, 0)),
                  pl.BlockSpec((tc, gates_tk.shape[1]), lambda i: (i, 0)),
                  pl.BlockSpec((1, N_MOD, d), lambda i: (i // tiles_per_seq, 0, 0)),
                  pl.BlockSpec((1, d), lambda i: (0, 0)),
                  pl.BlockSpec(memory_space=pl.ANY)],
        out_specs=pl.BlockSpec((tc, d), lambda i: (i, 0)),
        out_shape=jax.ShapeDtypeStruct((t, d), out_dtype),
        scratch_shapes=[pltpu.VMEM((TOP_K * tc, d // 2), jnp.uint32), pltpu.SemaphoreType.DMA(())],
        compiler_params=_params("arbitrary"),
        name="final",
    )(slots3, h1, gates_tk, mod3, final_norm_w.astype(F32).reshape(1, d), y_slots)


def kernel(x, c, w_ada, b_ada, w_in, pool_w, pool_scale, conv_w, conv_b, dt_bias, a_log, d_skip, ssd_norm_w,
           w_branch_pool, w_branch_ssd, w_out, w_router, b_router, w_gate_up, b_gate_up, w_down, b_down,
           final_norm_w):
    bsz, seq, d = x.shape
    depth = w_ada.shape[0]
    t = bsz * seq
    inner = ssd_norm_w.shape[1]
    heads = dt_bias.shape[1]
    bc = SSD_GROUPS * SSD_STATE
    ne = w_router.shape[2]
    assert depth == 1, "the final RMSNorm is fused into the last kernel of the single layer"
    assert SEQ_TILE == 2 * LANES
    assert seq % SEQ_TILE == 0 and heads <= LANES and (7 * d) % bc == 0 and (4 * d) % inner == 0
    assert inner // SSD_GROUPS % LANES == 0 and d % (len(POOL_WINDOWS) * LANES) == 0

    s1 = d
    s2 = s1 + inner
    s3 = s2 + inner + 2 * bc
    s4 = s3 + heads
    s5 = s4 + d
    blk = {"z": 0, "xs": 1, "p": (2 * inner) // d, "B": (2 * inner + 3 * d) // bc, "C": (2 * inner + 3 * d) // bc + 1}
    gp_off, gs_off = 2 * inner + d, 2 * inner + 2 * d

    h = x.astype(F32).reshape(t, d)
    for layer in range(depth):
        wl = w_in[layer]
        w_main = jnp.concatenate([wl[:, s1:s2], wl[:, s2:s2 + inner], wl[:, :s1], wl[:, s4:s5], wl[:, s5:],
                                  wl[:, s2 + inner:s2 + inner + bc], wl[:, s2 + inner + bc:s3]], axis=1).astype(BF16)
        w_dt = jnp.pad(wl[:, s3:s4].astype(F32), ((0, 0), (0, LANES - heads)))

        mod3 = _ada(c, w_ada[layer], b_ada[layer]).reshape(bsz, N_MOD, d)
        proj, dt_raw = _inproj(h, mod3, w_main, w_dt, seq)
        ypool = _pool(proj, blk["p"], pool_w[layer].astype(BF16), pool_scale[layer], seq, d)
        yssd = _ssd(proj, dt_raw, blk, conv_w[layer], conv_b[layer], dt_bias[layer], a_log[layer], d_skip[layer],
                    ssd_norm_w[layer], seq)
        merged = _merge(ypool, yssd, proj, gp_off, gs_off, w_branch_pool[layer].astype(BF16),
                        w_branch_ssd[layer].astype(BF16))
        h1, u2, idx_kt, gate_kt, rank_kt, counts = _outproj(merged, w_out[layer].astype(BF16), h, mod3,
                                                            w_router[layer], b_router[layer], seq)

        n_blocks = (t * TOP_K) // MOE_BM + ne
        ft = w_down.shape[2] // min(MOE_TF, w_down.shape[2])
        pstart, tables = _moe_tables(counts[:, 0], n_blocks, ft)
        eids = jnp.arange(ne, dtype=jnp.int32)[:, None, None]
        slot_kt = (jnp.sum(jnp.where(idx_kt[:TOP_K][None] == eids, pstart[:, None, None], 0), axis=0)
                   + rank_kt[:TOP_K]).astype(jnp.int32)

        xs = _dispatch(u2, slot_kt.T, n_blocks * MOE_BM)
        y_slots = _moe(xs, tables, w_gate_up[layer].astype(F32), b_gate_up[layer].astype(F32)[:, None, :],
                       w_down[layer].astype(F32), b_down[layer].astype(F32)[:, None, :])
        h = _final(h1, gate_kt.T, slot_kt, mod3, final_norm_w, y_slots, seq, x.dtype)
    return h.reshape(bsz, seq, d)
```

```python
import functools

import jax
import jax.numpy as jnp
from jax import lax
from jax.experimental import pallas as pl
from jax.experimental.pallas import tpu as pltpu

F32 = jnp.float32
BF16 = jnp.bfloat16
HIGHEST = lax.Precision.HIGHEST

EPS = 1e-6
POOL_WINDOWS = (2, 4, 8, 16)
SSD_GROUPS = 8
SSD_STATE = 128
SSD_CONV = 4
SSD_HEAD_DIM = 64
TOP_K = 4
SWIGLU_LIMIT = 7.0
SWIGLU_ALPHA = 1.702
N_MOD = 6

LOG2_E = 1.4426950408889634
LANES = 128
CONV_HALO = 8
V7X_VMEM_LIMIT = 56 * 1024 * 1024

SEQ_TILE = 256
INPROJ_TM, INPROJ_TN = 1024, 1024
MERGE_TM, MERGE_TN = 512, 512
OUT_TM = 512
MOE_BM, MOE_TF = 512, 512
MOE_CHUNK_BLOCKS = 2
DISPATCH_TS = 256
FINAL_TC = 256
FINAL_ISSUE_UNROLL = 8


def _params(*sem):
    return pltpu.CompilerParams(dimension_semantics=sem, vmem_limit_bytes=V7X_VMEM_LIMIT)


def _silu(v):
    return v * jax.nn.sigmoid(v)


def _pack_bf16_pairs(v):
    half = v.shape[1] // 2
    lo = lax.bitcast_convert_type(v[:, :half].astype(BF16).astype(F32), jnp.uint32)
    hi = lax.bitcast_convert_type(v[:, half:].astype(BF16).astype(F32), jnp.uint32)
    return (hi & jnp.uint32(0xFFFF0000)) | (lo >> 16)


def _unpack_bf16_pairs(w):
    lo = lax.bitcast_convert_type(w << 16, F32).astype(BF16)
    hi = lax.bitcast_convert_type(w & jnp.uint32(0xFFFF0000), F32).astype(BF16)
    return lo, hi


def _ada_kernel(cb_ref, w_ref, b_ref, o_ref):
    nb, tn = cb_ref.shape[0], w_ref.shape[1]
    for b in range(nb):
        ca = _silu(cb_ref[b])
        cols = [jnp.sum(w_ref[:, j * LANES:(j + 1) * LANES] * ca, axis=0, keepdims=True)
                for j in range(tn // LANES)]
        o_ref[b:b + 1, :] = jnp.concatenate(cols, axis=1) + b_ref[...]


def _ada(c, w_ada, b_ada):
    nb, k = c.shape
    n = w_ada.shape[1]
    tn = 1024 if n % 1024 == 0 else n
    cb = jnp.broadcast_to(c.astype(F32)[:, :, None], (nb, k, LANES))
    return pl.pallas_call(
        _ada_kernel,
        grid=(n // tn,),
        in_specs=[pl.BlockSpec((nb, k, LANES), lambda j: (0, 0, 0)),
                  pl.BlockSpec((k, tn), lambda j: (0, j)),
                  pl.BlockSpec((1, tn), lambda j: (0, j))],
        out_specs=pl.BlockSpec((nb, tn), lambda j: (0, j)),
        out_shape=jax.ShapeDtypeStruct((nb, n), F32),
        compiler_params=_params("arbitrary"),
        name="ada",
    )(cb, w_ada, b_ada.reshape(1, n))


def _inproj_kernel(x_ref, mod_ref, w_ref, wdt_ref, o_ref, dt_ref, u_s, *, rows_per_chunk):
    j = pl.program_id(1)
    tm = x_ref.shape[0]

    @pl.when(j == 0)
    def _():
        sh = mod_ref[0, 0:1, :]
        sc = mod_ref[0, 1:2, :]

        def body(r, carry):
            rows = pl.ds(pl.multiple_of(r * rows_per_chunk, rows_per_chunk), rows_per_chunk)
            xv = x_ref[rows, :]
            ms = jnp.mean(xv * xv, axis=-1, keepdims=True)
            u = xv * lax.rsqrt(ms + EPS) * (1.0 + sc) + sh
            u_s[rows, :] = u.astype(BF16)
            dt_ref[rows, :] = jnp.dot(u, wdt_ref[...], precision=HIGHEST, preferred_element_type=F32)
            return carry

        lax.fori_loop(0, tm // rows_per_chunk, body, 0)

    o_ref[...] = lax.dot_general(u_s[...], w_ref[...], (((1,), (1,)), ((), ())),
                                 preferred_element_type=F32).astype(o_ref.dtype)


def _inproj(x2, mod3, w_main_t, w_dt, seq):
    t, d = x2.shape
    n = w_main_t.shape[0]
    tm = min(INPROJ_TM, seq)
    tn = INPROJ_TN
    tiles_per_seq = seq // tm
    return pl.pallas_call(
        functools.partial(_inproj_kernel, rows_per_chunk=min(128, tm)),
        grid=(t // tm, n // tn),
        in_specs=[pl.BlockSpec((tm, d), lambda i, j: (i, 0)),
                  pl.BlockSpec((1, N_MOD, d), lambda i, j: (i // tiles_per_seq, 0, 0)),
                  pl.BlockSpec((tn, d), lambda i, j: (j, 0)),
                  pl.BlockSpec((d, LANES), lambda i, j: (0, 0))],
        out_specs=[pl.BlockSpec((tm, tn), lambda i, j: (i, j)),
                   pl.BlockSpec((tm, LANES), lambda i, j: (i, 0))],
        out_shape=[jax.ShapeDtypeStruct((t, n), BF16), jax.ShapeDtypeStruct((t, LANES), F32)],
        scratch_shapes=[pltpu.VMEM((tm, d), BF16)],
        compiler_params=_params("arbitrary", "arbitrary"),
        name="inproj",
    )(x2, mod3, w_main_t, w_dt)


def _pool_kernel(p_ref, pw_ref, ps_ref, o_ref, prev_s, *, tiles_per_seq):
    i = pl.program_id(0)
    tl, d = p_ref.shape
    gd = d // len(POOL_WINDOWS)
    it = i % tiles_per_seq

    @pl.when(it == 0)
    def _():
        prev_s[...] = jnp.zeros_like(prev_s)

    row = lax.broadcasted_iota(jnp.int32, (tl, 2 * tl), 0)
    col = lax.broadcasted_iota(jnp.int32, (tl, 2 * tl), 1)
    pos = (it * tl + lax.broadcasted_iota(jnp.int32, (tl, 1), 0) + 1).astype(F32)
    for g, w in enumerate(POOL_WINDOWS):
        sl = slice(g * gd, (g + 1) * gd)
        cur = p_ref[:, sl]
        ext = jnp.concatenate([prev_s[:, sl], cur], axis=0)
        band = ((col <= row + tl) & (col > row + tl - w)).astype(BF16)
        win_sum = jnp.dot(band, ext, preferred_element_type=F32)
        mean = win_sum / jnp.minimum(pos, float(w))
        dlt = (mean - cur.astype(F32)).astype(BF16)
        mixed = jnp.dot(dlt, pw_ref[g], preferred_element_type=F32)
        o_ref[:, sl] = (mixed * ps_ref[:, sl]).astype(o_ref.dtype)
    prev_s[...] = p_ref[...]


def _pool(proj, p_blk, pool_w_bf, pool_scale, seq, d):
    t = proj.shape[0]
    tl = SEQ_TILE
    g, gd = pool_w_bf.shape[0], pool_w_bf.shape[1]
    return pl.pallas_call(
        functools.partial(_pool_kernel, tiles_per_seq=seq // tl),
        grid=(t // tl,),
        in_specs=[pl.BlockSpec((tl, d), lambda i: (i, p_blk)),
                  pl.BlockSpec((g, gd, gd), lambda i: (0, 0, 0)),
                  pl.BlockSpec((1, d), lambda i: (0, 0))],
        out_specs=pl.BlockSpec((tl, d), lambda i: (i, 0)),
        out_shape=jax.ShapeDtypeStruct((t, d), BF16),
        scratch_shapes=[pltpu.VMEM((tl, d), BF16)],
        compiler_params=_params("arbitrary"),
        name="pool",
    )(proj, pool_w_bf, pool_scale.reshape(1, d))


def _ssd_kernel(z_ref, xs_ref, bm_ref, cm_ref, dt_ref,
                cwx_ref, cwb_ref, cwc_ref, cbx_ref, cbb_ref, cbc_ref,
                dtb_ref, alog_ref, dsk_ref, nw_ref, exp_ref,
                o_ref,
                extx_s, extb_s, extc_s, state_s, xdt_s, y_s, cb_s, cs_s, cst_s, xc_s, bc_s, cc_s,
                *, tiles_per_seq, heads):
    i = pl.program_id(0)
    tl, inner = xs_ref.shape
    gw = inner // SSD_GROUPS
    pairs_per_group = gw // LANES
    n_pairs = inner // LANES

    @pl.when(i % tiles_per_seq == 0)
    def _():
        extx_s[0:CONV_HALO, :] = jnp.zeros((CONV_HALO, inner), F32)
        extb_s[0:CONV_HALO, :] = jnp.zeros((CONV_HALO, extb_s.shape[1]), F32)
        extc_s[0:CONV_HALO, :] = jnp.zeros((CONV_HALO, extc_s.shape[1]), F32)
        state_s[...] = jnp.zeros_like(state_s)

    def conv_silu(ext_ref, src_ref, w_ref, b_ref, dst_ref, width, cw):
        for c0 in range(0, width, cw):
            cs = slice(c0, c0 + cw)
            ext_ref[CONV_HALO:CONV_HALO + tl, cs] = src_ref[:, cs].astype(F32)
            acc = b_ref[:, cs] + w_ref[SSD_CONV - 1:SSD_CONV, cs] * ext_ref[CONV_HALO:CONV_HALO + tl, cs]
            for k in range(1, SSD_CONV):
                acc = acc + w_ref[SSD_CONV - 1 - k:SSD_CONV - k, cs] * ext_ref[pl.ds(CONV_HALO - k, tl), cs]
            dst_ref[:, cs] = _silu(acc)
            ext_ref[0:CONV_HALO, cs] = ext_ref[tl:tl + CONV_HALO, cs]

    conv_silu(extx_s, xs_ref, cwx_ref, cbx_ref, xc_s, inner, LANES)
    conv_silu(extb_s, bm_ref, cwb_ref, cbb_ref, bc_s, bm_ref.shape[1], LANES)
    conv_silu(extc_s, cm_ref, cwc_ref, cbc_ref, cc_s, cm_ref.shape[1], LANES)

    dtv = dt_ref[...] + dtb_ref[...]
    dt = jnp.maximum(dtv, 0.0) + jnp.log1p(jnp.exp(-jnp.abs(dtv)))
    a = -jnp.exp(alog_ref[...])
    da = dt * a
    ri = lax.broadcasted_iota(jnp.int32, (tl, tl), 0)
    ci = lax.broadcasted_iota(jnp.int32, (tl, tl), 1)
    causal = ri >= ci
    cs = jnp.dot(causal.astype(F32), da, precision=HIGHEST, preferred_element_type=F32)
    cs2 = cs * LOG2_E
    cs_s[...] = cs2
    cst_s[...] = cs2.T
    last = cs[tl - 1:tl, :]
    exp_m = exp_ref[...]
    dt_x = jnp.dot(dt.astype(BF16), exp_m, preferred_element_type=F32)
    ecs_x = jnp.dot(jnp.exp(cs).astype(BF16), exp_m, preferred_element_type=F32)
    dte_x = jnp.dot(jnp.exp(last - cs).astype(BF16), exp_m, preferred_element_type=F32)

    for g in range(SSD_GROUPS):
        gs = slice(g * gw, (g + 1) * gw)
        ns = slice(g * SSD_STATE, (g + 1) * SSD_STATE)
        xg = xc_s[:, gs]
        xdt = xg * dt_x[:, gs]
        xdt_bf = xdt.astype(BF16)
        xd_bf = (xdt * dte_x[:, gs]).astype(BF16)
        bg = bc_s[:, ns]
        cg = cc_s[:, ns].astype(BF16)
        cb_s[g] = lax.dot_general(cg, bg.astype(BF16), (((1,), (1,)), ((), ())), preferred_element_type=F32)
        s_old = state_s[g]
        y_off = jnp.dot(cg, s_old.astype(BF16), preferred_element_type=F32) * ecs_x[:, gs]
        state_s[g] = (s_old * ecs_x[tl - 1:tl, gs]
                      + jnp.dot(bg.T.astype(BF16), xd_bf, preferred_element_type=F32))
        y0 = y_off + dsk_ref[:, gs] * xg
        for q in range(pairs_per_group):
            qs = slice(q * LANES, (q + 1) * LANES)
            xdt_s[g * pairs_per_group + q] = xdt_bf[:, qs]
            y_s[g * pairs_per_group + q] = y0[:, qs]

    lane_h = lax.broadcasted_iota(jnp.int32, (tl, LANES), 1)
    low_half = lane_h < SSD_HEAD_DIM

    half = tl // 2
    causal_top = (lax.broadcasted_iota(jnp.int32, (half, half), 0)
                  >= lax.broadcasted_iota(jnp.int32, (half, half), 1))
    causal_bot = (lax.broadcasted_iota(jnp.int32, (half, tl), 0) + half
                  >= lax.broadcasted_iota(jnp.int32, (half, tl), 1))
    lane_hh = lax.broadcasted_iota(jnp.int32, (half, LANES), 1)
    low_hh = lane_hh < SSD_HEAD_DIM

    def pair_body(hp, carry):
        g = hp // pairs_per_group
        cb_top = cb_s[g, 0:half, 0:half]
        cb_bot = cb_s[g, half:tl, :]
        x_top = xdt_s[hp, 0:half, :]
        x_all = xdt_s[hp]
        cs_top = cs_s[0:half, :]
        cs_bot = cs_s[half:tl, :]
        tops, bots = [], []
        for e in range(2):
            h = 2 * hp + e
            head_lane = jnp.full((half, LANES), h, jnp.int32)
            col_top = jnp.take_along_axis(cs_top, head_lane, axis=1)
            col_bot = jnp.take_along_axis(cs_bot, head_lane, axis=1)
            col_bot = jnp.concatenate([col_bot] * (tl // LANES), axis=1)
            row_all = cst_s[pl.ds(h, 1), :]
            row_top = row_all[:, 0:half]
            m_top = jnp.exp2(jnp.where(causal_top, col_top - row_top, -jnp.inf)) * cb_top
            m_bot = jnp.exp2(jnp.where(causal_bot, col_bot - row_all, -jnp.inf)) * cb_bot
            tops.append(jnp.dot(m_top.astype(BF16), x_top, preferred_element_type=F32))
            bots.append(jnp.dot(m_bot.astype(BF16), x_all, preferred_element_type=F32))
        y_s[hp, 0:half, :] = y_s[hp, 0:half, :] + jnp.where(low_hh, tops[0], tops[1])
        y_s[hp, half:tl, :] = y_s[hp, half:tl, :] + jnp.where(low_hh, bots[0], bots[1])
        return carry

    lax.fori_loop(0, n_pairs, pair_body, 0, unroll=16)

    for g in range(SSD_GROUPS):
        gs = slice(g * gw, (g + 1) * gw)
        yg = jnp.concatenate([y_s[g * pairs_per_group + q] for q in range(pairs_per_group)], axis=1)
        yg = yg * _silu(z_ref[:, gs].astype(F32))
        ms = jnp.mean(yg * yg, axis=-1, keepdims=True)
        o_ref[:, gs] = (yg * lax.rsqrt(ms + EPS) * nw_ref[:, gs]).astype(o_ref.dtype)


def _ssd(proj, dt_raw, blk, conv_w, conv_b, dt_bias, a_log, d_skip, ssd_norm_w, seq):
    t = proj.shape[0]
    tl = SEQ_TILE
    heads = dt_bias.shape[0]
    inner = ssd_norm_w.shape[0]
    bc = SSD_GROUPS * SSD_STATE
    gw = inner // SSD_GROUPS
    pad = LANES - heads
    row = lambda v: v.astype(F32).reshape(1, -1)
    dtb = jnp.pad(row(dt_bias), ((0, 0), (0, pad)))
    alog = jnp.pad(row(a_log), ((0, 0), (0, pad)))
    dsk = jnp.repeat(d_skip.astype(F32), SSD_HEAD_DIM).reshape(1, inner)
    expand = (lax.broadcasted_iota(jnp.int32, (LANES, inner), 1) // SSD_HEAD_DIM
              == lax.broadcasted_iota(jnp.int32, (LANES, inner), 0)).astype(BF16)
    cw = conv_w.astype(F32)
    cbias = row(conv_b)
    full = lambda shape: pl.BlockSpec(shape, lambda i: tuple(0 for _ in shape))
    return pl.pallas_call(
        functools.partial(_ssd_kernel, tiles_per_seq=seq // tl, heads=heads),
        grid=(t // tl,),
        in_specs=[pl.BlockSpec((tl, inner), lambda i: (i, blk["z"])),
                  pl.BlockSpec((tl, inner), lambda i: (i, blk["xs"])),
                  pl.BlockSpec((tl, bc), lambda i: (i, blk["B"])),
                  pl.BlockSpec((tl, bc), lambda i: (i, blk["C"])),
                  pl.BlockSpec((tl, LANES), lambda i: (i, 0)),
                  full((SSD_CONV, inner)), full((SSD_CONV, bc)), full((SSD_CONV, bc)),
                  full((1, inner)), full((1, bc)), full((1, bc)),
                  full((1, LANES)), full((1, LANES)), full((1, inner)), full((1, inner)),
                  full((LANES, inner))],
        out_specs=pl.BlockSpec((tl, inner), lambda i: (i, 0)),
        out_shape=jax.ShapeDtypeStruct((t, inner), BF16),
        scratch_shapes=[pltpu.VMEM((tl + CONV_HALO, inner), F32),
                        pltpu.VMEM((tl + CONV_HALO, bc), F32),
                        pltpu.VMEM((tl + CONV_HALO, bc), F32),
                        pltpu.VMEM((SSD_GROUPS, SSD_STATE, gw), F32),
                        pltpu.VMEM((inner // LANES, tl, LANES), BF16),
                        pltpu.VMEM((inner // LANES, tl, LANES), F32),
                        pltpu.VMEM((SSD_GROUPS, tl, tl), F32),
                        pltpu.VMEM((tl, LANES), F32),
                        pltpu.VMEM((LANES, tl), F32),
                        pltpu.VMEM((tl, inner), F32),
                        pltpu.VMEM((tl, bc), F32),
                        pltpu.VMEM((tl, bc), F32)],
        compiler_params=_params("arbitrary"),
        name="ssd",
    )(proj, proj, proj, proj, dt_raw,
      cw[:, :inner], cw[:, inner:inner + bc], cw[:, inner + bc:],
      cbias[:, :inner], cbias[:, inner:inner + bc], cbias[:, inner + bc:],
      dtb, alog, dsk, row(ssd_norm_w), expand)


def _merge_kernel(yp_ref, ys_ref, gp_ref, gs_ref, wp_ref, ws_ref, o_ref):
    a = jnp.dot(yp_ref[...], wp_ref[...], preferred_element_type=F32)
    b = jnp.dot(ys_ref[...], ws_ref[...], preferred_element_type=F32)
    o_ref[...] = (jax.nn.sigmoid(gp_ref[...].astype(F32)) * a
                  + jax.nn.sigmoid(gs_ref[...].astype(F32)) * b).astype(o_ref.dtype)


def _merge(ypool, yssd, proj, gp_off, gs_off, wbp, wbs):
    t, d = ypool.shape
    inner = yssd.shape[1]
    tm, tn = min(MERGE_TM, t), MERGE_TN
    return pl.pallas_call(
        _merge_kernel,
        grid=(t // tm, d // tn),
        in_specs=[pl.BlockSpec((tm, d), lambda i, j: (i, 0)),
                  pl.BlockSpec((tm, inner), lambda i, j: (i, 0)),
                  pl.BlockSpec((tm, tn), lambda i, j: (i, gp_off // tn + j)),
                  pl.BlockSpec((tm, tn), lambda i, j: (i, gs_off // tn + j)),
                  pl.BlockSpec((d, tn), lambda i, j: (0, j)),
                  pl.BlockSpec((inner, tn), lambda i, j: (0, j))],
        out_specs=pl.BlockSpec((tm, tn), lambda i, j: (i, j)),
        out_shape=jax.ShapeDtypeStruct((t, d), BF16),
        compiler_params=_params("arbitrary", "arbitrary"),
        name="merge",
    )(ypool, yssd, proj, proj, wbp, wbs)


def _out_kernel(m_ref, wo_ref, x_ref, mod_ref, wr_ref, br_ref,
                h_ref, u_ref, idx_ref, gate_ref, rank_ref, cnt_ref, carry_s):
    i = pl.program_id(0)
    tm = m_ref.shape[0]
    ne = wr_ref.shape[0]

    @pl.when(i == 0)
    def _():
        carry_s[...] = jnp.zeros_like(carry_s)

    mix = jnp.dot(m_ref[...], wo_ref[...], preferred_element_type=F32)
    h1 = x_ref[...] + mod_ref[0, 2:3, :] * mix
    h_ref[...] = h1
    ms = jnp.mean(h1 * h1, axis=-1, keepdims=True)
    u = h1 * lax.rsqrt(ms + EPS) * (1.0 + mod_ref[0, 4:5, :]) + mod_ref[0, 3:4, :]
    u_ref[...] = _pack_bf16_pairs(u)
    logits = lax.dot_general(wr_ref[...], u, (((1,), (1,)), ((), ())), precision=HIGHEST,
                             preferred_element_type=F32) + br_ref[...]
    eidx = lax.broadcasted_iota(jnp.int32, (ne, tm), 0)
    work = logits
    vals, idxs, hots = [], [], []
    for _ in range(TOP_K):
        mx = jnp.max(work, axis=0, keepdims=True)
        sel = jnp.min(jnp.where(work == mx, eidx, ne), axis=0, keepdims=True)
        hot = eidx == sel
        vals.append(mx)
        idxs.append(sel)
        hots.append(hot)
        work = jnp.where(hot, -jnp.inf, work)
    exps = [jnp.exp(v - vals[0]) for v in vals]
    den = exps[0]
    for e in exps[1:]:
        den = den + e
    cnt = hots[0].astype(F32)
    for hot in hots[1:]:
        cnt = cnt + hot.astype(F32)
    ti = lax.broadcasted_iota(jnp.int32, (tm, tm), 0)
    tj = lax.broadcasted_iota(jnp.int32, (tm, tm), 1)
    before = (ti < tj).astype(BF16)
    prefix = jnp.dot(cnt.astype(BF16), before, preferred_element_type=F32)
    base = carry_s[:, 0:1] + prefix
    pad_rows = idx_ref.shape[0] - TOP_K
    ranks = [jnp.sum(jnp.where(hot, base, 0.0), axis=0, keepdims=True) for hot in hots]
    idx_ref[...] = jnp.concatenate(idxs + [jnp.zeros((pad_rows, tm), jnp.int32)], axis=0)
    gate_ref[...] = jnp.concatenate([e / den for e in exps] + [jnp.zeros((pad_rows, tm), F32)], axis=0)
    rank_ref[...] = jnp.concatenate([r.astype(jnp.int32) for r in ranks]
                                    + [jnp.zeros((pad_rows, tm), jnp.int32)], axis=0)
    carry_s[...] = carry_s[...] + jnp.sum(cnt, axis=1, keepdims=True)
    cnt_ref[...] = carry_s[...].astype(jnp.int32)


def _outproj(merged, wo, x2, mod3, w_router, b_router, seq):
    t, d = x2.shape
    ne = w_router.shape[1]
    tm = min(OUT_TM, seq)
    tiles_per_seq = seq // tm
    rows = 8
    return pl.pallas_call(
        _out_kernel,
        grid=(t // tm,),
        in_specs=[pl.BlockSpec((tm, d), lambda i: (i, 0)),
                  pl.BlockSpec((d, d), lambda i: (0, 0), pipeline_mode=pl.Buffered(1)),
                  pl.BlockSpec((tm, d), lambda i: (i, 0)),
                  pl.BlockSpec((1, N_MOD, d), lambda i: (i // tiles_per_seq, 0, 0)),
                  pl.BlockSpec((ne, d), lambda i: (0, 0)),
                  pl.BlockSpec((ne, 1), lambda i: (0, 0))],
        out_specs=[pl.BlockSpec((tm, d), lambda i: (i, 0)),
                   pl.BlockSpec((tm, d // 2), lambda i: (i, 0)),
                   pl.BlockSpec((rows, tm), lambda i: (0, i)),
                   pl.BlockSpec((rows, tm), lambda i: (0, i)),
                   pl.BlockSpec((rows, tm), lambda i: (0, i)),
                   pl.BlockSpec((ne, LANES), lambda i: (0, 0))],
        out_shape=[jax.ShapeDtypeStruct((t, d), F32), jax.ShapeDtypeStruct((t, d // 2), jnp.uint32),
                   jax.ShapeDtypeStruct((rows, t), jnp.int32), jax.ShapeDtypeStruct((rows, t), F32),
                   jax.ShapeDtypeStruct((rows, t), jnp.int32), jax.ShapeDtypeStruct((ne, LANES), jnp.int32)],
        scratch_shapes=[pltpu.VMEM((ne, LANES), F32)],
        compiler_params=_params("arbitrary"),
        name="outproj",
    )(merged, wo, x2, mod3, w_router.astype(F32).T, b_router.astype(F32).reshape(ne, 1))


def _row_copy(src_hbm, src_row, dst_ref, dst_row, sem):
    return pltpu.make_async_copy(src_hbm.at[pl.ds(src_row, 1)], dst_ref.at[pl.ds(dst_row, 1)], sem)


def _dispatch_kernel(slot_ref, u_ref, init_hbm, xs_hbm, sem):
    del init_hbm
    ts = u_ref.shape[0]

    def issue(tok, carry):
        for k in range(TOP_K):
            _row_copy(u_ref, tok, xs_hbm, slot_ref[0, 0, tok * TOP_K + k], sem).start(priority=k % 2)
        return carry

    lax.fori_loop(0, ts, issue, 0)
    rows = pl.ds(0, ts * TOP_K)
    pltpu.make_async_copy(xs_hbm.at[rows], xs_hbm.at[rows], sem).wait()


def _dispatch(u2, slot_tk, n_slots):
    t, d = u2.shape
    ts = min(DISPATCH_TS, t)
    slots3 = slot_tk.reshape(t // ts, 1, ts * TOP_K)
    init = jnp.zeros((n_slots, d), u2.dtype)
    return pl.pallas_call(
        _dispatch_kernel,
        grid=(t // ts,),
        in_specs=[pl.BlockSpec((1, 1, ts * TOP_K), lambda i: (i, 0, 0), memory_space=pltpu.SMEM),
                  pl.BlockSpec((ts, d), lambda i: (i, 0)),
                  pl.BlockSpec(memory_space=pl.ANY)],
        out_specs=pl.BlockSpec(memory_space=pl.ANY),
        out_shape=jax.ShapeDtypeStruct((n_slots, d), u2.dtype),
        scratch_shapes=[pltpu.SemaphoreType.DMA(())],
        input_output_aliases={2: 0},
        compiler_params=_params("arbitrary"),
        name="dispatch",
    )(slots3, u2, init)


ITEM_IDLE, ITEM_ACTIVE, ITEM_ZERO_FILL = 0, 1, 2


def _moe_kernel(tb_ref, tj_ref, te_ref, tr_ref, tflag_ref, tout_ref, tslot_ref, tne_ref, tnj_ref, thn_ref,
                x_ref, wgu_hbm, bg_ref, bu_ref, wd_hbm, bd_ref, o_ref,
                wg_s, wu_s, wd_s, xb_s, acc_s, wg_buf, wu_buf, wd_buf, sem, *, last_j):
    del tb_ref, tout_ref
    i = pl.program_id(0)
    flag = tflag_ref[i]
    r = tr_ref[i]
    j = tj_ref[i]
    half = x_ref.shape[1]
    tf = wg_s.shape[1]
    up_col0 = wgu_hbm.shape[2] // 2

    def weight_copies(e, jt, slot):
        c0 = pl.multiple_of(jt * tf, tf)
        return (pltpu.make_async_copy(wgu_hbm.at[e, :, pl.ds(c0, tf)], wg_buf.at[slot], sem.at[slot]),
                pltpu.make_async_copy(wgu_hbm.at[e, :, pl.ds(up_col0 + c0, tf)], wu_buf.at[slot], sem.at[slot]),
                pltpu.make_async_copy(wd_hbm.at[e, pl.ds(c0, tf), :], wd_buf.at[slot], sem.at[slot]))

    @pl.when((flag == ITEM_ACTIVE) & (r == 0))
    def _():
        slot = tslot_ref[i]

        @pl.when(i == 0)
        def _():
            for cp in weight_copies(te_ref[i], j, slot):
                cp.start()

        for cp in weight_copies(te_ref[i], j, slot):
            cp.wait()
        for s in range(2):
            @pl.when(slot == s)
            def _():
                wg_s[...] = wg_buf[s].astype(BF16)
                wu_s[...] = wu_buf[s].astype(BF16)
                wd_s[...] = wd_buf[s].astype(BF16)

        @pl.when(thn_ref[i] == 1)
        def _():
            for cp in weight_copies(tne_ref[i], tnj_ref[i], 1 - slot):
                cp.start()

    @pl.when((flag == ITEM_ACTIVE) & (j == 0))
    def _():
        lo, hi = _unpack_bf16_pairs(x_ref[...])
        xb_s[r, :, :half] = lo
        xb_s[r, :, half:] = hi

    @pl.when(flag == ITEM_ACTIVE)
    def _():
        xb = xb_s[r]
        gate = jnp.dot(xb, wg_s[...], preferred_element_type=F32) + bg_ref[...]
        up = jnp.dot(xb, wu_s[...], preferred_element_type=F32) + bu_ref[...]
        gate = jnp.minimum(gate, SWIGLU_LIMIT)
        up = jnp.clip(up, -SWIGLU_LIMIT, SWIGLU_LIMIT)
        act = (up + 1.0) * gate * jax.nn.sigmoid(SWIGLU_ALPHA * gate)
        part = jnp.dot(act.astype(BF16), wd_s[...], preferred_element_type=F32)

        @pl.when(j == 0)
        def _():
            acc_s[r] = part + bd_ref[...]

        @pl.when(j > 0)
        def _():
            acc_s[r] = acc_s[r] + part

        @pl.when(j == last_j)
        def _():
            o_ref[...] = _pack_bf16_pairs(acc_s[r])

    @pl.when(flag == ITEM_ZERO_FILL)
    def _():
        o_ref[...] = jnp.zeros_like(o_ref)


def _lookup(table, idx):
    hot = idx[:, None] == jnp.arange(table.shape[0], dtype=jnp.int32)[None, :]
    return jnp.sum(jnp.where(hot, table[None, :].astype(jnp.int32), 0), axis=1).astype(jnp.int32)


def _moe_tables(counts, n_blocks, ft):
    ne = counts.shape[0]
    padded = (counts + MOE_BM - 1) // MOE_BM * MOE_BM
    pend = jnp.cumsum(padded)
    pstart = pend - padded
    blocks_per_expert = padded // MOE_BM
    first_block = pstart // MOE_BM
    n_active = (pend[-1] // MOE_BM).astype(jnp.int32)

    blocks = jnp.arange(n_blocks, dtype=jnp.int32)
    block_expert = jnp.minimum(jnp.sum(pend[None, :] <= (blocks * MOE_BM)[:, None], axis=1), ne - 1).astype(jnp.int32)
    off = blocks - _lookup(first_block, block_expert)
    is_start = (off % MOE_CHUNK_BLOCKS == 0) & (blocks < n_active)
    chunk_size = jnp.minimum(MOE_CHUNK_BLOCKS,
                             _lookup(blocks_per_expert, block_expert) - off // MOE_CHUNK_BLOCKS * MOE_CHUNK_BLOCKS)

    items = jnp.arange(n_blocks * ft, dtype=jnp.int32)
    active = items < ft * n_active
    src = jnp.minimum(items, ft * n_active - 1)
    cand = jnp.where(is_start[None, :] & (ft * blocks[None, :] <= src[:, None]), blocks[None, :], 0)
    cstart = jnp.max(cand, axis=1).astype(jnp.int32)
    m = jnp.maximum(_lookup(chunk_size, cstart), 1)
    local = src - ft * cstart
    tj = local // m
    tr = local % m
    tb = cstart + tr
    te = _lookup(block_expert, cstart)
    k = items - ft * n_active
    zero_fill = (~active) & (n_active + k < n_blocks)
    tout = jnp.where(active, jnp.where(tj == ft - 1, tb, cstart), jnp.minimum(n_active + k, n_blocks - 1))
    tflag = jnp.where(active, ITEM_ACTIVE, jnp.where(zero_fill, ITEM_ZERO_FILL, ITEM_IDLE))
    group_start = active & (tr == 0)
    tslot = (jnp.cumsum(group_start.astype(jnp.int32)) - 1) % 2
    nxt = items + m
    has_next = group_start & (nxt < ft * n_active)
    nxt = jnp.minimum(nxt, n_blocks * ft - 1)
    tne, tnj = jnp.take(te, nxt), jnp.take(tj, nxt)
    as_i32 = lambda v: v.astype(jnp.int32)
    return pstart, tuple(map(as_i32, (tb, tj, te, tr, tflag, tout, tslot, tne, tnj, has_next)))


def _moe(xs, tables, wgu, bgu, wd, bd):
    n_slots = xs.shape[0]
    ne, d, f2 = wgu.shape
    f = f2 // 2
    bm, tf = MOE_BM, min(MOE_TF, f)
    nb, ft = n_slots // bm, f // tf
    grid_spec = pltpu.PrefetchScalarGridSpec(
        num_scalar_prefetch=len(tables),
        grid=(nb * ft,),
        in_specs=[pl.BlockSpec((bm, d // 2), lambda i, tb, tj, te, *_: (tb[i], 0)),
                  pl.BlockSpec(memory_space=pl.ANY),
                  pl.BlockSpec((None, 1, tf), lambda i, tb, tj, te, *_: (te[i], 0, tj[i])),
                  pl.BlockSpec((None, 1, tf), lambda i, tb, tj, te, *_: (te[i], 0, ft + tj[i])),
                  pl.BlockSpec(memory_space=pl.ANY),
                  pl.BlockSpec((None, 1, d), lambda i, tb, tj, te, *_: (te[i], 0, 0))],
        out_specs=pl.BlockSpec((bm, d // 2), lambda i, tb, tj, te, tr, tg, to, *_: (to[i], 0)),
        scratch_shapes=[pltpu.VMEM((d, tf), BF16), pltpu.VMEM((d, tf), BF16), pltpu.VMEM((tf, d), BF16),
                        pltpu.VMEM((MOE_CHUNK_BLOCKS, bm, d), BF16), pltpu.VMEM((MOE_CHUNK_BLOCKS, bm, d), F32),
                        pltpu.VMEM((2, d, tf), F32), pltpu.VMEM((2, d, tf), F32), pltpu.VMEM((2, tf, d), F32),
                        pltpu.SemaphoreType.DMA((2,))],
    )
    return pl.pallas_call(
        functools.partial(_moe_kernel, last_j=ft - 1),
        grid_spec=grid_spec,
        out_shape=jax.ShapeDtypeStruct((n_slots, d // 2), jnp.uint32),
        compiler_params=_params("arbitrary"),
        name="moe",
    )(*tables, xs, wgu, bgu, bgu, wd, bd)


def _final_kernel(slot_ref, h_ref, gate_ref, mod_ref, fw_ref, y_hbm, o_ref, ybuf, sem):
    tc, d = h_ref.shape
    half = d // 2

    def issue(grp, carry):
        for q in range(FINAL_ISSUE_UNROLL):
            tk = FINAL_ISSUE_UNROLL * grp + q
            _row_copy(y_hbm, slot_ref[0, 0, tk], ybuf, tk, sem).start(priority=q % 2)
        return carry

    lax.fori_loop(0, tc * TOP_K // FINAL_ISSUE_UNROLL, issue, 0)
    pltpu.make_async_copy(y_hbm.at[pl.ds(0, tc * TOP_K)], ybuf, sem).wait()

    ffn_lo = jnp.zeros((tc, half), F32)
    ffn_hi = jnp.zeros((tc, half), F32)
    for k in range(TOP_K):
        w = ybuf[k * tc:(k + 1) * tc, :]
        g = gate_ref[:, k:k + 1]
        ffn_lo = ffn_lo + g * lax.bitcast_convert_type(w << 16, F32)
        ffn_hi = ffn_hi + g * lax.bitcast_convert_type(w & jnp.uint32(0xFFFF0000), F32)
    h2_lo = h_ref[:, :half] + mod_ref[0, 5:6, :half] * ffn_lo
    h2_hi = h_ref[:, half:] + mod_ref[0, 5:6, half:] * ffn_hi
    ms = (jnp.sum(h2_lo * h2_lo, axis=-1, keepdims=True) + jnp.sum(h2_hi * h2_hi, axis=-1, keepdims=True)) / d
    inv = lax.rsqrt(ms + EPS)
    o_ref[:, :half] = (h2_lo * inv * fw_ref[:, :half]).astype(o_ref.dtype)
    o_ref[:, half:] = (h2_hi * inv * fw_ref[:, half:]).astype(o_ref.dtype)


def _final(h1, gates_tk, slot_kt, mod3, final_norm_w, y_slots, seq, out_dtype):
    t, d = h1.shape
    tc = min(FINAL_TC, seq)
    tiles_per_seq = seq // tc
    slots3 = slot_kt.reshape(TOP_K, t // tc, tc).transpose(1, 0, 2).reshape(t // tc, 1, TOP_K * tc)
    return pl.pallas_call(
        _final_kernel,
        grid=(t // tc,),
        in_specs=[pl.BlockSpec((1, 1, TOP_K * tc), lambda i: (i, 0, 0), memory_space=pltpu.SMEM),
                  pl.BlockSpec((tc, d), lambda i: (i, 0)),
                  pl.BlockSpec((tc, gates_tk.shape[1]), lambda i: (i, 0)),
                  pl.BlockSpec((1, N_MOD, d), lambda i: (i // tiles_per_seq, 0, 0)),
                  pl.BlockSpec((1, d), lambda i: (0, 0)),
                  pl.BlockSpec(memory_space=pl.ANY)],
        out_specs=pl.BlockSpec((tc, d), lambda i: (i, 0)),
        out_shape=jax.ShapeDtypeStruct((t, d), out_dtype),
        scratch_shapes=[pltpu.VMEM((TOP_K * tc, d // 2), jnp.uint32), pltpu.SemaphoreType.DMA(())],
        compiler_params=_params("arbitrary"),
        name="final",
    )(slots3, h1, gates_tk, mod3, final_norm_w.astype(F32).reshape(1, d), y_slots)


def kernel(x, c, w_ada, b_ada, w_in, pool_w, pool_scale, conv_w, conv_b, dt_bias, a_log, d_skip, ssd_norm_w,
           w_branch_pool, w_branch_ssd, w_out, w_router, b_router, w_gate_up, b_gate_up, w_down, b_down,
           final_norm_w):
    bsz, seq, d = x.shape
    depth = w_ada.shape[0]
    t = bsz * seq
    inner = ssd_norm_w.shape[1]
    heads = dt_bias.shape[1]
    bc = SSD_GROUPS * SSD_STATE
    ne = w_router.shape[2]
    assert depth == 1, "the final RMSNorm is fused into the last kernel of the single layer"
    assert SEQ_TILE == 2 * LANES
    assert seq % SEQ_TILE == 0 and heads <= LANES and (7 * d) % bc == 0 and (4 * d) % inner == 0
    assert inner // SSD_GROUPS % LANES == 0 and d % (len(POOL_WINDOWS) * LANES) == 0

    s1 = d
    s2 = s1 + inner
    s3 = s2 + inner + 2 * bc
    s4 = s3 + heads
    s5 = s4 + d
    blk = {"z": 0, "xs": 1, "p": (2 * inner) // d, "B": (2 * inner + 3 * d) // bc, "C": (2 * inner + 3 * d) // bc + 1}
    gp_off, gs_off = 2 * inner + d, 2 * inner + 2 * d

    h = x.astype(F32).reshape(t, d)
    for layer in range(depth):
        wl = w_in[layer]
        wt = jnp.swapaxes(wl, 0, 1)
        w_main_t = jnp.concatenate([wt[s1:s2], wt[s2:s2 + inner], wt[:s1], wt[s4:s5], wt[s5:],
                                    wt[s2 + inner:s2 + inner + bc], wt[s2 + inner + bc:s3]], axis=0).astype(BF16)
        w_dt = jnp.pad(wl[:, s3:s4].astype(F32), ((0, 0), (0, LANES - heads)))

        mod3 = _ada(c, w_ada[layer], b_ada[layer]).reshape(bsz, N_MOD, d)
        proj, dt_raw = _inproj(h, mod3, w_main_t, w_dt, seq)
        ypool = _pool(proj, blk["p"], pool_w[layer].astype(BF16), pool_scale[layer], seq, d)
        yssd = _ssd(proj, dt_raw, blk, conv_w[layer], conv_b[layer], dt_bias[layer], a_log[layer], d_skip[layer],
                    ssd_norm_w[layer], seq)
        merged = _merge(ypool, yssd, proj, gp_off, gs_off, w_branch_pool[layer].astype(BF16),
                        w_branch_ssd[layer].astype(BF16))
        h1, u2, idx_kt, gate_kt, rank_kt, counts = _outproj(merged, w_out[layer].astype(BF16), h, mod3,
                                                            w_router[layer], b_router[layer], seq)

        n_blocks = (t * TOP_K) // MOE_BM + ne
        ft = w_down.shape[2] // min(MOE_TF, w_down.shape[2])
        pstart, tables = _moe_tables(counts[:, 0], n_blocks, ft)
        eids = jnp.arange(ne, dtype=jnp.int32)[:, None, None]
        slot_kt = (jnp.sum(jnp.where(idx_kt[:TOP_K][None] == eids, pstart[:, None, None], 0), axis=0)
                   + rank_kt[:TOP_K]).astype(jnp.int32)

        xs = _dispatch(u2, slot_kt.T, n_blocks * MOE_BM)
        y_slots = _moe(xs, tables, w_gate_up[layer].astype(F32), b_gate_up[layer].astype(F32)[:, None, :],
                       w_down[layer].astype(F32), b_down[layer].astype(F32)[:, None, :])
        h = _final(h1, gate_kt.T, slot_kt, mod3, final_norm_w, y_slots, seq, x.dtype)
    return h.reshape(bsz, seq, d)
```

```python
import functools

import jax
import jax.numpy as jnp
from jax import lax
from jax.experimental import pallas as pl
from jax.experimental.pallas import tpu as pltpu

F32 = jnp.float32
BF16 = jnp.bfloat16
HIGHEST = lax.Precision.HIGHEST

EPS = 1e-6
POOL_WINDOWS = (2, 4, 8, 16)
SSD_GROUPS = 8
SSD_STATE = 128
SSD_CONV = 4
SSD_HEAD_DIM = 64
TOP_K = 4
SWIGLU_LIMIT = 7.0
SWIGLU_ALPHA = 1.702
N_MOD = 6

LOG2_E = 1.4426950408889634
LANES = 128
CONV_HALO = 8
V7X_VMEM_LIMIT = 56 * 1024 * 1024

SEQ_TILE = 256
INPROJ_TM, INPROJ_TN = 1024, 1024
MERGE_TM, MERGE_TN = 512, 512
OUT_TM = 512
MOE_BM, MOE_TF = 512, 512
MOE_CHUNK_BLOCKS = 2
DISPATCH_TS = 256
FINAL_TC = 256
FINAL_ISSUE_UNROLL = 8


def _params(*sem):
    return pltpu.CompilerParams(dimension_semantics=sem, vmem_limit_bytes=V7X_VMEM_LIMIT)


def _silu(v):
    return v * jax.nn.sigmoid(v)


def _pack_bf16_pairs(v):
    half = v.shape[1] // 2
    lo = lax.bitcast_convert_type(v[:, :half].astype(BF16).astype(F32), jnp.uint32)
    hi = lax.bitcast_convert_type(v[:, half:].astype(BF16).astype(F32), jnp.uint32)
    return (hi & jnp.uint32(0xFFFF0000)) | (lo >> 16)


def _unpack_bf16_pairs(w):
    lo = lax.bitcast_convert_type(w << 16, F32).astype(BF16)
    hi = lax.bitcast_convert_type(w & jnp.uint32(0xFFFF0000), F32).astype(BF16)
    return lo, hi


def _ada_kernel(cb_ref, w_ref, b_ref, o_ref):
    nb, tn = cb_ref.shape[0], w_ref.shape[1]
    for b in range(nb):
        ca = _silu(cb_ref[b])
        cols = [jnp.sum(w_ref[:, j * LANES:(j + 1) * LANES] * ca, axis=0, keepdims=True)
                for j in range(tn // LANES)]
        o_ref[b:b + 1, :] = jnp.concatenate(cols, axis=1) + b_ref[...]


def _ada(c, w_ada, b_ada):
    nb, k = c.shape
    n = w_ada.shape[1]
    tn = 1024 if n % 1024 == 0 else n
    cb = jnp.broadcast_to(c.astype(F32)[:, :, None], (nb, k, LANES))
    return pl.pallas_call(
        _ada_kernel,
        grid=(n // tn,),
        in_specs=[pl.BlockSpec((nb, k, LANES), lambda j: (0, 0, 0)),
                  pl.BlockSpec((k, tn), lambda j: (0, j)),
                  pl.BlockSpec((1, tn), lambda j: (0, j))],
        out_specs=pl.BlockSpec((nb, tn), lambda j: (0, j)),
        out_shape=jax.ShapeDtypeStruct((nb, n), F32),
        compiler_params=_params("arbitrary"),
        name="ada",
    )(cb, w_ada, b_ada.reshape(1, n))


def _inproj_kernel(x_ref, mod_ref, w_ref, wdt_hi_ref, wdt_lo_ref, o_ref, dt_ref, u_s, *, rows_per_chunk):
    j = pl.program_id(1)
    tm = x_ref.shape[0]

    @pl.when(j == 0)
    def _():
        sh = mod_ref[0, 0:1, :]
        sc = mod_ref[0, 1:2, :]

        def body(r, carry):
            rows = pl.ds(pl.multiple_of(r * rows_per_chunk, rows_per_chunk), rows_per_chunk)
            xv = x_ref[rows, :]
            ms = jnp.mean(xv * xv, axis=-1, keepdims=True)
            u = xv * lax.rsqrt(ms + EPS) * (1.0 + sc) + sh
            u_hi = u.astype(BF16)
            u_s[rows, :] = u_hi
            u_lo = (u - u_hi.astype(F32)).astype(BF16)
            dt_ref[rows, :] = (jnp.dot(u_hi, wdt_hi_ref[...], preferred_element_type=F32)
                               + (jnp.dot(u_hi, wdt_lo_ref[...], preferred_element_type=F32)
                                  + jnp.dot(u_lo, wdt_hi_ref[...], preferred_element_type=F32)))
            return carry

        lax.fori_loop(0, tm // rows_per_chunk, body, 0)

    o_ref[...] = jnp.dot(u_s[...], w_ref[...], preferred_element_type=F32).astype(o_ref.dtype)


def _inproj(x2, mod3, w_main, w_dt, seq):
    t, d = x2.shape
    w_dt_hi = w_dt.astype(BF16)
    w_dt_lo = (w_dt - w_dt_hi.astype(F32)).astype(BF16)
    n = w_main.shape[1]
    tm = min(INPROJ_TM, seq)
    tn = INPROJ_TN
    tiles_per_seq = seq // tm
    return pl.pallas_call(
        functools.partial(_inproj_kernel, rows_per_chunk=min(128, tm)),
        grid=(t // tm, n // tn),
        in_specs=[pl.BlockSpec((tm, d), lambda i, j: (i, 0)),
                  pl.BlockSpec((1, N_MOD, d), lambda i, j: (i // tiles_per_seq, 0, 0)),
                  pl.BlockSpec((d, tn), lambda i, j: (0, j)),
                  pl.BlockSpec((d, LANES), lambda i, j: (0, 0)),
                  pl.BlockSpec((d, LANES), lambda i, j: (0, 0))],
        out_specs=[pl.BlockSpec((tm, tn), lambda i, j: (i, j)),
                   pl.BlockSpec((tm, LANES), lambda i, j: (i, 0))],
        out_shape=[jax.ShapeDtypeStruct((t, n), BF16), jax.ShapeDtypeStruct((t, LANES), F32)],
        scratch_shapes=[pltpu.VMEM((tm, d), BF16)],
        compiler_params=_params("arbitrary", "arbitrary"),
        name="inproj",
    )(x2, mod3, w_main, w_dt_hi, w_dt_lo)


def _pool_kernel(p_ref, pw_ref, ps_ref, o_ref, prev_s, *, tiles_per_seq):
    i = pl.program_id(0)
    tl, d = p_ref.shape
    gd = d // len(POOL_WINDOWS)
    it = i % tiles_per_seq

    @pl.when(it == 0)
    def _():
        prev_s[...] = jnp.zeros_like(prev_s)

    row = lax.broadcasted_iota(jnp.int32, (tl, 2 * tl), 0)
    col = lax.broadcasted_iota(jnp.int32, (tl, 2 * tl), 1)
    pos = (it * tl + lax.broadcasted_iota(jnp.int32, (tl, 1), 0) + 1).astype(F32)
    for g, w in enumerate(POOL_WINDOWS):
        sl = slice(g * gd, (g + 1) * gd)
        cur = p_ref[:, sl]
        ext = jnp.concatenate([prev_s[:, sl], cur], axis=0)
        band = ((col <= row + tl) & (col > row + tl - w)).astype(BF16)
        win_sum = jnp.dot(band, ext, preferred_element_type=F32)
        mean = win_sum / jnp.minimum(pos, float(w))
        dlt = (mean - cur.astype(F32)).astype(BF16)
        mixed = jnp.dot(dlt, pw_ref[g], preferred_element_type=F32)
        o_ref[:, sl] = (mixed * ps_ref[:, sl]).astype(o_ref.dtype)
    prev_s[...] = p_ref[...]


def _pool(proj, p_blk, pool_w_bf, pool_scale, seq, d):
    t = proj.shape[0]
    tl = SEQ_TILE
    g, gd = pool_w_bf.shape[0], pool_w_bf.shape[1]
    return pl.pallas_call(
        functools.partial(_pool_kernel, tiles_per_seq=seq // tl),
        grid=(t // tl,),
        in_specs=[pl.BlockSpec((tl, d), lambda i: (i, p_blk)),
                  pl.BlockSpec((g, gd, gd), lambda i: (0, 0, 0)),
                  pl.BlockSpec((1, d), lambda i: (0, 0))],
        out_specs=pl.BlockSpec((tl, d), lambda i: (i, 0)),
        out_shape=jax.ShapeDtypeStruct((t, d), BF16),
        scratch_shapes=[pltpu.VMEM((tl, d), BF16)],
        compiler_params=_params("arbitrary"),
        name="pool",
    )(proj, pool_w_bf, pool_scale.reshape(1, d))


def _ssd_kernel(z_ref, xs_ref, bm_ref, cm_ref, dt_ref,
                cwx_ref, cwb_ref, cwc_ref, cbx_ref, cbb_ref, cbc_ref,
                dtb_ref, alog_ref, dsk_ref, nw_ref, exp_ref,
                o_ref,
                extx_s, extb_s, extc_s, state_s, xdt_s, y_s, cb_s, cs_s, cst_s, xc_s, bc_s, cc_s,
                *, tiles_per_seq, heads):
    i = pl.program_id(0)
    tl, inner = xs_ref.shape
    gw = inner // SSD_GROUPS
    pairs_per_group = gw // LANES
    n_pairs = inner // LANES

    @pl.when(i % tiles_per_seq == 0)
    def _():
        extx_s[0:CONV_HALO, :] = jnp.zeros((CONV_HALO, inner), F32)
        extb_s[0:CONV_HALO, :] = jnp.zeros((CONV_HALO, extb_s.shape[1]), F32)
        extc_s[0:CONV_HALO, :] = jnp.zeros((CONV_HALO, extc_s.shape[1]), F32)
        state_s[...] = jnp.zeros_like(state_s)

    def conv_silu(ext_ref, src_ref, w_ref, b_ref, dst_ref, width, cw):
        for c0 in range(0, width, cw):
            cs = slice(c0, c0 + cw)
            ext_ref[CONV_HALO:CONV_HALO + tl, cs] = src_ref[:, cs].astype(F32)
            acc = b_ref[:, cs] + w_ref[SSD_CONV - 1:SSD_CONV, cs] * ext_ref[CONV_HALO:CONV_HALO + tl, cs]
            for k in range(1, SSD_CONV):
                acc = acc + w_ref[SSD_CONV - 1 - k:SSD_CONV - k, cs] * ext_ref[pl.ds(CONV_HALO - k, tl), cs]
            dst_ref[:, cs] = _silu(acc)
            ext_ref[0:CONV_HALO, cs] = ext_ref[tl:tl + CONV_HALO, cs]

    conv_silu(extx_s, xs_ref, cwx_ref, cbx_ref, xc_s, inner, LANES)
    conv_silu(extb_s, bm_ref, cwb_ref, cbb_ref, bc_s, bm_ref.shape[1], LANES)
    conv_silu(extc_s, cm_ref, cwc_ref, cbc_ref, cc_s, cm_ref.shape[1], LANES)

    dtv = dt_ref[...] + dtb_ref[...]
    dt = jnp.maximum(dtv, 0.0) + jnp.log1p(jnp.exp(-jnp.abs(dtv)))
    a = -jnp.exp(alog_ref[...])
    da = dt * a
    ri = lax.broadcasted_iota(jnp.int32, (tl, tl), 0)
    ci = lax.broadcasted_iota(jnp.int32, (tl, tl), 1)
    causal = ri >= ci
    cs = jnp.dot(causal.astype(F32), da, precision=HIGHEST, preferred_element_type=F32)
    cs2 = cs * LOG2_E
    cs_s[...] = cs2
    cst_s[...] = cs2.T
    last = cs[tl - 1:tl, :]
    exp_m = exp_ref[...]
    dt_x = jnp.dot(dt.astype(BF16), exp_m, preferred_element_type=F32)
    ecs_x = jnp.dot(jnp.exp(cs).astype(BF16), exp_m, preferred_element_type=F32)
    dte_x = jnp.dot(jnp.exp(last - cs).astype(BF16), exp_m, preferred_element_type=F32)

    for g in range(SSD_GROUPS):
        gs = slice(g * gw, (g + 1) * gw)
        ns = slice(g * SSD_STATE, (g + 1) * SSD_STATE)
        xg = xc_s[:, gs]
        xdt = xg * dt_x[:, gs]
        xdt_bf = xdt.astype(BF16)
        xd_bf = (xdt * dte_x[:, gs]).astype(BF16)
        bg = bc_s[:, ns]
        cg = cc_s[:, ns].astype(BF16)
        cb_s[g] = lax.dot_general(cg, bg.astype(BF16), (((1,), (1,)), ((), ())), preferred_element_type=F32)
        s_old = state_s[g]
        y_off = jnp.dot(cg, s_old.astype(BF16), preferred_element_type=F32) * ecs_x[:, gs]
        state_s[g] = (s_old * ecs_x[tl - 1:tl, gs]
                      + jnp.dot(bg.T.astype(BF16), xd_bf, preferred_element_type=F32))
        y0 = y_off + dsk_ref[:, gs] * xg
        for q in range(pairs_per_group):
            qs = slice(q * LANES, (q + 1) * LANES)
            xdt_s[g * pairs_per_group + q] = xdt_bf[:, qs]
            y_s[g * pairs_per_group + q] = y0[:, qs]

    lane_h = lax.broadcasted_iota(jnp.int32, (tl, LANES), 1)
    low_half = lane_h < SSD_HEAD_DIM

    half = tl // 2
    causal_top = (lax.broadcasted_iota(jnp.int32, (half, half), 0)
                  >= lax.broadcasted_iota(jnp.int32, (half, half), 1))
    causal_bot = (lax.broadcasted_iota(jnp.int32, (half, tl), 0) + half
                  >= lax.broadcasted_iota(jnp.int32, (half, tl), 1))
    lane_hh = lax.broadcasted_iota(jnp.int32, (half, LANES), 1)
    low_hh = lane_hh < SSD_HEAD_DIM

    def pair_body(hp, carry):
        g = hp // pairs_per_group
        cb_top = cb_s[g, 0:half, 0:half]
        cb_bot = cb_s[g, half:tl, :]
        x_top = xdt_s[hp, 0:half, :]
        x_all = xdt_s[hp]
        cs_top = cs_s[0:half, :]
        cs_bot = cs_s[half:tl, :]
        tops, bots = [], []
        for e in range(2):
            h = 2 * hp + e
            head_lane = jnp.full((half, LANES), h, jnp.int32)
            col_top = jnp.take_along_axis(cs_top, head_lane, axis=1)
            col_bot = jnp.take_along_axis(cs_bot, head_lane, axis=1)
            col_bot = jnp.concatenate([col_bot] * (tl // LANES), axis=1)
            row_all = cst_s[pl.ds(h, 1), :]
            row_top = row_all[:, 0:half]
            m_top = jnp.exp2(jnp.where(causal_top, col_top - row_top, -jnp.inf)) * cb_top
            m_bot = jnp.exp2(jnp.where(causal_bot, col_bot - row_all, -jnp.inf)) * cb_bot
            tops.append(jnp.dot(m_top.astype(BF16), x_top, preferred_element_type=F32))
            bots.append(jnp.dot(m_bot.astype(BF16), x_all, preferred_element_type=F32))
        y_s[hp, 0:half, :] = y_s[hp, 0:half, :] + jnp.where(low_hh, tops[0], tops[1])
        y_s[hp, half:tl, :] = y_s[hp, half:tl, :] + jnp.where(low_hh, bots[0], bots[1])
        return carry

    lax.fori_loop(0, n_pairs, pair_body, 0, unroll=16)

    for g in range(SSD_GROUPS):
        gs = slice(g * gw, (g + 1) * gw)
        yg = jnp.concatenate([y_s[g * pairs_per_group + q] for q in range(pairs_per_group)], axis=1)
        yg = yg * _silu(z_ref[:, gs].astype(F32))
        ms = jnp.mean(yg * yg, axis=-1, keepdims=True)
        o_ref[:, gs] = (yg * lax.rsqrt(ms + EPS) * nw_ref[:, gs]).astype(o_ref.dtype)


def _ssd(proj, dt_raw, blk, conv_w, conv_b, dt_bias, a_log, d_skip, ssd_norm_w, seq):
    t = proj.shape[0]
    tl = SEQ_TILE
    heads = dt_bias.shape[0]
    inner = ssd_norm_w.shape[0]
    bc = SSD_GROUPS * SSD_STATE
    gw = inner // SSD_GROUPS
    pad = LANES - heads
    row = lambda v: v.astype(F32).reshape(1, -1)
    dtb = jnp.pad(row(dt_bias), ((0, 0), (0, pad)))
    alog = jnp.pad(row(a_log), ((0, 0), (0, pad)))
    dsk = jnp.repeat(d_skip.astype(F32), SSD_HEAD_DIM).reshape(1, inner)
    expand = (lax.broadcasted_iota(jnp.int32, (LANES, inner), 1) // SSD_HEAD_DIM
              == lax.broadcasted_iota(jnp.int32, (LANES, inner), 0)).astype(BF16)
    cw = conv_w.astype(F32)
    cbias = row(conv_b)
    full = lambda shape: pl.BlockSpec(shape, lambda i: tuple(0 for _ in shape))
    return pl.pallas_call(
        functools.partial(_ssd_kernel, tiles_per_seq=seq // tl, heads=heads),
        grid=(t // tl,),
        in_specs=[pl.BlockSpec((tl, inner), lambda i: (i, blk["z"])),
                  pl.BlockSpec((tl, inner), lambda i: (i, blk["xs"])),
                  pl.BlockSpec((tl, bc), lambda i: (i, blk["B"])),
                  pl.BlockSpec((tl, bc), lambda i: (i, blk["C"])),
                  pl.BlockSpec((tl, LANES), lambda i: (i, 0)),
                  full((SSD_CONV, inner)), full((SSD_CONV, bc)), full((SSD_CONV, bc)),
                  full((1, inner)), full((1, bc)), full((1, bc)),
                  full((1, LANES)), full((1, LANES)), full((1, inner)), full((1, inner)),
                  full((LANES, inner))],
        out_specs=pl.BlockSpec((tl, inner), lambda i: (i, 0)),
        out_shape=jax.ShapeDtypeStruct((t, inner), BF16),
        scratch_shapes=[pltpu.VMEM((tl + CONV_HALO, inner), F32),
                        pltpu.VMEM((tl + CONV_HALO, bc), F32),
                        pltpu.VMEM((tl + CONV_HALO, bc), F32),
                        pltpu.VMEM((SSD_GROUPS, SSD_STATE, gw), F32),
                        pltpu.VMEM((inner // LANES, tl, LANES), BF16),
                        pltpu.VMEM((inner // LANES, tl, LANES), F32),
                        pltpu.VMEM((SSD_GROUPS, tl, tl), F32),
                        pltpu.VMEM((tl, LANES), F32),
                        pltpu.VMEM((LANES, tl), F32),
                        pltpu.VMEM((tl, inner), F32),
                        pltpu.VMEM((tl, bc), F32),
                        pltpu.VMEM((tl, bc), F32)],
        compiler_params=_params("arbitrary"),
        name="ssd",
    )(proj, proj, proj, proj, dt_raw,
      cw[:, :inner], cw[:, inner:inner + bc], cw[:, inner + bc:],
      cbias[:, :inner], cbias[:, inner:inner + bc], cbias[:, inner + bc:],
      dtb, alog, dsk, row(ssd_norm_w), expand)


def _merge_kernel(yp_ref, ys_ref, gp_ref, gs_ref, wp_ref, ws_ref, o_ref):
    a = jnp.dot(yp_ref[...], wp_ref[...], preferred_element_type=F32)
    b = jnp.dot(ys_ref[...], ws_ref[...], preferred_element_type=F32)
    o_ref[...] = (jax.nn.sigmoid(gp_ref[...].astype(F32)) * a
                  + jax.nn.sigmoid(gs_ref[...].astype(F32)) * b).astype(o_ref.dtype)


def _merge(ypool, yssd, proj, gp_off, gs_off, wbp, wbs):
    t, d = ypool.shape
    inner = yssd.shape[1]
    tm, tn = min(MERGE_TM, t), MERGE_TN
    return pl.pallas_call(
        _merge_kernel,
        grid=(t // tm, d // tn),
        in_specs=[pl.BlockSpec((tm, d), lambda i, j: (i, 0)),
                  pl.BlockSpec((tm, inner), lambda i, j: (i, 0)),
                  pl.BlockSpec((tm, tn), lambda i, j: (i, gp_off // tn + j)),
                  pl.BlockSpec((tm, tn), lambda i, j: (i, gs_off // tn + j)),
                  pl.BlockSpec((d, tn), lambda i, j: (0, j)),
                  pl.BlockSpec((inner, tn), lambda i, j: (0, j))],
        out_specs=pl.BlockSpec((tm, tn), lambda i, j: (i, j)),
        out_shape=jax.ShapeDtypeStruct((t, d), BF16),
        compiler_params=_params("arbitrary", "arbitrary"),
        name="merge",
    )(ypool, yssd, proj, proj, wbp, wbs)


def _out_kernel(m_ref, wo_ref, x_ref, mod_ref, wr_ref, br_ref,
                h_ref, u_ref, idx_ref, gate_ref, rank_ref, cnt_ref, carry_s):
    i = pl.program_id(0)
    tm = m_ref.shape[0]
    ne = wr_ref.shape[0]

    @pl.when(i == 0)
    def _():
        carry_s[...] = jnp.zeros_like(carry_s)

    mix = jnp.dot(m_ref[...], wo_ref[...], preferred_element_type=F32)
    h1 = x_ref[...] + mod_ref[0, 2:3, :] * mix
    h_ref[...] = h1
    ms = jnp.mean(h1 * h1, axis=-1, keepdims=True)
    u = h1 * lax.rsqrt(ms + EPS) * (1.0 + mod_ref[0, 4:5, :]) + mod_ref[0, 3:4, :]
    u_ref[...] = _pack_bf16_pairs(u)
    logits = lax.dot_general(wr_ref[...], u, (((1,), (1,)), ((), ())), precision=HIGHEST,
                             preferred_element_type=F32) + br_ref[...]
    eidx = lax.broadcasted_iota(jnp.int32, (ne, tm), 0)
    work = logits
    vals, idxs, hots = [], [], []
    for _ in range(TOP_K):
        mx = jnp.max(work, axis=0, keepdims=True)
        sel = jnp.min(jnp.where(work == mx, eidx, ne), axis=0, keepdims=True)
        hot = eidx == sel
        vals.append(mx)
        idxs.append(sel)
        hots.append(hot)
        work = jnp.where(hot, -jnp.inf, work)
    exps = [jnp.exp(v - vals[0]) for v in vals]
    den = exps[0]
    for e in exps[1:]:
        den = den + e
    cnt = hots[0].astype(F32)
    for hot in hots[1:]:
        cnt = cnt + hot.astype(F32)
    ti = lax.broadcasted_iota(jnp.int32, (tm, tm), 0)
    tj = lax.broadcasted_iota(jnp.int32, (tm, tm), 1)
    before = (ti < tj).astype(BF16)
    prefix = jnp.dot(cnt.astype(BF16), before, preferred_element_type=F32)
    base = carry_s[:, 0:1] + prefix
    pad_rows = idx_ref.shape[0] - TOP_K
    ranks = [jnp.sum(jnp.where(hot, base, 0.0), axis=0, keepdims=True) for hot in hots]
    idx_ref[...] = jnp.concatenate(idxs + [jnp.zeros((pad_rows, tm), jnp.int32)], axis=0)
    gate_ref[...] = jnp.concatenate([e / den for e in exps] + [jnp.zeros((pad_rows, tm), F32)], axis=0)
    rank_ref[...] = jnp.concatenate([r.astype(jnp.int32) for r in ranks]
                                    + [jnp.zeros((pad_rows, tm), jnp.int32)], axis=0)
    carry_s[...] = carry_s[...] + jnp.sum(cnt, axis=1, keepdims=True)
    cnt_ref[...] = carry_s[...].astype(jnp.int32)


def _outproj(merged, wo, x2, mod3, w_router, b_router, seq):
    t, d = x2.shape
    ne = w_router.shape[1]
    tm = min(OUT_TM, seq)
    tiles_per_seq = seq // tm
    rows = 8
    return pl.pallas_call(
        _out_kernel,
        grid=(t // tm,),
        in_specs=[pl.BlockSpec((tm, d), lambda i: (i, 0)),
                  pl.BlockSpec((d, d), lambda i: (0, 0), pipeline_mode=pl.Buffered(1)),
                  pl.BlockSpec((tm, d), lambda i: (i, 0)),
                  pl.BlockSpec((1, N_MOD, d), lambda i: (i // tiles_per_seq, 0, 0)),
                  pl.BlockSpec((ne, d), lambda i: (0, 0)),
                  pl.BlockSpec((ne, 1), lambda i: (0, 0))],
        out_specs=[pl.BlockSpec((tm, d), lambda i: (i, 0)),
                   pl.BlockSpec((tm, d // 2), lambda i: (i, 0)),
                   pl.BlockSpec((rows, tm), lambda i: (0, i)),
                   pl.BlockSpec((rows, tm), lambda i: (0, i)),
                   pl.BlockSpec((rows, tm), lambda i: (0, i)),
                   pl.BlockSpec((ne, LANES), lambda i: (0, 0))],
        out_shape=[jax.ShapeDtypeStruct((t, d), F32), jax.ShapeDtypeStruct((t, d // 2), jnp.uint32),
                   jax.ShapeDtypeStruct((rows, t), jnp.int32), jax.ShapeDtypeStruct((rows, t), F32),
                   jax.ShapeDtypeStruct((rows, t), jnp.int32), jax.ShapeDtypeStruct((ne, LANES), jnp.int32)],
        scratch_shapes=[pltpu.VMEM((ne, LANES), F32)],
        compiler_params=_params("arbitrary"),
        name="outproj",
    )(merged, wo, x2, mod3, w_router.astype(F32).T, b_router.astype(F32).reshape(ne, 1))


def _row_copy(src_hbm, src_row, dst_ref, dst_row, sem):
    return pltpu.make_async_copy(src_hbm.at[pl.ds(src_row, 1)], dst_ref.at[pl.ds(dst_row, 1)], sem)


def _dispatch_kernel(slot_ref, u_ref, init_hbm, xs_hbm, sem):
    del init_hbm
    ts = u_ref.shape[0]

    def issue(tok, carry):
        for k in range(TOP_K):
            _row_copy(u_ref, tok, xs_hbm, slot_ref[0, 0, tok * TOP_K + k], sem).start(priority=k % 2)
        return carry

    lax.fori_loop(0, ts, issue, 0)
    rows = pl.ds(0, ts * TOP_K)
    pltpu.make_async_copy(xs_hbm.at[rows], xs_hbm.at[rows], sem).wait()


def _dispatch(u2, slot_tk, n_slots):
    t, d = u2.shape
    ts = min(DISPATCH_TS, t)
    slots3 = slot_tk.reshape(t // ts, 1, ts * TOP_K)
    init = jnp.zeros((n_slots, d), u2.dtype)
    return pl.pallas_call(
        _dispatch_kernel,
        grid=(t // ts,),
        in_specs=[pl.BlockSpec((1, 1, ts * TOP_K), lambda i: (i, 0, 0), memory_space=pltpu.SMEM),
                  pl.BlockSpec((ts, d), lambda i: (i, 0)),
                  pl.BlockSpec(memory_space=pl.ANY)],
        out_specs=pl.BlockSpec(memory_space=pl.ANY),
        out_shape=jax.ShapeDtypeStruct((n_slots, d), u2.dtype),
        scratch_shapes=[pltpu.SemaphoreType.DMA(())],
        input_output_aliases={2: 0},
        compiler_params=_params("arbitrary"),
        name="dispatch",
    )(slots3, u2, init)


ITEM_IDLE, ITEM_ACTIVE, ITEM_ZERO_FILL, ITEM_ACTIVE_HALF = 0, 1, 2, 3


def _moe_kernel(tb_ref, tj_ref, te_ref, tr_ref, tflag_ref, tout_ref, tslot_ref, tne_ref, tnj_ref, thn_ref,
                x_ref, wgu_hbm, bg_ref, bu_ref, wd_hbm, bd_ref, o_ref,
                wg_s, wu_s, wd_s, xb_s, acc_s, wg_buf, wu_buf, wd_buf, sem, *, last_j):
    del tb_ref, tout_ref
    i = pl.program_id(0)
    flag = tflag_ref[i]
    r = tr_ref[i]
    j = tj_ref[i]
    half = x_ref.shape[1]
    tf = wg_s.shape[1]
    up_col0 = wgu_hbm.shape[2] // 2

    def weight_copies(e, jt, slot):
        c0 = pl.multiple_of(jt * tf, tf)
        return (pltpu.make_async_copy(wgu_hbm.at[e, :, pl.ds(c0, tf)], wg_buf.at[slot], sem.at[slot]),
                pltpu.make_async_copy(wgu_hbm.at[e, :, pl.ds(up_col0 + c0, tf)], wu_buf.at[slot], sem.at[slot]),
                pltpu.make_async_copy(wd_hbm.at[e, pl.ds(c0, tf), :], wd_buf.at[slot], sem.at[slot]))

    is_active = (flag & 1) == 1

    @pl.when(is_active & (r == 0))
    def _():
        slot = tslot_ref[i]

        @pl.when(i == 0)
        def _():
            for cp in weight_copies(te_ref[i], j, slot):
                cp.start()

        for cp in weight_copies(te_ref[i], j, slot):
            cp.wait()
        for s in range(2):
            @pl.when(slot == s)
            def _():
                wg_s[...] = wg_buf[s].astype(BF16)
                wu_s[...] = wu_buf[s].astype(BF16)
                wd_s[...] = wd_buf[s].astype(BF16)

        @pl.when(thn_ref[i] == 1)
        def _():
            for cp in weight_copies(tne_ref[i], tnj_ref[i], 1 - slot):
                cp.start()

    @pl.when(is_active & (j == 0))
    def _():
        lo, hi = _unpack_bf16_pairs(x_ref[...])
        xb_s[r, :, :half] = lo
        xb_s[r, :, half:] = hi

    def expert_mlp(rows):
        xb = xb_s[r, 0:rows, :]
        gate = jnp.dot(xb, wg_s[...], preferred_element_type=F32) + bg_ref[...]
        up = jnp.dot(xb, wu_s[...], preferred_element_type=F32) + bu_ref[...]
        gate = jnp.minimum(gate, SWIGLU_LIMIT)
        up = jnp.clip(up, -SWIGLU_LIMIT, SWIGLU_LIMIT)
        act = (up + 1.0) * gate * jax.nn.sigmoid(SWIGLU_ALPHA * gate)
        part = jnp.dot(act.astype(BF16), wd_s[...], preferred_element_type=F32)

        @pl.when(j == 0)
        def _():
            acc_s[r, 0:rows, :] = part + bd_ref[...]

        @pl.when(j > 0)
        def _():
            acc_s[r, 0:rows, :] = acc_s[r, 0:rows, :] + part

        @pl.when(j == last_j)
        def _():
            o_ref[0:rows, :] = _pack_bf16_pairs(acc_s[r, 0:rows, :])
            if rows < o_ref.shape[0]:
                o_ref[rows:, :] = jnp.zeros((o_ref.shape[0] - rows, half), o_ref.dtype)

    @pl.when(flag == ITEM_ACTIVE)
    def _():
        expert_mlp(o_ref.shape[0])

    @pl.when(flag == ITEM_ACTIVE_HALF)
    def _():
        expert_mlp(o_ref.shape[0] // 2)

    @pl.when(flag == ITEM_ZERO_FILL)
    def _():
        o_ref[...] = jnp.zeros_like(o_ref)


def _lookup(table, idx):
    hot = idx[:, None] == jnp.arange(table.shape[0], dtype=jnp.int32)[None, :]
    return jnp.sum(jnp.where(hot, table[None, :].astype(jnp.int32), 0), axis=1).astype(jnp.int32)


def _moe_tables(counts, n_blocks, ft):
    ne = counts.shape[0]
    padded = (counts + MOE_BM - 1) // MOE_BM * MOE_BM
    pend = jnp.cumsum(padded)
    pstart = pend - padded
    blocks_per_expert = padded // MOE_BM
    first_block = pstart // MOE_BM
    n_active = (pend[-1] // MOE_BM).astype(jnp.int32)

    blocks = jnp.arange(n_blocks, dtype=jnp.int32)
    block_expert = jnp.minimum(jnp.sum(pend[None, :] <= (blocks * MOE_BM)[:, None], axis=1), ne - 1).astype(jnp.int32)
    off = blocks - _lookup(first_block, block_expert)
    is_start = (off % MOE_CHUNK_BLOCKS == 0) & (blocks < n_active)
    chunk_size = jnp.minimum(MOE_CHUNK_BLOCKS,
                             _lookup(blocks_per_expert, block_expert) - off // MOE_CHUNK_BLOCKS * MOE_CHUNK_BLOCKS)

    items = jnp.arange(n_blocks * ft, dtype=jnp.int32)
    active = items < ft * n_active
    src = jnp.minimum(items, ft * n_active - 1)
    cand = jnp.where(is_start[None, :] & (ft * blocks[None, :] <= src[:, None]), blocks[None, :], 0)
    cstart = jnp.max(cand, axis=1).astype(jnp.int32)
    m = jnp.maximum(_lookup(chunk_size, cstart), 1)
    local = src - ft * cstart
    tj = local // m
    tr = local % m
    tb = cstart + tr
    te = _lookup(block_expert, cstart)
    k = items - ft * n_active
    zero_fill = (~active) & (n_active + k < n_blocks)
    tout = jnp.where(active, jnp.where(tj == ft - 1, tb, cstart), jnp.minimum(n_active + k, n_blocks - 1))
    valid_rows = _lookup(counts, te) - (tb - _lookup(first_block, te)) * MOE_BM
    kind = jnp.where(valid_rows <= MOE_BM // 2, ITEM_ACTIVE_HALF, ITEM_ACTIVE)
    tflag = jnp.where(active, kind, jnp.where(zero_fill, ITEM_ZERO_FILL, ITEM_IDLE))
    group_start = active & (tr == 0)
    tslot = (jnp.cumsum(group_start.astype(jnp.int32)) - 1) % 2
    nxt = items + m
    has_next = group_start & (nxt < ft * n_active)
    nxt = jnp.minimum(nxt, n_blocks * ft - 1)
    tne, tnj = jnp.take(te, nxt), jnp.take(tj, nxt)
    as_i32 = lambda v: v.astype(jnp.int32)
    return pstart, tuple(map(as_i32, (tb, tj, te, tr, tflag, tout, tslot, tne, tnj, has_next)))


def _moe(xs, tables, wgu, bgu, wd, bd):
    n_slots = xs.shape[0]
    ne, d, f2 = wgu.shape
    f = f2 // 2
    bm, tf = MOE_BM, min(MOE_TF, f)
    nb, ft = n_slots // bm, f // tf
    grid_spec = pltpu.PrefetchScalarGridSpec(
        num_scalar_prefetch=len(tables),
        grid=(nb * ft,),
        in_specs=[pl.BlockSpec((bm, d // 2), lambda i, tb, tj, te, *_: (tb[i], 0)),
                  pl.BlockSpec(memory_space=pl.ANY),
                  pl.BlockSpec((None, 1, tf), lambda i, tb, tj, te, *_: (te[i], 0, tj[i])),
                  pl.BlockSpec((None, 1, tf), lambda i, tb, tj, te, *_: (te[i], 0, ft + tj[i])),
                  pl.BlockSpec(memory_space=pl.ANY),
                  pl.BlockSpec((None, 1, d), lambda i, tb, tj, te, *_: (te[i], 0, 0))],
        out_specs=pl.BlockSpec((bm, d // 2), lambda i, tb, tj, te, tr, tg, to, *_: (to[i], 0)),
        scratch_shapes=[pltpu.VMEM((d, tf), BF16), pltpu.VMEM((d, tf), BF16), pltpu.VMEM((tf, d), BF16),
                        pltpu.VMEM((MOE_CHUNK_BLOCKS, bm, d), BF16), pltpu.VMEM((MOE_CHUNK_BLOCKS, bm, d), F32),
                        pltpu.VMEM((2, d, tf), F32), pltpu.VMEM((2, d, tf), F32), pltpu.VMEM((2, tf, d), F32),
                        pltpu.SemaphoreType.DMA((2,))],
    )
    return pl.pallas_call(
        functools.partial(_moe_kernel, last_j=ft - 1),
        grid_spec=grid_spec,
        out_shape=jax.ShapeDtypeStruct((n_slots, d // 2), jnp.uint32),
        compiler_params=_params("arbitrary"),
        name="moe",
    )(*tables, xs, wgu, bgu, bgu, wd, bd)


def _final_kernel(slot_ref, h_ref, gate_ref, mod_ref, fw_ref, y_hbm, o_ref, ybuf, sem):
    tc, d = h_ref.shape
    half = d // 2

    def issue(grp, carry):
        for q in range(FINAL_ISSUE_UNROLL):
            tk = FINAL_ISSUE_UNROLL * grp + q
            _row_copy(y_hbm, slot_ref[0, 0, tk], ybuf, tk, sem).start(priority=q % 2)
        return carry

    lax.fori_loop(0, tc * TOP_K // FINAL_ISSUE_UNROLL, issue, 0)
    pltpu.make_async_copy(y_hbm.at[pl.ds(0, tc * TOP_K)], ybuf, sem).wait()

    ffn_lo = jnp.zeros((tc, half), F32)
    ffn_hi = jnp.zeros((tc, half), F32)
    for k in range(TOP_K):
        w = ybuf[k * tc:(k + 1) * tc, :]
        g = gate_ref[:, k:k + 1]
        ffn_lo = ffn_lo + g * lax.bitcast_convert_type(w << 16, F32)
        ffn_hi = ffn_hi + g * lax.bitcast_convert_type(w & jnp.uint32(0xFFFF0000), F32)
    h2_lo = h_ref[:, :half] + mod_ref[0, 5:6, :half] * ffn_lo
    h2_hi = h_ref[:, half:] + mod_ref[0, 5:6, half:] * ffn_hi
    ms = (jnp.sum(h2_lo * h2_lo, axis=-1, keepdims=True) + jnp.sum(h2_hi * h2_hi, axis=-1, keepdims=True)) / d
    inv = lax.rsqrt(ms + EPS)
    o_ref[:, :half] = (h2_lo * inv * fw_ref[:, :half]).astype(o_ref.dtype)
    o_ref[:, half:] = (h2_hi * inv * fw_ref[:, half:]).astype(o_ref.dtype)


def _final(h1, gates_tk, slot_kt, mod3, final_norm_w, y_slots, seq, out_dtype):
    t, d = h1.shape
    tc = min(FINAL_TC, seq)
    tiles_per_seq = seq // tc
    slots3 = slot_kt.reshape(TOP_K, t // tc, tc).transpose(1, 0, 2).reshape(t // tc, 1, TOP_K * tc)
    return pl.pallas_call(
        _final_kernel,
        grid=(t // tc,),
        in_specs=[pl.BlockSpec((1, 1, TOP_K * tc), lambda i: (i, 0, 0), memory_space=pltpu.SMEM),
                  pl.BlockSpec((tc, d), lambda i: (i, 0)),
                  pl.BlockSpec((tc, gates_tk.shape[1]), lambda i: (i, 0)),
                  pl.BlockSpec((1, N_MOD, d), lambda i: (i // tiles_per_seq, 0, 0)),
                  pl.BlockSpec((1, d), lambda i: (0, 0)),
                  pl.BlockSpec(memory_space=pl.ANY)],
        out_specs=pl.BlockSpec((tc, d), lambda i: (i, 0)),
        out_shape=jax.ShapeDtypeStruct((t, d), out_dtype),
        scratch_shapes=[pltpu.VMEM((TOP_K * tc, d // 2), jnp.uint32), pltpu.SemaphoreType.DMA(())],
        compiler_params=_params("arbitrary"),
        name="final",
    )(slots3, h1, gates_tk, mod3, final_norm_w.astype(F32).reshape(1, d), y_slots)


def kernel(x, c, w_ada, b_ada, w_in, pool_w, pool_scale, conv_w, conv_b, dt_bias, a_log, d_skip, ssd_norm_w,
           w_branch_pool, w_branch_ssd, w_out, w_router, b_router, w_gate_up, b_gate_up, w_down, b_down,
           final_norm_w):
    bsz, seq, d = x.shape
    depth = w_ada.shape[0]
    t = bsz * seq
    inner = ssd_norm_w.shape[1]
    heads = dt_bias.shape[1]
    bc = SSD_GROUPS * SSD_STATE
    ne = w_router.shape[2]
    assert depth == 1, "the final RMSNorm is fused into the last kernel of the single layer"
    assert SEQ_TILE == 2 * LANES
    assert seq % SEQ_TILE == 0 and heads <= LANES and (7 * d) % bc == 0 and (4 * d) % inner == 0
    assert inner // SSD_GROUPS % LANES == 0 and d % (len(POOL_WINDOWS) * LANES) == 0

    s1 = d
    s2 = s1 + inner
    s3 = s2 + inner + 2 * bc
    s4 = s3 + heads
    s5 = s4 + d
    blk = {"z": 0, "xs": 1, "p": (2 * inner) // d, "B": (2 * inner + 3 * d) // bc, "C": (2 * inner + 3 * d) // bc + 1}
    gp_off, gs_off = 2 * inner + d, 2 * inner + 2 * d

    h = x.astype(F32).reshape(t, d)
    for layer in range(depth):
        wl = w_in[layer]
        w_main = jnp.concatenate([wl[:, s1:s2], wl[:, s2:s2 + inner], wl[:, :s1], wl[:, s4:s5], wl[:, s5:],
                                  wl[:, s2 + inner:s2 + inner + bc], wl[:, s2 + inner + bc:s3]], axis=1).astype(BF16)
        w_dt = jnp.pad(wl[:, s3:s4].astype(F32), ((0, 0), (0, LANES - heads)))

        mod3 = _ada(c, w_ada[layer], b_ada[layer]).reshape(bsz, N_MOD, d)
        proj, dt_raw = _inproj(h, mod3, w_main, w_dt, seq)
        ypool = _pool(proj, blk["p"], pool_w[layer].astype(BF16), pool_scale[layer], seq, d)
        yssd = _ssd(proj, dt_raw, blk, conv_w[layer], conv_b[layer], dt_bias[layer], a_log[layer], d_skip[layer],
                    ssd_norm_w[layer], seq)
        merged = _merge(ypool, yssd, proj, gp_off, gs_off, w_branch_pool[layer].astype(BF16),
                        w_branch_ssd[layer].astype(BF16))
        h1, u2, idx_kt, gate_kt, rank_kt, counts = _outproj(merged, w_out[layer].astype(BF16), h, mod3,
                                                            w_router[layer], b_router[layer], seq)

        n_blocks = (t * TOP_K) // MOE_BM + ne
        ft = w_down.shape[2] // min(MOE_TF, w_down.shape[2])
        pstart, tables = _moe_tables(counts[:, 0], n_blocks, ft)
        eids = jnp.arange(ne, dtype=jnp.int32)[:, None, None]
        slot_kt = (jnp.sum(jnp.where(idx_kt[:TOP_K][None] == eids, pstart[:, None, None], 0), axis=0)
                   + rank_kt[:TOP_K]).astype(jnp.int32)

        xs = _dispatch(u2, slot_kt.T, n_blocks * MOE_BM)
        y_slots = _moe(xs, tables, w_gate_up[layer].astype(F32), b_gate_up[layer].astype(F32)[:, None, :],
                       w_down[layer].astype(F32), b_down[layer].astype(F32)[:, None, :])
        h = _final(h1, gate_kt.T, slot_kt, mod3, final_norm_w, y_slots, seq, x.dtype)
    return h.reshape(bsz, seq, d)
```

```python
import functools

import jax
import jax.numpy as jnp
from jax import lax
from jax.experimental import pallas as pl
from jax.experimental.pallas import tpu as pltpu

F32 = jnp.float32
BF16 = jnp.bfloat16
HIGHEST = lax.Precision.HIGHEST

EPS = 1e-6
POOL_WINDOWS = (2, 4, 8, 16)
SSD_GROUPS = 8
SSD_STATE = 128
SSD_CONV = 4
SSD_HEAD_DIM = 64
TOP_K = 4
SWIGLU_LIMIT = 7.0
SWIGLU_ALPHA = 1.702
N_MOD = 6

LOG2_E = 1.4426950408889634
LANES = 128
SUBLANES = 8
CONV_HALO = 8
V7X_VMEM_LIMIT = 56 * 1024 * 1024

SEQ_TILE = 256
INPROJ_TM, INPROJ_TN = 1024, 1024
MERGE_TM, MERGE_TN = 512, 512
OUT_TM = 512
MOE_BM, MOE_TF = 512, 512
MOE_CHUNK_BLOCKS = 2
DISPATCH_TS = 256
FINAL_TC = 256
FINAL_ISSUE_UNROLL = 8


def _params(*sem):
    return pltpu.CompilerParams(dimension_semantics=sem, vmem_limit_bytes=V7X_VMEM_LIMIT)


def _silu(v):
    return v * jax.nn.sigmoid(v)


def _pack_bf16_pairs(v):
    half = v.shape[1] // 2
    lo = lax.bitcast_convert_type(v[:, :half].astype(BF16).astype(F32), jnp.uint32)
    hi = lax.bitcast_convert_type(v[:, half:].astype(BF16).astype(F32), jnp.uint32)
    return (hi & jnp.uint32(0xFFFF0000)) | (lo >> 16)


def _token_tile(d):
    return (d // 2 // LANES, LANES)


def _rows_to_tiles(w):
    return w.reshape(w.shape[0], w.shape[1] // LANES, LANES)


def _tiles_to_rows(w):
    return w.reshape(w.shape[0], w.shape[1] * LANES)


def _unpack_bf16_pairs(w):
    lo = lax.bitcast_convert_type(w << 16, F32).astype(BF16)
    hi = lax.bitcast_convert_type(w & jnp.uint32(0xFFFF0000), F32).astype(BF16)
    return lo, hi


def _ada_kernel(cb_ref, w_ref, b_ref, o_ref):
    nb, tn = cb_ref.shape[0], w_ref.shape[1]
    for b in range(nb):
        ca = _silu(cb_ref[b])
        cols = [jnp.sum(w_ref[:, j * LANES:(j + 1) * LANES] * ca, axis=0, keepdims=True)
                for j in range(tn // LANES)]
        o_ref[b:b + 1, :] = jnp.concatenate(cols, axis=1) + b_ref[...]


def _ada(c, w_ada, b_ada):
    nb, k = c.shape
    n = w_ada.shape[1]
    tn = 1024 if n % 1024 == 0 else n
    cb = jnp.broadcast_to(c.astype(F32)[:, :, None], (nb, k, LANES))
    return pl.pallas_call(
        _ada_kernel,
        grid=(n // tn,),
        in_specs=[pl.BlockSpec((nb, k, LANES), lambda j: (0, 0, 0)),
                  pl.BlockSpec((k, tn), lambda j: (0, j)),
                  pl.BlockSpec((1, tn), lambda j: (0, j))],
        out_specs=pl.BlockSpec((nb, tn), lambda j: (0, j)),
        out_shape=jax.ShapeDtypeStruct((nb, n), F32),
        compiler_params=_params("arbitrary"),
        name="ada",
    )(cb, w_ada, b_ada.reshape(1, n))


def _inproj_kernel(x_ref, mod_ref, w_ref, wdt_hi_ref, wdt_lo_ref, o_ref, dt_ref, u_s, *, rows_per_chunk):
    j = pl.program_id(1)
    tm = x_ref.shape[0]

    @pl.when(j == 0)
    def _():
        sh = mod_ref[0, 0:1, :]
        sc = mod_ref[0, 1:2, :]

        def body(r, carry):
            rows = pl.ds(pl.multiple_of(r * rows_per_chunk, rows_per_chunk), rows_per_chunk)
            xv = x_ref[rows, :]
            ms = jnp.mean(xv * xv, axis=-1, keepdims=True)
            u = xv * lax.rsqrt(ms + EPS) * (1.0 + sc) + sh
            u_hi = u.astype(BF16)
            u_s[rows, :] = u_hi
            u_lo = (u - u_hi.astype(F32)).astype(BF16)
            dt_ref[rows, :] = (jnp.dot(u_hi, wdt_hi_ref[...], preferred_element_type=F32)
                               + (jnp.dot(u_hi, wdt_lo_ref[...], preferred_element_type=F32)
                                  + jnp.dot(u_lo, wdt_hi_ref[...], preferred_element_type=F32)))
            return carry

        lax.fori_loop(0, tm // rows_per_chunk, body, 0)

    o_ref[...] = jnp.dot(u_s[...], w_ref[...], preferred_element_type=F32).astype(o_ref.dtype)


def _inproj(x2, mod3, w_main, w_dt, seq):
    t, d = x2.shape
    w_dt_hi = w_dt.astype(BF16)
    w_dt_lo = (w_dt - w_dt_hi.astype(F32)).astype(BF16)
    n = w_main.shape[1]
    tm = min(INPROJ_TM, seq)
    tn = INPROJ_TN
    tiles_per_seq = seq // tm
    return pl.pallas_call(
        functools.partial(_inproj_kernel, rows_per_chunk=min(128, tm)),
        grid=(t // tm, n // tn),
        in_specs=[pl.BlockSpec((tm, d), lambda i, j: (i, 0)),
                  pl.BlockSpec((1, N_MOD, d), lambda i, j: (i // tiles_per_seq, 0, 0)),
                  pl.BlockSpec((d, tn), lambda i, j: (0, j)),
                  pl.BlockSpec((d, LANES), lambda i, j: (0, 0)),
                  pl.BlockSpec((d, LANES), lambda i, j: (0, 0))],
        out_specs=[pl.BlockSpec((tm, tn), lambda i, j: (i, j)),
                   pl.BlockSpec((tm, LANES), lambda i, j: (i, 0))],
        out_shape=[jax.ShapeDtypeStruct((t, n), BF16), jax.ShapeDtypeStruct((t, LANES), F32)],
        scratch_shapes=[pltpu.VMEM((tm, d), BF16)],
        compiler_params=_params("arbitrary", "arbitrary"),
        name="inproj",
    )(x2, mod3, w_main, w_dt_hi, w_dt_lo)


def _pool_kernel(p_ref, pw_ref, ps_ref, o_ref, prev_s, *, tiles_per_seq):
    i = pl.program_id(0)
    tl, d = p_ref.shape
    gd = d // len(POOL_WINDOWS)
    it = i % tiles_per_seq

    @pl.when(it == 0)
    def _():
        prev_s[...] = jnp.zeros_like(prev_s)

    row = lax.broadcasted_iota(jnp.int32, (tl, 2 * tl), 0)
    col = lax.broadcasted_iota(jnp.int32, (tl, 2 * tl), 1)
    pos = (it * tl + lax.broadcasted_iota(jnp.int32, (tl, 1), 0) + 1).astype(F32)
    for g, w in enumerate(POOL_WINDOWS):
        sl = slice(g * gd, (g + 1) * gd)
        cur = p_ref[:, sl]
        ext = jnp.concatenate([prev_s[:, sl], cur], axis=0)
        band = ((col <= row + tl) & (col > row + tl - w)).astype(BF16)
        win_sum = jnp.dot(band, ext, preferred_element_type=F32)
        mean = win_sum / jnp.minimum(pos, float(w))
        dlt = (mean - cur.astype(F32)).astype(BF16)
        mixed = jnp.dot(dlt, pw_ref[g], preferred_element_type=F32)
        o_ref[:, sl] = (mixed * ps_ref[:, sl]).astype(o_ref.dtype)
    prev_s[...] = p_ref[...]


def _pool(proj, p_blk, pool_w_bf, pool_scale, seq, d):
    t = proj.shape[0]
    tl = SEQ_TILE
    g, gd = pool_w_bf.shape[0], pool_w_bf.shape[1]
    return pl.pallas_call(
        functools.partial(_pool_kernel, tiles_per_seq=seq // tl),
        grid=(t // tl,),
        in_specs=[pl.BlockSpec((tl, d), lambda i: (i, p_blk)),
                  pl.BlockSpec((g, gd, gd), lambda i: (0, 0, 0)),
                  pl.BlockSpec((1, d), lambda i: (0, 0))],
        out_specs=pl.BlockSpec((tl, d), lambda i: (i, 0)),
        out_shape=jax.ShapeDtypeStruct((t, d), BF16),
        scratch_shapes=[pltpu.VMEM((tl, d), BF16)],
        compiler_params=_params("arbitrary"),
        name="pool",
    )(proj, pool_w_bf, pool_scale.reshape(1, d))


def _ssd_kernel(z_ref, xs_ref, bm_ref, cm_ref, dt_ref,
                cwx_ref, cwb_ref, cwc_ref, cbx_ref, cbb_ref, cbc_ref,
                dtb_ref, alog_ref, dsk_ref, nw_ref, exp_ref,
                o_ref,
                extx_s, extb_s, extc_s, state_s, xdt_s, y_s, cb_s, cs_s, cst_s, xc_s, bc_s, cc_s,
                *, tiles_per_seq, heads):
    i = pl.program_id(0)
    tl, inner = xs_ref.shape
    gw = inner // SSD_GROUPS
    pairs_per_group = gw // LANES
    n_pairs = inner // LANES

    @pl.when(i % tiles_per_seq == 0)
    def _():
        extx_s[0:CONV_HALO, :] = jnp.zeros((CONV_HALO, inner), F32)
        extb_s[0:CONV_HALO, :] = jnp.zeros((CONV_HALO, extb_s.shape[1]), F32)
        extc_s[0:CONV_HALO, :] = jnp.zeros((CONV_HALO, extc_s.shape[1]), F32)
        state_s[...] = jnp.zeros_like(state_s)

    def conv_silu(ext_ref, src_ref, w_ref, b_ref, dst_ref, width, cw):
        for c0 in range(0, width, cw):
            cs = slice(c0, c0 + cw)
            ext_ref[CONV_HALO:CONV_HALO + tl, cs] = src_ref[:, cs].astype(F32)
            acc = b_ref[:, cs] + w_ref[SSD_CONV - 1:SSD_CONV, cs] * ext_ref[CONV_HALO:CONV_HALO + tl, cs]
            for k in range(1, SSD_CONV):
                acc = acc + w_ref[SSD_CONV - 1 - k:SSD_CONV - k, cs] * ext_ref[pl.ds(CONV_HALO - k, tl), cs]
            dst_ref[:, cs] = _silu(acc)
            ext_ref[0:CONV_HALO, cs] = ext_ref[tl:tl + CONV_HALO, cs]

    conv_silu(extx_s, xs_ref, cwx_ref, cbx_ref, xc_s, inner, LANES)
    conv_silu(extb_s, bm_ref, cwb_ref, cbb_ref, bc_s, bm_ref.shape[1], LANES)
    conv_silu(extc_s, cm_ref, cwc_ref, cbc_ref, cc_s, cm_ref.shape[1], LANES)

    dtv = dt_ref[...] + dtb_ref[...]
    dt = jnp.maximum(dtv, 0.0) + jnp.log1p(jnp.exp(-jnp.abs(dtv)))
    a = -jnp.exp(alog_ref[...])
    da = dt * a
    ri = lax.broadcasted_iota(jnp.int32, (tl, tl), 0)
    ci = lax.broadcasted_iota(jnp.int32, (tl, tl), 1)
    causal = ri >= ci
    cs = jnp.dot(causal.astype(F32), da, precision=HIGHEST, preferred_element_type=F32)
    cs2 = cs * LOG2_E
    cs_s[...] = cs2
    cst_s[...] = cs2.T
    last = cs[tl - 1:tl, :]
    exp_m = exp_ref[...]
    dt_x = jnp.dot(dt.astype(BF16), exp_m, preferred_element_type=F32)
    ecs_x = jnp.dot(jnp.exp(cs).astype(BF16), exp_m, preferred_element_type=F32)
    dte_x = jnp.dot(jnp.exp(last - cs).astype(BF16), exp_m, preferred_element_type=F32)

    for g in range(SSD_GROUPS):
        gs = slice(g * gw, (g + 1) * gw)
        ns = slice(g * SSD_STATE, (g + 1) * SSD_STATE)
        xg = xc_s[:, gs]
        xdt = xg * dt_x[:, gs]
        xdt_bf = xdt.astype(BF16)
        xd_bf = (xdt * dte_x[:, gs]).astype(BF16)
        bg = bc_s[:, ns]
        cg = cc_s[:, ns].astype(BF16)
        cb_s[g] = lax.dot_general(cg, bg.astype(BF16), (((1,), (1,)), ((), ())), preferred_element_type=F32)
        s_old = state_s[g]
        y_off = jnp.dot(cg, s_old.astype(BF16), preferred_element_type=F32) * ecs_x[:, gs]
        state_s[g] = (s_old * ecs_x[tl - 1:tl, gs]
                      + jnp.dot(bg.T.astype(BF16), xd_bf, preferred_element_type=F32))
        y0 = y_off + dsk_ref[:, gs] * xg
        for q in range(pairs_per_group):
            qs = slice(q * LANES, (q + 1) * LANES)
            xdt_s[g * pairs_per_group + q] = xdt_bf[:, qs]
            y_s[g * pairs_per_group + q] = y0[:, qs]

    lane_h = lax.broadcasted_iota(jnp.int32, (tl, LANES), 1)
    low_half = lane_h < SSD_HEAD_DIM

    half = tl // 2
    causal_top = (lax.broadcasted_iota(jnp.int32, (half, half), 0)
                  >= lax.broadcasted_iota(jnp.int32, (half, half), 1))
    causal_bot = (lax.broadcasted_iota(jnp.int32, (half, tl), 0) + half
                  >= lax.broadcasted_iota(jnp.int32, (half, tl), 1))
    lane_hh = lax.broadcasted_iota(jnp.int32, (half, LANES), 1)
    low_hh = lane_hh < SSD_HEAD_DIM

    def pair_body(hp, carry):
        g = hp // pairs_per_group
        cb_top = cb_s[g, 0:half, 0:half]
        cb_bot = cb_s[g, half:tl, :]
        x_top = xdt_s[hp, 0:half, :]
        x_all = xdt_s[hp]
        cs_top = cs_s[0:half, :]
        cs_bot = cs_s[half:tl, :]
        tops, bots = [], []
        for e in range(2):
            h = 2 * hp + e
            head_lane = jnp.full((half, LANES), h, jnp.int32)
            col_top = jnp.take_along_axis(cs_top, head_lane, axis=1)
            col_bot = jnp.take_along_axis(cs_bot, head_lane, axis=1)
            col_bot = jnp.concatenate([col_bot] * (tl // LANES), axis=1)
            row_all = cst_s[pl.ds(h, 1), :]
            row_top = row_all[:, 0:half]
            m_top = jnp.exp2(jnp.where(causal_top, col_top - row_top, -jnp.inf)) * cb_top
            m_bot = jnp.exp2(jnp.where(causal_bot, col_bot - row_all, -jnp.inf)) * cb_bot
            tops.append(jnp.dot(m_top.astype(BF16), x_top, preferred_element_type=F32))
            bots.append(jnp.dot(m_bot.astype(BF16), x_all, preferred_element_type=F32))
        y_s[hp, 0:half, :] = y_s[hp, 0:half, :] + jnp.where(low_hh, tops[0], tops[1])
        y_s[hp, half:tl, :] = y_s[hp, half:tl, :] + jnp.where(low_hh, bots[0], bots[1])
        return carry

    lax.fori_loop(0, n_pairs, pair_body, 0, unroll=16)

    for g in range(SSD_GROUPS):
        gs = slice(g * gw, (g + 1) * gw)
        yg = jnp.concatenate([y_s[g * pairs_per_group + q] for q in range(pairs_per_group)], axis=1)
        yg = yg * _silu(z_ref[:, gs].astype(F32))
        ms = jnp.mean(yg * yg, axis=-1, keepdims=True)
        o_ref[:, gs] = (yg * lax.rsqrt(ms + EPS) * nw_ref[:, gs]).astype(o_ref.dtype)


def _ssd(proj, dt_raw, blk, conv_w, conv_b, dt_bias, a_log, d_skip, ssd_norm_w, seq):
    t = proj.shape[0]
    tl = SEQ_TILE
    heads = dt_bias.shape[0]
    inner = ssd_norm_w.shape[0]
    bc = SSD_GROUPS * SSD_STATE
    gw = inner // SSD_GROUPS
    pad = LANES - heads
    row = lambda v: v.astype(F32).reshape(1, -1)
    dtb = jnp.pad(row(dt_bias), ((0, 0), (0, pad)))
    alog = jnp.pad(row(a_log), ((0, 0), (0, pad)))
    dsk = jnp.repeat(d_skip.astype(F32), SSD_HEAD_DIM).reshape(1, inner)
    expand = (lax.broadcasted_iota(jnp.int32, (LANES, inner), 1) // SSD_HEAD_DIM
              == lax.broadcasted_iota(jnp.int32, (LANES, inner), 0)).astype(BF16)
    cw = conv_w.astype(F32)
    cbias = row(conv_b)
    full = lambda shape: pl.BlockSpec(shape, lambda i: tuple(0 for _ in shape))
    return pl.pallas_call(
        functools.partial(_ssd_kernel, tiles_per_seq=seq // tl, heads=heads),
        grid=(t // tl,),
        in_specs=[pl.BlockSpec((tl, inner), lambda i: (i, blk["z"])),
                  pl.BlockSpec((tl, inner), lambda i: (i, blk["xs"])),
                  pl.BlockSpec((tl, bc), lambda i: (i, blk["B"])),
                  pl.BlockSpec((tl, bc), lambda i: (i, blk["C"])),
                  pl.BlockSpec((tl, LANES), lambda i: (i, 0)),
                  full((SSD_CONV, inner)), full((SSD_CONV, bc)), full((SSD_CONV, bc)),
                  full((1, inner)), full((1, bc)), full((1, bc)),
                  full((1, LANES)), full((1, LANES)), full((1, inner)), full((1, inner)),
                  full((LANES, inner))],
        out_specs=pl.BlockSpec((tl, inner), lambda i: (i, 0)),
        out_shape=jax.ShapeDtypeStruct((t, inner), BF16),
        scratch_shapes=[pltpu.VMEM((tl + CONV_HALO, inner), F32),
                        pltpu.VMEM((tl + CONV_HALO, bc), F32),
                        pltpu.VMEM((tl + CONV_HALO, bc), F32),
                        pltpu.VMEM((SSD_GROUPS, SSD_STATE, gw), F32),
                        pltpu.VMEM((inner // LANES, tl, LANES), BF16),
                        pltpu.VMEM((inner // LANES, tl, LANES), F32),
                        pltpu.VMEM((SSD_GROUPS, tl, tl), F32),
                        pltpu.VMEM((tl, LANES), F32),
                        pltpu.VMEM((LANES, tl), F32),
                        pltpu.VMEM((tl, inner), F32),
                        pltpu.VMEM((tl, bc), F32),
                        pltpu.VMEM((tl, bc), F32)],
        compiler_params=_params("arbitrary"),
        name="ssd",
    )(proj, proj, proj, proj, dt_raw,
      cw[:, :inner], cw[:, inner:inner + bc], cw[:, inner + bc:],
      cbias[:, :inner], cbias[:, inner:inner + bc], cbias[:, inner + bc:],
      dtb, alog, dsk, row(ssd_norm_w), expand)


def _merge_kernel(yp_ref, ys_ref, gp_ref, gs_ref, wp_ref, ws_ref, o_ref):
    a = jnp.dot(yp_ref[...], wp_ref[...], preferred_element_type=F32)
    b = jnp.dot(ys_ref[...], ws_ref[...], preferred_element_type=F32)
    o_ref[...] = (jax.nn.sigmoid(gp_ref[...].astype(F32)) * a
                  + jax.nn.sigmoid(gs_ref[...].astype(F32)) * b).astype(o_ref.dtype)


def _merge(ypool, yssd, proj, gp_off, gs_off, wbp, wbs):
    t, d = ypool.shape
    inner = yssd.shape[1]
    tm, tn = min(MERGE_TM, t), MERGE_TN
    return pl.pallas_call(
        _merge_kernel,
        grid=(t // tm, d // tn),
        in_specs=[pl.BlockSpec((tm, d), lambda i, j: (i, 0)),
                  pl.BlockSpec((tm, inner), lambda i, j: (i, 0)),
                  pl.BlockSpec((tm, tn), lambda i, j: (i, gp_off // tn + j)),
                  pl.BlockSpec((tm, tn), lambda i, j: (i, gs_off // tn + j)),
                  pl.BlockSpec((d, tn), lambda i, j: (0, j)),
                  pl.BlockSpec((inner, tn), lambda i, j: (0, j))],
        out_specs=pl.BlockSpec((tm, tn), lambda i, j: (i, j)),
        out_shape=jax.ShapeDtypeStruct((t, d), BF16),
        compiler_params=_params("arbitrary", "arbitrary"),
        name="merge",
    )(ypool, yssd, proj, proj, wbp, wbs)


def _out_kernel(m_ref, wo_ref, x_ref, mod_ref, wr_ref, br_ref,
                h_ref, u_ref, idx_ref, gate_ref, rank_ref, cnt_ref, carry_s):
    i = pl.program_id(0)
    tm = m_ref.shape[0]
    ne = wr_ref.shape[0]

    @pl.when(i == 0)
    def _():
        carry_s[...] = jnp.zeros_like(carry_s)

    mix = jnp.dot(m_ref[...], wo_ref[...], preferred_element_type=F32)
    h1 = x_ref[...] + mod_ref[0, 2:3, :] * mix
    h_ref[...] = h1
    ms = jnp.mean(h1 * h1, axis=-1, keepdims=True)
    u = h1 * lax.rsqrt(ms + EPS) * (1.0 + mod_ref[0, 4:5, :]) + mod_ref[0, 3:4, :]
    u_ref[...] = _rows_to_tiles(_pack_bf16_pairs(u))
    logits = lax.dot_general(wr_ref[...], u, (((1,), (1,)), ((), ())), precision=HIGHEST,
                             preferred_element_type=F32) + br_ref[...]
    eidx = lax.broadcasted_iota(jnp.int32, (ne, tm), 0)
    work = logits
    vals, idxs, hots = [], [], []
    for _ in range(TOP_K):
        mx = jnp.max(work, axis=0, keepdims=True)
        sel = jnp.min(jnp.where(work == mx, eidx, ne), axis=0, keepdims=True)
        hot = eidx == sel
        vals.append(mx)
        idxs.append(sel)
        hots.append(hot)
        work = jnp.where(hot, -jnp.inf, work)
    exps = [jnp.exp(v - vals[0]) for v in vals]
    den = exps[0]
    for e in exps[1:]:
        den = den + e
    cnt = hots[0].astype(F32)
    for hot in hots[1:]:
        cnt = cnt + hot.astype(F32)
    ti = lax.broadcasted_iota(jnp.int32, (tm, tm), 0)
    tj = lax.broadcasted_iota(jnp.int32, (tm, tm), 1)
    before = (ti < tj).astype(BF16)
    prefix = jnp.dot(cnt.astype(BF16), before, preferred_element_type=F32)
    base = carry_s[:, 0:1] + prefix
    pad_rows = idx_ref.shape[0] - TOP_K
    ranks = [jnp.sum(jnp.where(hot, base, 0.0), axis=0, keepdims=True) for hot in hots]
    idx_ref[...] = jnp.concatenate(idxs + [jnp.zeros((pad_rows, tm), jnp.int32)], axis=0)
    gate_ref[...] = jnp.concatenate([e / den for e in exps] + [jnp.zeros((pad_rows, tm), F32)], axis=0)
    rank_ref[...] = jnp.concatenate([r.astype(jnp.int32) for r in ranks]
                                    + [jnp.zeros((pad_rows, tm), jnp.int32)], axis=0)
    carry_s[...] = carry_s[...] + jnp.sum(cnt, axis=1, keepdims=True)
    cnt_ref[...] = carry_s[...].astype(jnp.int32)


def _outproj(merged, wo, x2, mod3, w_router, b_router, seq):
    t, d = x2.shape
    ne = w_router.shape[1]
    tm = min(OUT_TM, seq)
    tiles_per_seq = seq // tm
    rows = 8
    return pl.pallas_call(
        _out_kernel,
        grid=(t // tm,),
        in_specs=[pl.BlockSpec((tm, d), lambda i: (i, 0)),
                  pl.BlockSpec((d, d), lambda i: (0, 0), pipeline_mode=pl.Buffered(1)),
                  pl.BlockSpec((tm, d), lambda i: (i, 0)),
                  pl.BlockSpec((1, N_MOD, d), lambda i: (i // tiles_per_seq, 0, 0)),
                  pl.BlockSpec((ne, d), lambda i: (0, 0)),
                  pl.BlockSpec((ne, 1), lambda i: (0, 0))],
        out_specs=[pl.BlockSpec((tm, d), lambda i: (i, 0)),
                   pl.BlockSpec((tm,) + _token_tile(d), lambda i: (i, 0, 0)),
                   pl.BlockSpec((rows, tm), lambda i: (0, i)),
                   pl.BlockSpec((rows, tm), lambda i: (0, i)),
                   pl.BlockSpec((rows, tm), lambda i: (0, i)),
                   pl.BlockSpec((ne, LANES), lambda i: (0, 0))],
        out_shape=[jax.ShapeDtypeStruct((t, d), F32), jax.ShapeDtypeStruct((t,) + _token_tile(d), jnp.uint32),
                   jax.ShapeDtypeStruct((rows, t), jnp.int32), jax.ShapeDtypeStruct((rows, t), F32),
                   jax.ShapeDtypeStruct((rows, t), jnp.int32), jax.ShapeDtypeStruct((ne, LANES), jnp.int32)],
        scratch_shapes=[pltpu.VMEM((ne, LANES), F32)],
        compiler_params=_params("arbitrary"),
        name="outproj",
    )(merged, wo, x2, mod3, w_router.astype(F32).T, b_router.astype(F32).reshape(ne, 1))


def _row_copy(src_hbm, src_row, dst_ref, dst_row, sem):
    return pltpu.make_async_copy(src_hbm.at[pl.ds(src_row, 1)], dst_ref.at[pl.ds(dst_row, 1)], sem)


def _dispatch_kernel(slot_ref, u_ref, init_hbm, xs_hbm, sem):
    del init_hbm
    ts = u_ref.shape[0]

    def issue(tok, carry):
        for k in range(TOP_K):
            _row_copy(u_ref, tok, xs_hbm, slot_ref[0, 0, tok * TOP_K + k], sem).start(priority=k % 2)
        return carry

    lax.fori_loop(0, ts, issue, 0)
    rows = pl.ds(0, ts * TOP_K)
    pltpu.make_async_copy(xs_hbm.at[rows], xs_hbm.at[rows], sem).wait()


def _dispatch(u2, slot_tk, n_slots):
    t, tile = u2.shape[0], u2.shape[1:]
    ts = min(DISPATCH_TS, t)
    slots3 = slot_tk.reshape(t // ts, 1, ts * TOP_K)
    init = jnp.zeros((n_slots,) + tile, u2.dtype)
    return pl.pallas_call(
        _dispatch_kernel,
        grid=(t // ts,),
        in_specs=[pl.BlockSpec((1, 1, ts * TOP_K), lambda i: (i, 0, 0), memory_space=pltpu.SMEM),
                  pl.BlockSpec((ts,) + tile, lambda i: (i, 0, 0)),
                  pl.BlockSpec(memory_space=pl.ANY)],
        out_specs=pl.BlockSpec(memory_space=pl.ANY),
        out_shape=jax.ShapeDtypeStruct((n_slots,) + tile, u2.dtype),
        scratch_shapes=[pltpu.SemaphoreType.DMA(())],
        input_output_aliases={2: 0},
        compiler_params=_params("arbitrary"),
        name="dispatch",
    )(slots3, u2, init)


ITEM_IDLE, ITEM_ACTIVE, ITEM_ZERO_FILL, ITEM_ACTIVE_HALF = 0, 1, 2, 3


def _moe_kernel(tb_ref, tj_ref, te_ref, tr_ref, tflag_ref, tout_ref, tslot_ref, tne_ref, tnj_ref, thn_ref,
                x_ref, wgu_hbm, bg_ref, bu_ref, wd_hbm, bd_ref, o_ref,
                wg_s, wu_s, wd_s, xb_s, acc_s, wg_buf, wu_buf, wd_buf, sem, *, last_j):
    del tb_ref, tout_ref
    i = pl.program_id(0)
    flag = tflag_ref[i]
    r = tr_ref[i]
    j = tj_ref[i]
    half = x_ref.shape[1] * x_ref.shape[2]
    tf = wg_s.shape[1]
    up_col0 = wgu_hbm.shape[2] // 2

    def weight_copies(e, jt, slot):
        c0 = pl.multiple_of(jt * tf, tf)
        return (pltpu.make_async_copy(wgu_hbm.at[e, :, pl.ds(c0, tf)], wg_buf.at[slot], sem.at[slot]),
                pltpu.make_async_copy(wgu_hbm.at[e, :, pl.ds(up_col0 + c0, tf)], wu_buf.at[slot], sem.at[slot]),
                pltpu.make_async_copy(wd_hbm.at[e, pl.ds(c0, tf), :], wd_buf.at[slot], sem.at[slot]))

    is_active = (flag & 1) == 1

    @pl.when(is_active & (r == 0))
    def _():
        slot = tslot_ref[i]

        @pl.when(i == 0)
        def _():
            for cp in weight_copies(te_ref[i], j, slot):
                cp.start()

        for cp in weight_copies(te_ref[i], j, slot):
            cp.wait()
        for s in range(2):
            @pl.when(slot == s)
            def _():
                wg_s[...] = wg_buf[s].astype(BF16)
                wu_s[...] = wu_buf[s].astype(BF16)
                wd_s[...] = wd_buf[s].astype(BF16)

        @pl.when(thn_ref[i] == 1)
        def _():
            for cp in weight_copies(tne_ref[i], tnj_ref[i], 1 - slot):
                cp.start()

    @pl.when(is_active & (j == 0))
    def _():
        lo, hi = _unpack_bf16_pairs(_tiles_to_rows(x_ref[...]))
        xb_s[r, :, :half] = lo
        xb_s[r, :, half:] = hi

    def expert_mlp(rows):
        xb = xb_s[r, 0:rows, :]
        gate = jnp.dot(xb, wg_s[...], preferred_element_type=F32) + bg_ref[...]
        up = jnp.dot(xb, wu_s[...], preferred_element_type=F32) + bu_ref[...]
        gate = jnp.minimum(gate, SWIGLU_LIMIT)
        up = jnp.clip(up, -SWIGLU_LIMIT, SWIGLU_LIMIT)
        act = (up + 1.0) * gate * jax.nn.sigmoid(SWIGLU_ALPHA * gate)
        part = jnp.dot(act.astype(BF16), wd_s[...], preferred_element_type=F32)

        @pl.when(j == 0)
        def _():
            acc_s[r, 0:rows, :] = part + bd_ref[...]

        @pl.when(j > 0)
        def _():
            acc_s[r, 0:rows, :] = acc_s[r, 0:rows, :] + part

        @pl.when(j == last_j)
        def _():
            o_ref[0:rows] = _rows_to_tiles(_pack_bf16_pairs(acc_s[r, 0:rows, :]))
            if rows < o_ref.shape[0]:
                o_ref[rows:] = jnp.zeros((o_ref.shape[0] - rows,) + o_ref.shape[1:], o_ref.dtype)

    @pl.when(flag == ITEM_ACTIVE)
    def _():
        expert_mlp(o_ref.shape[0])

    @pl.when(flag == ITEM_ACTIVE_HALF)
    def _():
        expert_mlp(o_ref.shape[0] // 2)

    @pl.when(flag == ITEM_ZERO_FILL)
    def _():
        o_ref[...] = jnp.zeros_like(o_ref)


def _lookup(table, idx):
    hot = idx[:, None] == jnp.arange(table.shape[0], dtype=jnp.int32)[None, :]
    return jnp.sum(jnp.where(hot, table[None, :].astype(jnp.int32), 0), axis=1).astype(jnp.int32)


def _moe_tables(counts, n_blocks, ft):
    ne = counts.shape[0]
    padded = (counts + MOE_BM - 1) // MOE_BM * MOE_BM
    pend = jnp.cumsum(padded)
    pstart = pend - padded
    blocks_per_expert = padded // MOE_BM
    first_block = pstart // MOE_BM
    n_active = (pend[-1] // MOE_BM).astype(jnp.int32)

    blocks = jnp.arange(n_blocks, dtype=jnp.int32)
    block_expert = jnp.minimum(jnp.sum(pend[None, :] <= (blocks * MOE_BM)[:, None], axis=1), ne - 1).astype(jnp.int32)
    off = blocks - _lookup(first_block, block_expert)
    is_start = (off % MOE_CHUNK_BLOCKS == 0) & (blocks < n_active)
    chunk_size = jnp.minimum(MOE_CHUNK_BLOCKS,
                             _lookup(blocks_per_expert, block_expert) - off // MOE_CHUNK_BLOCKS * MOE_CHUNK_BLOCKS)

    items = jnp.arange(n_blocks * ft, dtype=jnp.int32)
    active = items < ft * n_active
    src = jnp.minimum(items, jnp.maximum(ft * n_active - 1, 0))
    cand = jnp.where(is_start[None, :] & (ft * blocks[None, :] <= src[:, None]), blocks[None, :], 0)
    cstart = jnp.max(cand, axis=1).astype(jnp.int32)
    m = jnp.maximum(_lookup(chunk_size, cstart), 1)
    local = src - ft * cstart
    tj = local // m
    tr = local % m
    tb = cstart + tr
    te = _lookup(block_expert, cstart)
    k = items - ft * n_active
    zero_fill = (~active) & (n_active + k < n_blocks)
    tout = jnp.where(active, jnp.where(tj == ft - 1, tb, cstart), jnp.minimum(n_active + k, n_blocks - 1))
    valid_rows = _lookup(counts, te) - (tb - _lookup(first_block, te)) * MOE_BM
    kind = jnp.where(valid_rows <= MOE_BM // 2, ITEM_ACTIVE_HALF, ITEM_ACTIVE)
    tflag = jnp.where(active, kind, jnp.where(zero_fill, ITEM_ZERO_FILL, ITEM_IDLE))
    group_start = active & (tr == 0)
    tslot = (jnp.cumsum(group_start.astype(jnp.int32)) - 1) % 2
    nxt = items + m
    has_next = group_start & (nxt < ft * n_active)
    nxt = jnp.minimum(nxt, n_blocks * ft - 1)
    tne, tnj = jnp.take(te, nxt), jnp.take(tj, nxt)
    as_i32 = lambda v: v.astype(jnp.int32)
    return pstart, tuple(map(as_i32, (tb, tj, te, tr, tflag, tout, tslot, tne, tnj, has_next)))


def _moe(xs, tables, wgu, bgu, wd, bd):
    n_slots = xs.shape[0]
    ne, d, f2 = wgu.shape
    f = f2 // 2
    bm, tf = MOE_BM, min(MOE_TF, f)
    nb, ft = n_slots // bm, f // tf
    grid_spec = pltpu.PrefetchScalarGridSpec(
        num_scalar_prefetch=len(tables),
        grid=(nb * ft,),
        in_specs=[pl.BlockSpec((bm,) + _token_tile(d), lambda i, tb, tj, te, *_: (tb[i], 0, 0)),
                  pl.BlockSpec(memory_space=pl.ANY),
                  pl.BlockSpec((None, 1, tf), lambda i, tb, tj, te, *_: (te[i], 0, tj[i])),
                  pl.BlockSpec((None, 1, tf), lambda i, tb, tj, te, *_: (te[i], 0, ft + tj[i])),
                  pl.BlockSpec(memory_space=pl.ANY),
                  pl.BlockSpec((None, 1, d), lambda i, tb, tj, te, *_: (te[i], 0, 0))],
        out_specs=pl.BlockSpec((bm,) + _token_tile(d), lambda i, tb, tj, te, tr, tg, to, *_: (to[i], 0, 0)),
        scratch_shapes=[pltpu.VMEM((d, tf), BF16), pltpu.VMEM((d, tf), BF16), pltpu.VMEM((tf, d), BF16),
                        pltpu.VMEM((MOE_CHUNK_BLOCKS, bm, d), BF16), pltpu.VMEM((MOE_CHUNK_BLOCKS, bm, d), F32),
                        pltpu.VMEM((2, d, tf), F32), pltpu.VMEM((2, d, tf), F32), pltpu.VMEM((2, tf, d), F32),
                        pltpu.SemaphoreType.DMA((2,))],
    )
    return pl.pallas_call(
        functools.partial(_moe_kernel, last_j=ft - 1),
        grid_spec=grid_spec,
        out_shape=jax.ShapeDtypeStruct((n_slots,) + _token_tile(d), jnp.uint32),
        compiler_params=_params("arbitrary"),
        name="moe",
    )(*tables, xs, wgu, bgu, bgu, wd, bd)


def _final_kernel(slot_ref, h_ref, gate_ref, mod_ref, fw_ref, y_hbm, o_ref, ybuf, sem):
    tc, d = h_ref.shape
    half = d // 2

    def issue(grp, carry):
        for q in range(FINAL_ISSUE_UNROLL):
            tk = FINAL_ISSUE_UNROLL * grp + q
            _row_copy(y_hbm, slot_ref[0, 0, tk], ybuf, tk, sem).start(priority=q % 2)
        return carry

    lax.fori_loop(0, tc * TOP_K // FINAL_ISSUE_UNROLL, issue, 0)
    pltpu.make_async_copy(y_hbm.at[pl.ds(0, tc * TOP_K)], ybuf, sem).wait()

    ffn_lo = jnp.zeros((tc, half), F32)
    ffn_hi = jnp.zeros((tc, half), F32)
    for k in range(TOP_K):
        w = _tiles_to_rows(ybuf[k * tc:(k + 1) * tc])
        g = gate_ref[:, k:k + 1]
        ffn_lo = ffn_lo + g * lax.bitcast_convert_type(w << 16, F32)
        ffn_hi = ffn_hi + g * lax.bitcast_convert_type(w & jnp.uint32(0xFFFF0000), F32)
    h2_lo = h_ref[:, :half] + mod_ref[0, 5:6, :half] * ffn_lo
    h2_hi = h_ref[:, half:] + mod_ref[0, 5:6, half:] * ffn_hi
    ms = (jnp.sum(h2_lo * h2_lo, axis=-1, keepdims=True) + jnp.sum(h2_hi * h2_hi, axis=-1, keepdims=True)) / d
    inv = lax.rsqrt(ms + EPS)
    o_ref[:, :half] = (h2_lo * inv * fw_ref[:, :half]).astype(o_ref.dtype)
    o_ref[:, half:] = (h2_hi * inv * fw_ref[:, half:]).astype(o_ref.dtype)


def _final(h1, gates_tk, slot_kt, mod3, final_norm_w, y_slots, seq, out_dtype):
    t, d = h1.shape
    tc = min(FINAL_TC, seq)
    tiles_per_seq = seq // tc
    slots3 = slot_kt.reshape(TOP_K, t // tc, tc).transpose(1, 0, 2).reshape(t // tc, 1, TOP_K * tc)
    return pl.pallas_call(
        _final_kernel,
        grid=(t // tc,),
        in_specs=[pl.BlockSpec((1, 1, TOP_K * tc), lambda i: (i, 0, 0), memory_space=pltpu.SMEM),
                  pl.BlockSpec((tc, d), lambda i: (i, 0)),
                  pl.BlockSpec((tc, gates_tk.shape[1]), lambda i: (i, 0)),
                  pl.BlockSpec((1, N_MOD, d), lambda i: (i // tiles_per_seq, 0, 0)),
                  pl.BlockSpec((1, d), lambda i: (0, 0)),
                  pl.BlockSpec(memory_space=pl.ANY)],
        out_specs=pl.BlockSpec((tc, d), lambda i: (i, 0)),
        out_shape=jax.ShapeDtypeStruct((t, d), out_dtype),
        scratch_shapes=[pltpu.VMEM((TOP_K * tc,) + _token_tile(d), jnp.uint32), pltpu.SemaphoreType.DMA(())],
        compiler_params=_params("arbitrary"),
        name="final",
    )(slots3, h1, gates_tk, mod3, final_norm_w.astype(F32).reshape(1, d), y_slots)


def kernel(x, c, w_ada, b_ada, w_in, pool_w, pool_scale, conv_w, conv_b, dt_bias, a_log, d_skip, ssd_norm_w,
           w_branch_pool, w_branch_ssd, w_out, w_router, b_router, w_gate_up, b_gate_up, w_down, b_down,
           final_norm_w):
    bsz, seq, d = x.shape
    depth = w_ada.shape[0]
    t = bsz * seq
    inner = ssd_norm_w.shape[1]
    heads = dt_bias.shape[1]
    bc = SSD_GROUPS * SSD_STATE
    ne = w_router.shape[2]
    assert depth == 1, "the final RMSNorm is fused into the last kernel of the single layer"
    assert SEQ_TILE == 2 * LANES
    assert seq % SEQ_TILE == 0 and heads <= LANES and (7 * d) % bc == 0 and (4 * d) % inner == 0
    assert inner // SSD_GROUPS % LANES == 0 and d % (len(POOL_WINDOWS) * LANES) == 0
    assert d % (2 * SUBLANES * LANES) == 0

    s1 = d
    s2 = s1 + inner
    s3 = s2 + inner + 2 * bc
    s4 = s3 + heads
    s5 = s4 + d
    blk = {"z": 0, "xs": 1, "p": (2 * inner) // d, "B": (2 * inner + 3 * d) // bc, "C": (2 * inner + 3 * d) // bc + 1}
    gp_off, gs_off = 2 * inner + d, 2 * inner + 2 * d

    h = x.astype(F32).reshape(t, d)
    for layer in range(depth):
        wl = w_in[layer]
        w_main = jnp.concatenate([wl[:, s1:s2], wl[:, s2:s2 + inner], wl[:, :s1], wl[:, s4:s5], wl[:, s5:],
                                  wl[:, s2 + inner:s2 + inner + bc], wl[:, s2 + inner + bc:s3]], axis=1).astype(BF16)
        w_dt = jnp.pad(wl[:, s3:s4].astype(F32), ((0, 0), (0, LANES - heads)))

        mod3 = _ada(c, w_ada[layer], b_ada[layer]).reshape(bsz, N_MOD, d)
        proj, dt_raw = _inproj(h, mod3, w_main, w_dt, seq)
        ypool = _pool(proj, blk["p"], pool_w[layer].astype(BF16), pool_scale[layer], seq, d)
        yssd = _ssd(proj, dt_raw, blk, conv_w[layer], conv_b[layer], dt_bias[layer], a_log[layer], d_skip[layer],
                    ssd_norm_w[layer], seq)
        merged = _merge(ypool, yssd, proj, gp_off, gs_off, w_branch_pool[layer].astype(BF16),
                        w_branch_ssd[layer].astype(BF16))
        h1, u2, idx_kt, gate_kt, rank_kt, counts = _outproj(merged, w_out[layer].astype(BF16), h, mod3,
                                                            w_router[layer], b_router[layer], seq)

        n_blocks = (t * TOP_K) // MOE_BM + ne
        ft = w_down.shape[2] // min(MOE_TF, w_down.shape[2])
        pstart, tables = _moe_tables(counts[:, 0], n_blocks, ft)
        eids = jnp.arange(ne, dtype=jnp.int32)[:, None, None]
        slot_kt = (jnp.sum(jnp.where(idx_kt[:TOP_K][None] == eids, pstart[:, None, None], 0), axis=0)
                   + rank_kt[:TOP_K]).astype(jnp.int32)

        xs = _dispatch(u2, slot_kt.T, n_blocks * MOE_BM)
        y_slots = _moe(xs, tables, w_gate_up[layer].astype(F32), b_gate_up[layer].astype(F32)[:, None, :],
                       w_down[layer].astype(F32), b_down[layer].astype(F32)[:, None, :])
        h = _final(h1, gate_kt.T, slot_kt, mod3, final_norm_w, y_slots, seq, x.dtype)
    return h.reshape(bsz, seq, d)
```

```python
import functools

import jax
import jax.numpy as jnp
from jax import lax
from jax.experimental import pallas as pl
from jax.experimental.pallas import tpu as pltpu

F32 = jnp.float32
BF16 = jnp.bfloat16
HIGHEST = lax.Precision.HIGHEST

EPS = 1e-6
POOL_WINDOWS = (2, 4, 8, 16)
SSD_GROUPS = 8
SSD_STATE = 128
SSD_CONV = 4
SSD_HEAD_DIM = 64
TOP_K = 4
SWIGLU_LIMIT = 7.0
SWIGLU_ALPHA = 1.702
N_MOD = 6

LOG2_E = 1.4426950408889634
LANES = 128
SUBLANES = 8
CONV_HALO = 8
V7X_VMEM_LIMIT = 56 * 1024 * 1024

SEQ_TILE = 256
INPROJ_TM, INPROJ_TN = 1024, 2048
MERGE_TM, MERGE_TN = 1024, 512
OUT_TM = 512
MOE_BM, MOE_TF = 512, 512
MOE_CHUNK_BLOCKS = 2
DISPATCH_TS = 512
FINAL_TC = 512
FINAL_ISSUE_UNROLL = 8


def _params(*sem):
    return pltpu.CompilerParams(dimension_semantics=sem, vmem_limit_bytes=V7X_VMEM_LIMIT)


def _silu(v):
    return v * jax.nn.sigmoid(v)


def _pack_bf16_pairs(v):
    half = v.shape[1] // 2
    lo = lax.bitcast_convert_type(v[:, :half].astype(BF16).astype(F32), jnp.uint32)
    hi = lax.bitcast_convert_type(v[:, half:].astype(BF16).astype(F32), jnp.uint32)
    return (hi & jnp.uint32(0xFFFF0000)) | (lo >> 16)


def _token_tile(d):
    return (d // 2 // LANES, LANES)


def _rows_to_tiles(w):
    return w.reshape(w.shape[0], w.shape[1] // LANES, LANES)


def _tiles_to_rows(w):
    return w.reshape(w.shape[0], w.shape[1] * LANES)


def _unpack_bf16_pairs(w):
    lo = lax.bitcast_convert_type(w << 16, F32).astype(BF16)
    hi = lax.bitcast_convert_type(w & jnp.uint32(0xFFFF0000), F32).astype(BF16)
    return lo, hi


def _ada_kernel(cb_ref, w_ref, b_ref, o_ref):
    nb, tn = cb_ref.shape[0], w_ref.shape[1]
    for b in range(nb):
        ca = _silu(cb_ref[b])
        cols = [jnp.sum(w_ref[:, j * LANES:(j + 1) * LANES] * ca, axis=0, keepdims=True)
                for j in range(tn // LANES)]
        o_ref[b:b + 1, :] = jnp.concatenate(cols, axis=1) + b_ref[...]


def _ada(c, w_ada, b_ada):
    nb, k = c.shape
    n = w_ada.shape[1]
    tn = 1024 if n % 1024 == 0 else n
    cb = jnp.broadcast_to(c.astype(F32)[:, :, None], (nb, k, LANES))
    return pl.pallas_call(
        _ada_kernel,
        grid=(n // tn,),
        in_specs=[pl.BlockSpec((nb, k, LANES), lambda j: (0, 0, 0)),
                  pl.BlockSpec((k, tn), lambda j: (0, j)),
                  pl.BlockSpec((1, tn), lambda j: (0, j))],
        out_specs=pl.BlockSpec((nb, tn), lambda j: (0, j)),
        out_shape=jax.ShapeDtypeStruct((nb, n), F32),
        compiler_params=_params("arbitrary"),
        name="ada",
    )(cb, w_ada, b_ada.reshape(1, n))


def _inproj_kernel(x_ref, mod_ref, w_ref, wdt_hi_ref, wdt_lo_ref, o_ref, dt_ref, u_s, *, rows_per_chunk):
    j = pl.program_id(1)
    tm = x_ref.shape[0]

    @pl.when(j == 0)
    def _():
        sh = mod_ref[0, 0:1, :]
        sc = mod_ref[0, 1:2, :]

        def body(r, carry):
            rows = pl.ds(pl.multiple_of(r * rows_per_chunk, rows_per_chunk), rows_per_chunk)
            xv = x_ref[rows, :]
            ms = jnp.mean(xv * xv, axis=-1, keepdims=True)
            u = xv * lax.rsqrt(ms + EPS) * (1.0 + sc) + sh
            u_hi = u.astype(BF16)
            u_s[rows, :] = u_hi
            u_lo = (u - u_hi.astype(F32)).astype(BF16)
            dt_ref[rows, :] = (jnp.dot(u_hi, wdt_hi_ref[...], preferred_element_type=F32)
                               + (jnp.dot(u_hi, wdt_lo_ref[...], preferred_element_type=F32)
                                  + jnp.dot(u_lo, wdt_hi_ref[...], preferred_element_type=F32)))
            return carry

        lax.fori_loop(0, tm // rows_per_chunk, body, 0)

    o_ref[...] = jnp.dot(u_s[...], w_ref[...], preferred_element_type=F32).astype(o_ref.dtype)


def _inproj(x2, mod3, w_main, w_dt, seq):
    t, d = x2.shape
    w_dt_hi = w_dt.astype(BF16)
    w_dt_lo = (w_dt - w_dt_hi.astype(F32)).astype(BF16)
    n = w_main.shape[1]
    tm = min(INPROJ_TM, seq)
    tn = INPROJ_TN
    tiles_per_seq = seq // tm
    return pl.pallas_call(
        functools.partial(_inproj_kernel, rows_per_chunk=min(128, tm)),
        grid=(t // tm, n // tn),
        in_specs=[pl.BlockSpec((tm, d), lambda i, j: (i, 0)),
                  pl.BlockSpec((1, N_MOD, d), lambda i, j: (i // tiles_per_seq, 0, 0)),
                  pl.BlockSpec((d, tn), lambda i, j: (0, j)),
                  pl.BlockSpec((d, LANES), lambda i, j: (0, 0)),
                  pl.BlockSpec((d, LANES), lambda i, j: (0, 0))],
        out_specs=[pl.BlockSpec((tm, tn), lambda i, j: (i, j)),
                   pl.BlockSpec((tm, LANES), lambda i, j: (i, 0))],
        out_shape=[jax.ShapeDtypeStruct((t, n), BF16), jax.ShapeDtypeStruct((t, LANES), F32)],
        scratch_shapes=[pltpu.VMEM((tm, d), BF16)],
        compiler_params=_params("arbitrary", "arbitrary"),
        name="inproj",
    )(x2, mod3, w_main, w_dt_hi, w_dt_lo)


def _pool_kernel(p_ref, pw_ref, ps_ref, o_ref, prev_s, *, tiles_per_seq):
    i = pl.program_id(0)
    tl, d = p_ref.shape
    gd = d // len(POOL_WINDOWS)
    it = i % tiles_per_seq

    @pl.when(it == 0)
    def _():
        prev_s[...] = jnp.zeros_like(prev_s)

    row = lax.broadcasted_iota(jnp.int32, (tl, 2 * tl), 0)
    col = lax.broadcasted_iota(jnp.int32, (tl, 2 * tl), 1)
    pos = (it * tl + lax.broadcasted_iota(jnp.int32, (tl, 1), 0) + 1).astype(F32)
    for g, w in enumerate(POOL_WINDOWS):
        sl = slice(g * gd, (g + 1) * gd)
        cur = p_ref[:, sl]
        ext = jnp.concatenate([prev_s[:, sl], cur], axis=0)
        band = ((col <= row + tl) & (col > row + tl - w)).astype(BF16)
        win_sum = jnp.dot(band, ext, preferred_element_type=F32)
        mean = win_sum / jnp.minimum(pos, float(w))
        dlt = (mean - cur.astype(F32)).astype(BF16)
        mixed = jnp.dot(dlt, pw_ref[g], preferred_element_type=F32)
        o_ref[:, sl] = (mixed * ps_ref[:, sl]).astype(o_ref.dtype)
    prev_s[...] = p_ref[...]


def _pool(proj, p_blk, pool_w_bf, pool_scale, seq, d):
    t = proj.shape[0]
    tl = SEQ_TILE
    g, gd = pool_w_bf.shape[0], pool_w_bf.shape[1]
    return pl.pallas_call(
        functools.partial(_pool_kernel, tiles_per_seq=seq // tl),
        grid=(t // tl,),
        in_specs=[pl.BlockSpec((tl, d), lambda i: (i, p_blk)),
                  pl.BlockSpec((g, gd, gd), lambda i: (0, 0, 0)),
                  pl.BlockSpec((1, d), lambda i: (0, 0))],
        out_specs=pl.BlockSpec((tl, d), lambda i: (i, 0)),
        out_shape=jax.ShapeDtypeStruct((t, d), BF16),
        scratch_shapes=[pltpu.VMEM((tl, d), BF16)],
        compiler_params=_params("arbitrary"),
        name="pool",
    )(proj, pool_w_bf, pool_scale.reshape(1, d))


def _ssd_kernel(z_ref, xs_ref, bm_ref, cm_ref, dt_ref,
                cwx_ref, cwb_ref, cwc_ref, cbx_ref, cbb_ref, cbc_ref,
                dtb_ref, alog_ref, dsk_ref, nw_ref, exp_ref,
                o_ref,
                extx_s, extb_s, extc_s, state_s, xdt_s, y_s, cb_s, cs_s, cst_s, xc_s, bc_s, cc_s,
                *, tiles_per_seq, heads):
    i = pl.program_id(0)
    tl, inner = xs_ref.shape
    gw = inner // SSD_GROUPS
    pairs_per_group = gw // LANES
    n_pairs = inner // LANES

    @pl.when(i % tiles_per_seq == 0)
    def _():
        extx_s[0:CONV_HALO, :] = jnp.zeros((CONV_HALO, inner), F32)
        extb_s[0:CONV_HALO, :] = jnp.zeros((CONV_HALO, extb_s.shape[1]), F32)
        extc_s[0:CONV_HALO, :] = jnp.zeros((CONV_HALO, extc_s.shape[1]), F32)
        state_s[...] = jnp.zeros_like(state_s)

    def conv_silu(ext_ref, src_ref, w_ref, b_ref, dst_ref, width, cw):
        for c0 in range(0, width, cw):
            cs = slice(c0, c0 + cw)
            ext_ref[CONV_HALO:CONV_HALO + tl, cs] = src_ref[:, cs].astype(F32)
            acc = b_ref[:, cs] + w_ref[SSD_CONV - 1:SSD_CONV, cs] * ext_ref[CONV_HALO:CONV_HALO + tl, cs]
            for k in range(1, SSD_CONV):
                acc = acc + w_ref[SSD_CONV - 1 - k:SSD_CONV - k, cs] * ext_ref[pl.ds(CONV_HALO - k, tl), cs]
            dst_ref[:, cs] = _silu(acc)
            ext_ref[0:CONV_HALO, cs] = ext_ref[tl:tl + CONV_HALO, cs]

    conv_silu(extx_s, xs_ref, cwx_ref, cbx_ref, xc_s, inner, LANES)
    conv_silu(extb_s, bm_ref, cwb_ref, cbb_ref, bc_s, bm_ref.shape[1], LANES)
    conv_silu(extc_s, cm_ref, cwc_ref, cbc_ref, cc_s, cm_ref.shape[1], LANES)

    dtv = dt_ref[...] + dtb_ref[...]
    dt = jnp.maximum(dtv, 0.0) + jnp.log1p(jnp.exp(-jnp.abs(dtv)))
    a = -jnp.exp(alog_ref[...])
    da = dt * a
    ri = lax.broadcasted_iota(jnp.int32, (tl, tl), 0)
    ci = lax.broadcasted_iota(jnp.int32, (tl, tl), 1)
    causal = ri >= ci
    cs = jnp.dot(causal.astype(F32), da, precision=HIGHEST, preferred_element_type=F32)
    cs2 = cs * LOG2_E
    cs_s[...] = cs2
    cst_s[...] = cs2.T
    last = cs[tl - 1:tl, :]
    dt_bf = dt.astype(BF16)
    ecs_bf = jnp.exp(cs).astype(BF16)
    dte_bf = jnp.exp(last - cs).astype(BF16)

    for g in range(SSD_GROUPS):
        gs = slice(g * gw, (g + 1) * gw)
        ns = slice(g * SSD_STATE, (g + 1) * SSD_STATE)
        exp_g = exp_ref[:, gs]
        dt_x = jnp.dot(dt_bf, exp_g, preferred_element_type=F32)
        ecs_x = jnp.dot(ecs_bf, exp_g, preferred_element_type=F32)
        dte_x = jnp.dot(dte_bf, exp_g, preferred_element_type=F32)
        xg = xc_s[:, gs]
        xdt = xg * dt_x
        xdt_bf = xdt.astype(BF16)
        xd_bf = (xdt * dte_x).astype(BF16)
        bg = bc_s[:, ns]
        cg = cc_s[:, ns].astype(BF16)
        cb_s[g] = lax.dot_general(cg, bg.astype(BF16), (((1,), (1,)), ((), ())), preferred_element_type=F32)
        s_old = state_s[g]
        y_off = jnp.dot(cg, s_old.astype(BF16), preferred_element_type=F32) * ecs_x
        state_s[g] = (s_old * ecs_x[tl - 1:tl, :]
                      + jnp.dot(bg.T.astype(BF16), xd_bf, preferred_element_type=F32))
        y0 = y_off + dsk_ref[:, gs] * xg
        for q in range(pairs_per_group):
            qs = slice(q * LANES, (q + 1) * LANES)
            xdt_s[g * pairs_per_group + q] = xdt_bf[:, qs]
            y_s[g * pairs_per_group + q] = y0[:, qs]

    lane_h = lax.broadcasted_iota(jnp.int32, (tl, LANES), 1)
    low_half = lane_h < SSD_HEAD_DIM

    half = tl // 2
    causal_top = (lax.broadcasted_iota(jnp.int32, (half, half), 0)
                  >= lax.broadcasted_iota(jnp.int32, (half, half), 1))
    causal_bot = (lax.broadcasted_iota(jnp.int32, (half, tl), 0) + half
                  >= lax.broadcasted_iota(jnp.int32, (half, tl), 1))
    lane_hh = lax.broadcasted_iota(jnp.int32, (half, LANES), 1)
    low_hh = lane_hh < SSD_HEAD_DIM

    def pair_body(hp, carry):
        g = hp // pairs_per_group
        cb_top = cb_s[g, 0:half, 0:half]
        cb_bot = cb_s[g, half:tl, :]
        x_top = xdt_s[hp, 0:half, :]
        x_all = xdt_s[hp]
        cs_top = cs_s[0:half, :]
        cs_bot = cs_s[half:tl, :]
        tops, bots = [], []
        for e in range(2):
            h = 2 * hp + e
            head_lane = jnp.full((half, LANES), h, jnp.int32)
            col_top = jnp.take_along_axis(cs_top, head_lane, axis=1)
            col_bot = jnp.take_along_axis(cs_bot, head_lane, axis=1)
            col_bot = jnp.concatenate([col_bot] * (tl // LANES), axis=1)
            row_all = cst_s[pl.ds(h, 1), :]
            row_top = row_all[:, 0:half]
            m_top = jnp.exp2(jnp.where(causal_top, col_top - row_top, -jnp.inf)) * cb_top
            m_bot = jnp.exp2(jnp.where(causal_bot, col_bot - row_all, -jnp.inf)) * cb_bot
            tops.append(jnp.dot(m_top.astype(BF16), x_top, preferred_element_type=F32))
            bots.append(jnp.dot(m_bot.astype(BF16), x_all, preferred_element_type=F32))
        y_s[hp, 0:half, :] = y_s[hp, 0:half, :] + jnp.where(low_hh, tops[0], tops[1])
        y_s[hp, half:tl, :] = y_s[hp, half:tl, :] + jnp.where(low_hh, bots[0], bots[1])
        return carry

    lax.fori_loop(0, n_pairs, pair_body, 0, unroll=16)

    for g in range(SSD_GROUPS):
        gs = slice(g * gw, (g + 1) * gw)
        yg = jnp.concatenate([y_s[g * pairs_per_group + q] for q in range(pairs_per_group)], axis=1)
        yg = yg * _silu(z_ref[:, gs].astype(F32))
        ms = jnp.mean(yg * yg, axis=-1, keepdims=True)
        o_ref[:, gs] = (yg * lax.rsqrt(ms + EPS) * nw_ref[:, gs]).astype(o_ref.dtype)


def _ssd(proj, dt_raw, blk, conv_w, conv_b, dt_bias, a_log, d_skip, ssd_norm_w, seq):
    t = proj.shape[0]
    tl = SEQ_TILE
    heads = dt_bias.shape[0]
    inner = ssd_norm_w.shape[0]
    bc = SSD_GROUPS * SSD_STATE
    gw = inner // SSD_GROUPS
    pad = LANES - heads
    row = lambda v: v.astype(F32).reshape(1, -1)
    dtb = jnp.pad(row(dt_bias), ((0, 0), (0, pad)))
    alog = jnp.pad(row(a_log), ((0, 0), (0, pad)))
    dsk = jnp.repeat(d_skip.astype(F32), SSD_HEAD_DIM).reshape(1, inner)
    expand = (lax.broadcasted_iota(jnp.int32, (LANES, inner), 1) // SSD_HEAD_DIM
              == lax.broadcasted_iota(jnp.int32, (LANES, inner), 0)).astype(BF16)
    cw = conv_w.astype(F32)
    cbias = row(conv_b)
    full = lambda shape: pl.BlockSpec(shape, lambda i: tuple(0 for _ in shape))
    return pl.pallas_call(
        functools.partial(_ssd_kernel, tiles_per_seq=seq // tl, heads=heads),
        grid=(t // tl,),
        in_specs=[pl.BlockSpec((tl, inner), lambda i: (i, blk["z"])),
                  pl.BlockSpec((tl, inner), lambda i: (i, blk["xs"])),
                  pl.BlockSpec((tl, bc), lambda i: (i, blk["B"])),
                  pl.BlockSpec((tl, bc), lambda i: (i, blk["C"])),
                  pl.BlockSpec((tl, LANES), lambda i: (i, 0)),
                  full((SSD_CONV, inner)), full((SSD_CONV, bc)), full((SSD_CONV, bc)),
                  full((1, inner)), full((1, bc)), full((1, bc)),
                  full((1, LANES)), full((1, LANES)), full((1, inner)), full((1, inner)),
                  full((LANES, inner))],
        out_specs=pl.BlockSpec((tl, inner), lambda i: (i, 0)),
        out_shape=jax.ShapeDtypeStruct((t, inner), BF16),
        scratch_shapes=[pltpu.VMEM((tl + CONV_HALO, inner), F32),
                        pltpu.VMEM((tl + CONV_HALO, bc), F32),
                        pltpu.VMEM((tl + CONV_HALO, bc), F32),
                        pltpu.VMEM((SSD_GROUPS, SSD_STATE, gw), F32),
                        pltpu.VMEM((inner // LANES, tl, LANES), BF16),
                        pltpu.VMEM((inner // LANES, tl, LANES), F32),
                        pltpu.VMEM((SSD_GROUPS, tl, tl), F32),
                        pltpu.VMEM((tl, LANES), F32),
                        pltpu.VMEM((LANES, tl), F32),
                        pltpu.VMEM((tl, inner), F32),
                        pltpu.VMEM((tl, bc), F32),
                        pltpu.VMEM((tl, bc), F32)],
        compiler_params=_params("arbitrary"),
        name="ssd",
    )(proj, proj, proj, proj, dt_raw,
      cw[:, :inner], cw[:, inner:inner + bc], cw[:, inner + bc:],
      cbias[:, :inner], cbias[:, inner:inner + bc], cbias[:, inner + bc:],
      dtb, alog, dsk, row(ssd_norm_w), expand)


def _merge_kernel(yp_ref, ys_ref, gp_ref, gs_ref, wp_ref, ws_ref, o_ref):
    a = jnp.dot(yp_ref[...], wp_ref[...], preferred_element_type=F32)
    b = jnp.dot(ys_ref[...], ws_ref[...], preferred_element_type=F32)
    o_ref[...] = (jax.nn.sigmoid(gp_ref[...].astype(F32)) * a
                  + jax.nn.sigmoid(gs_ref[...].astype(F32)) * b).astype(o_ref.dtype)


def _merge(ypool, yssd, proj, gp_off, gs_off, wbp, wbs):
    t, d = ypool.shape
    inner = yssd.shape[1]
    tm, tn = min(MERGE_TM, t), MERGE_TN
    return pl.pallas_call(
        _merge_kernel,
        grid=(t // tm, d // tn),
        in_specs=[pl.BlockSpec((tm, d), lambda i, j: (i, 0)),
                  pl.BlockSpec((tm, inner), lambda i, j: (i, 0)),
                  pl.BlockSpec((tm, tn), lambda i, j: (i, gp_off // tn + j)),
                  pl.BlockSpec((tm, tn), lambda i, j: (i, gs_off // tn + j)),
                  pl.BlockSpec((d, tn), lambda i, j: (0, j)),
                  pl.BlockSpec((inner, tn), lambda i, j: (0, j))],
        out_specs=pl.BlockSpec((tm, tn), lambda i, j: (i, j)),
        out_shape=jax.ShapeDtypeStruct((t, d), BF16),
        compiler_params=_params("arbitrary", "arbitrary"),
        name="merge",
    )(ypool, yssd, proj, proj, wbp, wbs)


def _out_kernel(m_ref, wo_ref, x_ref, mod_ref, wr_ref, br_ref,
                h_ref, u_ref, idx_ref, gate_ref, rank_ref, cnt_ref, carry_s):
    i = pl.program_id(0)
    tm = m_ref.shape[0]
    ne = wr_ref.shape[0]

    @pl.when(i == 0)
    def _():
        carry_s[...] = jnp.zeros_like(carry_s)

    mix = jnp.dot(m_ref[...], wo_ref[...], preferred_element_type=F32)
    h1 = x_ref[...] + mod_ref[0, 2:3, :] * mix
    h_ref[...] = h1
    ms = jnp.mean(h1 * h1, axis=-1, keepdims=True)
    u = h1 * lax.rsqrt(ms + EPS) * (1.0 + mod_ref[0, 4:5, :]) + mod_ref[0, 3:4, :]
    u_ref[...] = _rows_to_tiles(_pack_bf16_pairs(u))
    logits = lax.dot_general(wr_ref[...], u, (((1,), (1,)), ((), ())), precision=HIGHEST,
                             preferred_element_type=F32) + br_ref[...]
    eidx = lax.broadcasted_iota(jnp.int32, (ne, tm), 0)
    work = logits
    vals, idxs, hots = [], [], []
    for _ in range(TOP_K):
        mx = jnp.max(work, axis=0, keepdims=True)
        sel = jnp.min(jnp.where(work == mx, eidx, ne), axis=0, keepdims=True)
        hot = eidx == sel
        vals.append(mx)
        idxs.append(sel)
        hots.append(hot)
        work = jnp.where(hot, -jnp.inf, work)
    exps = [jnp.exp(v - vals[0]) for v in vals]
    den = exps[0]
    for e in exps[1:]:
        den = den + e
    cnt = hots[0].astype(F32)
    for hot in hots[1:]:
        cnt = cnt + hot.astype(F32)
    ti = lax.broadcasted_iota(jnp.int32, (tm, tm), 0)
    tj = lax.broadcasted_iota(jnp.int32, (tm, tm), 1)
    before = (ti < tj).astype(BF16)
    prefix = jnp.dot(cnt.astype(BF16), before, preferred_element_type=F32)
    base = carry_s[:, 0:1] + prefix
    pad_rows = idx_ref.shape[0] - TOP_K
    ranks = [jnp.sum(jnp.where(hot, base, 0.0), axis=0, keepdims=True) for hot in hots]
    idx_ref[...] = jnp.concatenate(idxs + [jnp.zeros((pad_rows, tm), jnp.int32)], axis=0)
    gate_ref[...] = jnp.concatenate([e / den for e in exps] + [jnp.zeros((pad_rows, tm), F32)], axis=0)
    rank_ref[...] = jnp.concatenate([r.astype(jnp.int32) for r in ranks]
                                    + [jnp.zeros((pad_rows, tm), jnp.int32)], axis=0)
    carry_s[...] = carry_s[...] + jnp.sum(cnt, axis=1, keepdims=True)
    cnt_ref[...] = carry_s[...].astype(jnp.int32)


def _outproj(merged, wo, x2, mod3, w_router, b_router, seq):
    t, d = x2.shape
    ne = w_router.shape[1]
    tm = min(OUT_TM, seq)
    tiles_per_seq = seq // tm
    rows = 8
    return pl.pallas_call(
        _out_kernel,
        grid=(t // tm,),
        in_specs=[pl.BlockSpec((tm, d), lambda i: (i, 0)),
                  pl.BlockSpec((d, d), lambda i: (0, 0), pipeline_mode=pl.Buffered(1)),
                  pl.BlockSpec((tm, d), lambda i: (i, 0)),
                  pl.BlockSpec((1, N_MOD, d), lambda i: (i // tiles_per_seq, 0, 0)),
                  pl.BlockSpec((ne, d), lambda i: (0, 0)),
                  pl.BlockSpec((ne, 1), lambda i: (0, 0))],
        out_specs=[pl.BlockSpec((tm, d), lambda i: (i, 0)),
                   pl.BlockSpec((tm,) + _token_tile(d), lambda i: (i, 0, 0)),
                   pl.BlockSpec((rows, tm), lambda i: (0, i)),
                   pl.BlockSpec((rows, tm), lambda i: (0, i)),
                   pl.BlockSpec((rows, tm), lambda i: (0, i)),
                   pl.BlockSpec((ne, LANES), lambda i: (0, 0))],
        out_shape=[jax.ShapeDtypeStruct((t, d), F32), jax.ShapeDtypeStruct((t,) + _token_tile(d), jnp.uint32),
                   jax.ShapeDtypeStruct((rows, t), jnp.int32), jax.ShapeDtypeStruct((rows, t), F32),
                   jax.ShapeDtypeStruct((rows, t), jnp.int32), jax.ShapeDtypeStruct((ne, LANES), jnp.int32)],
        scratch_shapes=[pltpu.VMEM((ne, LANES), F32)],
        compiler_params=_params("arbitrary"),
        name="outproj",
    )(merged, wo, x2, mod3, w_router.astype(F32).T, b_router.astype(F32).reshape(ne, 1))


def _row_copy(src_hbm, src_row, dst_ref, dst_row, sem):
    return pltpu.make_async_copy(src_hbm.at[pl.ds(src_row, 1)], dst_ref.at[pl.ds(dst_row, 1)], sem)


def _dispatch_kernel(slot_ref, u_ref, init_hbm, xs_hbm, sem):
    del init_hbm
    ts = u_ref.shape[0]

    def issue(tok, carry):
        for k in range(TOP_K):
            _row_copy(u_ref, tok, xs_hbm, slot_ref[0, 0, tok * TOP_K + k], sem).start(priority=k % 2)
        return carry

    lax.fori_loop(0, ts, issue, 0)
    rows = pl.ds(0, ts * TOP_K)
    pltpu.make_async_copy(xs_hbm.at[rows], xs_hbm.at[rows], sem).wait()


def _dispatch(u2, slot_tk, n_slots):
    t, tile = u2.shape[0], u2.shape[1:]
    ts = min(DISPATCH_TS, t)
    slots3 = slot_tk.reshape(t // ts, 1, ts * TOP_K)
    init = jnp.zeros((n_slots,) + tile, u2.dtype)
    return pl.pallas_call(
        _dispatch_kernel,
        grid=(t // ts,),
        in_specs=[pl.BlockSpec((1, 1, ts * TOP_K), lambda i: (i, 0, 0), memory_space=pltpu.SMEM),
                  pl.BlockSpec((ts,) + tile, lambda i: (i, 0, 0)),
                  pl.BlockSpec(memory_space=pl.ANY)],
        out_specs=pl.BlockSpec(memory_space=pl.ANY),
        out_shape=jax.ShapeDtypeStruct((n_slots,) + tile, u2.dtype),
        scratch_shapes=[pltpu.SemaphoreType.DMA(())],
        input_output_aliases={2: 0},
        compiler_params=_params("arbitrary"),
        name="dispatch",
    )(slots3, u2, init)


ITEM_IDLE, ITEM_ACTIVE, ITEM_ZERO_FILL, ITEM_ACTIVE_HALF = 0, 1, 2, 3


def _moe_kernel(tb_ref, tj_ref, te_ref, tr_ref, tflag_ref, tout_ref, tslot_ref, tne_ref, tnj_ref, thn_ref,
                x_ref, wgu_hbm, bg_ref, bu_ref, wd_hbm, bd_ref, o_ref,
                wg_s, wu_s, wd_s, xb_s, acc_s, wg_buf, wu_buf, wd_buf, sem, *, last_j):
    del tb_ref, tout_ref
    i = pl.program_id(0)
    flag = tflag_ref[i]
    r = tr_ref[i]
    j = tj_ref[i]
    half = x_ref.shape[1] * x_ref.shape[2]
    tf = wg_s.shape[1]
    up_col0 = wgu_hbm.shape[2] // 2

    def weight_copies(e, jt, slot):
        c0 = pl.multiple_of(jt * tf, tf)
        return (pltpu.make_async_copy(wgu_hbm.at[e, :, pl.ds(c0, tf)], wg_buf.at[slot], sem.at[slot]),
                pltpu.make_async_copy(wgu_hbm.at[e, :, pl.ds(up_col0 + c0, tf)], wu_buf.at[slot], sem.at[slot]),
                pltpu.make_async_copy(wd_hbm.at[e, pl.ds(c0, tf), :], wd_buf.at[slot], sem.at[slot]))

    is_active = (flag & 1) == 1

    @pl.when(is_active & (r == 0))
    def _():
        slot = tslot_ref[i]

        @pl.when(i == 0)
        def _():
            for cp in weight_copies(te_ref[i], j, slot):
                cp.start()

        for cp in weight_copies(te_ref[i], j, slot):
            cp.wait()
        for s in range(2):
            @pl.when(slot == s)
            def _():
                wg_s[...] = wg_buf[s].astype(BF16)
                wu_s[...] = wu_buf[s].astype(BF16)
                wd_s[...] = wd_buf[s].astype(BF16)

        @pl.when(thn_ref[i] == 1)
        def _():
            for cp in weight_copies(tne_ref[i], tnj_ref[i], 1 - slot):
                cp.start()

    @pl.when(is_active & (j == 0))
    def _():
        lo, hi = _unpack_bf16_pairs(_tiles_to_rows(x_ref[...]))
        xb_s[r, :, :half] = lo
        xb_s[r, :, half:] = hi

    def expert_mlp(rows):
        xb = xb_s[r, 0:rows, :]
        gate = jnp.dot(xb, wg_s[...], preferred_element_type=F32) + bg_ref[...]
        up = jnp.dot(xb, wu_s[...], preferred_element_type=F32) + bu_ref[...]
        gate = jnp.minimum(gate, SWIGLU_LIMIT)
        up = jnp.clip(up, -SWIGLU_LIMIT, SWIGLU_LIMIT)
        act = (up + 1.0) * gate * jax.nn.sigmoid(SWIGLU_ALPHA * gate)
        part = jnp.dot(act.astype(BF16), wd_s[...], preferred_element_type=F32)

        @pl.when(j == 0)
        def _():
            acc_s[r, 0:rows, :] = part + bd_ref[...]

        @pl.when(j > 0)
        def _():
            acc_s[r, 0:rows, :] = acc_s[r, 0:rows, :] + part

        @pl.when(j == last_j)
        def _():
            o_ref[0:rows] = _rows_to_tiles(_pack_bf16_pairs(acc_s[r, 0:rows, :]))
            if rows < o_ref.shape[0]:
                o_ref[rows:] = jnp.zeros((o_ref.shape[0] - rows,) + o_ref.shape[1:], o_ref.dtype)

    @pl.when(flag == ITEM_ACTIVE)
    def _():
        expert_mlp(o_ref.shape[0])

    @pl.when(flag == ITEM_ACTIVE_HALF)
    def _():
        expert_mlp(o_ref.shape[0] // 2)

    @pl.when(flag == ITEM_ZERO_FILL)
    def _():
        o_ref[...] = jnp.zeros_like(o_ref)


def _lookup(table, idx):
    hot = idx[:, None] == jnp.arange(table.shape[0], dtype=jnp.int32)[None, :]
    return jnp.sum(jnp.where(hot, table[None, :].astype(jnp.int32), 0), axis=1).astype(jnp.int32)


def _moe_tables(counts, n_blocks, ft):
    ne = counts.shape[0]
    padded = (counts + MOE_BM - 1) // MOE_BM * MOE_BM
    pend = jnp.cumsum(padded)
    pstart = pend - padded
    blocks_per_expert = padded // MOE_BM
    first_block = pstart // MOE_BM
    n_active = (pend[-1] // MOE_BM).astype(jnp.int32)

    blocks = jnp.arange(n_blocks, dtype=jnp.int32)
    block_expert = jnp.minimum(jnp.sum(pend[None, :] <= (blocks * MOE_BM)[:, None], axis=1), ne - 1).astype(jnp.int32)
    off = blocks - _lookup(first_block, block_expert)
    is_start = (off % MOE_CHUNK_BLOCKS == 0) & (blocks < n_active)
    chunk_size = jnp.minimum(MOE_CHUNK_BLOCKS,
                             _lookup(blocks_per_expert, block_expert) - off // MOE_CHUNK_BLOCKS * MOE_CHUNK_BLOCKS)

    items = jnp.arange(n_blocks * ft, dtype=jnp.int32)
    active = items < ft * n_active
    src = jnp.minimum(items, jnp.maximum(ft * n_active - 1, 0))
    cand = jnp.where(is_start[None, :] & (ft * blocks[None, :] <= src[:, None]), blocks[None, :], 0)
    cstart = jnp.max(cand, axis=1).astype(jnp.int32)
    m = jnp.maximum(_lookup(chunk_size, cstart), 1)
    local = src - ft * cstart
    tj = local // m
    tr = local % m
    tb = cstart + tr
    te = _lookup(block_expert, cstart)
    k = items - ft * n_active
    zero_fill = (~active) & (n_active + k < n_blocks)
    tout = jnp.where(active, jnp.where(tj == ft - 1, tb, cstart), jnp.minimum(n_active + k, n_blocks - 1))
    valid_rows = _lookup(counts, te) - (tb - _lookup(first_block, te)) * MOE_BM
    kind = jnp.where(valid_rows <= MOE_BM // 2, ITEM_ACTIVE_HALF, ITEM_ACTIVE)
    tflag = jnp.where(active, kind, jnp.where(zero_fill, ITEM_ZERO_FILL, ITEM_IDLE))
    group_start = active & (tr == 0)
    tslot = (jnp.cumsum(group_start.astype(jnp.int32)) - 1) % 2
    nxt = items + m
    has_next = group_start & (nxt < ft * n_active)
    nxt = jnp.minimum(nxt, n_blocks * ft - 1)
    tne, tnj = jnp.take(te, nxt), jnp.take(tj, nxt)
    as_i32 = lambda v: v.astype(jnp.int32)
    return pstart, tuple(map(as_i32, (tb, tj, te, tr, tflag, tout, tslot, tne, tnj, has_next)))


def _moe(xs, tables, wgu, bgu, wd, bd):
    n_slots = xs.shape[0]
    ne, d, f2 = wgu.shape
    f = f2 // 2
    bm, tf = MOE_BM, min(MOE_TF, f)
    nb, ft = n_slots // bm, f // tf
    grid_spec = pltpu.PrefetchScalarGridSpec(
        num_scalar_prefetch=len(tables),
        grid=(nb * ft,),
        in_specs=[pl.BlockSpec((bm,) + _token_tile(d), lambda i, tb, tj, te, *_: (tb[i], 0, 0)),
                  pl.BlockSpec(memory_space=pl.ANY),
                  pl.BlockSpec((None, 1, tf), lambda i, tb, tj, te, *_: (te[i], 0, tj[i])),
                  pl.BlockSpec((None, 1, tf), lambda i, tb, tj, te, *_: (te[i], 0, ft + tj[i])),
                  pl.BlockSpec(memory_space=pl.ANY),
                  pl.BlockSpec((None, 1, d), lambda i, tb, tj, te, *_: (te[i], 0, 0))],
        out_specs=pl.BlockSpec((bm,) + _token_tile(d), lambda i, tb, tj, te, tr, tg, to, *_: (to[i], 0, 0)),
        scratch_shapes=[pltpu.VMEM((d, tf), BF16), pltpu.VMEM((d, tf), BF16), pltpu.VMEM((tf, d), BF16),
                        pltpu.VMEM((MOE_CHUNK_BLOCKS, bm, d), BF16), pltpu.VMEM((MOE_CHUNK_BLOCKS, bm, d), F32),
                        pltpu.VMEM((2, d, tf), F32), pltpu.VMEM((2, d, tf), F32), pltpu.VMEM((2, tf, d), F32),
                        pltpu.SemaphoreType.DMA((2,))],
    )
    return pl.pallas_call(
        functools.partial(_moe_kernel, last_j=ft - 1),
        grid_spec=grid_spec,
        out_shape=jax.ShapeDtypeStruct((n_slots,) + _token_tile(d), jnp.uint32),
        compiler_params=_params("arbitrary"),
        name="moe",
    )(*tables, xs, wgu, bgu, bgu, wd, bd)


def _final_kernel(slot_ref, h_ref, gate_ref, mod_ref, fw_ref, y_hbm, o_ref, ybuf, sem):
    tc, d = h_ref.shape
    half = d // 2

    def issue(grp, carry):
        for q in range(FINAL_ISSUE_UNROLL):
            tk = FINAL_ISSUE_UNROLL * grp + q
            _row_copy(y_hbm, slot_ref[0, 0, tk], ybuf, tk, sem).start(priority=q % 2)
        return carry

    lax.fori_loop(0, tc * TOP_K // FINAL_ISSUE_UNROLL, issue, 0)
    pltpu.make_async_copy(y_hbm.at[pl.ds(0, tc * TOP_K)], ybuf, sem).wait()

    ffn_lo = jnp.zeros((tc, half), F32)
    ffn_hi = jnp.zeros((tc, half), F32)
    for k in range(TOP_K):
        w = _tiles_to_rows(ybuf[k * tc:(k + 1) * tc])
        g = gate_ref[:, k:k + 1]
        ffn_lo = ffn_lo + g * lax.bitcast_convert_type(w << 16, F32)
        ffn_hi = ffn_hi + g * lax.bitcast_convert_type(w & jnp.uint32(0xFFFF0000), F32)
    h2_lo = h_ref[:, :half] + mod_ref[0, 5:6, :half] * ffn_lo
    h2_hi = h_ref[:, half:] + mod_ref[0, 5:6, half:] * ffn_hi
    ms = (jnp.sum(h2_lo * h2_lo, axis=-1, keepdims=True) + jnp.sum(h2_hi * h2_hi, axis=-1, keepdims=True)) / d
    inv = lax.rsqrt(ms + EPS)
    o_ref[:, :half] = (h2_lo * inv * fw_ref[:, :half]).astype(o_ref.dtype)
    o_ref[:, half:] = (h2_hi * inv * fw_ref[:, half:]).astype(o_ref.dtype)


def _final(h1, gates_tk, slot_kt, mod3, final_norm_w, y_slots, seq, out_dtype):
    t, d = h1.shape
    tc = min(FINAL_TC, seq)
    tiles_per_seq = seq // tc
    slots3 = slot_kt.reshape(TOP_K, t // tc, tc).transpose(1, 0, 2).reshape(t // tc, 1, TOP_K * tc)
    return pl.pallas_call(
        _final_kernel,
        grid=(t // tc,),
        in_specs=[pl.BlockSpec((1, 1, TOP_K * tc), lambda i: (i, 0, 0), memory_space=pltpu.SMEM),
                  pl.BlockSpec((tc, d), lambda i: (i, 0)),
                  pl.BlockSpec((tc, gates_tk.shape[1]), lambda i: (i, 0)),
                  pl.BlockSpec((1, N_MOD, d), lambda i: (i // tiles_per_seq, 0, 0)),
                  pl.BlockSpec((1, d), lambda i: (0, 0)),
                  pl.BlockSpec(memory_space=pl.ANY)],
        out_specs=pl.BlockSpec((tc, d), lambda i: (i, 0)),
        out_shape=jax.ShapeDtypeStruct((t, d), out_dtype),
        scratch_shapes=[pltpu.VMEM((TOP_K * tc,) + _token_tile(d), jnp.uint32), pltpu.SemaphoreType.DMA(())],
        compiler_params=_params("arbitrary"),
        name="final",
    )(slots3, h1, gates_tk, mod3, final_norm_w.astype(F32).reshape(1, d), y_slots)


def kernel(x, c, w_ada, b_ada, w_in, pool_w, pool_scale, conv_w, conv_b, dt_bias, a_log, d_skip, ssd_norm_w,
           w_branch_pool, w_branch_ssd, w_out, w_router, b_router, w_gate_up, b_gate_up, w_down, b_down,
           final_norm_w):
    bsz, seq, d = x.shape
    depth = w_ada.shape[0]
    t = bsz * seq
    inner = ssd_norm_w.shape[1]
    heads = dt_bias.shape[1]
    bc = SSD_GROUPS * SSD_STATE
    ne = w_router.shape[2]
    assert depth == 1, "the final RMSNorm is fused into the last kernel of the single layer"
    assert SEQ_TILE == 2 * LANES
    assert seq % SEQ_TILE == 0 and heads <= LANES and (7 * d) % bc == 0 and (4 * d) % inner == 0
    assert inner // SSD_GROUPS % LANES == 0 and d % (len(POOL_WINDOWS) * LANES) == 0
    assert d % (2 * SUBLANES * LANES) == 0

    s1 = d
    s2 = s1 + inner
    s3 = s2 + inner + 2 * bc
    s4 = s3 + heads
    s5 = s4 + d
    blk = {"z": 0, "xs": 1, "p": (2 * inner) // d, "B": (2 * inner + 3 * d) // bc, "C": (2 * inner + 3 * d) // bc + 1}
    gp_off, gs_off = 2 * inner + d, 2 * inner + 2 * d

    h = x.astype(F32).reshape(t, d)
    for layer in range(depth):
        wl = w_in[layer]
        w_main = jnp.concatenate([wl[:, s1:s2], wl[:, s2:s2 + inner], wl[:, :s1], wl[:, s4:s5], wl[:, s5:],
                                  wl[:, s2 + inner:s2 + inner + bc], wl[:, s2 + inner + bc:s3]], axis=1).astype(BF16)
        w_dt = jnp.pad(wl[:, s3:s4].astype(F32), ((0, 0), (0, LANES - heads)))

        mod3 = _ada(c, w_ada[layer], b_ada[layer]).reshape(bsz, N_MOD, d)
        proj, dt_raw = _inproj(h, mod3, w_main, w_dt, seq)
        ypool = _pool(proj, blk["p"], pool_w[layer].astype(BF16), pool_scale[layer], seq, d)
        yssd = _ssd(proj, dt_raw, blk, conv_w[layer], conv_b[layer], dt_bias[layer], a_log[layer], d_skip[layer],
                    ssd_norm_w[layer], seq)
        merged = _merge(ypool, yssd, proj, gp_off, gs_off, w_branch_pool[layer].astype(BF16),
                        w_branch_ssd[layer].astype(BF16))
        h1, u2, idx_kt, gate_kt, rank_kt, counts = _outproj(merged, w_out[layer].astype(BF16), h, mod3,
                                                            w_router[layer], b_router[layer], seq)

        n_blocks = (t * TOP_K) // MOE_BM + ne
        ft = w_down.shape[2] // min(MOE_TF, w_down.shape[2])
        pstart, tables = _moe_tables(counts[:, 0], n_blocks, ft)
        eids = jnp.arange(ne, dtype=jnp.int32)[:, None, None]
        slot_kt = (jnp.sum(jnp.where(idx_kt[:TOP_K][None] == eids, pstart[:, None, None], 0), axis=0)
                   + rank_kt[:TOP_K]).astype(jnp.int32)

        xs = _dispatch(u2, slot_kt.T, n_blocks * MOE_BM)
        y_slots = _moe(xs, tables, w_gate_up[layer].astype(F32), b_gate_up[layer].astype(F32)[:, None, :],
                       w_down[layer].astype(F32), b_down[layer].astype(F32)[:, None, :])
        h = _final(h1, gate_kt.T, slot_kt, mod3, final_norm_w, y_slots, seq, x.dtype)
    return h.reshape(bsz, seq, d)
```

```python
import functools

import jax
import jax.numpy as jnp
from jax import lax
from jax.experimental import pallas as pl
from jax.experimental.pallas import tpu as pltpu

F32 = jnp.float32
BF16 = jnp.bfloat16
HIGHEST = lax.Precision.HIGHEST

EPS = 1e-6
POOL_WINDOWS = (2, 4, 8, 16)
SSD_GROUPS = 8
SSD_STATE = 128
SSD_CONV = 4
SSD_HEAD_DIM = 64
TOP_K = 4
SWIGLU_LIMIT = 7.0
SWIGLU_ALPHA = 1.702
N_MOD = 6

LOG2_E = 1.4426950408889634
LANES = 128
SUBLANES = 8
CONV_HALO = 8
V7X_VMEM_LIMIT = 56 * 1024 * 1024

SEQ_TILE = 256
INPROJ_TM, INPROJ_TN = 1024, 2048
MERGE_TM, MERGE_TN = 1024, 512
OUT_TM = 512
MOE_BM, MOE_TF = 512, 512
MOE_CHUNK_BLOCKS = 2
DISPATCH_TS = 512
FINAL_TC = 512
FINAL_ISSUE_UNROLL = 8


def _params(*sem):
    return pltpu.CompilerParams(dimension_semantics=sem, vmem_limit_bytes=V7X_VMEM_LIMIT)


def _silu(v):
    return v * jax.nn.sigmoid(v)


def _pack_bf16_pairs(v):
    half = v.shape[1] // 2
    lo = lax.bitcast_convert_type(v[:, :half].astype(BF16).astype(F32), jnp.uint32)
    hi = lax.bitcast_convert_type(v[:, half:].astype(BF16).astype(F32), jnp.uint32)
    return (hi & jnp.uint32(0xFFFF0000)) | (lo >> 16)


def _token_tile(d):
    return (d // 2 // LANES, LANES)


def _rows_to_tiles(w):
    return w.reshape(w.shape[0], w.shape[1] // LANES, LANES)


def _tiles_to_rows(w):
    return w.reshape(w.shape[0], w.shape[1] * LANES)


def _unpack_bf16_pairs(w):
    lo = lax.bitcast_convert_type(w << 16, F32).astype(BF16)
    hi = lax.bitcast_convert_type(w & jnp.uint32(0xFFFF0000), F32).astype(BF16)
    return lo, hi


def _ada_kernel(cb_ref, w_ref, b_ref, o_ref):
    nb, tn = cb_ref.shape[0], w_ref.shape[1]
    for b in range(nb):
        ca = _silu(cb_ref[b])
        cols = [jnp.sum(w_ref[:, j * LANES:(j + 1) * LANES] * ca, axis=0, keepdims=True)
                for j in range(tn // LANES)]
        o_ref[b:b + 1, :] = jnp.concatenate(cols, axis=1) + b_ref[...]


def _ada(c, w_ada, b_ada):
    nb, k = c.shape
    n = w_ada.shape[1]
    tn = 1024 if n % 1024 == 0 else n
    cb = jnp.broadcast_to(c.astype(F32)[:, :, None], (nb, k, LANES))
    return pl.pallas_call(
        _ada_kernel,
        grid=(n // tn,),
        in_specs=[pl.BlockSpec((nb, k, LANES), lambda j: (0, 0, 0)),
                  pl.BlockSpec((k, tn), lambda j: (0, j)),
                  pl.BlockSpec((1, tn), lambda j: (0, j))],
        out_specs=pl.BlockSpec((nb, tn), lambda j: (0, j)),
        out_shape=jax.ShapeDtypeStruct((nb, n), F32),
        compiler_params=_params("arbitrary"),
        name="ada",
    )(cb, w_ada, b_ada.reshape(1, n))


def _inproj_kernel(x_ref, mod_ref, w_ref, wdt_hi_ref, wdt_lo_ref, o_ref, dt_ref, u_s, *, rows_per_chunk):
    j = pl.program_id(1)
    tm = x_ref.shape[0]

    @pl.when(j == 0)
    def _():
        sh = mod_ref[0, 0:1, :]
        sc = mod_ref[0, 1:2, :]

        def body(r, carry):
            rows = pl.ds(pl.multiple_of(r * rows_per_chunk, rows_per_chunk), rows_per_chunk)
            xv = x_ref[rows, :]
            ms = jnp.mean(xv * xv, axis=-1, keepdims=True)
            u = xv * lax.rsqrt(ms + EPS) * (1.0 + sc) + sh
            u_hi = u.astype(BF16)
            u_s[rows, :] = u_hi
            u_lo = (u - u_hi.astype(F32)).astype(BF16)
            dt_ref[rows, :] = (jnp.dot(u_hi, wdt_hi_ref[...], preferred_element_type=F32)
                               + (jnp.dot(u_hi, wdt_lo_ref[...], preferred_element_type=F32)
                                  + jnp.dot(u_lo, wdt_hi_ref[...], preferred_element_type=F32)))
            return carry

        lax.fori_loop(0, tm // rows_per_chunk, body, 0)

    o_ref[...] = jnp.dot(u_s[...], w_ref[...], preferred_element_type=F32).astype(o_ref.dtype)


def _inproj(x2, mod3, w_main, w_dt, seq):
    t, d = x2.shape
    w_dt_hi = w_dt.astype(BF16)
    w_dt_lo = (w_dt - w_dt_hi.astype(F32)).astype(BF16)
    n = w_main.shape[1]
    tm = min(INPROJ_TM, seq)
    tn = INPROJ_TN
    tiles_per_seq = seq // tm
    return pl.pallas_call(
        functools.partial(_inproj_kernel, rows_per_chunk=min(128, tm)),
        grid=(t // tm, n // tn),
        in_specs=[pl.BlockSpec((tm, d), lambda i, j: (i, 0)),
                  pl.BlockSpec((1, N_MOD, d), lambda i, j: (i // tiles_per_seq, 0, 0)),
                  pl.BlockSpec((d, tn), lambda i, j: (0, j)),
                  pl.BlockSpec((d, LANES), lambda i, j: (0, 0)),
                  pl.BlockSpec((d, LANES), lambda i, j: (0, 0))],
        out_specs=[pl.BlockSpec((tm, tn), lambda i, j: (i, j)),
                   pl.BlockSpec((tm, LANES), lambda i, j: (i, 0))],
        out_shape=[jax.ShapeDtypeStruct((t, n), BF16), jax.ShapeDtypeStruct((t, LANES), F32)],
        scratch_shapes=[pltpu.VMEM((tm, d), BF16)],
        compiler_params=_params("arbitrary", "arbitrary"),
        name="inproj",
    )(x2, mod3, w_main, w_dt_hi, w_dt_lo)


def _pool_kernel(p_ref, pw_ref, ps_ref, o_ref, prev_s, *, tiles_per_seq):
    i = pl.program_id(0)
    tl, d = p_ref.shape
    gd = d // len(POOL_WINDOWS)
    it = i % tiles_per_seq

    @pl.when(it == 0)
    def _():
        prev_s[...] = jnp.zeros_like(prev_s)

    row = lax.broadcasted_iota(jnp.int32, (tl, 2 * tl), 0)
    col = lax.broadcasted_iota(jnp.int32, (tl, 2 * tl), 1)
    pos = (it * tl + lax.broadcasted_iota(jnp.int32, (tl, 1), 0) + 1).astype(F32)
    for g, w in enumerate(POOL_WINDOWS):
        sl = slice(g * gd, (g + 1) * gd)
        cur = p_ref[:, sl]
        ext = jnp.concatenate([prev_s[:, sl], cur], axis=0)
        band = ((col <= row + tl) & (col > row + tl - w)).astype(BF16)
        win_sum = jnp.dot(band, ext, preferred_element_type=F32)
        mean = win_sum / jnp.minimum(pos, float(w))
        dlt = (mean - cur.astype(F32)).astype(BF16)
        mixed = jnp.dot(dlt, pw_ref[g], preferred_element_type=F32)
        o_ref[:, sl] = (mixed * ps_ref[:, sl]).astype(o_ref.dtype)
    prev_s[...] = p_ref[...]


def _pool(proj, p_blk, pool_w_bf, pool_scale, seq, d):
    t = proj.shape[0]
    tl = SEQ_TILE
    g, gd = pool_w_bf.shape[0], pool_w_bf.shape[1]
    return pl.pallas_call(
        functools.partial(_pool_kernel, tiles_per_seq=seq // tl),
        grid=(t // tl,),
        in_specs=[pl.BlockSpec((tl, d), lambda i: (i, p_blk)),
                  pl.BlockSpec((g, gd, gd), lambda i: (0, 0, 0)),
                  pl.BlockSpec((1, d), lambda i: (0, 0))],
        out_specs=pl.BlockSpec((tl, d), lambda i: (i, 0)),
        out_shape=jax.ShapeDtypeStruct((t, d), BF16),
        scratch_shapes=[pltpu.VMEM((tl, d), BF16)],
        compiler_params=_params("arbitrary"),
        name="pool",
    )(proj, pool_w_bf, pool_scale.reshape(1, d))


def _ssd_kernel(z_ref, xs_ref, bm_ref, cm_ref, dt_ref,
                cwx_ref, cwb_ref, cwc_ref, cbx_ref, cbb_ref, cbc_ref,
                dtb_ref, alog_ref, dsk_ref, nw_ref, exp_ref,
                o_ref,
                extx_s, extb_s, extc_s, state_s, xdt_s, y_s, cb_s, cs_s, cst_s, xc_s, bc_s, cc_s,
                *, tiles_per_seq, heads):
    i = pl.program_id(0)
    tl, inner = xs_ref.shape
    gw = inner // SSD_GROUPS
    pairs_per_group = gw // LANES
    n_pairs = inner // LANES

    @pl.when(i % tiles_per_seq == 0)
    def _():
        extx_s[0:CONV_HALO, :] = jnp.zeros((CONV_HALO, inner), F32)
        extb_s[0:CONV_HALO, :] = jnp.zeros((CONV_HALO, extb_s.shape[1]), F32)
        extc_s[0:CONV_HALO, :] = jnp.zeros((CONV_HALO, extc_s.shape[1]), F32)
        state_s[...] = jnp.zeros_like(state_s)

    def conv_silu(ext_ref, src_ref, w_ref, b_ref, dst_ref, width, cw):
        for c0 in range(0, width, cw):
            cs = slice(c0, c0 + cw)
            ext_ref[CONV_HALO:CONV_HALO + tl, cs] = src_ref[:, cs].astype(F32)
            acc = b_ref[:, cs] + w_ref[SSD_CONV - 1:SSD_CONV, cs] * ext_ref[CONV_HALO:CONV_HALO + tl, cs]
            for k in range(1, SSD_CONV):
                acc = acc + w_ref[SSD_CONV - 1 - k:SSD_CONV - k, cs] * ext_ref[pl.ds(CONV_HALO - k, tl), cs]
            dst_ref[:, cs] = _silu(acc)
            ext_ref[0:CONV_HALO, cs] = ext_ref[tl:tl + CONV_HALO, cs]

    conv_silu(extx_s, xs_ref, cwx_ref, cbx_ref, xc_s, inner, LANES)
    conv_silu(extb_s, bm_ref, cwb_ref, cbb_ref, bc_s, bm_ref.shape[1], LANES)
    conv_silu(extc_s, cm_ref, cwc_ref, cbc_ref, cc_s, cm_ref.shape[1], LANES)

    dtv = dt_ref[...] + dtb_ref[...]
    dt = jnp.maximum(dtv, 0.0) + jnp.log1p(jnp.exp(-jnp.abs(dtv)))
    a = -jnp.exp(alog_ref[...])
    da = dt * a
    ri = lax.broadcasted_iota(jnp.int32, (tl, tl), 0)
    ci = lax.broadcasted_iota(jnp.int32, (tl, tl), 1)
    causal = ri >= ci
    cs = jnp.dot(causal.astype(F32), da, precision=HIGHEST, preferred_element_type=F32)
    cs2 = cs * LOG2_E
    cs_s[...] = cs2
    cst_s[...] = cs2.T
    last = cs[tl - 1:tl, :]
    dt_bf = dt.astype(BF16)
    ecs_bf = jnp.exp(cs).astype(BF16)
    dte_bf = jnp.exp(last - cs).astype(BF16)

    for g in range(SSD_GROUPS):
        gs = slice(g * gw, (g + 1) * gw)
        ns = slice(g * SSD_STATE, (g + 1) * SSD_STATE)
        exp_g = exp_ref[:, gs]
        dt_x = jnp.dot(dt_bf, exp_g, preferred_element_type=F32)
        ecs_x = jnp.dot(ecs_bf, exp_g, preferred_element_type=F32)
        dte_x = jnp.dot(dte_bf, exp_g, preferred_element_type=F32)
        xg = xc_s[:, gs]
        xdt = xg * dt_x
        xdt_bf = xdt.astype(BF16)
        xd_bf = (xdt * dte_x).astype(BF16)
        bg = bc_s[:, ns]
        cg = cc_s[:, ns].astype(BF16)
        cb_s[g] = lax.dot_general(cg, bg.astype(BF16), (((1,), (1,)), ((), ())), preferred_element_type=F32)
        s_old = state_s[g]
        y_off = jnp.dot(cg, s_old.astype(BF16), preferred_element_type=F32) * ecs_x
        state_s[g] = (s_old * ecs_x[tl - 1:tl, :]
                      + jnp.dot(bg.T.astype(BF16), xd_bf, preferred_element_type=F32))
        y0 = y_off + dsk_ref[:, gs] * xg
        for q in range(pairs_per_group):
            qs = slice(q * LANES, (q + 1) * LANES)
            xdt_s[g * pairs_per_group + q] = xdt_bf[:, qs]
            y_s[g * pairs_per_group + q] = y0[:, qs]

    lane_h = lax.broadcasted_iota(jnp.int32, (tl, LANES), 1)
    low_half = lane_h < SSD_HEAD_DIM

    half = tl // 2
    causal_top = (lax.broadcasted_iota(jnp.int32, (half, half), 0)
                  >= lax.broadcasted_iota(jnp.int32, (half, half), 1))
    causal_bot = (lax.broadcasted_iota(jnp.int32, (half, tl), 0) + half
                  >= lax.broadcasted_iota(jnp.int32, (half, tl), 1))
    lane_hh = lax.broadcasted_iota(jnp.int32, (half, LANES), 1)
    low_hh = lane_hh < SSD_HEAD_DIM

    def pair_body(hp, carry):
        g = hp // pairs_per_group
        cb_top = cb_s[g, 0:half, 0:half]
        cb_bot = cb_s[g, half:tl, :]
        x_top = xdt_s[hp, 0:half, :]
        x_all = xdt_s[hp]
        cs_top = cs_s[0:half, :]
        cs_bot = cs_s[half:tl, :]
        tops, bots = [], []
        for e in range(2):
            h = 2 * hp + e
            head_lane = jnp.full((half, LANES), h, jnp.int32)
            col_top = jnp.take_along_axis(cs_top, head_lane, axis=1)
            col_bot = jnp.take_along_axis(cs_bot, head_lane, axis=1)
            col_bot = jnp.concatenate([col_bot] * (tl // LANES), axis=1)
            row_all = cst_s[pl.ds(h, 1), :]
            row_top = row_all[:, 0:half]
            m_top = jnp.exp2(jnp.where(causal_top, col_top - row_top, -jnp.inf)) * cb_top
            m_bot = jnp.exp2(jnp.where(causal_bot, col_bot - row_all, -jnp.inf)) * cb_bot
            tops.append(jnp.dot(m_top.astype(BF16), x_top, preferred_element_type=F32))
            bots.append(jnp.dot(m_bot.astype(BF16), x_all, preferred_element_type=F32))
        y_s[hp, 0:half, :] = y_s[hp, 0:half, :] + jnp.where(low_hh, tops[0], tops[1])
        y_s[hp, half:tl, :] = y_s[hp, half:tl, :] + jnp.where(low_hh, bots[0], bots[1])
        return carry

    lax.fori_loop(0, n_pairs, pair_body, 0, unroll=16)

    for g in range(SSD_GROUPS):
        gs = slice(g * gw, (g + 1) * gw)
        yg = jnp.concatenate([y_s[g * pairs_per_group + q] for q in range(pairs_per_group)], axis=1)
        yg = yg * _silu(z_ref[:, gs].astype(F32))
        ms = jnp.mean(yg * yg, axis=-1, keepdims=True)
        o_ref[:, gs] = (yg * lax.rsqrt(ms + EPS) * nw_ref[:, gs]).astype(o_ref.dtype)


def _ssd(proj, dt_raw, blk, conv_w, conv_b, dt_bias, a_log, d_skip, ssd_norm_w, seq):
    t = proj.shape[0]
    tl = SEQ_TILE
    heads = dt_bias.shape[0]
    inner = ssd_norm_w.shape[0]
    bc = SSD_GROUPS * SSD_STATE
    gw = inner // SSD_GROUPS
    pad = LANES - heads
    row = lambda v: v.astype(F32).reshape(1, -1)
    dtb = jnp.pad(row(dt_bias), ((0, 0), (0, pad)))
    alog = jnp.pad(row(a_log), ((0, 0), (0, pad)))
    dsk = jnp.repeat(d_skip.astype(F32), SSD_HEAD_DIM).reshape(1, inner)
    expand = (lax.broadcasted_iota(jnp.int32, (LANES, inner), 1) // SSD_HEAD_DIM
              == lax.broadcasted_iota(jnp.int32, (LANES, inner), 0)).astype(BF16)
    cw = conv_w.astype(F32)
    cbias = row(conv_b)
    full = lambda shape: pl.BlockSpec(shape, lambda i: tuple(0 for _ in shape))
    return pl.pallas_call(
        functools.partial(_ssd_kernel, tiles_per_seq=seq // tl, heads=heads),
        grid=(t // tl,),
        in_specs=[pl.BlockSpec((tl, inner), lambda i: (i, blk["z"])),
                  pl.BlockSpec((tl, inner), lambda i: (i, blk["xs"])),
                  pl.BlockSpec((tl, bc), lambda i: (i, blk["B"])),
                  pl.BlockSpec((tl, bc), lambda i: (i, blk["C"])),
                  pl.BlockSpec((tl, LANES), lambda i: (i, 0)),
                  full((SSD_CONV, inner)), full((SSD_CONV, bc)), full((SSD_CONV, bc)),
                  full((1, inner)), full((1, bc)), full((1, bc)),
                  full((1, LANES)), full((1, LANES)), full((1, inner)), full((1, inner)),
                  full((LANES, inner))],
        out_specs=pl.BlockSpec((tl, inner), lambda i: (i, 0)),
        out_shape=jax.ShapeDtypeStruct((t, inner), BF16),
        scratch_shapes=[pltpu.VMEM((tl + CONV_HALO, inner), F32),
                        pltpu.VMEM((tl + CONV_HALO, bc), F32),
                        pltpu.VMEM((tl + CONV_HALO, bc), F32),
                        pltpu.VMEM((SSD_GROUPS, SSD_STATE, gw), F32),
                        pltpu.VMEM((inner // LANES, tl, LANES), BF16),
                        pltpu.VMEM((inner // LANES, tl, LANES), F32),
                        pltpu.VMEM((SSD_GROUPS, tl, tl), F32),
                        pltpu.VMEM((tl, LANES), F32),
                        pltpu.VMEM((LANES, tl), F32),
                        pltpu.VMEM((tl, inner), F32),
                        pltpu.VMEM((tl, bc), F32),
                        pltpu.VMEM((tl, bc), F32)],
        compiler_params=_params("arbitrary"),
        name="ssd",
    )(proj, proj, proj, proj, dt_raw,
      cw[:, :inner], cw[:, inner:inner + bc], cw[:, inner + bc:],
      cbias[:, :inner], cbias[:, inner:inner + bc], cbias[:, inner + bc:],
      dtb, alog, dsk, row(ssd_norm_w), expand)


def _merge_kernel(yp_ref, ys_ref, gp_ref, gs_ref, wp_ref, ws_ref, o_ref):
    a = jnp.dot(yp_ref[...], wp_ref[...], preferred_element_type=F32)
    b = jnp.dot(ys_ref[...], ws_ref[...], preferred_element_type=F32)
    o_ref[...] = (jax.nn.sigmoid(gp_ref[...].astype(F32)) * a
                  + jax.nn.sigmoid(gs_ref[...].astype(F32)) * b).astype(o_ref.dtype)


def _merge(ypool, yssd, proj, gp_off, gs_off, wbp, wbs):
    t, d = ypool.shape
    inner = yssd.shape[1]
    tm, tn = min(MERGE_TM, t), MERGE_TN
    return pl.pallas_call(
        _merge_kernel,
        grid=(t // tm, d // tn),
        in_specs=[pl.BlockSpec((tm, d), lambda i, j: (i, 0)),
                  pl.BlockSpec((tm, inner), lambda i, j: (i, 0)),
                  pl.BlockSpec((tm, tn), lambda i, j: (i, gp_off // tn + j)),
                  pl.BlockSpec((tm, tn), lambda i, j: (i, gs_off // tn + j)),
                  pl.BlockSpec((d, tn), lambda i, j: (0, j)),
                  pl.BlockSpec((inner, tn), lambda i, j: (0, j))],
        out_specs=pl.BlockSpec((tm, tn), lambda i, j: (i, j)),
        out_shape=jax.ShapeDtypeStruct((t, d), BF16),
        compiler_params=_params("arbitrary", "arbitrary"),
        name="merge",
    )(ypool, yssd, proj, proj, wbp, wbs)


def _out_kernel(m_ref, wo_ref, x_ref, mod_ref, wr_ref, br_ref,
                h_ref, u_ref, idx_ref, gate_ref, rank_ref, cnt_ref, carry_s):
    i = pl.program_id(0)
    tm = m_ref.shape[0]
    ne = wr_ref.shape[0]

    @pl.when(i == 0)
    def _():
        carry_s[...] = jnp.zeros_like(carry_s)

    mix = jnp.dot(m_ref[...], wo_ref[...], preferred_element_type=F32)
    h1 = x_ref[...] + mod_ref[0, 2:3, :] * mix
    h_ref[...] = h1
    ms = jnp.mean(h1 * h1, axis=-1, keepdims=True)
    u = h1 * lax.rsqrt(ms + EPS) * (1.0 + mod_ref[0, 4:5, :]) + mod_ref[0, 3:4, :]
    u_ref[...] = _rows_to_tiles(_pack_bf16_pairs(u))
    logits = lax.dot_general(wr_ref[...], u, (((1,), (1,)), ((), ())), precision=HIGHEST,
                             preferred_element_type=F32) + br_ref[...]
    eidx = lax.broadcasted_iota(jnp.int32, (ne, tm), 0)
    work = logits
    vals, idxs, hots = [], [], []
    for _ in range(TOP_K):
        mx = jnp.max(work, axis=0, keepdims=True)
        sel = jnp.min(jnp.where(work == mx, eidx, ne), axis=0, keepdims=True)
        hot = eidx == sel
        vals.append(mx)
        idxs.append(sel)
        hots.append(hot)
        work = jnp.where(hot, -jnp.inf, work)
    exps = [jnp.exp(v - vals[0]) for v in vals]
    den = exps[0]
    for e in exps[1:]:
        den = den + e
    cnt = hots[0].astype(F32)
    for hot in hots[1:]:
        cnt = cnt + hot.astype(F32)
    ti = lax.broadcasted_iota(jnp.int32, (tm, tm), 0)
    tj = lax.broadcasted_iota(jnp.int32, (tm, tm), 1)
    before = (ti < tj).astype(BF16)
    prefix = jnp.dot(cnt.astype(BF16), before, preferred_element_type=F32)
    base = carry_s[:, 0:1] + prefix
    pad_rows = idx_ref.shape[0] - TOP_K
    ranks = [jnp.sum(jnp.where(hot, base, 0.0), axis=0, keepdims=True) for hot in hots]
    idx_ref[...] = jnp.concatenate(idxs + [jnp.zeros((pad_rows, tm), jnp.int32)], axis=0)
    gate_ref[...] = jnp.concatenate([e / den for e in exps] + [jnp.zeros((pad_rows, tm), F32)], axis=0)
    rank_ref[...] = jnp.concatenate([r.astype(jnp.int32) for r in ranks]
                                    + [jnp.zeros((pad_rows, tm), jnp.int32)], axis=0)
    carry_s[...] = carry_s[...] + jnp.sum(cnt, axis=1, keepdims=True)
    cnt_ref[...] = carry_s[...].astype(jnp.int32)


def _outproj(merged, wo, x2, mod3, w_router, b_router, seq):
    t, d = x2.shape
    ne = w_router.shape[1]
    tm = min(OUT_TM, seq)
    tiles_per_seq = seq // tm
    rows = 8
    return pl.pallas_call(
        _out_kernel,
        grid=(t // tm,),
        in_specs=[pl.BlockSpec((tm, d), lambda i: (i, 0)),
                  pl.BlockSpec((d, d), lambda i: (0, 0), pipeline_mode=pl.Buffered(1)),
                  pl.BlockSpec((tm, d), lambda i: (i, 0)),
                  pl.BlockSpec((1, N_MOD, d), lambda i: (i // tiles_per_seq, 0, 0)),
                  pl.BlockSpec((ne, d), lambda i: (0, 0)),
                  pl.BlockSpec((ne, 1), lambda i: (0, 0))],
        out_specs=[pl.BlockSpec((tm, d), lambda i: (i, 0)),
                   pl.BlockSpec((tm,) + _token_tile(d), lambda i: (i, 0, 0)),
                   pl.BlockSpec((rows, tm), lambda i: (0, i)),
                   pl.BlockSpec((rows, tm), lambda i: (0, i)),
                   pl.BlockSpec((rows, tm), lambda i: (0, i)),
                   pl.BlockSpec((ne, LANES), lambda i: (0, 0))],
        out_shape=[jax.ShapeDtypeStruct((t, d), F32), jax.ShapeDtypeStruct((t,) + _token_tile(d), jnp.uint32),
                   jax.ShapeDtypeStruct((rows, t), jnp.int32), jax.ShapeDtypeStruct((rows, t), F32),
                   jax.ShapeDtypeStruct((rows, t), jnp.int32), jax.ShapeDtypeStruct((ne, LANES), jnp.int32)],
        scratch_shapes=[pltpu.VMEM((ne, LANES), F32)],
        compiler_params=_params("arbitrary"),
        name="outproj",
    )(merged, wo, x2, mod3, w_router.astype(F32).T, b_router.astype(F32).reshape(ne, 1))


def _row_copy(src_hbm, src_row, dst_ref, dst_row, sem):
    return pltpu.make_async_copy(src_hbm.at[pl.ds(src_row, 1)], dst_ref.at[pl.ds(dst_row, 1)], sem)


def _dispatch_kernel(slot_ref, u_ref, init_hbm, xs_hbm, sem):
    del init_hbm
    ts = u_ref.shape[0]

    def issue(tok, carry):
        for k in range(TOP_K):
            _row_copy(u_ref, tok, xs_hbm, slot_ref[0, 0, tok * TOP_K + k], sem).start(priority=k % 2)
        return carry

    lax.fori_loop(0, ts, issue, 0)
    rows = pl.ds(0, ts * TOP_K)
    pltpu.make_async_copy(xs_hbm.at[rows], xs_hbm.at[rows], sem).wait()


def _dispatch(u2, slot_tk, n_slots):
    t, tile = u2.shape[0], u2.shape[1:]
    ts = min(DISPATCH_TS, t)
    slots3 = slot_tk.reshape(t // ts, 1, ts * TOP_K)
    init = jnp.zeros((n_slots,) + tile, u2.dtype)
    return pl.pallas_call(
        _dispatch_kernel,
        grid=(t // ts,),
        in_specs=[pl.BlockSpec((1, 1, ts * TOP_K), lambda i: (i, 0, 0), memory_space=pltpu.SMEM),
                  pl.BlockSpec((ts,) + tile, lambda i: (i, 0, 0)),
                  pl.BlockSpec(memory_space=pl.ANY)],
        out_specs=pl.BlockSpec(memory_space=pl.ANY),
        out_shape=jax.ShapeDtypeStruct((n_slots,) + tile, u2.dtype),
        scratch_shapes=[pltpu.SemaphoreType.DMA(())],
        input_output_aliases={2: 0},
        compiler_params=_params("arbitrary"),
        name="dispatch",
    )(slots3, u2, init)


ITEM_IDLE, ITEM_ACTIVE, ITEM_ZERO_FILL, ITEM_ACTIVE_HALF = 0, 1, 2, 3


def _moe_kernel(tb_ref, tj_ref, te_ref, tr_ref, tflag_ref, tout_ref, tslot_ref, tne_ref, tnj_ref, thn_ref,
                x_ref, wgu_hbm, bg_ref, bu_ref, wd_hbm, bd_ref, o_ref,
                wg_s, wu_s, wd_s, xb_s, acc_s, wg_buf, wu_buf, wd_buf, sem, *, last_j):
    del tb_ref, tout_ref
    i = pl.program_id(0)
    flag = tflag_ref[i]
    r = tr_ref[i]
    j = tj_ref[i]
    half = x_ref.shape[1] * x_ref.shape[2]
    tf = wg_s.shape[1]
    up_col0 = wgu_hbm.shape[2] // 2

    def weight_copies(e, jt, slot):
        c0 = pl.multiple_of(jt * tf, tf)
        return (pltpu.make_async_copy(wgu_hbm.at[e, :, pl.ds(c0, tf)], wg_buf.at[slot], sem.at[slot]),
                pltpu.make_async_copy(wgu_hbm.at[e, :, pl.ds(up_col0 + c0, tf)], wu_buf.at[slot], sem.at[slot]),
                pltpu.make_async_copy(wd_hbm.at[e, pl.ds(c0, tf), :], wd_buf.at[slot], sem.at[slot]))

    is_active = (flag & 1) == 1

    @pl.when(i == 0)
    def _():
        acc_s[...] = jnp.zeros_like(acc_s)

    @pl.when(is_active & (r == 0))
    def _():
        slot = tslot_ref[i]

        @pl.when(i == 0)
        def _():
            for cp in weight_copies(te_ref[i], j, slot):
                cp.start()

        for cp in weight_copies(te_ref[i], j, slot):
            cp.wait()
        for s in range(2):
            @pl.when(slot == s)
            def _():
                wg_s[...] = wg_buf[s].astype(BF16)
                wu_s[...] = wu_buf[s].astype(BF16)
                wd_s[...] = wd_buf[s].astype(BF16)

        @pl.when(thn_ref[i] == 1)
        def _():
            for cp in weight_copies(tne_ref[i], tnj_ref[i], 1 - slot):
                cp.start()

    @pl.when(is_active & (j == 0))
    def _():
        lo, hi = _unpack_bf16_pairs(_tiles_to_rows(x_ref[...]))
        xb_s[r, :, :half] = lo
        xb_s[r, :, half:] = hi

    def expert_mlp(rows):
        xb = xb_s[r, 0:rows, :]
        gate = jnp.dot(xb, wg_s[...], preferred_element_type=F32) + bg_ref[...]
        up = jnp.dot(xb, wu_s[...], preferred_element_type=F32) + bu_ref[...]
        gate = jnp.minimum(gate, SWIGLU_LIMIT)
        up = jnp.clip(up, -SWIGLU_LIMIT, SWIGLU_LIMIT)
        act = (up + 1.0) * gate * jax.nn.sigmoid(SWIGLU_ALPHA * gate)
        part = jnp.dot(act.astype(BF16), wd_s[...], preferred_element_type=F32)
        base = jnp.where(j == 0, jnp.broadcast_to(bd_ref[...], (rows, bd_ref.shape[1])), acc_s[r, 0:rows, :])
        acc_s[r, 0:rows, :] = base + part

        @pl.when(j == last_j)
        def _():
            o_ref[0:rows] = _rows_to_tiles(_pack_bf16_pairs(acc_s[r, 0:rows, :]))
            if rows < o_ref.shape[0]:
                o_ref[rows:] = jnp.zeros((o_ref.shape[0] - rows,) + o_ref.shape[1:], o_ref.dtype)

    @pl.when(flag == ITEM_ACTIVE)
    def _():
        expert_mlp(o_ref.shape[0])

    @pl.when(flag == ITEM_ACTIVE_HALF)
    def _():
        expert_mlp(o_ref.shape[0] // 2)

    @pl.when(flag == ITEM_ZERO_FILL)
    def _():
        o_ref[...] = jnp.zeros_like(o_ref)


def _lookup(table, idx):
    hot = idx[:, None] == jnp.arange(table.shape[0], dtype=jnp.int32)[None, :]
    return jnp.sum(jnp.where(hot, table[None, :].astype(jnp.int32), 0), axis=1).astype(jnp.int32)


def _moe_tables(counts, n_blocks, ft):
    ne = counts.shape[0]
    padded = (counts + MOE_BM - 1) // MOE_BM * MOE_BM
    pend = jnp.cumsum(padded)
    pstart = pend - padded
    blocks_per_expert = padded // MOE_BM
    first_block = pstart // MOE_BM
    n_active = (pend[-1] // MOE_BM).astype(jnp.int32)

    blocks = jnp.arange(n_blocks, dtype=jnp.int32)
    block_expert = jnp.minimum(jnp.sum(pend[None, :] <= (blocks * MOE_BM)[:, None], axis=1), ne - 1).astype(jnp.int32)
    off = blocks - _lookup(first_block, block_expert)
    is_start = (off % MOE_CHUNK_BLOCKS == 0) & (blocks < n_active)
    chunk_size = jnp.minimum(MOE_CHUNK_BLOCKS,
                             _lookup(blocks_per_expert, block_expert) - off // MOE_CHUNK_BLOCKS * MOE_CHUNK_BLOCKS)

    items = jnp.arange(n_blocks * ft, dtype=jnp.int32)
    active = items < ft * n_active
    src = jnp.minimum(items, jnp.maximum(ft * n_active - 1, 0))
    cand = jnp.where(is_start[None, :] & (ft * blocks[None, :] <= src[:, None]), blocks[None, :], 0)
    cstart = jnp.max(cand, axis=1).astype(jnp.int32)
    m = jnp.maximum(_lookup(chunk_size, cstart), 1)
    local = src - ft * cstart
    tj = local // m
    tr = local % m
    tb = cstart + tr
    te = _lookup(block_expert, cstart)
    k = items - ft * n_active
    zero_fill = (~active) & (n_active + k < n_blocks)
    tout = jnp.where(active, jnp.where(tj == ft - 1, tb, cstart), jnp.minimum(n_active + k, n_blocks - 1))
    valid_rows = _lookup(counts, te) - (tb - _lookup(first_block, te)) * MOE_BM
    kind = jnp.where(valid_rows <= MOE_BM // 2, ITEM_ACTIVE_HALF, ITEM_ACTIVE)
    tflag = jnp.where(active, kind, jnp.where(zero_fill, ITEM_ZERO_FILL, ITEM_IDLE))
    group_start = active & (tr == 0)
    tslot = (jnp.cumsum(group_start.astype(jnp.int32)) - 1) % 2
    nxt = items + m
    has_next = group_start & (nxt < ft * n_active)
    nxt = jnp.minimum(nxt, n_blocks * ft - 1)
    tne, tnj = jnp.take(te, nxt), jnp.take(tj, nxt)
    as_i32 = lambda v: v.astype(jnp.int32)
    return pstart, tuple(map(as_i32, (tb, tj, te, tr, tflag, tout, tslot, tne, tnj, has_next)))


def _moe(xs, tables, wgu, bgu, wd, bd):
    n_slots = xs.shape[0]
    ne, d, f2 = wgu.shape
    f = f2 // 2
    bm, tf = MOE_BM, min(MOE_TF, f)
    nb, ft = n_slots // bm, f // tf
    grid_spec = pltpu.PrefetchScalarGridSpec(
        num_scalar_prefetch=len(tables),
        grid=(nb * ft,),
        in_specs=[pl.BlockSpec((bm,) + _token_tile(d), lambda i, tb, tj, te, *_: (tb[i], 0, 0)),
                  pl.BlockSpec(memory_space=pl.ANY),
                  pl.BlockSpec((None, 1, tf), lambda i, tb, tj, te, *_: (te[i], 0, tj[i])),
                  pl.BlockSpec((None, 1, tf), lambda i, tb, tj, te, *_: (te[i], 0, ft + tj[i])),
                  pl.BlockSpec(memory_space=pl.ANY),
                  pl.BlockSpec((None, 1, d), lambda i, tb, tj, te, *_: (te[i], 0, 0))],
        out_specs=pl.BlockSpec((bm,) + _token_tile(d), lambda i, tb, tj, te, tr, tg, to, *_: (to[i], 0, 0)),
        scratch_shapes=[pltpu.VMEM((d, tf), BF16), pltpu.VMEM((d, tf), BF16), pltpu.VMEM((tf, d), BF16),
                        pltpu.VMEM((MOE_CHUNK_BLOCKS, bm, d), BF16), pltpu.VMEM((MOE_CHUNK_BLOCKS, bm, d), F32),
                        pltpu.VMEM((2, d, tf), F32), pltpu.VMEM((2, d, tf), F32), pltpu.VMEM((2, tf, d), F32),
                        pltpu.SemaphoreType.DMA((2,))],
    )
    return pl.pallas_call(
        functools.partial(_moe_kernel, last_j=ft - 1),
        grid_spec=grid_spec,
        out_shape=jax.ShapeDtypeStruct((n_slots,) + _token_tile(d), jnp.uint32),
        compiler_params=_params("arbitrary"),
        name="moe",
    )(*tables, xs, wgu, bgu, bgu, wd, bd)


def _final_kernel(slot_ref, h_ref, gate_ref, mod_ref, fw_ref, y_hbm, o_ref, ybuf, sem):
    tc, d = h_ref.shape
    half = d // 2

    def issue(grp, carry):
        for q in range(FINAL_ISSUE_UNROLL):
            tk = FINAL_ISSUE_UNROLL * grp + q
            _row_copy(y_hbm, slot_ref[0, 0, tk], ybuf, tk, sem).start(priority=q % 2)
        return carry

    lax.fori_loop(0, tc * TOP_K // FINAL_ISSUE_UNROLL, issue, 0)
    pltpu.make_async_copy(y_hbm.at[pl.ds(0, tc * TOP_K)], ybuf, sem).wait()

    ffn_lo = jnp.zeros((tc, half), F32)
    ffn_hi = jnp.zeros((tc, half), F32)
    for k in range(TOP_K):
        w = _tiles_to_rows(ybuf[k * tc:(k + 1) * tc])
        g = gate_ref[:, k:k + 1]
        ffn_lo = ffn_lo + g * lax.bitcast_convert_type(w << 16, F32)
        ffn_hi = ffn_hi + g * lax.bitcast_convert_type(w & jnp.uint32(0xFFFF0000), F32)
    h2_lo = h_ref[:, :half] + mod_ref[0, 5:6, :half] * ffn_lo
    h2_hi = h_ref[:, half:] + mod_ref[0, 5:6, half:] * ffn_hi
    ms = (jnp.sum(h2_lo * h2_lo, axis=-1, keepdims=True) + jnp.sum(h2_hi * h2_hi, axis=-1, keepdims=True)) / d
    inv = lax.rsqrt(ms + EPS)
    o_ref[:, :half] = (h2_lo * inv * fw_ref[:, :half]).astype(o_ref.dtype)
    o_ref[:, half:] = (h2_hi * inv * fw_ref[:, half:]).astype(o_ref.dtype)


def _final(h1, gates_tk, slot_kt, mod3, final_norm_w, y_slots, seq, out_dtype):
    t, d = h1.shape
    tc = min(FINAL_TC, seq)
    tiles_per_seq = seq // tc
    slots3 = slot_kt.reshape(TOP_K, t // tc, tc).transpose(1, 0, 2).reshape(t // tc, 1, TOP_K * tc)
    return pl.pallas_call(
        _final_kernel,
        grid=(t // tc,),
        in_specs=[pl.BlockSpec((1, 1, TOP_K * tc), lambda i: (i, 0, 0), memory_space=pltpu.SMEM),
                  pl.BlockSpec((tc, d), lambda i: (i, 0)),
                  pl.BlockSpec((tc, gates_tk.shape[1]), lambda i: (i, 0)),
                  pl.BlockSpec((1, N_MOD, d), lambda i: (i // tiles_per_seq, 0, 0)),
                  pl.BlockSpec((1, d), lambda i: (0, 0)),
                  pl.BlockSpec(memory_space=pl.ANY)],
        out_specs=pl.BlockSpec((tc, d), lambda i: (i, 0)),
        out_shape=jax.ShapeDtypeStruct((t, d), out_dtype),
        scratch_shapes=[pltpu.VMEM((TOP_K * tc,) + _token_tile(d), jnp.uint32), pltpu.SemaphoreType.DMA(())],
        compiler_params=_params("arbitrary"),
        name="final",
    )(slots3, h1, gates_tk, mod3, final_norm_w.astype(F32).reshape(1, d), y_slots)


def kernel(x, c, w_ada, b_ada, w_in, pool_w, pool_scale, conv_w, conv_b, dt_bias, a_log, d_skip, ssd_norm_w,
           w_branch_pool, w_branch_ssd, w_out, w_router, b_router, w_gate_up, b_gate_up, w_down, b_down,
           final_norm_w):
    bsz, seq, d = x.shape
    depth = w_ada.shape[0]
    t = bsz * seq
    inner = ssd_norm_w.shape[1]
    heads = dt_bias.shape[1]
    bc = SSD_GROUPS * SSD_STATE
    ne = w_router.shape[2]
    assert depth == 1, "the final RMSNorm is fused into the last kernel of the single layer"
    assert SEQ_TILE == 2 * LANES
    assert seq % SEQ_TILE == 0 and heads <= LANES and (7 * d) % bc == 0 and (4 * d) % inner == 0
    assert inner // SSD_GROUPS % LANES == 0 and d % (len(POOL_WINDOWS) * LANES) == 0
    assert d % (2 * SUBLANES * LANES) == 0

    s1 = d
    s2 = s1 + inner
    s3 = s2 + inner + 2 * bc
    s4 = s3 + heads
    s5 = s4 + d
    blk = {"z": 0, "xs": 1, "p": (2 * inner) // d, "B": (2 * inner + 3 * d) // bc, "C": (2 * inner + 3 * d) // bc + 1}
    gp_off, gs_off = 2 * inner + d, 2 * inner + 2 * d

    h = x.astype(F32).reshape(t, d)
    for layer in range(depth):
        wl = w_in[layer]
        w_main = jnp.concatenate([wl[:, s1:s2], wl[:, s2:s2 + inner], wl[:, :s1], wl[:, s4:s5], wl[:, s5:],
                                  wl[:, s2 + inner:s2 + inner + bc], wl[:, s2 + inner + bc:s3]], axis=1).astype(BF16)
        w_dt = jnp.pad(wl[:, s3:s4].astype(F32), ((0, 0), (0, LANES - heads)))

        mod3 = _ada(c, w_ada[layer], b_ada[layer]).reshape(bsz, N_MOD, d)
        proj, dt_raw = _inproj(h, mod3, w_main, w_dt, seq)
        ypool = _pool(proj, blk["p"], pool_w[layer].astype(BF16), pool_scale[layer], seq, d)
        yssd = _ssd(proj, dt_raw, blk, conv_w[layer], conv_b[layer], dt_bias[layer], a_log[layer], d_skip[layer],
                    ssd_norm_w[layer], seq)
        merged = _merge(ypool, yssd, proj, gp_off, gs_off, w_branch_pool[layer].astype(BF16),
                        w_branch_ssd[layer].astype(BF16))
        h1, u2, idx_kt, gate_kt, rank_kt, counts = _outproj(merged, w_out[layer].astype(BF16), h, mod3,
                                                            w_router[layer], b_router[layer], seq)

        n_blocks = (t * TOP_K) // MOE_BM + ne
        ft = w_down.shape[2] // min(MOE_TF, w_down.shape[2])
        pstart, tables = _moe_tables(counts[:, 0], n_blocks, ft)
        eids = jnp.arange(ne, dtype=jnp.int32)[:, None, None]
        slot_kt = (jnp.sum(jnp.where(idx_kt[:TOP_K][None] == eids, pstart[:, None, None], 0), axis=0)
                   + rank_kt[:TOP_K]).astype(jnp.int32)

        xs = _dispatch(u2, slot_kt.T, n_blocks * MOE_BM)
        y_slots = _moe(xs, tables, w_gate_up[layer].astype(F32), b_gate_up[layer].astype(F32)[:, None, :],
                       w_down[layer].astype(F32), b_down[layer].astype(F32)[:, None, :])
        h = _final(h1, gate_kt.T, slot_kt, mod3, final_norm_w, y_slots, seq, x.dtype)
    return h.reshape(bsz, seq, d)
```

```python
import functools

import jax
import jax.numpy as jnp
from jax import lax
from jax.experimental import pallas as pl
from jax.experimental.pallas import tpu as pltpu

F32 = jnp.float32
BF16 = jnp.bfloat16
HIGHEST = lax.Precision.HIGHEST

EPS = 1e-6
POOL_WINDOWS = (2, 4, 8, 16)
SSD_GROUPS = 8
SSD_STATE = 128
SSD_CONV = 4
SSD_HEAD_DIM = 64
TOP_K = 4
SWIGLU_LIMIT = 7.0
SWIGLU_ALPHA = 1.702
N_MOD = 6

LOG2_E = 1.4426950408889634
LANES = 128
SUBLANES = 8
CONV_HALO = 8
V7X_VMEM_LIMIT = 56 * 1024 * 1024

ADA_TN = 1024
SEQ_TILE = 256
INPROJ_TM, INPROJ_TN = 1024, 2048
MERGE_TM, MERGE_TN = 1024, 512
OUT_TM = 512
MOE_BM, MOE_TF = 512, 512
MOE_CHUNK_BLOCKS = 2
DISPATCH_TS = 512
FINAL_TC = 512
FINAL_ISSUE_UNROLL = 8


def _params(*sem):
    return pltpu.CompilerParams(dimension_semantics=sem, vmem_limit_bytes=V7X_VMEM_LIMIT)


def _silu(v):
    return v * jax.nn.sigmoid(v)


def _pack_bf16_pairs(v):
    half = v.shape[1] // 2
    lo = lax.bitcast_convert_type(v[:, :half].astype(BF16).astype(F32), jnp.uint32)
    hi = lax.bitcast_convert_type(v[:, half:].astype(BF16).astype(F32), jnp.uint32)
    return (hi & jnp.uint32(0xFFFF0000)) | (lo >> 16)


def _token_tile(d):
    return (d // 2 // LANES, LANES)


def _rows_to_tiles(w):
    return w.reshape(w.shape[0], w.shape[1] // LANES, LANES)


def _tiles_to_rows(w):
    return w.reshape(w.shape[0], w.shape[1] * LANES)


def _unpack_bf16_pairs(w):
    lo = lax.bitcast_convert_type(w << 16, F32).astype(BF16)
    hi = lax.bitcast_convert_type(w & jnp.uint32(0xFFFF0000), F32).astype(BF16)
    return lo, hi


def _ada_kernel(cb_ref, w_ref, b_ref, o_ref):
    nb, tn = cb_ref.shape[0], w_ref.shape[1]
    for b in range(nb):
        ca = _silu(cb_ref[b])
        cols = [jnp.sum(w_ref[:, j * LANES:(j + 1) * LANES] * ca, axis=0, keepdims=True)
                for j in range(tn // LANES)]
        o_ref[b:b + 1, :] = jnp.concatenate(cols, axis=1) + b_ref[...]


def _ada(c, w_ada, b_ada):
    nb, k = c.shape
    n = w_ada.shape[1]
    tn = ADA_TN if n % ADA_TN == 0 else n
    cb = jnp.broadcast_to(c.astype(F32)[:, :, None], (nb, k, LANES))
    return pl.pallas_call(
        _ada_kernel,
        grid=(n // tn,),
        in_specs=[pl.BlockSpec((nb, k, LANES), lambda j: (0, 0, 0)),
                  pl.BlockSpec((k, tn), lambda j: (0, j)),
                  pl.BlockSpec((1, tn), lambda j: (0, j))],
        out_specs=pl.BlockSpec((nb, tn), lambda j: (0, j)),
        out_shape=jax.ShapeDtypeStruct((nb, n), F32),
        compiler_params=_params("arbitrary"),
        name="ada",
    )(cb, w_ada, b_ada.reshape(1, n))


def _inproj_kernel(x_ref, mod_ref, w_ref, wdt_hi_ref, wdt_lo_ref, o_ref, dt_ref, u_s, *, rows_per_chunk):
    j = pl.program_id(1)
    tm = x_ref.shape[0]

    @pl.when(j == 0)
    def _():
        sh = mod_ref[0, 0:1, :]
        sc = mod_ref[0, 1:2, :]

        def body(r, carry):
            rows = pl.ds(pl.multiple_of(r * rows_per_chunk, rows_per_chunk), rows_per_chunk)
            xv = x_ref[rows, :]
            ms = jnp.mean(xv * xv, axis=-1, keepdims=True)
            u = xv * lax.rsqrt(ms + EPS) * (1.0 + sc) + sh
            u_hi = u.astype(BF16)
            u_s[rows, :] = u_hi
            u_lo = (u - u_hi.astype(F32)).astype(BF16)
            dt_ref[rows, :] = (jnp.dot(u_hi, wdt_hi_ref[...], preferred_element_type=F32)
                               + (jnp.dot(u_hi, wdt_lo_ref[...], preferred_element_type=F32)
                                  + jnp.dot(u_lo, wdt_hi_ref[...], preferred_element_type=F32)))
            return carry

        lax.fori_loop(0, tm // rows_per_chunk, body, 0)

    o_ref[...] = jnp.dot(u_s[...], w_ref[...], preferred_element_type=F32).astype(o_ref.dtype)


def _inproj(x2, mod3, w_main, w_dt, seq):
    t, d = x2.shape
    w_dt_hi = w_dt.astype(BF16)
    w_dt_lo = (w_dt - w_dt_hi.astype(F32)).astype(BF16)
    n = w_main.shape[1]
    tm = min(INPROJ_TM, seq)
    tn = INPROJ_TN
    tiles_per_seq = seq // tm
    return pl.pallas_call(
        functools.partial(_inproj_kernel, rows_per_chunk=min(128, tm)),
        grid=(t // tm, n // tn),
        in_specs=[pl.BlockSpec((tm, d), lambda i, j: (i, 0)),
                  pl.BlockSpec((1, N_MOD, d), lambda i, j: (i // tiles_per_seq, 0, 0)),
                  pl.BlockSpec((d, tn), lambda i, j: (0, j)),
                  pl.BlockSpec((d, LANES), lambda i, j: (0, 0)),
                  pl.BlockSpec((d, LANES), lambda i, j: (0, 0))],
        out_specs=[pl.BlockSpec((tm, tn), lambda i, j: (i, j)),
                   pl.BlockSpec((tm, LANES), lambda i, j: (i, 0))],
        out_shape=[jax.ShapeDtypeStruct((t, n), BF16), jax.ShapeDtypeStruct((t, LANES), F32)],
        scratch_shapes=[pltpu.VMEM((tm, d), BF16)],
        compiler_params=_params("arbitrary", "arbitrary"),
        name="inproj",
    )(x2, mod3, w_main, w_dt_hi, w_dt_lo)


def _pool_kernel(p_ref, pw_ref, ps_ref, o_ref, prev_s, *, tiles_per_seq):
    i = pl.program_id(0)
    tl, d = p_ref.shape
    gd = d // len(POOL_WINDOWS)
    it = i % tiles_per_seq

    @pl.when(it == 0)
    def _():
        prev_s[...] = jnp.zeros_like(prev_s)

    row = lax.broadcasted_iota(jnp.int32, (tl, 2 * tl), 0)
    col = lax.broadcasted_iota(jnp.int32, (tl, 2 * tl), 1)
    pos = (it * tl + lax.broadcasted_iota(jnp.int32, (tl, 1), 0) + 1).astype(F32)
    for g, w in enumerate(POOL_WINDOWS):
        sl = slice(g * gd, (g + 1) * gd)
        cur = p_ref[:, sl]
        ext = jnp.concatenate([prev_s[:, sl], cur], axis=0)
        band = ((col <= row + tl) & (col > row + tl - w)).astype(BF16)
        win_sum = jnp.dot(band, ext, preferred_element_type=F32)
        mean = win_sum / jnp.minimum(pos, float(w))
        dlt = (mean - cur.astype(F32)).astype(BF16)
        mixed = jnp.dot(dlt, pw_ref[g], preferred_element_type=F32)
        o_ref[:, sl] = (mixed * ps_ref[:, sl]).astype(o_ref.dtype)
    prev_s[...] = p_ref[...]


def _pool(proj, p_blk, pool_w_bf, pool_scale, seq, d):
    t = proj.shape[0]
    tl = SEQ_TILE
    g, gd = pool_w_bf.shape[0], pool_w_bf.shape[1]
    return pl.pallas_call(
        functools.partial(_pool_kernel, tiles_per_seq=seq // tl),
        grid=(t // tl,),
        in_specs=[pl.BlockSpec((tl, d), lambda i: (i, p_blk)),
                  pl.BlockSpec((g, gd, gd), lambda i: (0, 0, 0)),
                  pl.BlockSpec((1, d), lambda i: (0, 0))],
        out_specs=pl.BlockSpec((tl, d), lambda i: (i, 0)),
        out_shape=jax.ShapeDtypeStruct((t, d), BF16),
        scratch_shapes=[pltpu.VMEM((tl, d), BF16)],
        compiler_params=_params("arbitrary"),
        name="pool",
    )(proj, pool_w_bf, pool_scale.reshape(1, d))


def _ssd_kernel(z_ref, xs_ref, bm_ref, cm_ref, dt_ref,
                cwx_ref, cwb_ref, cwc_ref, cbx_ref, cbb_ref, cbc_ref,
                dtb_ref, alog_ref, dsk_ref, nw_ref, exp_ref,
                o_ref,
                extx_s, extb_s, extc_s, state_s, xdt_s, y_s, cb_s, cs_s, cst_s, xc_s, bc_s, cc_s,
                *, tiles_per_seq):
    i = pl.program_id(0)
    tl, inner = xs_ref.shape
    gw = inner // SSD_GROUPS
    pairs_per_group = gw // LANES
    n_pairs = inner // LANES

    @pl.when(i % tiles_per_seq == 0)
    def _():
        extx_s[0:CONV_HALO, :] = jnp.zeros((CONV_HALO, inner), F32)
        extb_s[0:CONV_HALO, :] = jnp.zeros((CONV_HALO, extb_s.shape[1]), F32)
        extc_s[0:CONV_HALO, :] = jnp.zeros((CONV_HALO, extc_s.shape[1]), F32)
        state_s[...] = jnp.zeros_like(state_s)

    def conv_silu(ext_ref, src_ref, w_ref, b_ref, dst_ref, width, cw):
        for c0 in range(0, width, cw):
            cs = slice(c0, c0 + cw)
            ext_ref[CONV_HALO:CONV_HALO + tl, cs] = src_ref[:, cs].astype(F32)
            acc = b_ref[:, cs] + w_ref[SSD_CONV - 1:SSD_CONV, cs] * ext_ref[CONV_HALO:CONV_HALO + tl, cs]
            for k in range(1, SSD_CONV):
                acc = acc + w_ref[SSD_CONV - 1 - k:SSD_CONV - k, cs] * ext_ref[pl.ds(CONV_HALO - k, tl), cs]
            dst_ref[:, cs] = _silu(acc)
            ext_ref[0:CONV_HALO, cs] = ext_ref[tl:tl + CONV_HALO, cs]

    conv_silu(extx_s, xs_ref, cwx_ref, cbx_ref, xc_s, inner, LANES)
    conv_silu(extb_s, bm_ref, cwb_ref, cbb_ref, bc_s, bm_ref.shape[1], LANES)
    conv_silu(extc_s, cm_ref, cwc_ref, cbc_ref, cc_s, cm_ref.shape[1], LANES)

    dtv = dt_ref[...] + dtb_ref[...]
    dt = jnp.maximum(dtv, 0.0) + jnp.log1p(jnp.exp(-jnp.abs(dtv)))
    a = -jnp.exp(alog_ref[...])
    da = dt * a
    ri = lax.broadcasted_iota(jnp.int32, (tl, tl), 0)
    ci = lax.broadcasted_iota(jnp.int32, (tl, tl), 1)
    causal = ri >= ci
    cs = jnp.dot(causal.astype(F32), da, precision=HIGHEST, preferred_element_type=F32)
    cs2 = cs * LOG2_E
    cs_s[...] = cs2
    cst_s[...] = cs2.T
    last = cs[tl - 1:tl, :]
    dt_bf = dt.astype(BF16)
    ecs_bf = jnp.exp(cs).astype(BF16)
    dte_bf = jnp.exp(last - cs).astype(BF16)

    for g in range(SSD_GROUPS):
        gs = slice(g * gw, (g + 1) * gw)
        ns = slice(g * SSD_STATE, (g + 1) * SSD_STATE)
        exp_g = exp_ref[:, gs]
        dt_x = jnp.dot(dt_bf, exp_g, preferred_element_type=F32)
        ecs_x = jnp.dot(ecs_bf, exp_g, preferred_element_type=F32)
        dte_x = jnp.dot(dte_bf, exp_g, preferred_element_type=F32)
        xg = xc_s[:, gs]
        xdt = xg * dt_x
        xdt_bf = xdt.astype(BF16)
        xd_bf = (xdt * dte_x).astype(BF16)
        bg = bc_s[:, ns]
        cg = cc_s[:, ns].astype(BF16)
        cb_s[g] = lax.dot_general(cg, bg.astype(BF16), (((1,), (1,)), ((), ())), preferred_element_type=F32)
        s_old = state_s[g]
        y_off = jnp.dot(cg, s_old.astype(BF16), preferred_element_type=F32) * ecs_x
        state_s[g] = (s_old * ecs_x[tl - 1:tl, :]
                      + jnp.dot(bg.T.astype(BF16), xd_bf, preferred_element_type=F32))
        y0 = y_off + dsk_ref[:, gs] * xg
        for q in range(pairs_per_group):
            qs = slice(q * LANES, (q + 1) * LANES)
            xdt_s[g * pairs_per_group + q] = xdt_bf[:, qs]
            y_s[g * pairs_per_group + q] = y0[:, qs]

    half = tl // 2
    causal_top = (lax.broadcasted_iota(jnp.int32, (half, half), 0)
                  >= lax.broadcasted_iota(jnp.int32, (half, half), 1))
    causal_bot = (lax.broadcasted_iota(jnp.int32, (half, tl), 0) + half
                  >= lax.broadcasted_iota(jnp.int32, (half, tl), 1))
    first_head_lanes = lax.broadcasted_iota(jnp.int32, (half, LANES), 1) < SSD_HEAD_DIM

    def pair_body(hp, carry):
        g = hp // pairs_per_group
        cb_top = cb_s[g, 0:half, 0:half]
        cb_bot = cb_s[g, half:tl, :]
        x_top = xdt_s[hp, 0:half, :]
        x_all = xdt_s[hp]
        cs_top = cs_s[0:half, :]
        cs_bot = cs_s[half:tl, :]
        tops, bots = [], []
        for e in range(2):
            h = 2 * hp + e
            head_lane = jnp.full((half, LANES), h, jnp.int32)
            col_top = jnp.take_along_axis(cs_top, head_lane, axis=1)
            col_bot = jnp.take_along_axis(cs_bot, head_lane, axis=1)
            col_bot = jnp.concatenate([col_bot] * (tl // LANES), axis=1)
            row_all = cst_s[pl.ds(h, 1), :]
            row_top = row_all[:, 0:half]
            m_top = jnp.exp2(jnp.where(causal_top, col_top - row_top, -jnp.inf)) * cb_top
            m_bot = jnp.exp2(jnp.where(causal_bot, col_bot - row_all, -jnp.inf)) * cb_bot
            tops.append(jnp.dot(m_top.astype(BF16), x_top, preferred_element_type=F32))
            bots.append(jnp.dot(m_bot.astype(BF16), x_all, preferred_element_type=F32))
        y_s[hp, 0:half, :] = y_s[hp, 0:half, :] + jnp.where(first_head_lanes, tops[0], tops[1])
        y_s[hp, half:tl, :] = y_s[hp, half:tl, :] + jnp.where(first_head_lanes, bots[0], bots[1])
        return carry

    lax.fori_loop(0, n_pairs, pair_body, 0, unroll=16)

    for g in range(SSD_GROUPS):
        gs = slice(g * gw, (g + 1) * gw)
        yg = jnp.concatenate([y_s[g * pairs_per_group + q] for q in range(pairs_per_group)], axis=1)
        yg = yg * _silu(z_ref[:, gs].astype(F32))
        ms = jnp.mean(yg * yg, axis=-1, keepdims=True)
        o_ref[:, gs] = (yg * lax.rsqrt(ms + EPS) * nw_ref[:, gs]).astype(o_ref.dtype)


def _ssd(proj, dt_raw, blk, conv_w, conv_b, dt_bias, a_log, d_skip, ssd_norm_w, seq):
    t = proj.shape[0]
    tl = SEQ_TILE
    heads = dt_bias.shape[0]
    inner = ssd_norm_w.shape[0]
    bc = SSD_GROUPS * SSD_STATE
    gw = inner // SSD_GROUPS
    pad = LANES - heads
    row = lambda v: v.astype(F32).reshape(1, -1)
    dtb = jnp.pad(row(dt_bias), ((0, 0), (0, pad)))
    alog = jnp.pad(row(a_log), ((0, 0), (0, pad)))
    dsk = jnp.repeat(d_skip.astype(F32), SSD_HEAD_DIM).reshape(1, inner)
    expand = (lax.broadcasted_iota(jnp.int32, (LANES, inner), 1) // SSD_HEAD_DIM
              == lax.broadcasted_iota(jnp.int32, (LANES, inner), 0)).astype(BF16)
    cw = conv_w.astype(F32)
    cbias = row(conv_b)
    full = lambda shape: pl.BlockSpec(shape, lambda i: tuple(0 for _ in shape))
    return pl.pallas_call(
        functools.partial(_ssd_kernel, tiles_per_seq=seq // tl),
        grid=(t // tl,),
        in_specs=[pl.BlockSpec((tl, inner), lambda i: (i, blk["z"])),
                  pl.BlockSpec((tl, inner), lambda i: (i, blk["xs"])),
                  pl.BlockSpec((tl, bc), lambda i: (i, blk["B"])),
                  pl.BlockSpec((tl, bc), lambda i: (i, blk["C"])),
                  pl.BlockSpec((tl, LANES), lambda i: (i, 0)),
                  full((SSD_CONV, inner)), full((SSD_CONV, bc)), full((SSD_CONV, bc)),
                  full((1, inner)), full((1, bc)), full((1, bc)),
                  full((1, LANES)), full((1, LANES)), full((1, inner)), full((1, inner)),
                  full((LANES, inner))],
        out_specs=pl.BlockSpec((tl, inner), lambda i: (i, 0)),
        out_shape=jax.ShapeDtypeStruct((t, inner), BF16),
        scratch_shapes=[pltpu.VMEM((tl + CONV_HALO, inner), F32),
                        pltpu.VMEM((tl + CONV_HALO, bc), F32),
                        pltpu.VMEM((tl + CONV_HALO, bc), F32),
                        pltpu.VMEM((SSD_GROUPS, SSD_STATE, gw), F32),
                        pltpu.VMEM((inner // LANES, tl, LANES), BF16),
                        pltpu.VMEM((inner // LANES, tl, LANES), F32),
                        pltpu.VMEM((SSD_GROUPS, tl, tl), F32),
                        pltpu.VMEM((tl, LANES), F32),
                        pltpu.VMEM((LANES, tl), F32),
                        pltpu.VMEM((tl, inner), F32),
                        pltpu.VMEM((tl, bc), F32),
                        pltpu.VMEM((tl, bc), F32)],
        compiler_params=_params("arbitrary"),
        name="ssd",
    )(proj, proj, proj, proj, dt_raw,
      cw[:, :inner], cw[:, inner:inner + bc], cw[:, inner + bc:],
      cbias[:, :inner], cbias[:, inner:inner + bc], cbias[:, inner + bc:],
      dtb, alog, dsk, row(ssd_norm_w), expand)


def _merge_kernel(yp_ref, ys_ref, gp_ref, gs_ref, wp_ref, ws_ref, o_ref):
    a = jnp.dot(yp_ref[...], wp_ref[...], preferred_element_type=F32)
    b = jnp.dot(ys_ref[...], ws_ref[...], preferred_element_type=F32)
    o_ref[...] = (jax.nn.sigmoid(gp_ref[...].astype(F32)) * a
                  + jax.nn.sigmoid(gs_ref[...].astype(F32)) * b).astype(o_ref.dtype)


def _merge(ypool, yssd, proj, gp_off, gs_off, wbp, wbs):
    t, d = ypool.shape
    inner = yssd.shape[1]
    tm, tn = min(MERGE_TM, t), MERGE_TN
    return pl.pallas_call(
        _merge_kernel,
        grid=(t // tm, d // tn),
        in_specs=[pl.BlockSpec((tm, d), lambda i, j: (i, 0)),
                  pl.BlockSpec((tm, inner), lambda i, j: (i, 0)),
                  pl.BlockSpec((tm, tn), lambda i, j: (i, gp_off // tn + j)),
                  pl.BlockSpec((tm, tn), lambda i, j: (i, gs_off // tn + j)),
                  pl.BlockSpec((d, tn), lambda i, j: (0, j)),
                  pl.BlockSpec((inner, tn), lambda i, j: (0, j))],
        out_specs=pl.BlockSpec((tm, tn), lambda i, j: (i, j)),
        out_shape=jax.ShapeDtypeStruct((t, d), BF16),
        compiler_params=_params("arbitrary", "arbitrary"),
        name="merge",
    )(ypool, yssd, proj, proj, wbp, wbs)


def _out_kernel(m_ref, wo_ref, x_ref, mod_ref, wr_ref, br_ref,
                h_ref, u_ref, idx_ref, gate_ref, rank_ref, cnt_ref, carry_s):
    i = pl.program_id(0)
    tm = m_ref.shape[0]
    ne = wr_ref.shape[0]

    @pl.when(i == 0)
    def _():
        carry_s[...] = jnp.zeros_like(carry_s)

    mix = jnp.dot(m_ref[...], wo_ref[...], preferred_element_type=F32)
    h1 = x_ref[...] + mod_ref[0, 2:3, :] * mix
    h_ref[...] = h1
    ms = jnp.mean(h1 * h1, axis=-1, keepdims=True)
    u = h1 * lax.rsqrt(ms + EPS) * (1.0 + mod_ref[0, 4:5, :]) + mod_ref[0, 3:4, :]
    u_ref[...] = _rows_to_tiles(_pack_bf16_pairs(u))
    logits = lax.dot_general(wr_ref[...], u, (((1,), (1,)), ((), ())), precision=HIGHEST,
                             preferred_element_type=F32) + br_ref[...]
    eidx = lax.broadcasted_iota(jnp.int32, (ne, tm), 0)
    work = logits
    vals, idxs, hots = [], [], []
    for _ in range(TOP_K):
        mx = jnp.max(work, axis=0, keepdims=True)
        sel = jnp.min(jnp.where(work == mx, eidx, ne), axis=0, keepdims=True)
        hot = eidx == sel
        vals.append(mx)
        idxs.append(sel)
        hots.append(hot)
        work = jnp.where(hot, -jnp.inf, work)
    exps = [jnp.exp(v - vals[0]) for v in vals]
    den = exps[0]
    for e in exps[1:]:
        den = den + e
    cnt = hots[0].astype(F32)
    for hot in hots[1:]:
        cnt = cnt + hot.astype(F32)
    ti = lax.broadcasted_iota(jnp.int32, (tm, tm), 0)
    tj = lax.broadcasted_iota(jnp.int32, (tm, tm), 1)
    before = (ti < tj).astype(BF16)
    prefix = jnp.dot(cnt.astype(BF16), before, preferred_element_type=F32)
    base = carry_s[:, 0:1] + prefix
    pad_rows = idx_ref.shape[0] - TOP_K
    ranks = [jnp.sum(jnp.where(hot, base, 0.0), axis=0, keepdims=True) for hot in hots]
    idx_ref[...] = jnp.concatenate(idxs + [jnp.zeros((pad_rows, tm), jnp.int32)], axis=0)
    gate_ref[...] = jnp.concatenate([e / den for e in exps] + [jnp.zeros((pad_rows, tm), F32)], axis=0)
    rank_ref[...] = jnp.concatenate([r.astype(jnp.int32) for r in ranks]
                                    + [jnp.zeros((pad_rows, tm), jnp.int32)], axis=0)
    carry_s[...] = carry_s[...] + jnp.sum(cnt, axis=1, keepdims=True)
    cnt_ref[...] = carry_s[...].astype(jnp.int32)


def _outproj(merged, wo, x2, mod3, w_router, b_router, seq):
    t, d = x2.shape
    ne = w_router.shape[1]
    tm = min(OUT_TM, seq)
    tiles_per_seq = seq // tm
    rows = 8
    return pl.pallas_call(
        _out_kernel,
        grid=(t // tm,),
        in_specs=[pl.BlockSpec((tm, d), lambda i: (i, 0)),
                  pl.BlockSpec((d, d), lambda i: (0, 0), pipeline_mode=pl.Buffered(1)),
                  pl.BlockSpec((tm, d), lambda i: (i, 0)),
                  pl.BlockSpec((1, N_MOD, d), lambda i: (i // tiles_per_seq, 0, 0)),
                  pl.BlockSpec((ne, d), lambda i: (0, 0)),
                  pl.BlockSpec((ne, 1), lambda i: (0, 0))],
        out_specs=[pl.BlockSpec((tm, d), lambda i: (i, 0)),
                   pl.BlockSpec((tm,) + _token_tile(d), lambda i: (i, 0, 0)),
                   pl.BlockSpec((rows, tm), lambda i: (0, i)),
                   pl.BlockSpec((rows, tm), lambda i: (0, i)),
                   pl.BlockSpec((rows, tm), lambda i: (0, i)),
                   pl.BlockSpec((ne, LANES), lambda i: (0, 0))],
        out_shape=[jax.ShapeDtypeStruct((t, d), F32), jax.ShapeDtypeStruct((t,) + _token_tile(d), jnp.uint32),
                   jax.ShapeDtypeStruct((rows, t), jnp.int32), jax.ShapeDtypeStruct((rows, t), F32),
                   jax.ShapeDtypeStruct((rows, t), jnp.int32), jax.ShapeDtypeStruct((ne, LANES), jnp.int32)],
        scratch_shapes=[pltpu.VMEM((ne, LANES), F32)],
        compiler_params=_params("arbitrary"),
        name="outproj",
    )(merged, wo, x2, mod3, w_router.astype(F32).T, b_router.astype(F32).reshape(ne, 1))


def _row_copy(src_hbm, src_row, dst_ref, dst_row, sem):
    return pltpu.make_async_copy(src_hbm.at[pl.ds(src_row, 1)], dst_ref.at[pl.ds(dst_row, 1)], sem)


def _dispatch_kernel(slot_ref, u_ref, init_hbm, xs_hbm, sem):
    del init_hbm
    ts = u_ref.shape[0]

    def issue(tok, carry):
        for k in range(TOP_K):
            _row_copy(u_ref, tok, xs_hbm, slot_ref[0, 0, tok * TOP_K + k], sem).start(priority=k % 2)
        return carry

    lax.fori_loop(0, ts, issue, 0)
    rows = pl.ds(0, ts * TOP_K)
    pltpu.make_async_copy(xs_hbm.at[rows], xs_hbm.at[rows], sem).wait()


def _dispatch(u2, slot_tk, n_slots):
    t, tile = u2.shape[0], u2.shape[1:]
    ts = min(DISPATCH_TS, t)
    slots3 = slot_tk.reshape(t // ts, 1, ts * TOP_K)
    init = jnp.zeros((n_slots,) + tile, u2.dtype)
    return pl.pallas_call(
        _dispatch_kernel,
        grid=(t // ts,),
        in_specs=[pl.BlockSpec((1, 1, ts * TOP_K), lambda i: (i, 0, 0), memory_space=pltpu.SMEM),
                  pl.BlockSpec((ts,) + tile, lambda i: (i, 0, 0)),
                  pl.BlockSpec(memory_space=pl.ANY)],
        out_specs=pl.BlockSpec(memory_space=pl.ANY),
        out_shape=jax.ShapeDtypeStruct((n_slots,) + tile, u2.dtype),
        scratch_shapes=[pltpu.SemaphoreType.DMA(())],
        input_output_aliases={2: 0},
        compiler_params=_params("arbitrary"),
        name="dispatch",
    )(slots3, u2, init)


ITEM_IDLE, ITEM_ACTIVE, ITEM_ZERO_FILL, ITEM_ACTIVE_HALF = 0, 1, 2, 3


def _moe_kernel(tb_ref, tj_ref, te_ref, tr_ref, tflag_ref, tout_ref, tslot_ref, tne_ref, tnj_ref, thn_ref,
                x_ref, wgu_hbm, bg_ref, bu_ref, wd_hbm, bd_ref, o_ref,
                wg_s, wu_s, wd_s, xb_s, acc_s, wg_buf, wu_buf, wd_buf, sem, *, last_j):
    del tb_ref, tout_ref
    i = pl.program_id(0)
    flag = tflag_ref[i]
    r = tr_ref[i]
    j = tj_ref[i]
    half = x_ref.shape[1] * x_ref.shape[2]
    tf = wg_s.shape[1]
    up_col0 = wgu_hbm.shape[2] // 2

    def weight_copies(e, jt, slot):
        c0 = pl.multiple_of(jt * tf, tf)
        return (pltpu.make_async_copy(wgu_hbm.at[e, :, pl.ds(c0, tf)], wg_buf.at[slot], sem.at[slot]),
                pltpu.make_async_copy(wgu_hbm.at[e, :, pl.ds(up_col0 + c0, tf)], wu_buf.at[slot], sem.at[slot]),
                pltpu.make_async_copy(wd_hbm.at[e, pl.ds(c0, tf), :], wd_buf.at[slot], sem.at[slot]))

    is_active = (flag & 1) == 1

    @pl.when(i == 0)
    def _():
        acc_s[...] = jnp.zeros_like(acc_s)

    @pl.when(is_active & (r == 0))
    def _():
        slot = tslot_ref[i]

        @pl.when(i == 0)
        def _():
            for cp in weight_copies(te_ref[i], j, slot):
                cp.start()

        for cp in weight_copies(te_ref[i], j, slot):
            cp.wait()
        for s in range(2):
            @pl.when(slot == s)
            def _():
                wg_s[...] = wg_buf[s].astype(BF16)
                wu_s[...] = wu_buf[s].astype(BF16)
                wd_s[...] = wd_buf[s].astype(BF16)

        @pl.when(thn_ref[i] == 1)
        def _():
            for cp in weight_copies(tne_ref[i], tnj_ref[i], 1 - slot):
                cp.start()

    @pl.when(is_active & (j == 0))
    def _():
        lo, hi = _unpack_bf16_pairs(_tiles_to_rows(x_ref[...]))
        xb_s[r, :, :half] = lo
        xb_s[r, :, half:] = hi

    def expert_mlp(rows):
        xb = xb_s[r, 0:rows, :]
        gate = jnp.dot(xb, wg_s[...], preferred_element_type=F32) + bg_ref[...]
        up = jnp.dot(xb, wu_s[...], preferred_element_type=F32) + bu_ref[...]
        gate = jnp.minimum(gate, SWIGLU_LIMIT)
        up = jnp.clip(up, -SWIGLU_LIMIT, SWIGLU_LIMIT)
        act = (up + 1.0) * gate * jax.nn.sigmoid(SWIGLU_ALPHA * gate)
        part = jnp.dot(act.astype(BF16), wd_s[...], preferred_element_type=F32)
        base = jnp.where(j == 0, jnp.broadcast_to(bd_ref[...], (rows, bd_ref.shape[1])), acc_s[r, 0:rows, :])
        acc_s[r, 0:rows, :] = base + part

        @pl.when(j == last_j)
        def _():
            o_ref[0:rows] = _rows_to_tiles(_pack_bf16_pairs(acc_s[r, 0:rows, :]))
            if rows < o_ref.shape[0]:
                o_ref[rows:] = jnp.zeros((o_ref.shape[0] - rows,) + o_ref.shape[1:], o_ref.dtype)

    @pl.when(flag == ITEM_ACTIVE)
    def _():
        expert_mlp(o_ref.shape[0])

    @pl.when(flag == ITEM_ACTIVE_HALF)
    def _():
        expert_mlp(o_ref.shape[0] // 2)

    @pl.when(flag == ITEM_ZERO_FILL)
    def _():
        o_ref[...] = jnp.zeros_like(o_ref)


def _lookup(table, idx):
    hot = idx[:, None] == jnp.arange(table.shape[0], dtype=jnp.int32)[None, :]
    return jnp.sum(jnp.where(hot, table[None, :].astype(jnp.int32), 0), axis=1).astype(jnp.int32)


def _moe_tables(counts, n_blocks, ft):
    ne = counts.shape[0]
    padded = (counts + MOE_BM - 1) // MOE_BM * MOE_BM
    pend = jnp.cumsum(padded)
    pstart = pend - padded
    blocks_per_expert = padded // MOE_BM
    first_block = pstart // MOE_BM
    n_active = (pend[-1] // MOE_BM).astype(jnp.int32)

    blocks = jnp.arange(n_blocks, dtype=jnp.int32)
    block_expert = jnp.minimum(jnp.sum(pend[None, :] <= (blocks * MOE_BM)[:, None], axis=1), ne - 1).astype(jnp.int32)
    off = blocks - _lookup(first_block, block_expert)
    is_start = (off % MOE_CHUNK_BLOCKS == 0) & (blocks < n_active)
    chunk_size = jnp.minimum(MOE_CHUNK_BLOCKS,
                             _lookup(blocks_per_expert, block_expert) - off // MOE_CHUNK_BLOCKS * MOE_CHUNK_BLOCKS)

    items = jnp.arange(n_blocks * ft, dtype=jnp.int32)
    active = items < ft * n_active
    src = jnp.minimum(items, jnp.maximum(ft * n_active - 1, 0))
    cand = jnp.where(is_start[None, :] & (ft * blocks[None, :] <= src[:, None]), blocks[None, :], 0)
    cstart = jnp.max(cand, axis=1).astype(jnp.int32)
    m = jnp.maximum(_lookup(chunk_size, cstart), 1)
    local = src - ft * cstart
    tj = local // m
    tr = local % m
    tb = cstart + tr
    te = _lookup(block_expert, cstart)
    k = items - ft * n_active
    zero_fill = (~active) & (n_active + k < n_blocks)
    tout = jnp.where(active, jnp.where(tj == ft - 1, tb, cstart), jnp.minimum(n_active + k, n_blocks - 1))
    valid_rows = _lookup(counts, te) - (tb - _lookup(first_block, te)) * MOE_BM
    kind = jnp.where(valid_rows <= MOE_BM // 2, ITEM_ACTIVE_HALF, ITEM_ACTIVE)
    tflag = jnp.where(active, kind, jnp.where(zero_fill, ITEM_ZERO_FILL, ITEM_IDLE))
    group_start = active & (tr == 0)
    tslot = (jnp.cumsum(group_start.astype(jnp.int32)) - 1) % 2
    nxt = items + m
    has_next = group_start & (nxt < ft * n_active)
    nxt = jnp.minimum(nxt, n_blocks * ft - 1)
    tne, tnj = jnp.take(te, nxt), jnp.take(tj, nxt)
    as_i32 = lambda v: v.astype(jnp.int32)
    return pstart, tuple(map(as_i32, (tb, tj, te, tr, tflag, tout, tslot, tne, tnj, has_next)))


def _moe(xs, tables, wgu, bgu, wd, bd):
    n_slots = xs.shape[0]
    ne, d, f2 = wgu.shape
    f = f2 // 2
    bm, tf = MOE_BM, min(MOE_TF, f)
    nb, ft = n_slots // bm, f // tf
    grid_spec = pltpu.PrefetchScalarGridSpec(
        num_scalar_prefetch=len(tables),
        grid=(nb * ft,),
        in_specs=[pl.BlockSpec((bm,) + _token_tile(d), lambda i, tb, tj, te, *_: (tb[i], 0, 0)),
                  pl.BlockSpec(memory_space=pl.ANY),
                  pl.BlockSpec((None, 1, tf), lambda i, tb, tj, te, *_: (te[i], 0, tj[i])),
                  pl.BlockSpec((None, 1, tf), lambda i, tb, tj, te, *_: (te[i], 0, ft + tj[i])),
                  pl.BlockSpec(memory_space=pl.ANY),
                  pl.BlockSpec((None, 1, d), lambda i, tb, tj, te, *_: (te[i], 0, 0))],
        out_specs=pl.BlockSpec((bm,) + _token_tile(d), lambda i, tb, tj, te, tr, tg, to, *_: (to[i], 0, 0)),
        scratch_shapes=[pltpu.VMEM((d, tf), BF16), pltpu.VMEM((d, tf), BF16), pltpu.VMEM((tf, d), BF16),
                        pltpu.VMEM((MOE_CHUNK_BLOCKS, bm, d), BF16), pltpu.VMEM((MOE_CHUNK_BLOCKS, bm, d), F32),
                        pltpu.VMEM((2, d, tf), F32), pltpu.VMEM((2, d, tf), F32), pltpu.VMEM((2, tf, d), F32),
                        pltpu.SemaphoreType.DMA((2,))],
    )
    return pl.pallas_call(
        functools.partial(_moe_kernel, last_j=ft - 1),
        grid_spec=grid_spec,
        out_shape=jax.ShapeDtypeStruct((n_slots,) + _token_tile(d), jnp.uint32),
        compiler_params=_params("arbitrary"),
        name="moe",
    )(*tables, xs, wgu, bgu, bgu, wd, bd)


def _final_kernel(slot_ref, h_ref, gate_ref, mod_ref, fw_ref, y_hbm, o_ref, ybuf, sem):
    tc, d = h_ref.shape
    half = d // 2

    def issue(grp, carry):
        for q in range(FINAL_ISSUE_UNROLL):
            tk = FINAL_ISSUE_UNROLL * grp + q
            _row_copy(y_hbm, slot_ref[0, 0, tk], ybuf, tk, sem).start(priority=q % 2)
        return carry

    lax.fori_loop(0, tc * TOP_K // FINAL_ISSUE_UNROLL, issue, 0)
    pltpu.make_async_copy(y_hbm.at[pl.ds(0, tc * TOP_K)], ybuf, sem).wait()

    ffn_lo = jnp.zeros((tc, half), F32)
    ffn_hi = jnp.zeros((tc, half), F32)
    for k in range(TOP_K):
        w = _tiles_to_rows(ybuf[k * tc:(k + 1) * tc])
        g = gate_ref[:, k:k + 1]
        ffn_lo = ffn_lo + g * lax.bitcast_convert_type(w << 16, F32)
        ffn_hi = ffn_hi + g * lax.bitcast_convert_type(w & jnp.uint32(0xFFFF0000), F32)
    h2_lo = h_ref[:, :half] + mod_ref[0, 5:6, :half] * ffn_lo
    h2_hi = h_ref[:, half:] + mod_ref[0, 5:6, half:] * ffn_hi
    ms = (jnp.sum(h2_lo * h2_lo, axis=-1, keepdims=True) + jnp.sum(h2_hi * h2_hi, axis=-1, keepdims=True)) / d
    inv = lax.rsqrt(ms + EPS)
    o_ref[:, :half] = (h2_lo * inv * fw_ref[:, :half]).astype(o_ref.dtype)
    o_ref[:, half:] = (h2_hi * inv * fw_ref[:, half:]).astype(o_ref.dtype)


def _final(h1, gates_tk, slot_kt, mod3, final_norm_w, y_slots, seq, out_dtype):
    t, d = h1.shape
    tc = min(FINAL_TC, seq)
    tiles_per_seq = seq // tc
    slots3 = slot_kt.reshape(TOP_K, t // tc, tc).transpose(1, 0, 2).reshape(t // tc, 1, TOP_K * tc)
    return pl.pallas_call(
        _final_kernel,
        grid=(t // tc,),
        in_specs=[pl.BlockSpec((1, 1, TOP_K * tc), lambda i: (i, 0, 0), memory_space=pltpu.SMEM),
                  pl.BlockSpec((tc, d), lambda i: (i, 0)),
                  pl.BlockSpec((tc, gates_tk.shape[1]), lambda i: (i, 0)),
                  pl.BlockSpec((1, N_MOD, d), lambda i: (i // tiles_per_seq, 0, 0)),
                  pl.BlockSpec((1, d), lambda i: (0, 0)),
                  pl.BlockSpec(memory_space=pl.ANY)],
        out_specs=pl.BlockSpec((tc, d), lambda i: (i, 0)),
        out_shape=jax.ShapeDtypeStruct((t, d), out_dtype),
        scratch_shapes=[pltpu.VMEM((TOP_K * tc,) + _token_tile(d), jnp.uint32), pltpu.SemaphoreType.DMA(())],
        compiler_params=_params("arbitrary"),
        name="final",
    )(slots3, h1, gates_tk, mod3, final_norm_w.astype(F32).reshape(1, d), y_slots)


def kernel(x, c, w_ada, b_ada, w_in, pool_w, pool_scale, conv_w, conv_b, dt_bias, a_log, d_skip, ssd_norm_w,
           w_branch_pool, w_branch_ssd, w_out, w_router, b_router, w_gate_up, b_gate_up, w_down, b_down,
           final_norm_w):
    bsz, seq, d = x.shape
    depth = w_ada.shape[0]
    t = bsz * seq
    inner = ssd_norm_w.shape[1]
    heads = dt_bias.shape[1]
    bc = SSD_GROUPS * SSD_STATE
    ne = w_router.shape[2]
    assert depth == 1, "the final RMSNorm is fused into the last kernel of the single layer"
    assert SEQ_TILE == 2 * LANES
    assert seq % SEQ_TILE == 0 and heads <= LANES and (7 * d) % bc == 0 and (4 * d) % inner == 0
    assert inner // SSD_GROUPS % LANES == 0 and d % (len(POOL_WINDOWS) * LANES) == 0
    assert d % (2 * SUBLANES * LANES) == 0

    s1 = d
    s2 = s1 + inner
    s3 = s2 + inner + 2 * bc
    s4 = s3 + heads
    s5 = s4 + d
    blk = {"z": 0, "xs": 1, "p": (2 * inner) // d, "B": (2 * inner + 3 * d) // bc, "C": (2 * inner + 3 * d) // bc + 1}
    gp_off, gs_off = 2 * inner + d, 2 * inner + 2 * d

    h = x.astype(F32).reshape(t, d)
    for layer in range(depth):
        wl = w_in[layer]
        w_main = jnp.concatenate([wl[:, s1:s2], wl[:, s2:s2 + inner], wl[:, :s1], wl[:, s4:s5], wl[:, s5:],
                                  wl[:, s2 + inner:s2 + inner + bc], wl[:, s2 + inner + bc:s3]], axis=1).astype(BF16)
        w_dt = jnp.pad(wl[:, s3:s4].astype(F32), ((0, 0), (0, LANES - heads)))

        mod3 = _ada(c, w_ada[layer], b_ada[layer]).reshape(bsz, N_MOD, d)
        proj, dt_raw = _inproj(h, mod3, w_main, w_dt, seq)
        ypool = _pool(proj, blk["p"], pool_w[layer].astype(BF16), pool_scale[layer], seq, d)
        yssd = _ssd(proj, dt_raw, blk, conv_w[layer], conv_b[layer], dt_bias[layer], a_log[layer], d_skip[layer],
                    ssd_norm_w[layer], seq)
        merged = _merge(ypool, yssd, proj, gp_off, gs_off, w_branch_pool[layer].astype(BF16),
                        w_branch_ssd[layer].astype(BF16))
        h1, u2, idx_kt, gate_kt, rank_kt, counts = _outproj(merged, w_out[layer].astype(BF16), h, mod3,
                                                            w_router[layer], b_router[layer], seq)

        n_blocks = (t * TOP_K) // MOE_BM + ne
        ft = w_down.shape[2] // min(MOE_TF, w_down.shape[2])
        pstart, tables = _moe_tables(counts[:, 0], n_blocks, ft)
        eids = jnp.arange(ne, dtype=jnp.int32)[:, None, None]
        slot_kt = (jnp.sum(jnp.where(idx_kt[:TOP_K][None] == eids, pstart[:, None, None], 0), axis=0)
                   + rank_kt[:TOP_K]).astype(jnp.int32)

        xs = _dispatch(u2, slot_kt.T, n_blocks * MOE_BM)
        y_slots = _moe(xs, tables, w_gate_up[layer].astype(F32), b_gate_up[layer].astype(F32)[:, None, :],
                       w_down[layer].astype(F32), b_down[layer].astype(F32)[:, None, :])
        h = _final(h1, gate_kt.T, slot_kt, mod3, final_norm_w, y_slots, seq, x.dtype)
    return h.reshape(bsz, seq, d)
```

```python
import functools

import jax
import jax.numpy as jnp
from jax import lax
from jax.experimental import pallas as pl
from jax.experimental.pallas import tpu as pltpu

F32 = jnp.float32
BF16 = jnp.bfloat16
HIGHEST = lax.Precision.HIGHEST

EPS = 1e-6
POOL_WINDOWS = (2, 4, 8, 16)
SSD_GROUPS = 8
SSD_STATE = 128
SSD_CONV = 4
SSD_HEAD_DIM = 64
TOP_K = 4
SWIGLU_LIMIT = 7.0
SWIGLU_ALPHA = 1.702
N_MOD = 6

LOG2_E = 1.4426950408889634
LANES = 128
SUBLANES = 8
CONV_HALO = 8
V7X_VMEM_LIMIT = 56 * 1024 * 1024

ADA_TN = 1024
SEQ_TILE = 256
INPROJ_TM, INPROJ_TN = 1024, 2048
MERGE_TM, MERGE_TN = 1024, 512
OUT_TM = 512
MOE_BM, MOE_TF = 512, 512
MOE_CHUNK_BLOCKS = 2
DISPATCH_TS = 512
FINAL_TC = 512
FINAL_ISSUE_UNROLL = 8


def _params(*sem):
    return pltpu.CompilerParams(dimension_semantics=sem, vmem_limit_bytes=V7X_VMEM_LIMIT)


def _silu(v):
    return v * jax.nn.sigmoid(v)


def _pack_bf16_pairs(v):
    half = v.shape[1] // 2
    lo = lax.bitcast_convert_type(v[:, :half].astype(BF16).astype(F32), jnp.uint32)
    hi = lax.bitcast_convert_type(v[:, half:].astype(BF16).astype(F32), jnp.uint32)
    return (hi & jnp.uint32(0xFFFF0000)) | (lo >> 16)


def _token_tile(d):
    return (d // 2 // LANES, LANES)


def _rows_to_tiles(w):
    return w.reshape(w.shape[0], w.shape[1] // LANES, LANES)


def _tiles_to_rows(w):
    return w.reshape(w.shape[0], w.shape[1] * LANES)


def _unpack_bf16_pairs(w):
    lo = lax.bitcast_convert_type(w << 16, F32).astype(BF16)
    hi = lax.bitcast_convert_type(w & jnp.uint32(0xFFFF0000), F32).astype(BF16)
    return lo, hi


def _ada_kernel(cb_ref, w_ref, b_ref, o_ref):
    nb, tn = cb_ref.shape[0], w_ref.shape[1]
    for b in range(nb):
        ca = _silu(cb_ref[b])
        cols = [jnp.sum(w_ref[:, j * LANES:(j + 1) * LANES] * ca, axis=0, keepdims=True)
                for j in range(tn // LANES)]
        o_ref[b:b + 1, :] = jnp.concatenate(cols, axis=1) + b_ref[...]


def _ada(c, w_ada, b_ada):
    nb, k = c.shape
    n = w_ada.shape[1]
    tn = ADA_TN if n % ADA_TN == 0 else n
    cb = jnp.broadcast_to(c.astype(F32)[:, :, None], (nb, k, LANES))
    return pl.pallas_call(
        _ada_kernel,
        grid=(n // tn,),
        in_specs=[pl.BlockSpec((nb, k, LANES), lambda j: (0, 0, 0)),
                  pl.BlockSpec((k, tn), lambda j: (0, j)),
                  pl.BlockSpec((1, tn), lambda j: (0, j))],
        out_specs=pl.BlockSpec((nb, tn), lambda j: (0, j)),
        out_shape=jax.ShapeDtypeStruct((nb, n), F32),
        compiler_params=_params("arbitrary"),
        name="ada",
    )(cb, w_ada, b_ada.reshape(1, n))


def _inproj_kernel(x_ref, mod_ref, w_ref, wdt_hi_ref, wdt_lo_ref, o_ref, dt_ref, u_s, *, rows_per_chunk):
    j = pl.program_id(1)
    tm = x_ref.shape[0]

    @pl.when(j == 0)
    def _():
        sh = mod_ref[0, 0:1, :]
        sc = mod_ref[0, 1:2, :]

        def body(r, carry):
            rows = pl.ds(pl.multiple_of(r * rows_per_chunk, rows_per_chunk), rows_per_chunk)
            xv = x_ref[rows, :]
            ms = jnp.mean(xv * xv, axis=-1, keepdims=True)
            u = xv * lax.rsqrt(ms + EPS) * (1.0 + sc) + sh
            u_hi = u.astype(BF16)
            u_s[rows, :] = u_hi
            u_lo = (u - u_hi.astype(F32)).astype(BF16)
            dt_ref[rows, :] = (jnp.dot(u_hi, wdt_hi_ref[...], preferred_element_type=F32)
                               + (jnp.dot(u_hi, wdt_lo_ref[...], preferred_element_type=F32)
                                  + jnp.dot(u_lo, wdt_hi_ref[...], preferred_element_type=F32)))
            return carry

        lax.fori_loop(0, tm // rows_per_chunk, body, 0)

    o_ref[...] = jnp.dot(u_s[...], w_ref[...], preferred_element_type=F32).astype(o_ref.dtype)


def _inproj(x2, mod3, w_main, w_dt, seq):
    t, d = x2.shape
    w_dt_hi = w_dt.astype(BF16)
    w_dt_lo = (w_dt - w_dt_hi.astype(F32)).astype(BF16)
    n = w_main.shape[1]
    tm = min(INPROJ_TM, seq)
    tn = INPROJ_TN
    tiles_per_seq = seq // tm
    return pl.pallas_call(
        functools.partial(_inproj_kernel, rows_per_chunk=min(128, tm)),
        grid=(t // tm, n // tn),
        in_specs=[pl.BlockSpec((tm, d), lambda i, j: (i, 0)),
                  pl.BlockSpec((1, N_MOD, d), lambda i, j: (i // tiles_per_seq, 0, 0)),
                  pl.BlockSpec((d, tn), lambda i, j: (0, j)),
                  pl.BlockSpec((d, LANES), lambda i, j: (0, 0)),
                  pl.BlockSpec((d, LANES), lambda i, j: (0, 0))],
        out_specs=[pl.BlockSpec((tm, tn), lambda i, j: (i, j)),
                   pl.BlockSpec((tm, LANES), lambda i, j: (i, 0))],
        out_shape=[jax.ShapeDtypeStruct((t, n), BF16), jax.ShapeDtypeStruct((t, LANES), F32)],
        scratch_shapes=[pltpu.VMEM((tm, d), BF16)],
        compiler_params=_params("arbitrary", "arbitrary"),
        name="inproj",
    )(x2, mod3, w_main, w_dt_hi, w_dt_lo)


def _pool_kernel(p_ref, pw_ref, ps_ref, o_ref, prev_s, *, tiles_per_seq):
    i = pl.program_id(0)
    tl, d = p_ref.shape
    gd = d // len(POOL_WINDOWS)
    it = i % tiles_per_seq

    @pl.when(it == 0)
    def _():
        prev_s[...] = jnp.zeros_like(prev_s)

    row = lax.broadcasted_iota(jnp.int32, (tl, 2 * tl), 0)
    col = lax.broadcasted_iota(jnp.int32, (tl, 2 * tl), 1)
    pos = (it * tl + lax.broadcasted_iota(jnp.int32, (tl, 1), 0) + 1).astype(F32)
    for g, w in enumerate(POOL_WINDOWS):
        sl = slice(g * gd, (g + 1) * gd)
        cur = p_ref[:, sl]
        ext = jnp.concatenate([prev_s[:, sl], cur], axis=0)
        band = ((col <= row + tl) & (col > row + tl - w)).astype(BF16)
        win_sum = jnp.dot(band, ext, preferred_element_type=F32)
        mean = win_sum / jnp.minimum(pos, float(w))
        dlt = (mean - cur.astype(F32)).astype(BF16)
        mixed = jnp.dot(dlt, pw_ref[g], preferred_element_type=F32)
        o_ref[:, sl] = (mixed * ps_ref[:, sl]).astype(o_ref.dtype)
    prev_s[...] = p_ref[...]


def _pool(proj, p_blk, pool_w_bf, pool_scale, seq, d):
    t = proj.shape[0]
    tl = SEQ_TILE
    g, gd = pool_w_bf.shape[0], pool_w_bf.shape[1]
    return pl.pallas_call(
        functools.partial(_pool_kernel, tiles_per_seq=seq // tl),
        grid=(t // tl,),
        in_specs=[pl.BlockSpec((tl, d), lambda i: (i, p_blk)),
                  pl.BlockSpec((g, gd, gd), lambda i: (0, 0, 0)),
                  pl.BlockSpec((1, d), lambda i: (0, 0))],
        out_specs=pl.BlockSpec((tl, d), lambda i: (i, 0)),
        out_shape=jax.ShapeDtypeStruct((t, d), BF16),
        scratch_shapes=[pltpu.VMEM((tl, d), BF16)],
        compiler_params=_params("arbitrary"),
        name="pool",
    )(proj, pool_w_bf, pool_scale.reshape(1, d))


def _ssd_kernel(z_ref, xs_ref, bm_ref, cm_ref, dt_ref,
                cwx_ref, cwb_ref, cwc_ref, cbx_ref, cbb_ref, cbc_ref,
                dtb_ref, alog_ref, dsk_ref, nw_ref, exp_ref,
                o_ref,
                extx_s, extb_s, extc_s, state_s, xdt_s, y_s, cb_s, cs_s, cst_s, xc_s, bc_s, cc_s,
                *, tiles_per_seq):
    i = pl.program_id(0)
    tl, inner = xs_ref.shape
    gw = inner // SSD_GROUPS
    pairs_per_group = gw // LANES
    n_pairs = inner // LANES

    @pl.when(i % tiles_per_seq == 0)
    def _():
        extx_s[0:CONV_HALO, :] = jnp.zeros((CONV_HALO, inner), F32)
        extb_s[0:CONV_HALO, :] = jnp.zeros((CONV_HALO, extb_s.shape[1]), F32)
        extc_s[0:CONV_HALO, :] = jnp.zeros((CONV_HALO, extc_s.shape[1]), F32)
        state_s[...] = jnp.zeros_like(state_s)

    def conv_silu(ext_ref, src_ref, w_ref, b_ref, dst_ref, width, cw):
        for c0 in range(0, width, cw):
            cs = slice(c0, c0 + cw)
            ext_ref[CONV_HALO:CONV_HALO + tl, cs] = src_ref[:, cs].astype(F32)
            acc = b_ref[:, cs] + w_ref[SSD_CONV - 1:SSD_CONV, cs] * ext_ref[CONV_HALO:CONV_HALO + tl, cs]
            for k in range(1, SSD_CONV):
                acc = acc + w_ref[SSD_CONV - 1 - k:SSD_CONV - k, cs] * ext_ref[pl.ds(CONV_HALO - k, tl), cs]
            dst_ref[:, cs] = _silu(acc)
            ext_ref[0:CONV_HALO, cs] = ext_ref[tl:tl + CONV_HALO, cs]

    conv_silu(extx_s, xs_ref, cwx_ref, cbx_ref, xc_s, inner, LANES)
    conv_silu(extb_s, bm_ref, cwb_ref, cbb_ref, bc_s, bm_ref.shape[1], LANES)
    conv_silu(extc_s, cm_ref, cwc_ref, cbc_ref, cc_s, cm_ref.shape[1], LANES)

    dtv = dt_ref[...] + dtb_ref[...]
    dt = jnp.maximum(dtv, 0.0) + jnp.log1p(jnp.exp(-jnp.abs(dtv)))
    a = -jnp.exp(alog_ref[...])
    da = dt * a
    ri = lax.broadcasted_iota(jnp.int32, (tl, tl), 0)
    ci = lax.broadcasted_iota(jnp.int32, (tl, tl), 1)
    causal = ri >= ci
    cs = jnp.dot(causal.astype(F32), da, precision=HIGHEST, preferred_element_type=F32)
    cs2 = cs * LOG2_E
    cs_s[...] = cs2
    cst_s[...] = cs2.T
    last = cs[tl - 1:tl, :]
    dt_bf = dt.astype(BF16)
    ecs_bf = jnp.exp(cs).astype(BF16)
    dte_bf = jnp.exp(last - cs).astype(BF16)

    for g in range(SSD_GROUPS):
        gs = slice(g * gw, (g + 1) * gw)
        ns = slice(g * SSD_STATE, (g + 1) * SSD_STATE)
        exp_g = exp_ref[:, gs]
        dt_x = jnp.dot(dt_bf, exp_g, preferred_element_type=F32)
        ecs_x = jnp.dot(ecs_bf, exp_g, preferred_element_type=F32)
        dte_x = jnp.dot(dte_bf, exp_g, preferred_element_type=F32)
        xg = xc_s[:, gs]
        xdt = xg * dt_x
        xdt_bf = xdt.astype(BF16)
        xd_bf = (xdt * dte_x).astype(BF16)
        bg = bc_s[:, ns]
        cg = cc_s[:, ns].astype(BF16)
        cb_s[g] = lax.dot_general(cg, bg.astype(BF16), (((1,), (1,)), ((), ())), preferred_element_type=F32)
        s_old = state_s[g]
        y_off = jnp.dot(cg, s_old.astype(BF16), preferred_element_type=F32) * ecs_x
        state_s[g] = (s_old * ecs_x[tl - 1:tl, :]
                      + jnp.dot(bg.T.astype(BF16), xd_bf, preferred_element_type=F32))
        y0 = y_off + dsk_ref[:, gs] * xg
        for q in range(pairs_per_group):
            qs = slice(q * LANES, (q + 1) * LANES)
            xdt_s[g * pairs_per_group + q] = xdt_bf[:, qs]
            y_s[g * pairs_per_group + q] = y0[:, qs]

    half = tl // 2
    causal_top = (lax.broadcasted_iota(jnp.int32, (half, half), 0)
                  >= lax.broadcasted_iota(jnp.int32, (half, half), 1))
    causal_bot = (lax.broadcasted_iota(jnp.int32, (half, tl), 0) + half
                  >= lax.broadcasted_iota(jnp.int32, (half, tl), 1))
    first_head_lanes = lax.broadcasted_iota(jnp.int32, (half, LANES), 1) < SSD_HEAD_DIM

    def pair_body(hp, carry):
        g = hp // pairs_per_group
        cb_top = cb_s[g, 0:half, 0:half]
        cb_bot = cb_s[g, half:tl, :]
        x_top = xdt_s[hp, 0:half, :]
        x_all = xdt_s[hp]
        cs_top = cs_s[0:half, :]
        cs_bot = cs_s[half:tl, :]
        tops, bots = [], []
        for e in range(2):
            h = 2 * hp + e
            head_lane = jnp.full((half, LANES), h, jnp.int32)
            col_top = jnp.take_along_axis(cs_top, head_lane, axis=1)
            col_bot = jnp.take_along_axis(cs_bot, head_lane, axis=1)
            col_bot = jnp.concatenate([col_bot] * (tl // LANES), axis=1)
            row_all = cst_s[pl.ds(h, 1), :]
            row_top = row_all[:, 0:half]
            m_top = jnp.exp2(jnp.where(causal_top, col_top - row_top, -jnp.inf)) * cb_top
            m_bot = jnp.exp2(jnp.where(causal_bot, col_bot - row_all, -jnp.inf)) * cb_bot
            tops.append(jnp.dot(m_top.astype(BF16), x_top, preferred_element_type=F32))
            bots.append(jnp.dot(m_bot.astype(BF16), x_all, preferred_element_type=F32))
        y_s[hp, 0:half, :] = y_s[hp, 0:half, :] + jnp.where(first_head_lanes, tops[0], tops[1])
        y_s[hp, half:tl, :] = y_s[hp, half:tl, :] + jnp.where(first_head_lanes, bots[0], bots[1])
        return carry

    lax.fori_loop(0, n_pairs, pair_body, 0, unroll=16)

    for g in range(SSD_GROUPS):
        gs = slice(g * gw, (g + 1) * gw)
        yg = jnp.concatenate([y_s[g * pairs_per_group + q] for q in range(pairs_per_group)], axis=1)
        yg = yg * _silu(z_ref[:, gs].astype(F32))
        ms = jnp.mean(yg * yg, axis=-1, keepdims=True)
        o_ref[:, gs] = (yg * lax.rsqrt(ms + EPS) * nw_ref[:, gs]).astype(o_ref.dtype)


def _ssd(proj, dt_raw, blk, conv_w, conv_b, dt_bias, a_log, d_skip, ssd_norm_w, seq):
    t = proj.shape[0]
    tl = SEQ_TILE
    heads = dt_bias.shape[0]
    inner = ssd_norm_w.shape[0]
    bc = SSD_GROUPS * SSD_STATE
    gw = inner // SSD_GROUPS
    pad = LANES - heads
    row = lambda v: v.astype(F32).reshape(1, -1)
    dtb = jnp.pad(row(dt_bias), ((0, 0), (0, pad)))
    alog = jnp.pad(row(a_log), ((0, 0), (0, pad)))
    dsk = jnp.repeat(d_skip.astype(F32), SSD_HEAD_DIM).reshape(1, inner)
    expand = (lax.broadcasted_iota(jnp.int32, (LANES, inner), 1) // SSD_HEAD_DIM
              == lax.broadcasted_iota(jnp.int32, (LANES, inner), 0)).astype(BF16)
    cw = conv_w.astype(F32)
    cbias = row(conv_b)
    full = lambda shape: pl.BlockSpec(shape, lambda i: tuple(0 for _ in shape))
    return pl.pallas_call(
        functools.partial(_ssd_kernel, tiles_per_seq=seq // tl),
        grid=(t // tl,),
        in_specs=[pl.BlockSpec((tl, inner), lambda i: (i, blk["z"])),
                  pl.BlockSpec((tl, inner), lambda i: (i, blk["xs"])),
                  pl.BlockSpec((tl, bc), lambda i: (i, blk["B"])),
                  pl.BlockSpec((tl, bc), lambda i: (i, blk["C"])),
                  pl.BlockSpec((tl, LANES), lambda i: (i, 0)),
                  full((SSD_CONV, inner)), full((SSD_CONV, bc)), full((SSD_CONV, bc)),
                  full((1, inner)), full((1, bc)), full((1, bc)),
                  full((1, LANES)), full((1, LANES)), full((1, inner)), full((1, inner)),
                  full((LANES, inner))],
        out_specs=pl.BlockSpec((tl, inner), lambda i: (i, 0)),
        out_shape=jax.ShapeDtypeStruct((t, inner), BF16),
        scratch_shapes=[pltpu.VMEM((tl + CONV_HALO, inner), F32),
                        pltpu.VMEM((tl + CONV_HALO, bc), F32),
                        pltpu.VMEM((tl + CONV_HALO, bc), F32),
                        pltpu.VMEM((SSD_GROUPS, SSD_STATE, gw), F32),
                        pltpu.VMEM((inner // LANES, tl, LANES), BF16),
                        pltpu.VMEM((inner // LANES, tl, LANES), F32),
                        pltpu.VMEM((SSD_GROUPS, tl, tl), F32),
                        pltpu.VMEM((tl, LANES), F32),
                        pltpu.VMEM((LANES, tl), F32),
                        pltpu.VMEM((tl, inner), F32),
                        pltpu.VMEM((tl, bc), F32),
                        pltpu.VMEM((tl, bc), F32)],
        compiler_params=_params("arbitrary"),
        name="ssd",
    )(proj, proj, proj, proj, dt_raw,
      cw[:, :inner], cw[:, inner:inner + bc], cw[:, inner + bc:],
      cbias[:, :inner], cbias[:, inner:inner + bc], cbias[:, inner + bc:],
      dtb, alog, dsk, row(ssd_norm_w), expand)


def _merge_kernel(yp_ref, ys_ref, gp_ref, gs_ref, wp_ref, ws_ref, o_ref):
    a = jnp.dot(yp_ref[...], wp_ref[...], preferred_element_type=F32)
    b = jnp.dot(ys_ref[...], ws_ref[...], preferred_element_type=F32)
    o_ref[...] = (jax.nn.sigmoid(gp_ref[...].astype(F32)) * a
                  + jax.nn.sigmoid(gs_ref[...].astype(F32)) * b).astype(o_ref.dtype)


def _merge(ypool, yssd, proj, gp_off, gs_off, wbp, wbs):
    t, d = ypool.shape
    inner = yssd.shape[1]
    tm, tn = min(MERGE_TM, t), MERGE_TN
    return pl.pallas_call(
        _merge_kernel,
        grid=(t // tm, d // tn),
        in_specs=[pl.BlockSpec((tm, d), lambda i, j: (i, 0)),
                  pl.BlockSpec((tm, inner), lambda i, j: (i, 0)),
                  pl.BlockSpec((tm, tn), lambda i, j: (i, gp_off // tn + j)),
                  pl.BlockSpec((tm, tn), lambda i, j: (i, gs_off // tn + j)),
                  pl.BlockSpec((d, tn), lambda i, j: (0, j)),
                  pl.BlockSpec((inner, tn), lambda i, j: (0, j))],
        out_specs=pl.BlockSpec((tm, tn), lambda i, j: (i, j)),
        out_shape=jax.ShapeDtypeStruct((t, d), BF16),
        compiler_params=_params("arbitrary", "arbitrary"),
        name="merge",
    )(ypool, yssd, proj, proj, wbp, wbs)


def _out_kernel(m_ref, wo_ref, x_ref, mod_ref, wr_ref, br_ref,
                h_ref, u_ref, idx_ref, gate_ref, rank_ref, cnt_ref, carry_s):
    i = pl.program_id(0)
    tm = m_ref.shape[0]
    ne = wr_ref.shape[0]

    @pl.when(i == 0)
    def _():
        carry_s[...] = jnp.zeros_like(carry_s)

    mix = jnp.dot(m_ref[...], wo_ref[...], preferred_element_type=F32)
    h1 = x_ref[...] + mod_ref[0, 2:3, :] * mix
    h_ref[...] = h1
    ms = jnp.mean(h1 * h1, axis=-1, keepdims=True)
    u = h1 * lax.rsqrt(ms + EPS) * (1.0 + mod_ref[0, 4:5, :]) + mod_ref[0, 3:4, :]
    u_ref[...] = _rows_to_tiles(_pack_bf16_pairs(u))
    logits = lax.dot_general(wr_ref[...], u, (((1,), (1,)), ((), ())), precision=HIGHEST,
                             preferred_element_type=F32) + br_ref[...]
    eidx = lax.broadcasted_iota(jnp.int32, (ne, tm), 0)
    work = logits
    vals, idxs, hots = [], [], []
    for _ in range(TOP_K):
        mx = jnp.max(work, axis=0, keepdims=True)
        sel = jnp.min(jnp.where(work == mx, eidx, ne), axis=0, keepdims=True)
        hot = eidx == sel
        vals.append(mx)
        idxs.append(sel)
        hots.append(hot)
        work = jnp.where(hot, -jnp.inf, work)
    exps = [jnp.exp(v - vals[0]) for v in vals]
    den = exps[0]
    for e in exps[1:]:
        den = den + e
    cnt = hots[0].astype(F32)
    for hot in hots[1:]:
        cnt = cnt + hot.astype(F32)
    ti = lax.broadcasted_iota(jnp.int32, (tm, tm), 0)
    tj = lax.broadcasted_iota(jnp.int32, (tm, tm), 1)
    before = (ti < tj).astype(BF16)
    prefix = jnp.dot(cnt.astype(BF16), before, preferred_element_type=F32)
    base = carry_s[:, 0:1] + prefix
    pad_rows = idx_ref.shape[0] - TOP_K
    ranks = [jnp.sum(jnp.where(hot, base, 0.0), axis=0, keepdims=True) for hot in hots]
    idx_ref[...] = jnp.concatenate(idxs + [jnp.zeros((pad_rows, tm), jnp.int32)], axis=0)
    gate_ref[...] = jnp.concatenate([e / den for e in exps] + [jnp.zeros((pad_rows, tm), F32)], axis=0)
    rank_ref[...] = jnp.concatenate([r.astype(jnp.int32) for r in ranks]
                                    + [jnp.zeros((pad_rows, tm), jnp.int32)], axis=0)
    carry_s[...] = carry_s[...] + jnp.sum(cnt, axis=1, keepdims=True)
    cnt_ref[...] = carry_s[...].astype(jnp.int32)


def _outproj(merged, wo, x2, mod3, w_router, b_router, seq):
    t, d = x2.shape
    ne = w_router.shape[1]
    tm = min(OUT_TM, seq)
    tiles_per_seq = seq // tm
    rows = 8
    return pl.pallas_call(
        _out_kernel,
        grid=(t // tm,),
        in_specs=[pl.BlockSpec((tm, d), lambda i: (i, 0)),
                  pl.BlockSpec((d, d), lambda i: (0, 0), pipeline_mode=pl.Buffered(1)),
                  pl.BlockSpec((tm, d), lambda i: (i, 0)),
                  pl.BlockSpec((1, N_MOD, d), lambda i: (i // tiles_per_seq, 0, 0)),
                  pl.BlockSpec((ne, d), lambda i: (0, 0)),
                  pl.BlockSpec((ne, 1), lambda i: (0, 0))],
        out_specs=[pl.BlockSpec((tm, d), lambda i: (i, 0)),
                   pl.BlockSpec((tm,) + _token_tile(d), lambda i: (i, 0, 0)),
                   pl.BlockSpec((rows, tm), lambda i: (0, i)),
                   pl.BlockSpec((rows, tm), lambda i: (0, i)),
                   pl.BlockSpec((rows, tm), lambda i: (0, i)),
                   pl.BlockSpec((ne, LANES), lambda i: (0, 0))],
        out_shape=[jax.ShapeDtypeStruct((t, d), F32), jax.ShapeDtypeStruct((t,) + _token_tile(d), jnp.uint32),
                   jax.ShapeDtypeStruct((rows, t), jnp.int32), jax.ShapeDtypeStruct((rows, t), F32),
                   jax.ShapeDtypeStruct((rows, t), jnp.int32), jax.ShapeDtypeStruct((ne, LANES), jnp.int32)],
        scratch_shapes=[pltpu.VMEM((ne, LANES), F32)],
        compiler_params=_params("arbitrary"),
        name="outproj",
    )(merged, wo, x2, mod3, w_router.astype(F32).T, b_router.astype(F32).reshape(ne, 1))


def _row_copy(src_hbm, src_row, dst_ref, dst_row, sem):
    return pltpu.make_async_copy(src_hbm.at[pl.ds(src_row, 1)], dst_ref.at[pl.ds(dst_row, 1)], sem)


def _dispatch_kernel(slot_ref, u_ref, init_hbm, xs_hbm, sem):
    del init_hbm
    ts = u_ref.shape[0]

    def issue(tok, carry):
        for k in range(TOP_K):
            _row_copy(u_ref, tok, xs_hbm, slot_ref[0, 0, tok * TOP_K + k], sem).start(priority=k % 2)
        return carry

    lax.fori_loop(0, ts, issue, 0)
    rows = pl.ds(0, ts * TOP_K)
    pltpu.make_async_copy(xs_hbm.at[rows], xs_hbm.at[rows], sem).wait()


def _dispatch(u2, slot_tk, n_slots):
    t, tile = u2.shape[0], u2.shape[1:]
    ts = min(DISPATCH_TS, t)
    slots3 = slot_tk.reshape(t // ts, 1, ts * TOP_K)
    init = jnp.zeros((n_slots,) + tile, u2.dtype)
    return pl.pallas_call(
        _dispatch_kernel,
        grid=(t // ts,),
        in_specs=[pl.BlockSpec((1, 1, ts * TOP_K), lambda i: (i, 0, 0), memory_space=pltpu.SMEM),
                  pl.BlockSpec((ts,) + tile, lambda i: (i, 0, 0)),
                  pl.BlockSpec(memory_space=pl.ANY)],
        out_specs=pl.BlockSpec(memory_space=pl.ANY),
        out_shape=jax.ShapeDtypeStruct((n_slots,) + tile, u2.dtype),
        scratch_shapes=[pltpu.SemaphoreType.DMA(())],
        input_output_aliases={2: 0},
        compiler_params=_params("arbitrary"),
        name="dispatch",
    )(slots3, u2, init)


ITEM_IDLE, ITEM_ACTIVE, ITEM_ZERO_FILL, ITEM_ACTIVE_HALF = 0, 1, 2, 3


def _moe_kernel(tb_ref, tj_ref, te_ref, tr_ref, tflag_ref, tout_ref, tslot_ref, tne_ref, tnj_ref, thn_ref,
                x_ref, wgu_hbm, bg_ref, bu_ref, wd_hbm, bd_ref, o_ref,
                wg_s, wu_s, wd_s, xb_s, acc_s, wg_buf, wu_buf, wd_buf, sem, *, last_j):
    del tb_ref, tout_ref
    i = pl.program_id(0)
    flag = tflag_ref[i]
    r = tr_ref[i]
    j = tj_ref[i]
    half = x_ref.shape[1] * x_ref.shape[2]
    tf = wg_s.shape[1]
    up_col0 = wgu_hbm.shape[2] // 2

    def weight_copies(e, jt, slot):
        c0 = pl.multiple_of(jt * tf, tf)
        return (pltpu.make_async_copy(wgu_hbm.at[e, :, pl.ds(c0, tf)], wg_buf.at[slot], sem.at[slot]),
                pltpu.make_async_copy(wgu_hbm.at[e, :, pl.ds(up_col0 + c0, tf)], wu_buf.at[slot], sem.at[slot]),
                pltpu.make_async_copy(wd_hbm.at[e, pl.ds(c0, tf), :], wd_buf.at[slot], sem.at[slot]))

    is_active = (flag & 1) == 1

    @pl.when(i == 0)
    def _():
        acc_s[...] = jnp.zeros_like(acc_s)

    @pl.when(is_active & (r == 0))
    def _():
        slot = tslot_ref[i]

        @pl.when(i == 0)
        def _():
            for cp in weight_copies(te_ref[i], j, slot):
                cp.start()

        for cp in weight_copies(te_ref[i], j, slot):
            cp.wait()
        for s in range(2):
            @pl.when(slot == s)
            def _():
                wg_s[...] = wg_buf[s].astype(BF16)
                wu_s[...] = wu_buf[s].astype(BF16)
                wd_s[...] = wd_buf[s].astype(BF16)

        @pl.when(thn_ref[i] == 1)
        def _():
            for cp in weight_copies(tne_ref[i], tnj_ref[i], 1 - slot):
                cp.start()

    @pl.when(is_active & (j == 0))
    def _():
        lo, hi = _unpack_bf16_pairs(_tiles_to_rows(x_ref[...]))
        xb_s[r, :, :half] = lo
        xb_s[r, :, half:] = hi

    def expert_mlp(rows):
        xb = xb_s[r, 0:rows, :]
        gate = jnp.dot(xb, wg_s[...], preferred_element_type=F32) + bg_ref[...]
        up = jnp.dot(xb, wu_s[...], preferred_element_type=F32) + bu_ref[...]
        gate = jnp.minimum(gate, SWIGLU_LIMIT)
        up = jnp.clip(up, -SWIGLU_LIMIT, SWIGLU_LIMIT)
        act = (up + 1.0) * gate * jax.nn.sigmoid(SWIGLU_ALPHA * gate)
        part = jnp.dot(act.astype(BF16), wd_s[...], preferred_element_type=F32)
        base = jnp.where(j == 0, jnp.broadcast_to(bd_ref[...], (rows, bd_ref.shape[1])), acc_s[r, 0:rows, :])
        acc_s[r, 0:rows, :] = base + part

        @pl.when(j == last_j)
        def _():
            o_ref[0:rows] = _rows_to_tiles(_pack_bf16_pairs(acc_s[r, 0:rows, :]))
            if rows < o_ref.shape[0]:
                o_ref[rows:] = jnp.zeros((o_ref.shape[0] - rows,) + o_ref.shape[1:], o_ref.dtype)

    @pl.when(flag == ITEM_ACTIVE)
    def _():
        expert_mlp(o_ref.shape[0])

    @pl.when(flag == ITEM_ACTIVE_HALF)
    def _():
        expert_mlp(o_ref.shape[0] // 2)

    @pl.when(flag == ITEM_ZERO_FILL)
    def _():
        o_ref[...] = jnp.zeros_like(o_ref)


def _lookup(table, idx):
    hot = idx[:, None] == jnp.arange(table.shape[0], dtype=jnp.int32)[None, :]
    return jnp.sum(jnp.where(hot, table[None, :].astype(jnp.int32), 0), axis=1).astype(jnp.int32)


def _moe_tables(counts, n_blocks, ft):
    ne = counts.shape[0]
    padded = (counts + MOE_BM - 1) // MOE_BM * MOE_BM
    pend = jnp.cumsum(padded)
    pstart = pend - padded
    blocks_per_expert = padded // MOE_BM
    first_block = pstart // MOE_BM
    n_active = (pend[-1] // MOE_BM).astype(jnp.int32)

    blocks = jnp.arange(n_blocks, dtype=jnp.int32)
    block_expert = jnp.minimum(jnp.sum(pend[None, :] <= (blocks * MOE_BM)[:, None], axis=1), ne - 1).astype(jnp.int32)
    off = blocks - _lookup(first_block, block_expert)
    is_start = (off % MOE_CHUNK_BLOCKS == 0) & (blocks < n_active)
    chunk_size = jnp.minimum(MOE_CHUNK_BLOCKS,
                             _lookup(blocks_per_expert, block_expert) - off // MOE_CHUNK_BLOCKS * MOE_CHUNK_BLOCKS)

    items = jnp.arange(n_blocks * ft, dtype=jnp.int32)
    active = items < ft * n_active
    src = jnp.minimum(items, jnp.maximum(ft * n_active - 1, 0))
    cand = jnp.where(is_start[None, :] & (ft * blocks[None, :] <= src[:, None]), blocks[None, :], 0)
    cstart = jnp.max(cand, axis=1).astype(jnp.int32)
    m = jnp.maximum(_lookup(chunk_size, cstart), 1)
    local = src - ft * cstart
    tj = local // m
    tr = local % m
    tb = cstart + tr
    te = _lookup(block_expert, cstart)
    k = items - ft * n_active
    zero_fill = (~active) & (n_active + k < n_blocks)
    tout = jnp.where(active, jnp.where(tj == ft - 1, tb, cstart), jnp.minimum(n_active + k, n_blocks - 1))
    valid_rows = _lookup(counts, te) - (tb - _lookup(first_block, te)) * MOE_BM
    kind = jnp.where(valid_rows <= MOE_BM // 2, ITEM_ACTIVE_HALF, ITEM_ACTIVE)
    tflag = jnp.where(active, kind, jnp.where(zero_fill, ITEM_ZERO_FILL, ITEM_IDLE))
    group_start = active & (tr == 0)
    tslot = (jnp.cumsum(group_start.astype(jnp.int32)) - 1) % 2
    nxt = items + m
    has_next = group_start & (nxt < ft * n_active)
    nxt = jnp.minimum(nxt, n_blocks * ft - 1)
    tne, tnj = jnp.take(te, nxt), jnp.take(tj, nxt)
    as_i32 = lambda v: v.astype(jnp.int32)
    return pstart, tuple(map(as_i32, (tb, tj, te, tr, tflag, tout, tslot, tne, tnj, has_next)))


def _moe(xs, tables, wgu, bgu, wd, bd):
    n_slots = xs.shape[0]
    ne, d, f2 = wgu.shape
    f = f2 // 2
    bm, tf = MOE_BM, min(MOE_TF, f)
    nb, ft = n_slots // bm, f // tf
    grid_spec = pltpu.PrefetchScalarGridSpec(
        num_scalar_prefetch=len(tables),
        grid=(nb * ft,),
        in_specs=[pl.BlockSpec((bm,) + _token_tile(d), lambda i, tb, tj, te, *_: (tb[i], 0, 0)),
                  pl.BlockSpec(memory_space=pl.ANY),
                  pl.BlockSpec((None, 1, tf), lambda i, tb, tj, te, *_: (te[i], 0, tj[i])),
                  pl.BlockSpec((None, 1, tf), lambda i, tb, tj, te, *_: (te[i], 0, ft + tj[i])),
                  pl.BlockSpec(memory_space=pl.ANY),
                  pl.BlockSpec((None, 1, d), lambda i, tb, tj, te, *_: (te[i], 0, 0))],
        out_specs=pl.BlockSpec((bm,) + _token_tile(d), lambda i, tb, tj, te, tr, tg, to, *_: (to[i], 0, 0)),
        scratch_shapes=[pltpu.VMEM((d, tf), BF16), pltpu.VMEM((d, tf), BF16), pltpu.VMEM((tf, d), BF16),
                        pltpu.VMEM((MOE_CHUNK_BLOCKS, bm, d), BF16), pltpu.VMEM((MOE_CHUNK_BLOCKS, bm, d), F32),
                        pltpu.VMEM((2, d, tf), F32), pltpu.VMEM((2, d, tf), F32), pltpu.VMEM((2, tf, d), F32),
                        pltpu.SemaphoreType.DMA((2,))],
    )
    return pl.pallas_call(
        functools.partial(_moe_kernel, last_j=ft - 1),
        grid_spec=grid_spec,
        out_shape=jax.ShapeDtypeStruct((n_slots,) + _token_tile(d), jnp.uint32),
        compiler_params=_params("arbitrary"),
        name="moe",
    )(*tables, xs, wgu, bgu, bgu, wd, bd)


def _final_kernel(slot_ref, next_slot_ref, h_ref, gate_ref, mod_ref, fw_ref, y_hbm, o_ref, ybuf, sem):
    i = pl.program_id(0)
    tc, d = h_ref.shape
    half = d // 2
    n_rows = tc * TOP_K
    cur = i % 2

    def gather_rows(slots_ref, buf):
        def issue(grp, carry):
            for q in range(FINAL_ISSUE_UNROLL):
                tk = FINAL_ISSUE_UNROLL * grp + q
                _row_copy(y_hbm, slots_ref[0, 0, tk], ybuf.at[buf], tk, sem.at[buf]).start(priority=q % 2)
            return carry

        lax.fori_loop(0, n_rows // FINAL_ISSUE_UNROLL, issue, 0)

    @pl.when(i == 0)
    def _():
        gather_rows(slot_ref, 0)

    @pl.when(i + 1 < pl.num_programs(0))
    def _():
        gather_rows(next_slot_ref, 1 - cur)

    pltpu.make_async_copy(y_hbm.at[pl.ds(0, n_rows)], ybuf.at[cur], sem.at[cur]).wait()

    ffn_lo = jnp.zeros((tc, half), F32)
    ffn_hi = jnp.zeros((tc, half), F32)
    for k in range(TOP_K):
        w = _tiles_to_rows(ybuf[cur, k * tc:(k + 1) * tc])
        g = gate_ref[:, k:k + 1]
        ffn_lo = ffn_lo + g * lax.bitcast_convert_type(w << 16, F32)
        ffn_hi = ffn_hi + g * lax.bitcast_convert_type(w & jnp.uint32(0xFFFF0000), F32)
    h2_lo = h_ref[:, :half] + mod_ref[0, 5:6, :half] * ffn_lo
    h2_hi = h_ref[:, half:] + mod_ref[0, 5:6, half:] * ffn_hi
    ms = (jnp.sum(h2_lo * h2_lo, axis=-1, keepdims=True) + jnp.sum(h2_hi * h2_hi, axis=-1, keepdims=True)) / d
    inv = lax.rsqrt(ms + EPS)
    o_ref[:, :half] = (h2_lo * inv * fw_ref[:, :half]).astype(o_ref.dtype)
    o_ref[:, half:] = (h2_hi * inv * fw_ref[:, half:]).astype(o_ref.dtype)


def _final(h1, gates_tk, slot_kt, mod3, final_norm_w, y_slots, seq, out_dtype):
    t, d = h1.shape
    tc = min(FINAL_TC, seq)
    tiles_per_seq = seq // tc
    slots3 = slot_kt.reshape(TOP_K, t // tc, tc).transpose(1, 0, 2).reshape(t // tc, 1, TOP_K * tc)
    return pl.pallas_call(
        _final_kernel,
        grid=(t // tc,),
        in_specs=[pl.BlockSpec((1, 1, TOP_K * tc), lambda i: (i, 0, 0), memory_space=pltpu.SMEM),
                  pl.BlockSpec((1, 1, TOP_K * tc), lambda i: (jnp.minimum(i + 1, t // tc - 1), 0, 0),
                               memory_space=pltpu.SMEM),
                  pl.BlockSpec((tc, d), lambda i: (i, 0)),
                  pl.BlockSpec((tc, gates_tk.shape[1]), lambda i: (i, 0)),
                  pl.BlockSpec((1, N_MOD, d), lambda i: (i // tiles_per_seq, 0, 0)),
                  pl.BlockSpec((1, d), lambda i: (0, 0)),
                  pl.BlockSpec(memory_space=pl.ANY)],
        out_specs=pl.BlockSpec((tc, d), lambda i: (i, 0)),
        out_shape=jax.ShapeDtypeStruct((t, d), out_dtype),
        scratch_shapes=[pltpu.VMEM((2, TOP_K * tc) + _token_tile(d), jnp.uint32), pltpu.SemaphoreType.DMA((2,))],
        compiler_params=_params("arbitrary"),
        name="final",
    )(slots3, slots3, h1, gates_tk, mod3, final_norm_w.astype(F32).reshape(1, d), y_slots)


def kernel(x, c, w_ada, b_ada, w_in, pool_w, pool_scale, conv_w, conv_b, dt_bias, a_log, d_skip, ssd_norm_w,
           w_branch_pool, w_branch_ssd, w_out, w_router, b_router, w_gate_up, b_gate_up, w_down, b_down,
           final_norm_w):
    bsz, seq, d = x.shape
    depth = w_ada.shape[0]
    t = bsz * seq
    inner = ssd_norm_w.shape[1]
    heads = dt_bias.shape[1]
    bc = SSD_GROUPS * SSD_STATE
    ne = w_router.shape[2]
    assert depth == 1, "the final RMSNorm is fused into the last kernel of the single layer"
    assert SEQ_TILE == 2 * LANES
    assert seq % SEQ_TILE == 0 and heads <= LANES and (7 * d) % bc == 0 and (4 * d) % inner == 0
    assert inner // SSD_GROUPS % LANES == 0 and d % (len(POOL_WINDOWS) * LANES) == 0
    assert d % (2 * SUBLANES * LANES) == 0

    s1 = d
    s2 = s1 + inner
    s3 = s2 + inner + 2 * bc
    s4 = s3 + heads
    s5 = s4 + d
    blk = {"z": 0, "xs": 1, "p": (2 * inner) // d, "B": (2 * inner + 3 * d) // bc, "C": (2 * inner + 3 * d) // bc + 1}
    gp_off, gs_off = 2 * inner + d, 2 * inner + 2 * d

    h = x.astype(F32).reshape(t, d)
    for layer in range(depth):
        wl = w_in[layer]
        w_main = jnp.concatenate([wl[:, s1:s2], wl[:, s2:s2 + inner], wl[:, :s1], wl[:, s4:s5], wl[:, s5:],
                                  wl[:, s2 + inner:s2 + inner + bc], wl[:, s2 + inner + bc:s3]], axis=1).astype(BF16)
        w_dt = jnp.pad(wl[:, s3:s4].astype(F32), ((0, 0), (0, LANES - heads)))

        mod3 = _ada(c, w_ada[layer], b_ada[layer]).reshape(bsz, N_MOD, d)
        proj, dt_raw = _inproj(h, mod3, w_main, w_dt, seq)
        ypool = _pool(proj, blk["p"], pool_w[layer].astype(BF16), pool_scale[layer], seq, d)
        yssd = _ssd(proj, dt_raw, blk, conv_w[layer], conv_b[layer], dt_bias[layer], a_log[layer], d_skip[layer],
                    ssd_norm_w[layer], seq)
        merged = _merge(ypool, yssd, proj, gp_off, gs_off, w_branch_pool[layer].astype(BF16),
                        w_branch_ssd[layer].astype(BF16))
        h1, u2, idx_kt, gate_kt, rank_kt, counts = _outproj(merged, w_out[layer].astype(BF16), h, mod3,
                                                            w_router[layer], b_router[layer], seq)

        n_blocks = (t * TOP_K) // MOE_BM + ne
        ft = w_down.shape[2] // min(MOE_TF, w_down.shape[2])
        pstart, tables = _moe_tables(counts[:, 0], n_blocks, ft)
        eids = jnp.arange(ne, dtype=jnp.int32)[:, None, None]
        slot_kt = (jnp.sum(jnp.where(idx_kt[:TOP_K][None] == eids, pstart[:, None, None], 0), axis=0)
                   + rank_kt[:TOP_K]).astype(jnp.int32)

        xs = _dispatch(u2, slot_kt.T, n_blocks * MOE_BM)
        y_slots = _moe(xs, tables, w_gate_up[layer].astype(F32), b_gate_up[layer].astype(F32)[:, None, :],
                       w_down[layer].astype(F32), b_down[layer].astype(F32)[:, None, :])
        h = _final(h1, gate_kt.T, slot_kt, mod3, final_norm_w, y_slots, seq, x.dtype)
    return h.reshape(bsz, seq, d)
```

```python
import functools

import jax
import jax.numpy as jnp
from jax import lax
from jax.experimental import pallas as pl
from jax.experimental.pallas import tpu as pltpu

F32 = jnp.float32
BF16 = jnp.bfloat16
HIGHEST = lax.Precision.HIGHEST

EPS = 1e-6
POOL_WINDOWS = (2, 4, 8, 16)
SSD_GROUPS = 8
SSD_STATE = 128
SSD_CONV = 4
SSD_HEAD_DIM = 64
TOP_K = 4
SWIGLU_LIMIT = 7.0
SWIGLU_ALPHA = 1.702
N_MOD = 6

LOG2_E = 1.4426950408889634
LANES = 128
SUBLANES = 8
CONV_HALO = 8
V7X_VMEM_LIMIT = 56 * 1024 * 1024

ADA_TN = 1024
SEQ_TILE = 256
INPROJ_TM, INPROJ_TN = 1024, 2048
MERGE_TM, MERGE_TN = 1024, 512
OUT_TM = 512
MOE_BM, MOE_TF = 512, 512
MOE_CHUNK_BLOCKS = 2
DISPATCH_TS = 512
DISPATCH_TOKENS_PER_ITER = 4
FINAL_TC = 512
FINAL_ISSUE_UNROLL = 16


def _params(*sem):
    return pltpu.CompilerParams(dimension_semantics=sem, vmem_limit_bytes=V7X_VMEM_LIMIT)


def _silu(v):
    return v * jax.nn.sigmoid(v)


def _pack_bf16_pairs(v):
    half = v.shape[1] // 2
    lo = lax.bitcast_convert_type(v[:, :half].astype(BF16).astype(F32), jnp.uint32)
    hi = lax.bitcast_convert_type(v[:, half:].astype(BF16).astype(F32), jnp.uint32)
    return (hi & jnp.uint32(0xFFFF0000)) | (lo >> 16)


def _token_tile(d):
    return (d // 2 // LANES, LANES)


def _rows_to_tiles(w):
    return w.reshape(w.shape[0], w.shape[1] // LANES, LANES)


def _tiles_to_rows(w):
    return w.reshape(w.shape[0], w.shape[1] * LANES)


def _unpack_bf16_pairs(w):
    lo = lax.bitcast_convert_type(w << 16, F32).astype(BF16)
    hi = lax.bitcast_convert_type(w & jnp.uint32(0xFFFF0000), F32).astype(BF16)
    return lo, hi


def _ada_kernel(cb_ref, w_ref, b_ref, o_ref):
    nb, tn = cb_ref.shape[0], w_ref.shape[1]
    for b in range(nb):
        ca = _silu(cb_ref[b])
        cols = [jnp.sum(w_ref[:, j * LANES:(j + 1) * LANES] * ca, axis=0, keepdims=True)
                for j in range(tn // LANES)]
        o_ref[b:b + 1, :] = jnp.concatenate(cols, axis=1) + b_ref[...]


def _ada(c, w_ada, b_ada):
    nb, k = c.shape
    n = w_ada.shape[1]
    tn = ADA_TN if n % ADA_TN == 0 else n
    cb = jnp.broadcast_to(c.astype(F32)[:, :, None], (nb, k, LANES))
    return pl.pallas_call(
        _ada_kernel,
        grid=(n // tn,),
        in_specs=[pl.BlockSpec((nb, k, LANES), lambda j: (0, 0, 0)),
                  pl.BlockSpec((k, tn), lambda j: (0, j)),
                  pl.BlockSpec((1, tn), lambda j: (0, j))],
        out_specs=pl.BlockSpec((nb, tn), lambda j: (0, j)),
        out_shape=jax.ShapeDtypeStruct((nb, n), F32),
        compiler_params=_params("arbitrary"),
        name="ada",
    )(cb, w_ada, b_ada.reshape(1, n))


def _inproj_kernel(x_ref, mod_ref, w_ref, wdt_hi_ref, wdt_lo_ref, o_ref, dt_ref, u_s, *, rows_per_chunk):
    j = pl.program_id(1)
    tm = x_ref.shape[0]

    @pl.when(j == 0)
    def _():
        sh = mod_ref[0, 0:1, :]
        sc = mod_ref[0, 1:2, :]

        def body(r, carry):
            rows = pl.ds(pl.multiple_of(r * rows_per_chunk, rows_per_chunk), rows_per_chunk)
            xv = x_ref[rows, :]
            ms = jnp.mean(xv * xv, axis=-1, keepdims=True)
            u = xv * lax.rsqrt(ms + EPS) * (1.0 + sc) + sh
            u_hi = u.astype(BF16)
            u_s[rows, :] = u_hi
            u_lo = (u - u_hi.astype(F32)).astype(BF16)
            dt_ref[rows, :] = (jnp.dot(u_hi, wdt_hi_ref[...], preferred_element_type=F32)
                               + (jnp.dot(u_hi, wdt_lo_ref[...], preferred_element_type=F32)
                                  + jnp.dot(u_lo, wdt_hi_ref[...], preferred_element_type=F32)))
            return carry

        lax.fori_loop(0, tm // rows_per_chunk, body, 0)

    o_ref[...] = jnp.dot(u_s[...], w_ref[...], preferred_element_type=F32).astype(o_ref.dtype)


def _inproj(x2, mod3, w_main, w_dt, seq):
    t, d = x2.shape
    w_dt_hi = w_dt.astype(BF16)
    w_dt_lo = (w_dt - w_dt_hi.astype(F32)).astype(BF16)
    n = w_main.shape[1]
    tm = min(INPROJ_TM, seq)
    tn = INPROJ_TN
    tiles_per_seq = seq // tm
    return pl.pallas_call(
        functools.partial(_inproj_kernel, rows_per_chunk=min(128, tm)),
        grid=(t // tm, n // tn),
        in_specs=[pl.BlockSpec((tm, d), lambda i, j: (i, 0)),
                  pl.BlockSpec((1, N_MOD, d), lambda i, j: (i // tiles_per_seq, 0, 0)),
                  pl.BlockSpec((d, tn), lambda i, j: (0, j)),
                  pl.BlockSpec((d, LANES), lambda i, j: (0, 0)),
                  pl.BlockSpec((d, LANES), lambda i, j: (0, 0))],
        out_specs=[pl.BlockSpec((tm, tn), lambda i, j: (i, j)),
                   pl.BlockSpec((tm, LANES), lambda i, j: (i, 0))],
        out_shape=[jax.ShapeDtypeStruct((t, n), BF16), jax.ShapeDtypeStruct((t, LANES), F32)],
        scratch_shapes=[pltpu.VMEM((tm, d), BF16)],
        compiler_params=_params("arbitrary", "arbitrary"),
        name="inproj",
    )(x2, mod3, w_main, w_dt_hi, w_dt_lo)


def _pool_kernel(p_ref, pw_ref, ps_ref, o_ref, prev_s, *, tiles_per_seq):
    i = pl.program_id(0)
    tl, d = p_ref.shape
    gd = d // len(POOL_WINDOWS)
    it = i % tiles_per_seq

    @pl.when(it == 0)
    def _():
        prev_s[...] = jnp.zeros_like(prev_s)

    row = lax.broadcasted_iota(jnp.int32, (tl, 2 * tl), 0)
    col = lax.broadcasted_iota(jnp.int32, (tl, 2 * tl), 1)
    pos = (it * tl + lax.broadcasted_iota(jnp.int32, (tl, 1), 0) + 1).astype(F32)
    for g, w in enumerate(POOL_WINDOWS):
        sl = slice(g * gd, (g + 1) * gd)
        cur = p_ref[:, sl]
        ext = jnp.concatenate([prev_s[:, sl], cur], axis=0)
        band = ((col <= row + tl) & (col > row + tl - w)).astype(BF16)
        win_sum = jnp.dot(band, ext, preferred_element_type=F32)
        mean = win_sum / jnp.minimum(pos, float(w))
        dlt = (mean - cur.astype(F32)).astype(BF16)
        mixed = jnp.dot(dlt, pw_ref[g], preferred_element_type=F32)
        o_ref[:, sl] = (mixed * ps_ref[:, sl]).astype(o_ref.dtype)
    prev_s[...] = p_ref[...]


def _pool(proj, p_blk, pool_w_bf, pool_scale, seq, d):
    t = proj.shape[0]
    tl = SEQ_TILE
    g, gd = pool_w_bf.shape[0], pool_w_bf.shape[1]
    return pl.pallas_call(
        functools.partial(_pool_kernel, tiles_per_seq=seq // tl),
        grid=(t // tl,),
        in_specs=[pl.BlockSpec((tl, d), lambda i: (i, p_blk)),
                  pl.BlockSpec((g, gd, gd), lambda i: (0, 0, 0)),
                  pl.BlockSpec((1, d), lambda i: (0, 0))],
        out_specs=pl.BlockSpec((tl, d), lambda i: (i, 0)),
        out_shape=jax.ShapeDtypeStruct((t, d), BF16),
        scratch_shapes=[pltpu.VMEM((tl, d), BF16)],
        compiler_params=_params("arbitrary"),
        name="pool",
    )(proj, pool_w_bf, pool_scale.reshape(1, d))


def _ssd_kernel(z_ref, xs_ref, bm_ref, cm_ref, dt_ref,
                cwx_ref, cwb_ref, cwc_ref, cbx_ref, cbb_ref, cbc_ref,
                dtb_ref, alog_ref, dsk_ref, nw_ref, exp_ref,
                o_ref,
                extx_s, extb_s, extc_s, state_s, xdt_s, y_s, cb_s, cs_s, cst_s, xc_s, bc_s, cc_s,
                *, tiles_per_seq):
    i = pl.program_id(0)
    tl, inner = xs_ref.shape
    gw = inner // SSD_GROUPS
    pairs_per_group = gw // LANES
    n_pairs = inner // LANES

    @pl.when(i % tiles_per_seq == 0)
    def _():
        extx_s[0:CONV_HALO, :] = jnp.zeros((CONV_HALO, inner), F32)
        extb_s[0:CONV_HALO, :] = jnp.zeros((CONV_HALO, extb_s.shape[1]), F32)
        extc_s[0:CONV_HALO, :] = jnp.zeros((CONV_HALO, extc_s.shape[1]), F32)
        state_s[...] = jnp.zeros_like(state_s)

    def conv_silu(ext_ref, src_ref, w_ref, b_ref, dst_ref, width, cw):
        for c0 in range(0, width, cw):
            cs = slice(c0, c0 + cw)
            ext_ref[CONV_HALO:CONV_HALO + tl, cs] = src_ref[:, cs].astype(F32)
            acc = b_ref[:, cs] + w_ref[SSD_CONV - 1:SSD_CONV, cs] * ext_ref[CONV_HALO:CONV_HALO + tl, cs]
            for k in range(1, SSD_CONV):
                acc = acc + w_ref[SSD_CONV - 1 - k:SSD_CONV - k, cs] * ext_ref[pl.ds(CONV_HALO - k, tl), cs]
            dst_ref[:, cs] = _silu(acc)
            ext_ref[0:CONV_HALO, cs] = ext_ref[tl:tl + CONV_HALO, cs]

    conv_silu(extx_s, xs_ref, cwx_ref, cbx_ref, xc_s, inner, LANES)
    conv_silu(extb_s, bm_ref, cwb_ref, cbb_ref, bc_s, bm_ref.shape[1], LANES)
    conv_silu(extc_s, cm_ref, cwc_ref, cbc_ref, cc_s, cm_ref.shape[1], LANES)

    dtv = dt_ref[...] + dtb_ref[...]
    dt = jnp.maximum(dtv, 0.0) + jnp.log1p(jnp.exp(-jnp.abs(dtv)))
    a = -jnp.exp(alog_ref[...])
    da = dt * a
    ri = lax.broadcasted_iota(jnp.int32, (tl, tl), 0)
    ci = lax.broadcasted_iota(jnp.int32, (tl, tl), 1)
    causal = ri >= ci
    cs = jnp.dot(causal.astype(F32), da, precision=HIGHEST, preferred_element_type=F32)
    cs2 = cs * LOG2_E
    cs_s[...] = cs2
    cst_s[...] = cs2.T
    last = cs[tl - 1:tl, :]
    dt_bf = dt.astype(BF16)
    ecs_bf = jnp.exp(cs).astype(BF16)
    dte_bf = jnp.exp(last - cs).astype(BF16)

    for g in range(SSD_GROUPS):
        gs = slice(g * gw, (g + 1) * gw)
        ns = slice(g * SSD_STATE, (g + 1) * SSD_STATE)
        exp_g = exp_ref[:, gs]
        dt_x = jnp.dot(dt_bf, exp_g, preferred_element_type=F32)
        ecs_x = jnp.dot(ecs_bf, exp_g, preferred_element_type=F32)
        dte_x = jnp.dot(dte_bf, exp_g, preferred_element_type=F32)
        xg = xc_s[:, gs]
        xdt = xg * dt_x
        xdt_bf = xdt.astype(BF16)
        xd_bf = (xdt * dte_x).astype(BF16)
        bg = bc_s[:, ns]
        cg = cc_s[:, ns].astype(BF16)
        cb_s[g] = lax.dot_general(cg, bg.astype(BF16), (((1,), (1,)), ((), ())), preferred_element_type=F32)
        s_old = state_s[g]
        y_off = jnp.dot(cg, s_old.astype(BF16), preferred_element_type=F32) * ecs_x
        state_s[g] = (s_old * ecs_x[tl - 1:tl, :]
                      + jnp.dot(bg.T.astype(BF16), xd_bf, preferred_element_type=F32))
        y0 = y_off + dsk_ref[:, gs] * xg
        for q in range(pairs_per_group):
            qs = slice(q * LANES, (q + 1) * LANES)
            xdt_s[g * pairs_per_group + q] = xdt_bf[:, qs]
            y_s[g * pairs_per_group + q] = y0[:, qs]

    half = tl // 2
    causal_top = (lax.broadcasted_iota(jnp.int32, (half, half), 0)
                  >= lax.broadcasted_iota(jnp.int32, (half, half), 1))
    causal_bot = (lax.broadcasted_iota(jnp.int32, (half, tl), 0) + half
                  >= lax.broadcasted_iota(jnp.int32, (half, tl), 1))
    first_head_lanes = lax.broadcasted_iota(jnp.int32, (half, LANES), 1) < SSD_HEAD_DIM

    def pair_body(hp, carry):
        g = hp // pairs_per_group
        cb_top = cb_s[g, 0:half, 0:half]
        cb_bot = cb_s[g, half:tl, :]
        x_top = xdt_s[hp, 0:half, :]
        x_all = xdt_s[hp]
        cs_top = cs_s[0:half, :]
        cs_bot = cs_s[half:tl, :]
        tops, bots = [], []
        for e in range(2):
            h = 2 * hp + e
            head_lane = jnp.full((half, LANES), h, jnp.int32)
            col_top = jnp.take_along_axis(cs_top, head_lane, axis=1)
            col_bot = jnp.take_along_axis(cs_bot, head_lane, axis=1)
            col_bot = jnp.concatenate([col_bot] * (tl // LANES), axis=1)
            row_all = cst_s[pl.ds(h, 1), :]
            row_top = row_all[:, 0:half]
            m_top = jnp.exp2(jnp.where(causal_top, col_top - row_top, -jnp.inf)) * cb_top
            m_bot = jnp.exp2(jnp.where(causal_bot, col_bot - row_all, -jnp.inf)) * cb_bot
            tops.append(jnp.dot(m_top.astype(BF16), x_top, preferred_element_type=F32))
            bots.append(jnp.dot(m_bot.astype(BF16), x_all, preferred_element_type=F32))
        y_s[hp, 0:half, :] = y_s[hp, 0:half, :] + jnp.where(first_head_lanes, tops[0], tops[1])
        y_s[hp, half:tl, :] = y_s[hp, half:tl, :] + jnp.where(first_head_lanes, bots[0], bots[1])
        return carry

    lax.fori_loop(0, n_pairs, pair_body, 0, unroll=16)

    for g in range(SSD_GROUPS):
        gs = slice(g * gw, (g + 1) * gw)
        yg = jnp.concatenate([y_s[g * pairs_per_group + q] for q in range(pairs_per_group)], axis=1)
        yg = yg * _silu(z_ref[:, gs].astype(F32))
        ms = jnp.mean(yg * yg, axis=-1, keepdims=True)
        o_ref[:, gs] = (yg * lax.rsqrt(ms + EPS) * nw_ref[:, gs]).astype(o_ref.dtype)


def _ssd(proj, dt_raw, blk, conv_w, conv_b, dt_bias, a_log, d_skip, ssd_norm_w, seq):
    t = proj.shape[0]
    tl = SEQ_TILE
    heads = dt_bias.shape[0]
    inner = ssd_norm_w.shape[0]
    bc = SSD_GROUPS * SSD_STATE
    gw = inner // SSD_GROUPS
    pad = LANES - heads
    row = lambda v: v.astype(F32).reshape(1, -1)
    dtb = jnp.pad(row(dt_bias), ((0, 0), (0, pad)))
    alog = jnp.pad(row(a_log), ((0, 0), (0, pad)))
    dsk = jnp.repeat(d_skip.astype(F32), SSD_HEAD_DIM).reshape(1, inner)
    expand = (lax.broadcasted_iota(jnp.int32, (LANES, inner), 1) // SSD_HEAD_DIM
              == lax.broadcasted_iota(jnp.int32, (LANES, inner), 0)).astype(BF16)
    cw = conv_w.astype(F32)
    cbias = row(conv_b)
    full = lambda shape: pl.BlockSpec(shape, lambda i: tuple(0 for _ in shape))
    return pl.pallas_call(
        functools.partial(_ssd_kernel, tiles_per_seq=seq // tl),
        grid=(t // tl,),
        in_specs=[pl.BlockSpec((tl, inner), lambda i: (i, blk["z"])),
                  pl.BlockSpec((tl, inner), lambda i: (i, blk["xs"])),
                  pl.BlockSpec((tl, bc), lambda i: (i, blk["B"])),
                  pl.BlockSpec((tl, bc), lambda i: (i, blk["C"])),
                  pl.BlockSpec((tl, LANES), lambda i: (i, 0)),
                  full((SSD_CONV, inner)), full((SSD_CONV, bc)), full((SSD_CONV, bc)),
                  full((1, inner)), full((1, bc)), full((1, bc)),
                  full((1, LANES)), full((1, LANES)), full((1, inner)), full((1, inner)),
                  full((LANES, inner))],
        out_specs=pl.BlockSpec((tl, inner), lambda i: (i, 0)),
        out_shape=jax.ShapeDtypeStruct((t, inner), BF16),
        scratch_shapes=[pltpu.VMEM((tl + CONV_HALO, inner), F32),
                        pltpu.VMEM((tl + CONV_HALO, bc), F32),
                        pltpu.VMEM((tl + CONV_HALO, bc), F32),
                        pltpu.VMEM((SSD_GROUPS, SSD_STATE, gw), F32),
                        pltpu.VMEM((inner // LANES, tl, LANES), BF16),
                        pltpu.VMEM((inner // LANES, tl, LANES), F32),
                        pltpu.VMEM((SSD_GROUPS, tl, tl), F32),
                        pltpu.VMEM((tl, LANES), F32),
                        pltpu.VMEM((LANES, tl), F32),
                        pltpu.VMEM((tl, inner), F32),
                        pltpu.VMEM((tl, bc), F32),
                        pltpu.VMEM((tl, bc), F32)],
        compiler_params=_params("arbitrary"),
        name="ssd",
    )(proj, proj, proj, proj, dt_raw,
      cw[:, :inner], cw[:, inner:inner + bc], cw[:, inner + bc:],
      cbias[:, :inner], cbias[:, inner:inner + bc], cbias[:, inner + bc:],
      dtb, alog, dsk, row(ssd_norm_w), expand)


def _merge_kernel(yp_ref, ys_ref, gp_ref, gs_ref, wp_ref, ws_ref, o_ref):
    a = jnp.dot(yp_ref[...], wp_ref[...], preferred_element_type=F32)
    b = jnp.dot(ys_ref[...], ws_ref[...], preferred_element_type=F32)
    o_ref[...] = (jax.nn.sigmoid(gp_ref[...].astype(F32)) * a
                  + jax.nn.sigmoid(gs_ref[...].astype(F32)) * b).astype(o_ref.dtype)


def _merge(ypool, yssd, proj, gp_off, gs_off, wbp, wbs):
    t, d = ypool.shape
    inner = yssd.shape[1]
    tm, tn = min(MERGE_TM, t), MERGE_TN
    return pl.pallas_call(
        _merge_kernel,
        grid=(t // tm, d // tn),
        in_specs=[pl.BlockSpec((tm, d), lambda i, j: (i, 0)),
                  pl.BlockSpec((tm, inner), lambda i, j: (i, 0)),
                  pl.BlockSpec((tm, tn), lambda i, j: (i, gp_off // tn + j)),
                  pl.BlockSpec((tm, tn), lambda i, j: (i, gs_off // tn + j)),
                  pl.BlockSpec((d, tn), lambda i, j: (0, j)),
                  pl.BlockSpec((inner, tn), lambda i, j: (0, j))],
        out_specs=pl.BlockSpec((tm, tn), lambda i, j: (i, j)),
        out_shape=jax.ShapeDtypeStruct((t, d), BF16),
        compiler_params=_params("arbitrary", "arbitrary"),
        name="merge",
    )(ypool, yssd, proj, proj, wbp, wbs)


def _out_kernel(m_ref, wo_ref, x_ref, mod_ref, wr_ref, br_ref,
                h_ref, u_ref, idx_ref, gate_ref, rank_ref, cnt_ref, carry_s):
    i = pl.program_id(0)
    tm = m_ref.shape[0]
    ne = wr_ref.shape[0]

    @pl.when(i == 0)
    def _():
        carry_s[...] = jnp.zeros_like(carry_s)

    mix = jnp.dot(m_ref[...], wo_ref[...], preferred_element_type=F32)
    h1 = x_ref[...] + mod_ref[0, 2:3, :] * mix
    h_ref[...] = h1
    ms = jnp.mean(h1 * h1, axis=-1, keepdims=True)
    u = h1 * lax.rsqrt(ms + EPS) * (1.0 + mod_ref[0, 4:5, :]) + mod_ref[0, 3:4, :]
    u_ref[...] = _rows_to_tiles(_pack_bf16_pairs(u))
    logits = lax.dot_general(wr_ref[...], u, (((1,), (1,)), ((), ())), precision=HIGHEST,
                             preferred_element_type=F32) + br_ref[...]
    eidx = lax.broadcasted_iota(jnp.int32, (ne, tm), 0)
    work = logits
    vals, idxs, hots = [], [], []
    for _ in range(TOP_K):
        mx = jnp.max(work, axis=0, keepdims=True)
        sel = jnp.min(jnp.where(work == mx, eidx, ne), axis=0, keepdims=True)
        hot = eidx == sel
        vals.append(mx)
        idxs.append(sel)
        hots.append(hot)
        work = jnp.where(hot, -jnp.inf, work)
    exps = [jnp.exp(v - vals[0]) for v in vals]
    den = exps[0]
    for e in exps[1:]:
        den = den + e
    cnt = hots[0].astype(F32)
    for hot in hots[1:]:
        cnt = cnt + hot.astype(F32)
    ti = lax.broadcasted_iota(jnp.int32, (tm, tm), 0)
    tj = lax.broadcasted_iota(jnp.int32, (tm, tm), 1)
    before = (ti < tj).astype(BF16)
    prefix = jnp.dot(cnt.astype(BF16), before, preferred_element_type=F32)
    base = carry_s[:, 0:1] + prefix
    pad_rows = idx_ref.shape[0] - TOP_K
    ranks = [jnp.sum(jnp.where(hot, base, 0.0), axis=0, keepdims=True) for hot in hots]
    idx_ref[...] = jnp.concatenate(idxs + [jnp.zeros((pad_rows, tm), jnp.int32)], axis=0)
    gate_ref[...] = jnp.concatenate([e / den for e in exps] + [jnp.zeros((pad_rows, tm), F32)], axis=0)
    rank_ref[...] = jnp.concatenate([r.astype(jnp.int32) for r in ranks]
                                    + [jnp.zeros((pad_rows, tm), jnp.int32)], axis=0)
    carry_s[...] = carry_s[...] + jnp.sum(cnt, axis=1, keepdims=True)
    cnt_ref[...] = carry_s[...].astype(jnp.int32)


def _outproj(merged, wo, x2, mod3, w_router, b_router, seq):
    t, d = x2.shape
    ne = w_router.shape[1]
    tm = min(OUT_TM, seq)
    tiles_per_seq = seq // tm
    rows = 8
    return pl.pallas_call(
        _out_kernel,
        grid=(t // tm,),
        in_specs=[pl.BlockSpec((tm, d), lambda i: (i, 0)),
                  pl.BlockSpec((d, d), lambda i: (0, 0), pipeline_mode=pl.Buffered(1)),
                  pl.BlockSpec((tm, d), lambda i: (i, 0)),
                  pl.BlockSpec((1, N_MOD, d), lambda i: (i // tiles_per_seq, 0, 0)),
                  pl.BlockSpec((ne, d), lambda i: (0, 0)),
                  pl.BlockSpec((ne, 1), lambda i: (0, 0))],
        out_specs=[pl.BlockSpec((tm, d), lambda i: (i, 0)),
                   pl.BlockSpec((tm,) + _token_tile(d), lambda i: (i, 0, 0)),
                   pl.BlockSpec((rows, tm), lambda i: (0, i)),
                   pl.BlockSpec((rows, tm), lambda i: (0, i)),
                   pl.BlockSpec((rows, tm), lambda i: (0, i)),
                   pl.BlockSpec((ne, LANES), lambda i: (0, 0))],
        out_shape=[jax.ShapeDtypeStruct((t, d), F32), jax.ShapeDtypeStruct((t,) + _token_tile(d), jnp.uint32),
                   jax.ShapeDtypeStruct((rows, t), jnp.int32), jax.ShapeDtypeStruct((rows, t), F32),
                   jax.ShapeDtypeStruct((rows, t), jnp.int32), jax.ShapeDtypeStruct((ne, LANES), jnp.int32)],
        scratch_shapes=[pltpu.VMEM((ne, LANES), F32)],
        compiler_params=_params("arbitrary"),
        name="outproj",
    )(merged, wo, x2, mod3, w_router.astype(F32).T, b_router.astype(F32).reshape(ne, 1))


def _row_copy(src_hbm, src_row, dst_ref, dst_row, sem):
    return pltpu.make_async_copy(src_hbm.at[pl.ds(src_row, 1)], dst_ref.at[pl.ds(dst_row, 1)], sem)


def _dispatch_kernel(slot_ref, u_ref, init_hbm, xs_hbm, sem):
    del init_hbm
    ts = u_ref.shape[0]

    def issue(grp, carry):
        for q in range(DISPATCH_TOKENS_PER_ITER):
            tok = DISPATCH_TOKENS_PER_ITER * grp + q
            for k in range(TOP_K):
                _row_copy(u_ref, tok, xs_hbm, slot_ref[0, 0, tok * TOP_K + k], sem).start(priority=k % 2)
        return carry

    lax.fori_loop(0, ts // DISPATCH_TOKENS_PER_ITER, issue, 0)
    rows = pl.ds(0, ts * TOP_K)
    pltpu.make_async_copy(xs_hbm.at[rows], xs_hbm.at[rows], sem).wait()


def _dispatch(u2, slot_tk, n_slots):
    t, tile = u2.shape[0], u2.shape[1:]
    ts = min(DISPATCH_TS, t)
    slots3 = slot_tk.reshape(t // ts, 1, ts * TOP_K)
    init = jnp.zeros((n_slots,) + tile, u2.dtype)
    return pl.pallas_call(
        _dispatch_kernel,
        grid=(t // ts,),
        in_specs=[pl.BlockSpec((1, 1, ts * TOP_K), lambda i: (i, 0, 0), memory_space=pltpu.SMEM),
                  pl.BlockSpec((ts,) + tile, lambda i: (i, 0, 0)),
                  pl.BlockSpec(memory_space=pl.ANY)],
        out_specs=pl.BlockSpec(memory_space=pl.ANY),
        out_shape=jax.ShapeDtypeStruct((n_slots,) + tile, u2.dtype),
        scratch_shapes=[pltpu.SemaphoreType.DMA(())],
        input_output_aliases={2: 0},
        compiler_params=_params("arbitrary"),
        name="dispatch",
    )(slots3, u2, init)


ITEM_IDLE, ITEM_ACTIVE, ITEM_ZERO_FILL, ITEM_ACTIVE_HALF = 0, 1, 2, 3


def _moe_kernel(tb_ref, tj_ref, te_ref, tr_ref, tflag_ref, tout_ref, tslot_ref, tne_ref, tnj_ref, thn_ref,
                x_ref, wgu_hbm, bg_ref, bu_ref, wd_hbm, bd_ref, o_ref,
                wg_s, wu_s, wd_s, xb_s, acc_s, wg_buf, wu_buf, wd_buf, sem, *, last_j):
    del tb_ref, tout_ref
    i = pl.program_id(0)
    flag = tflag_ref[i]
    r = tr_ref[i]
    j = tj_ref[i]
    half = x_ref.shape[1] * x_ref.shape[2]
    tf = wg_s.shape[1]
    up_col0 = wgu_hbm.shape[2] // 2

    def weight_copies(e, jt, slot):
        c0 = pl.multiple_of(jt * tf, tf)
        return (pltpu.make_async_copy(wgu_hbm.at[e, :, pl.ds(c0, tf)], wg_buf.at[slot], sem.at[slot]),
                pltpu.make_async_copy(wgu_hbm.at[e, :, pl.ds(up_col0 + c0, tf)], wu_buf.at[slot], sem.at[slot]),
                pltpu.make_async_copy(wd_hbm.at[e, pl.ds(c0, tf), :], wd_buf.at[slot], sem.at[slot]))

    is_active = (flag & 1) == 1

    @pl.when(i == 0)
    def _():
        acc_s[...] = jnp.zeros_like(acc_s)

    @pl.when(is_active & (r == 0))
    def _():
        slot = tslot_ref[i]

        @pl.when(i == 0)
        def _():
            for cp in weight_copies(te_ref[i], j, slot):
                cp.start()

        for cp in weight_copies(te_ref[i], j, slot):
            cp.wait()
        for s in range(2):
            @pl.when(slot == s)
            def _():
                wg_s[...] = wg_buf[s].astype(BF16)
                wu_s[...] = wu_buf[s].astype(BF16)
                wd_s[...] = wd_buf[s].astype(BF16)

        @pl.when(thn_ref[i] == 1)
        def _():
            for cp in weight_copies(tne_ref[i], tnj_ref[i], 1 - slot):
                cp.start()

    @pl.when(is_active & (j == 0))
    def _():
        lo, hi = _unpack_bf16_pairs(_tiles_to_rows(x_ref[...]))
        xb_s[r, :, :half] = lo
        xb_s[r, :, half:] = hi

    def expert_mlp(rows):
        xb = xb_s[r, 0:rows, :]
        gate = jnp.dot(xb, wg_s[...], preferred_element_type=F32) + bg_ref[...]
        up = jnp.dot(xb, wu_s[...], preferred_element_type=F32) + bu_ref[...]
        gate = jnp.minimum(gate, SWIGLU_LIMIT)
        up = jnp.clip(up, -SWIGLU_LIMIT, SWIGLU_LIMIT)
        act = (up + 1.0) * gate * jax.nn.sigmoid(SWIGLU_ALPHA * gate)
        part = jnp.dot(act.astype(BF16), wd_s[...], preferred_element_type=F32)
        base = jnp.where(j == 0, jnp.broadcast_to(bd_ref[...], (rows, bd_ref.shape[1])), acc_s[r, 0:rows, :])
        acc_s[r, 0:rows, :] = base + part

        @pl.when(j == last_j)
        def _():
            o_ref[0:rows] = _rows_to_tiles(_pack_bf16_pairs(acc_s[r, 0:rows, :]))
            if rows < o_ref.shape[0]:
                o_ref[rows:] = jnp.zeros((o_ref.shape[0] - rows,) + o_ref.shape[1:], o_ref.dtype)

    @pl.when(flag == ITEM_ACTIVE)
    def _():
        expert_mlp(o_ref.shape[0])

    @pl.when(flag == ITEM_ACTIVE_HALF)
    def _():
        expert_mlp(o_ref.shape[0] // 2)

    @pl.when(flag == ITEM_ZERO_FILL)
    def _():
        o_ref[...] = jnp.zeros_like(o_ref)


def _lookup(table, idx):
    hot = idx[:, None] == jnp.arange(table.shape[0], dtype=jnp.int32)[None, :]
    return jnp.sum(jnp.where(hot, table[None, :].astype(jnp.int32), 0), axis=1).astype(jnp.int32)


def _moe_tables(counts, n_blocks, ft):
    ne = counts.shape[0]
    padded = (counts + MOE_BM - 1) // MOE_BM * MOE_BM
    pend = jnp.cumsum(padded)
    pstart = pend - padded
    blocks_per_expert = padded // MOE_BM
    first_block = pstart // MOE_BM
    n_active = (pend[-1] // MOE_BM).astype(jnp.int32)

    blocks = jnp.arange(n_blocks, dtype=jnp.int32)
    block_expert = jnp.minimum(jnp.sum(pend[None, :] <= (blocks * MOE_BM)[:, None], axis=1), ne - 1).astype(jnp.int32)
    off = blocks - _lookup(first_block, block_expert)
    is_start = (off % MOE_CHUNK_BLOCKS == 0) & (blocks < n_active)
    chunk_size = jnp.minimum(MOE_CHUNK_BLOCKS,
                             _lookup(blocks_per_expert, block_expert) - off // MOE_CHUNK_BLOCKS * MOE_CHUNK_BLOCKS)

    items = jnp.arange(n_blocks * ft, dtype=jnp.int32)
    active = items < ft * n_active
    src = jnp.minimum(items, jnp.maximum(ft * n_active - 1, 0))
    cand = jnp.where(is_start[None, :] & (ft * blocks[None, :] <= src[:, None]), blocks[None, :], 0)
    cstart = jnp.max(cand, axis=1).astype(jnp.int32)
    m = jnp.maximum(_lookup(chunk_size, cstart), 1)
    local = src - ft * cstart
    tj = local // m
    tr = local % m
    tb = cstart + tr
    te = _lookup(block_expert, cstart)
    k = items - ft * n_active
    zero_fill = (~active) & (n_active + k < n_blocks)
    tout = jnp.where(active, jnp.where(tj == ft - 1, tb, cstart), jnp.minimum(n_active + k, n_blocks - 1))
    valid_rows = _lookup(counts, te) - (tb - _lookup(first_block, te)) * MOE_BM
    kind = jnp.where(valid_rows <= MOE_BM // 2, ITEM_ACTIVE_HALF, ITEM_ACTIVE)
    tflag = jnp.where(active, kind, jnp.where(zero_fill, ITEM_ZERO_FILL, ITEM_IDLE))
    group_start = active & (tr == 0)
    tslot = (jnp.cumsum(group_start.astype(jnp.int32)) - 1) % 2
    nxt = items + m
    has_next = group_start & (nxt < ft * n_active)
    nxt = jnp.minimum(nxt, n_blocks * ft - 1)
    tne, tnj = jnp.take(te, nxt), jnp.take(tj, nxt)
    as_i32 = lambda v: v.astype(jnp.int32)
    return pstart, tuple(map(as_i32, (tb, tj, te, tr, tflag, tout, tslot, tne, tnj, has_next)))


def _moe(xs, tables, wgu, bgu, wd, bd):
    n_slots = xs.shape[0]
    ne, d, f2 = wgu.shape
    f = f2 // 2
    bm, tf = MOE_BM, min(MOE_TF, f)
    nb, ft = n_slots // bm, f // tf
    grid_spec = pltpu.PrefetchScalarGridSpec(
        num_scalar_prefetch=len(tables),
        grid=(nb * ft,),
        in_specs=[pl.BlockSpec((bm,) + _token_tile(d), lambda i, tb, tj, te, *_: (tb[i], 0, 0)),
                  pl.BlockSpec(memory_space=pl.ANY),
                  pl.BlockSpec((None, 1, tf), lambda i, tb, tj, te, *_: (te[i], 0, tj[i])),
                  pl.BlockSpec((None, 1, tf), lambda i, tb, tj, te, *_: (te[i], 0, ft + tj[i])),
                  pl.BlockSpec(memory_space=pl.ANY),
                  pl.BlockSpec((None, 1, d), lambda i, tb, tj, te, *_: (te[i], 0, 0))],
        out_specs=pl.BlockSpec((bm,) + _token_tile(d), lambda i, tb, tj, te, tr, tg, to, *_: (to[i], 0, 0)),
        scratch_shapes=[pltpu.VMEM((d, tf), BF16), pltpu.VMEM((d, tf), BF16), pltpu.VMEM((tf, d), BF16),
                        pltpu.VMEM((MOE_CHUNK_BLOCKS, bm, d), BF16), pltpu.VMEM((MOE_CHUNK_BLOCKS, bm, d), F32),
                        pltpu.VMEM((2, d, tf), F32), pltpu.VMEM((2, d, tf), F32), pltpu.VMEM((2, tf, d), F32),
                        pltpu.SemaphoreType.DMA((2,))],
    )
    return pl.pallas_call(
        functools.partial(_moe_kernel, last_j=ft - 1),
        grid_spec=grid_spec,
        out_shape=jax.ShapeDtypeStruct((n_slots,) + _token_tile(d), jnp.uint32),
        compiler_params=_params("arbitrary"),
        name="moe",
    )(*tables, xs, wgu, bgu, bgu, wd, bd)


def _final_kernel(slot_ref, next_slot_ref, h_ref, gate_ref, mod_ref, fw_ref, y_hbm, o_ref, ybuf, sem):
    i = pl.program_id(0)
    tc, d = h_ref.shape
    half = d // 2
    n_rows = tc * TOP_K
    cur = i % 2

    def gather_rows(slots_ref, buf):
        def issue(grp, carry):
            for q in range(FINAL_ISSUE_UNROLL):
                tk = FINAL_ISSUE_UNROLL * grp + q
                _row_copy(y_hbm, slots_ref[0, 0, tk], ybuf.at[buf], tk, sem.at[buf]).start(priority=q % 2)
            return carry

        lax.fori_loop(0, n_rows // FINAL_ISSUE_UNROLL, issue, 0)

    @pl.when(i == 0)
    def _():
        gather_rows(slot_ref, 0)

    @pl.when(i + 1 < pl.num_programs(0))
    def _():
        gather_rows(next_slot_ref, 1 - cur)

    pltpu.make_async_copy(y_hbm.at[pl.ds(0, n_rows)], ybuf.at[cur], sem.at[cur]).wait()

    ffn_lo = jnp.zeros((tc, half), F32)
    ffn_hi = jnp.zeros((tc, half), F32)
    for k in range(TOP_K):
        w = _tiles_to_rows(ybuf[cur, k * tc:(k + 1) * tc])
        g = gate_ref[:, k:k + 1]
        ffn_lo = ffn_lo + g * lax.bitcast_convert_type(w << 16, F32)
        ffn_hi = ffn_hi + g * lax.bitcast_convert_type(w & jnp.uint32(0xFFFF0000), F32)
    h2_lo = h_ref[:, :half] + mod_ref[0, 5:6, :half] * ffn_lo
    h2_hi = h_ref[:, half:] + mod_ref[0, 5:6, half:] * ffn_hi
    ms = (jnp.sum(h2_lo * h2_lo, axis=-1, keepdims=True) + jnp.sum(h2_hi * h2_hi, axis=-1, keepdims=True)) / d
    inv = lax.rsqrt(ms + EPS)
    o_ref[:, :half] = (h2_lo * inv * fw_ref[:, :half]).astype(o_ref.dtype)
    o_ref[:, half:] = (h2_hi * inv * fw_ref[:, half:]).astype(o_ref.dtype)


def _final(h1, gates_tk, slot_kt, mod3, final_norm_w, y_slots, seq, out_dtype):
    t, d = h1.shape
    tc = min(FINAL_TC, seq)
    tiles_per_seq = seq // tc
    slots3 = slot_kt.reshape(TOP_K, t // tc, tc).transpose(1, 0, 2).reshape(t // tc, 1, TOP_K * tc)
    return pl.pallas_call(
        _final_kernel,
        grid=(t // tc,),
        in_specs=[pl.BlockSpec((1, 1, TOP_K * tc), lambda i: (i, 0, 0), memory_space=pltpu.SMEM),
                  pl.BlockSpec((1, 1, TOP_K * tc), lambda i: (jnp.minimum(i + 1, t // tc - 1), 0, 0),
                               memory_space=pltpu.SMEM),
                  pl.BlockSpec((tc, d), lambda i: (i, 0)),
                  pl.BlockSpec((tc, gates_tk.shape[1]), lambda i: (i, 0)),
                  pl.BlockSpec((1, N_MOD, d), lambda i: (i // tiles_per_seq, 0, 0)),
                  pl.BlockSpec((1, d), lambda i: (0, 0)),
                  pl.BlockSpec(memory_space=pl.ANY)],
        out_specs=pl.BlockSpec((tc, d), lambda i: (i, 0)),
        out_shape=jax.ShapeDtypeStruct((t, d), out_dtype),
        scratch_shapes=[pltpu.VMEM((2, TOP_K * tc) + _token_tile(d), jnp.uint32), pltpu.SemaphoreType.DMA((2,))],
        compiler_params=_params("arbitrary"),
        name="final",
    )(slots3, slots3, h1, gates_tk, mod3, final_norm_w.astype(F32).reshape(1, d), y_slots)


def kernel(x, c, w_ada, b_ada, w_in, pool_w, pool_scale, conv_w, conv_b, dt_bias, a_log, d_skip, ssd_norm_w,
           w_branch_pool, w_branch_ssd, w_out, w_router, b_router, w_gate_up, b_gate_up, w_down, b_down,
           final_norm_w):
    bsz, seq, d = x.shape
    depth = w_ada.shape[0]
    t = bsz * seq
    inner = ssd_norm_w.shape[1]
    heads = dt_bias.shape[1]
    bc = SSD_GROUPS * SSD_STATE
    ne = w_router.shape[2]
    assert depth == 1, "the final RMSNorm is fused into the last kernel of the single layer"
    assert SEQ_TILE == 2 * LANES
    assert seq % SEQ_TILE == 0 and heads <= LANES and (7 * d) % bc == 0 and (4 * d) % inner == 0
    assert inner // SSD_GROUPS % LANES == 0 and d % (len(POOL_WINDOWS) * LANES) == 0
    assert d % (2 * SUBLANES * LANES) == 0

    s1 = d
    s2 = s1 + inner
    s3 = s2 + inner + 2 * bc
    s4 = s3 + heads
    s5 = s4 + d
    blk = {"z": 0, "xs": 1, "p": (2 * inner) // d, "B": (2 * inner + 3 * d) // bc, "C": (2 * inner + 3 * d) // bc + 1}
    gp_off, gs_off = 2 * inner + d, 2 * inner + 2 * d

    h = x.astype(F32).reshape(t, d)
    for layer in range(depth):
        wl = w_in[layer]
        w_main = jnp.concatenate([wl[:, s1:s2], wl[:, s2:s2 + inner], wl[:, :s1], wl[:, s4:s5], wl[:, s5:],
                                  wl[:, s2 + inner:s2 + inner + bc], wl[:, s2 + inner + bc:s3]], axis=1).astype(BF16)
        w_dt = jnp.pad(wl[:, s3:s4].astype(F32), ((0, 0), (0, LANES - heads)))

        mod3 = _ada(c, w_ada[layer], b_ada[layer]).reshape(bsz, N_MOD, d)
        proj, dt_raw = _inproj(h, mod3, w_main, w_dt, seq)
        ypool = _pool(proj, blk["p"], pool_w[layer].astype(BF16), pool_scale[layer], seq, d)
        yssd = _ssd(proj, dt_raw, blk, conv_w[layer], conv_b[layer], dt_bias[layer], a_log[layer], d_skip[layer],
                    ssd_norm_w[layer], seq)
        merged = _merge(ypool, yssd, proj, gp_off, gs_off, w_branch_pool[layer].astype(BF16),
                        w_branch_ssd[layer].astype(BF16))
        h1, u2, idx_kt, gate_kt, rank_kt, counts = _outproj(merged, w_out[layer].astype(BF16), h, mod3,
                                                            w_router[layer], b_router[layer], seq)

        n_blocks = (t * TOP_K) // MOE_BM + ne
        ft = w_down.shape[2] // min(MOE_TF, w_down.shape[2])
        pstart, tables = _moe_tables(counts[:, 0], n_blocks, ft)
        eids = jnp.arange(ne, dtype=jnp.int32)[:, None, None]
        slot_kt = (jnp.sum(jnp.where(idx_kt[:TOP_K][None] == eids, pstart[:, None, None], 0), axis=0)
                   + rank_kt[:TOP_K]).astype(jnp.int32)

        xs = _dispatch(u2, slot_kt.T, n_blocks * MOE_BM)
        y_slots = _moe(xs, tables, w_gate_up[layer].astype(F32), b_gate_up[layer].astype(F32)[:, None, :],
                       w_down[layer].astype(F32), b_down[layer].astype(F32)[:, None, :])
        h = _final(h1, gate_kt.T, slot_kt, mod3, final_norm_w, y_slots, seq, x.dtype)
    return h.reshape(bsz, seq, d)
```

```python
import functools

import jax
import jax.numpy as jnp
from jax import lax
from jax.experimental import pallas as pl
from jax.experimental.pallas import tpu as pltpu

F32 = jnp.float32
BF16 = jnp.bfloat16
HIGHEST = lax.Precision.HIGHEST

EPS = 1e-6
POOL_WINDOWS = (2, 4, 8, 16)
SSD_GROUPS = 8
SSD_STATE = 128
SSD_CONV = 4
SSD_HEAD_DIM = 64
TOP_K = 4
SWIGLU_LIMIT = 7.0
SWIGLU_ALPHA = 1.702
N_MOD = 6

LOG2_E = 1.4426950408889634
LANES = 128
SUBLANES = 8
CONV_HALO = 8
V7X_VMEM_LIMIT = 56 * 1024 * 1024

ADA_TN = 1024
SEQ_TILE = 256
INPROJ_TM, INPROJ_TN = 1024, 2048
MERGE_TM, MERGE_TN = 1024, 512
OUT_TM = 512
MOE_BM, MOE_TF = 512, 512
MOE_CHUNK_BLOCKS = 2
DISPATCH_TS = 512
DISPATCH_TOKENS_PER_ITER = 8
FINAL_TC = 512
FINAL_ISSUE_UNROLL = 16


def _params(*sem):
    return pltpu.CompilerParams(dimension_semantics=sem, vmem_limit_bytes=V7X_VMEM_LIMIT)


def _silu(v):
    return v * jax.nn.sigmoid(v)


def _pack_bf16_pairs(v):
    half = v.shape[1] // 2
    lo = lax.bitcast_convert_type(v[:, :half].astype(BF16).astype(F32), jnp.uint32)
    hi = lax.bitcast_convert_type(v[:, half:].astype(BF16).astype(F32), jnp.uint32)
    return (hi & jnp.uint32(0xFFFF0000)) | (lo >> 16)


def _token_tile(d):
    return (d // 2 // LANES, LANES)


def _rows_to_tiles(w):
    return w.reshape(w.shape[0], w.shape[1] // LANES, LANES)


def _tiles_to_rows(w):
    return w.reshape(w.shape[0], w.shape[1] * LANES)


def _unpack_bf16_pairs(w):
    lo = lax.bitcast_convert_type(w << 16, F32).astype(BF16)
    hi = lax.bitcast_convert_type(w & jnp.uint32(0xFFFF0000), F32).astype(BF16)
    return lo, hi


def _ada_kernel(cb_ref, w_ref, b_ref, o_ref):
    nb, tn = cb_ref.shape[0], w_ref.shape[1]
    for b in range(nb):
        ca = _silu(cb_ref[b])
        cols = [jnp.sum(w_ref[:, j * LANES:(j + 1) * LANES] * ca, axis=0, keepdims=True)
                for j in range(tn // LANES)]
        o_ref[b:b + 1, :] = jnp.concatenate(cols, axis=1) + b_ref[...]


def _ada(c, w_ada, b_ada):
    nb, k = c.shape
    n = w_ada.shape[1]
    tn = ADA_TN if n % ADA_TN == 0 else n
    cb = jnp.broadcast_to(c.astype(F32)[:, :, None], (nb, k, LANES))
    return pl.pallas_call(
        _ada_kernel,
        grid=(n // tn,),
        in_specs=[pl.BlockSpec((nb, k, LANES), lambda j: (0, 0, 0)),
                  pl.BlockSpec((k, tn), lambda j: (0, j)),
                  pl.BlockSpec((1, tn), lambda j: (0, j))],
        out_specs=pl.BlockSpec((nb, tn), lambda j: (0, j)),
        out_shape=jax.ShapeDtypeStruct((nb, n), F32),
        compiler_params=_params("arbitrary"),
        name="ada",
    )(cb, w_ada, b_ada.reshape(1, n))


def _inproj_kernel(x_ref, mod_ref, w_ref, wdt_hi_ref, wdt_lo_ref, o_ref, dt_ref, u_s, *, rows_per_chunk):
    j = pl.program_id(1)
    tm = x_ref.shape[0]

    @pl.when(j == 0)
    def _():
        sh = mod_ref[0, 0:1, :]
        sc = mod_ref[0, 1:2, :]

        def body(r, carry):
            rows = pl.ds(pl.multiple_of(r * rows_per_chunk, rows_per_chunk), rows_per_chunk)
            xv = x_ref[rows, :]
            ms = jnp.mean(xv * xv, axis=-1, keepdims=True)
            u = xv * lax.rsqrt(ms + EPS) * (1.0 + sc) + sh
            u_hi = u.astype(BF16)
            u_s[rows, :] = u_hi
            u_lo = (u - u_hi.astype(F32)).astype(BF16)
            dt_ref[rows, :] = (jnp.dot(u_hi, wdt_hi_ref[...], preferred_element_type=F32)
                               + (jnp.dot(u_hi, wdt_lo_ref[...], preferred_element_type=F32)
                                  + jnp.dot(u_lo, wdt_hi_ref[...], preferred_element_type=F32)))
            return carry

        lax.fori_loop(0, tm // rows_per_chunk, body, 0)

    o_ref[...] = jnp.dot(u_s[...], w_ref[...], preferred_element_type=F32).astype(o_ref.dtype)


def _inproj(x2, mod3, w_main, w_dt, seq):
    t, d = x2.shape
    w_dt_hi = w_dt.astype(BF16)
    w_dt_lo = (w_dt - w_dt_hi.astype(F32)).astype(BF16)
    n = w_main.shape[1]
    tm = min(INPROJ_TM, seq)
    tn = INPROJ_TN
    tiles_per_seq = seq // tm
    return pl.pallas_call(
        functools.partial(_inproj_kernel, rows_per_chunk=min(128, tm)),
        grid=(t // tm, n // tn),
        in_specs=[pl.BlockSpec((tm, d), lambda i, j: (i, 0)),
                  pl.BlockSpec((1, N_MOD, d), lambda i, j: (i // tiles_per_seq, 0, 0)),
                  pl.BlockSpec((d, tn), lambda i, j: (0, j)),
                  pl.BlockSpec((d, LANES), lambda i, j: (0, 0)),
                  pl.BlockSpec((d, LANES), lambda i, j: (0, 0))],
        out_specs=[pl.BlockSpec((tm, tn), lambda i, j: (i, j)),
                   pl.BlockSpec((tm, LANES), lambda i, j: (i, 0))],
        out_shape=[jax.ShapeDtypeStruct((t, n), BF16), jax.ShapeDtypeStruct((t, LANES), F32)],
        scratch_shapes=[pltpu.VMEM((tm, d), BF16)],
        compiler_params=_params("arbitrary", "arbitrary"),
        name="inproj",
    )(x2, mod3, w_main, w_dt_hi, w_dt_lo)


def _pool_kernel(p_ref, pw_ref, ps_ref, o_ref, prev_s, *, tiles_per_seq):
    i = pl.program_id(0)
    tl, d = p_ref.shape
    gd = d // len(POOL_WINDOWS)
    it = i % tiles_per_seq

    @pl.when(it == 0)
    def _():
        prev_s[...] = jnp.zeros_like(prev_s)

    row = lax.broadcasted_iota(jnp.int32, (tl, 2 * tl), 0)
    col = lax.broadcasted_iota(jnp.int32, (tl, 2 * tl), 1)
    pos = (it * tl + lax.broadcasted_iota(jnp.int32, (tl, 1), 0) + 1).astype(F32)
    for g, w in enumerate(POOL_WINDOWS):
        sl = slice(g * gd, (g + 1) * gd)
        cur = p_ref[:, sl]
        ext = jnp.concatenate([prev_s[:, sl], cur], axis=0)
        band = ((col <= row + tl) & (col > row + tl - w)).astype(BF16)
        win_sum = jnp.dot(band, ext, preferred_element_type=F32)
        mean = win_sum / jnp.minimum(pos, float(w))
        dlt = (mean - cur.astype(F32)).astype(BF16)
        mixed = jnp.dot(dlt, pw_ref[g], preferred_element_type=F32)
        o_ref[:, sl] = (mixed * ps_ref[:, sl]).astype(o_ref.dtype)
    prev_s[...] = p_ref[...]


def _pool(proj, p_blk, pool_w_bf, pool_scale, seq, d):
    t = proj.shape[0]
    tl = SEQ_TILE
    g, gd = pool_w_bf.shape[0], pool_w_bf.shape[1]
    return pl.pallas_call(
        functools.partial(_pool_kernel, tiles_per_seq=seq // tl),
        grid=(t // tl,),
        in_specs=[pl.BlockSpec((tl, d), lambda i: (i, p_blk)),
                  pl.BlockSpec((g, gd, gd), lambda i: (0, 0, 0)),
                  pl.BlockSpec((1, d), lambda i: (0, 0))],
        out_specs=pl.BlockSpec((tl, d), lambda i: (i, 0)),
        out_shape=jax.ShapeDtypeStruct((t, d), BF16),
        scratch_shapes=[pltpu.VMEM((tl, d), BF16)],
        compiler_params=_params("arbitrary"),
        name="pool",
    )(proj, pool_w_bf, pool_scale.reshape(1, d))


def _ssd_kernel(z_ref, xs_ref, bm_ref, cm_ref, dt_ref,
                cwx_ref, cwb_ref, cwc_ref, cbx_ref, cbb_ref, cbc_ref,
                dtb_ref, alog_ref, dsk_ref, nw_ref, exp_ref,
                o_ref,
                extx_s, extb_s, extc_s, state_s, xdt_s, y_s, cb_s, cs_s, cst_s, xc_s, bc_s, cc_s,
                *, tiles_per_seq):
    i = pl.program_id(0)
    tl, inner = xs_ref.shape
    gw = inner // SSD_GROUPS
    pairs_per_group = gw // LANES
    n_pairs = inner // LANES

    @pl.when(i % tiles_per_seq == 0)
    def _():
        extx_s[0:CONV_HALO, :] = jnp.zeros((CONV_HALO, inner), F32)
        extb_s[0:CONV_HALO, :] = jnp.zeros((CONV_HALO, extb_s.shape[1]), F32)
        extc_s[0:CONV_HALO, :] = jnp.zeros((CONV_HALO, extc_s.shape[1]), F32)
        state_s[...] = jnp.zeros_like(state_s)

    def conv_silu(ext_ref, src_ref, w_ref, b_ref, dst_ref, width, cw):
        for c0 in range(0, width, cw):
            cs = slice(c0, c0 + cw)
            ext_ref[CONV_HALO:CONV_HALO + tl, cs] = src_ref[:, cs].astype(F32)
            acc = b_ref[:, cs] + w_ref[SSD_CONV - 1:SSD_CONV, cs] * ext_ref[CONV_HALO:CONV_HALO + tl, cs]
            for k in range(1, SSD_CONV):
                acc = acc + w_ref[SSD_CONV - 1 - k:SSD_CONV - k, cs] * ext_ref[pl.ds(CONV_HALO - k, tl), cs]
            dst_ref[:, cs] = _silu(acc)
            ext_ref[0:CONV_HALO, cs] = ext_ref[tl:tl + CONV_HALO, cs]

    conv_silu(extx_s, xs_ref, cwx_ref, cbx_ref, xc_s, inner, LANES)
    conv_silu(extb_s, bm_ref, cwb_ref, cbb_ref, bc_s, bm_ref.shape[1], LANES)
    conv_silu(extc_s, cm_ref, cwc_ref, cbc_ref, cc_s, cm_ref.shape[1], LANES)

    dtv = dt_ref[...] + dtb_ref[...]
    dt = jnp.maximum(dtv, 0.0) + jnp.log1p(jnp.exp(-jnp.abs(dtv)))
    a = -jnp.exp(alog_ref[...])
    da = dt * a
    ri = lax.broadcasted_iota(jnp.int32, (tl, tl), 0)
    ci = lax.broadcasted_iota(jnp.int32, (tl, tl), 1)
    causal = ri >= ci
    cs = jnp.dot(causal.astype(F32), da, precision=HIGHEST, preferred_element_type=F32)
    cs2 = cs * LOG2_E
    cs_s[...] = cs2
    cst_s[...] = cs2.T
    last = cs[tl - 1:tl, :]
    dt_bf = dt.astype(BF16)
    ecs_bf = jnp.exp(cs).astype(BF16)
    dte_bf = jnp.exp(last - cs).astype(BF16)

    for g in range(SSD_GROUPS):
        gs = slice(g * gw, (g + 1) * gw)
        ns = slice(g * SSD_STATE, (g + 1) * SSD_STATE)
        exp_g = exp_ref[:, gs]
        dt_x = jnp.dot(dt_bf, exp_g, preferred_element_type=F32)
        ecs_x = jnp.dot(ecs_bf, exp_g, preferred_element_type=F32)
        dte_x = jnp.dot(dte_bf, exp_g, preferred_element_type=F32)
        xg = xc_s[:, gs]
        xdt = xg * dt_x
        xdt_bf = xdt.astype(BF16)
        xd_bf = (xdt * dte_x).astype(BF16)
        bg = bc_s[:, ns]
        cg = cc_s[:, ns].astype(BF16)
        cb_s[g] = lax.dot_general(cg, bg.astype(BF16), (((1,), (1,)), ((), ())), preferred_element_type=F32)
        s_old = state_s[g]
        y_off = jnp.dot(cg, s_old.astype(BF16), preferred_element_type=F32) * ecs_x
        state_s[g] = (s_old * ecs_x[tl - 1:tl, :]
                      + jnp.dot(bg.T.astype(BF16), xd_bf, preferred_element_type=F32))
        y0 = y_off + dsk_ref[:, gs] * xg
        for q in range(pairs_per_group):
            qs = slice(q * LANES, (q + 1) * LANES)
            xdt_s[g * pairs_per_group + q] = xdt_bf[:, qs]
            y_s[g * pairs_per_group + q] = y0[:, qs]

    half = tl // 2
    causal_top = (lax.broadcasted_iota(jnp.int32, (half, half), 0)
                  >= lax.broadcasted_iota(jnp.int32, (half, half), 1))
    causal_bot = (lax.broadcasted_iota(jnp.int32, (half, tl), 0) + half
                  >= lax.broadcasted_iota(jnp.int32, (half, tl), 1))
    first_head_lanes = lax.broadcasted_iota(jnp.int32, (half, LANES), 1) < SSD_HEAD_DIM

    def pair_body(hp, carry):
        g = hp // pairs_per_group
        cb_top = cb_s[g, 0:half, 0:half]
        cb_bot = cb_s[g, half:tl, :]
        x_top = xdt_s[hp, 0:half, :]
        x_all = xdt_s[hp]
        cs_top = cs_s[0:half, :]
        cs_bot = cs_s[half:tl, :]
        tops, bots = [], []
        for e in range(2):
            h = 2 * hp + e
            head_lane = jnp.full((half, LANES), h, jnp.int32)
            col_top = jnp.take_along_axis(cs_top, head_lane, axis=1)
            col_bot = jnp.take_along_axis(cs_bot, head_lane, axis=1)
            col_bot = jnp.concatenate([col_bot] * (tl // LANES), axis=1)
            row_all = cst_s[pl.ds(h, 1), :]
            row_top = row_all[:, 0:half]
            m_top = jnp.exp2(jnp.where(causal_top, col_top - row_top, -jnp.inf)) * cb_top
            m_bot = jnp.exp2(jnp.where(causal_bot, col_bot - row_all, -jnp.inf)) * cb_bot
            tops.append(jnp.dot(m_top.astype(BF16), x_top, preferred_element_type=F32))
            bots.append(jnp.dot(m_bot.astype(BF16), x_all, preferred_element_type=F32))
        y_s[hp, 0:half, :] = y_s[hp, 0:half, :] + jnp.where(first_head_lanes, tops[0], tops[1])
        y_s[hp, half:tl, :] = y_s[hp, half:tl, :] + jnp.where(first_head_lanes, bots[0], bots[1])
        return carry

    lax.fori_loop(0, n_pairs, pair_body, 0, unroll=16)

    for g in range(SSD_GROUPS):
        gs = slice(g * gw, (g + 1) * gw)
        yg = jnp.concatenate([y_s[g * pairs_per_group + q] for q in range(pairs_per_group)], axis=1)
        yg = yg * _silu(z_ref[:, gs].astype(F32))
        ms = jnp.mean(yg * yg, axis=-1, keepdims=True)
        o_ref[:, gs] = (yg * lax.rsqrt(ms + EPS) * nw_ref[:, gs]).astype(o_ref.dtype)


def _ssd(proj, dt_raw, blk, conv_w, conv_b, dt_bias, a_log, d_skip, ssd_norm_w, seq):
    t = proj.shape[0]
    tl = SEQ_TILE
    heads = dt_bias.shape[0]
    inner = ssd_norm_w.shape[0]
    bc = SSD_GROUPS * SSD_STATE
    gw = inner // SSD_GROUPS
    pad = LANES - heads
    row = lambda v: v.astype(F32).reshape(1, -1)
    dtb = jnp.pad(row(dt_bias), ((0, 0), (0, pad)))
    alog = jnp.pad(row(a_log), ((0, 0), (0, pad)))
    dsk = jnp.repeat(d_skip.astype(F32), SSD_HEAD_DIM).reshape(1, inner)
    expand = (lax.broadcasted_iota(jnp.int32, (LANES, inner), 1) // SSD_HEAD_DIM
              == lax.broadcasted_iota(jnp.int32, (LANES, inner), 0)).astype(BF16)
    cw = conv_w.astype(F32)
    cbias = row(conv_b)
    full = lambda shape: pl.BlockSpec(shape, lambda i: tuple(0 for _ in shape))
    return pl.pallas_call(
        functools.partial(_ssd_kernel, tiles_per_seq=seq // tl),
        grid=(t // tl,),
        in_specs=[pl.BlockSpec((tl, inner), lambda i: (i, blk["z"])),
                  pl.BlockSpec((tl, inner), lambda i: (i, blk["xs"])),
                  pl.BlockSpec((tl, bc), lambda i: (i, blk["B"])),
                  pl.BlockSpec((tl, bc), lambda i: (i, blk["C"])),
                  pl.BlockSpec((tl, LANES), lambda i: (i, 0)),
                  full((SSD_CONV, inner)), full((SSD_CONV, bc)), full((SSD_CONV, bc)),
                  full((1, inner)), full((1, bc)), full((1, bc)),
                  full((1, LANES)), full((1, LANES)), full((1, inner)), full((1, inner)),
                  full((LANES, inner))],
        out_specs=pl.BlockSpec((tl, inner), lambda i: (i, 0)),
        out_shape=jax.ShapeDtypeStruct((t, inner), BF16),
        scratch_shapes=[pltpu.VMEM((tl + CONV_HALO, inner), F32),
                        pltpu.VMEM((tl + CONV_HALO, bc), F32),
                        pltpu.VMEM((tl + CONV_HALO, bc), F32),
                        pltpu.VMEM((SSD_GROUPS, SSD_STATE, gw), F32),
                        pltpu.VMEM((inner // LANES, tl, LANES), BF16),
                        pltpu.VMEM((inner // LANES, tl, LANES), F32),
                        pltpu.VMEM((SSD_GROUPS, tl, tl), F32),
                        pltpu.VMEM((tl, LANES), F32),
                        pltpu.VMEM((LANES, tl), F32),
                        pltpu.VMEM((tl, inner), F32),
                        pltpu.VMEM((tl, bc), F32),
                        pltpu.VMEM((tl, bc), F32)],
        compiler_params=_params("arbitrary"),
        name="ssd",
    )(proj, proj, proj, proj, dt_raw,
      cw[:, :inner], cw[:, inner:inner + bc], cw[:, inner + bc:],
      cbias[:, :inner], cbias[:, inner:inner + bc], cbias[:, inner + bc:],
      dtb, alog, dsk, row(ssd_norm_w), expand)


def _merge_kernel(yp_ref, ys_ref, gp_ref, gs_ref, wp_ref, ws_ref, o_ref):
    a = jnp.dot(yp_ref[...], wp_ref[...], preferred_element_type=F32)
    b = jnp.dot(ys_ref[...], ws_ref[...], preferred_element_type=F32)
    o_ref[...] = (jax.nn.sigmoid(gp_ref[...].astype(F32)) * a
                  + jax.nn.sigmoid(gs_ref[...].astype(F32)) * b).astype(o_ref.dtype)


def _merge(ypool, yssd, proj, gp_off, gs_off, wbp, wbs):
    t, d = ypool.shape
    inner = yssd.shape[1]
    tm, tn = min(MERGE_TM, t), MERGE_TN
    return pl.pallas_call(
        _merge_kernel,
        grid=(t // tm, d // tn),
        in_specs=[pl.BlockSpec((tm, d), lambda i, j: (i, 0)),
                  pl.BlockSpec((tm, inner), lambda i, j: (i, 0)),
                  pl.BlockSpec((tm, tn), lambda i, j: (i, gp_off // tn + j)),
                  pl.BlockSpec((tm, tn), lambda i, j: (i, gs_off // tn + j)),
                  pl.BlockSpec((d, tn), lambda i, j: (0, j)),
                  pl.BlockSpec((inner, tn), lambda i, j: (0, j))],
        out_specs=pl.BlockSpec((tm, tn), lambda i, j: (i, j)),
        out_shape=jax.ShapeDtypeStruct((t, d), BF16),
        compiler_params=_params("arbitrary", "arbitrary"),
        name="merge",
    )(ypool, yssd, proj, proj, wbp, wbs)


def _out_kernel(m_ref, wo_ref, x_ref, mod_ref, wr_ref, br_ref,
                h_ref, u_ref, idx_ref, gate_ref, rank_ref, cnt_ref, carry_s):
    i = pl.program_id(0)
    tm = m_ref.shape[0]
    ne = wr_ref.shape[0]

    @pl.when(i == 0)
    def _():
        carry_s[...] = jnp.zeros_like(carry_s)

    mix = jnp.dot(m_ref[...], wo_ref[...], preferred_element_type=F32)
    h1 = x_ref[...] + mod_ref[0, 2:3, :] * mix
    h_ref[...] = h1
    ms = jnp.mean(h1 * h1, axis=-1, keepdims=True)
    u = h1 * lax.rsqrt(ms + EPS) * (1.0 + mod_ref[0, 4:5, :]) + mod_ref[0, 3:4, :]
    u_ref[...] = _rows_to_tiles(_pack_bf16_pairs(u))
    logits = lax.dot_general(wr_ref[...], u, (((1,), (1,)), ((), ())), precision=HIGHEST,
                             preferred_element_type=F32) + br_ref[...]
    eidx = lax.broadcasted_iota(jnp.int32, (ne, tm), 0)
    work = logits
    vals, idxs, hots = [], [], []
    for _ in range(TOP_K):
        mx = jnp.max(work, axis=0, keepdims=True)
        sel = jnp.min(jnp.where(work == mx, eidx, ne), axis=0, keepdims=True)
        hot = eidx == sel
        vals.append(mx)
        idxs.append(sel)
        hots.append(hot)
        work = jnp.where(hot, -jnp.inf, work)
    exps = [jnp.exp(v - vals[0]) for v in vals]
    den = exps[0]
    for e in exps[1:]:
        den = den + e
    cnt = hots[0].astype(F32)
    for hot in hots[1:]:
        cnt = cnt + hot.astype(F32)
    ti = lax.broadcasted_iota(jnp.int32, (tm, tm), 0)
    tj = lax.broadcasted_iota(jnp.int32, (tm, tm), 1)
    before = (ti < tj).astype(BF16)
    prefix = jnp.dot(cnt.astype(BF16), before, preferred_element_type=F32)
    base = carry_s[:, 0:1] + prefix
    pad_rows = idx_ref.shape[0] - TOP_K
    ranks = [jnp.sum(jnp.where(hot, base, 0.0), axis=0, keepdims=True) for hot in hots]
    idx_ref[...] = jnp.concatenate(idxs + [jnp.zeros((pad_rows, tm), jnp.int32)], axis=0)
    gate_ref[...] = jnp.concatenate([e / den for e in exps] + [jnp.zeros((pad_rows, tm), F32)], axis=0)
    rank_ref[...] = jnp.concatenate([r.astype(jnp.int32) for r in ranks]
                                    + [jnp.zeros((pad_rows, tm), jnp.int32)], axis=0)
    carry_s[...] = carry_s[...] + jnp.sum(cnt, axis=1, keepdims=True)
    cnt_ref[...] = carry_s[...].astype(jnp.int32)


def _outproj(merged, wo, x2, mod3, w_router, b_router, seq):
    t, d = x2.shape
    ne = w_router.shape[1]
    tm = min(OUT_TM, seq)
    tiles_per_seq = seq // tm
    rows = 8
    return pl.pallas_call(
        _out_kernel,
        grid=(t // tm,),
        in_specs=[pl.BlockSpec((tm, d), lambda i: (i, 0)),
                  pl.BlockSpec((d, d), lambda i: (0, 0), pipeline_mode=pl.Buffered(1)),
                  pl.BlockSpec((tm, d), lambda i: (i, 0)),
                  pl.BlockSpec((1, N_MOD, d), lambda i: (i // tiles_per_seq, 0, 0)),
                  pl.BlockSpec((ne, d), lambda i: (0, 0)),
                  pl.BlockSpec((ne, 1), lambda i: (0, 0))],
        out_specs=[pl.BlockSpec((tm, d), lambda i: (i, 0)),
                   pl.BlockSpec((tm,) + _token_tile(d), lambda i: (i, 0, 0)),
                   pl.BlockSpec((rows, tm), lambda i: (0, i)),
                   pl.BlockSpec((rows, tm), lambda i: (0, i)),
                   pl.BlockSpec((rows, tm), lambda i: (0, i)),
                   pl.BlockSpec((ne, LANES), lambda i: (0, 0))],
        out_shape=[jax.ShapeDtypeStruct((t, d), F32), jax.ShapeDtypeStruct((t,) + _token_tile(d), jnp.uint32),
                   jax.ShapeDtypeStruct((rows, t), jnp.int32), jax.ShapeDtypeStruct((rows, t), F32),
                   jax.ShapeDtypeStruct((rows, t), jnp.int32), jax.ShapeDtypeStruct((ne, LANES), jnp.int32)],
        scratch_shapes=[pltpu.VMEM((ne, LANES), F32)],
        compiler_params=_params("arbitrary"),
        name="outproj",
    )(merged, wo, x2, mod3, w_router.astype(F32).T, b_router.astype(F32).reshape(ne, 1))


def _row_copy(src_hbm, src_row, dst_ref, dst_row, sem):
    return pltpu.make_async_copy(src_hbm.at[pl.ds(src_row, 1)], dst_ref.at[pl.ds(dst_row, 1)], sem)


def _dispatch_kernel(slot_ref, u_ref, init_hbm, xs_hbm, sem):
    del init_hbm
    ts = u_ref.shape[0]

    def issue(grp, carry):
        for q in range(DISPATCH_TOKENS_PER_ITER):
            tok = DISPATCH_TOKENS_PER_ITER * grp + q
            for k in range(TOP_K):
                _row_copy(u_ref, tok, xs_hbm, slot_ref[0, 0, tok * TOP_K + k], sem).start(priority=k % 2)
        return carry

    lax.fori_loop(0, ts // DISPATCH_TOKENS_PER_ITER, issue, 0)
    rows = pl.ds(0, ts * TOP_K)
    pltpu.make_async_copy(xs_hbm.at[rows], xs_hbm.at[rows], sem).wait()


def _dispatch(u2, slot_tk, n_slots):
    t, tile = u2.shape[0], u2.shape[1:]
    ts = min(DISPATCH_TS, t)
    slots3 = slot_tk.reshape(t // ts, 1, ts * TOP_K)
    init = jnp.zeros((n_slots,) + tile, u2.dtype)
    return pl.pallas_call(
        _dispatch_kernel,
        grid=(t // ts,),
        in_specs=[pl.BlockSpec((1, 1, ts * TOP_K), lambda i: (i, 0, 0), memory_space=pltpu.SMEM),
                  pl.BlockSpec((ts,) + tile, lambda i: (i, 0, 0)),
                  pl.BlockSpec(memory_space=pl.ANY)],
        out_specs=pl.BlockSpec(memory_space=pl.ANY),
        out_shape=jax.ShapeDtypeStruct((n_slots,) + tile, u2.dtype),
        scratch_shapes=[pltpu.SemaphoreType.DMA(())],
        input_output_aliases={2: 0},
        compiler_params=_params("arbitrary"),
        name="dispatch",
    )(slots3, u2, init)


ITEM_IDLE, ITEM_ACTIVE, ITEM_ZERO_FILL, ITEM_ACTIVE_HALF, ITEM_ACTIVE_QUARTER = 0, 1, 2, 3, 5


def _moe_kernel(tb_ref, tj_ref, te_ref, tr_ref, tflag_ref, tout_ref, tslot_ref, tne_ref, tnj_ref, thn_ref,
                x_ref, wgu_hbm, bg_ref, bu_ref, wd_hbm, bd_ref, o_ref,
                wg_s, wu_s, wd_s, xb_s, acc_s, wg_buf, wu_buf, wd_buf, sem, *, last_j):
    del tb_ref, tout_ref
    i = pl.program_id(0)
    flag = tflag_ref[i]
    r = tr_ref[i]
    j = tj_ref[i]
    half = x_ref.shape[1] * x_ref.shape[2]
    tf = wg_s.shape[1]
    up_col0 = wgu_hbm.shape[2] // 2

    def weight_copies(e, jt, slot):
        c0 = pl.multiple_of(jt * tf, tf)
        return (pltpu.make_async_copy(wgu_hbm.at[e, :, pl.ds(c0, tf)], wg_buf.at[slot], sem.at[slot]),
                pltpu.make_async_copy(wgu_hbm.at[e, :, pl.ds(up_col0 + c0, tf)], wu_buf.at[slot], sem.at[slot]),
                pltpu.make_async_copy(wd_hbm.at[e, pl.ds(c0, tf), :], wd_buf.at[slot], sem.at[slot]))

    is_active = (flag & 1) == 1

    @pl.when(i == 0)
    def _():
        acc_s[...] = jnp.zeros_like(acc_s)

    @pl.when(is_active & (r == 0))
    def _():
        slot = tslot_ref[i]

        @pl.when(i == 0)
        def _():
            for cp in weight_copies(te_ref[i], j, slot):
                cp.start()

        for cp in weight_copies(te_ref[i], j, slot):
            cp.wait()
        for s in range(2):
            @pl.when(slot == s)
            def _():
                wg_s[...] = wg_buf[s].astype(BF16)
                wu_s[...] = wu_buf[s].astype(BF16)
                wd_s[...] = wd_buf[s].astype(BF16)

        @pl.when(thn_ref[i] == 1)
        def _():
            for cp in weight_copies(tne_ref[i], tnj_ref[i], 1 - slot):
                cp.start()

    @pl.when(is_active & (j == 0))
    def _():
        lo, hi = _unpack_bf16_pairs(_tiles_to_rows(x_ref[...]))
        xb_s[r, :, :half] = lo
        xb_s[r, :, half:] = hi

    def expert_mlp(rows):
        xb = xb_s[r, 0:rows, :]
        gate = jnp.dot(xb, wg_s[...], preferred_element_type=F32) + bg_ref[...]
        up = jnp.dot(xb, wu_s[...], preferred_element_type=F32) + bu_ref[...]
        gate = jnp.minimum(gate, SWIGLU_LIMIT)
        up = jnp.clip(up, -SWIGLU_LIMIT, SWIGLU_LIMIT)
        act = (up + 1.0) * gate * jax.nn.sigmoid(SWIGLU_ALPHA * gate)
        part = jnp.dot(act.astype(BF16), wd_s[...], preferred_element_type=F32)
        base = jnp.where(j == 0, jnp.broadcast_to(bd_ref[...], (rows, bd_ref.shape[1])), acc_s[r, 0:rows, :])
        acc_s[r, 0:rows, :] = base + part

        @pl.when(j == last_j)
        def _():
            o_ref[0:rows] = _rows_to_tiles(_pack_bf16_pairs(acc_s[r, 0:rows, :]))
            if rows < o_ref.shape[0]:
                o_ref[rows:] = jnp.zeros((o_ref.shape[0] - rows,) + o_ref.shape[1:], o_ref.dtype)

    @pl.when(flag == ITEM_ACTIVE)
    def _():
        expert_mlp(o_ref.shape[0])

    @pl.when(flag == ITEM_ACTIVE_HALF)
    def _():
        expert_mlp(o_ref.shape[0] // 2)

    @pl.when(flag == ITEM_ACTIVE_QUARTER)
    def _():
        expert_mlp(o_ref.shape[0] // 4)

    @pl.when(flag == ITEM_ZERO_FILL)
    def _():
        o_ref[...] = jnp.zeros_like(o_ref)


def _lookup(table, idx):
    hot = idx[:, None] == jnp.arange(table.shape[0], dtype=jnp.int32)[None, :]
    return jnp.sum(jnp.where(hot, table[None, :].astype(jnp.int32), 0), axis=1).astype(jnp.int32)


def _moe_tables(counts, n_blocks, ft):
    ne = counts.shape[0]
    padded = (counts + MOE_BM - 1) // MOE_BM * MOE_BM
    pend = jnp.cumsum(padded)
    pstart = pend - padded
    blocks_per_expert = padded // MOE_BM
    first_block = pstart // MOE_BM
    n_active = (pend[-1] // MOE_BM).astype(jnp.int32)

    blocks = jnp.arange(n_blocks, dtype=jnp.int32)
    block_expert = jnp.minimum(jnp.sum(pend[None, :] <= (blocks * MOE_BM)[:, None], axis=1), ne - 1).astype(jnp.int32)
    off = blocks - _lookup(first_block, block_expert)
    is_start = (off % MOE_CHUNK_BLOCKS == 0) & (blocks < n_active)
    chunk_size = jnp.minimum(MOE_CHUNK_BLOCKS,
                             _lookup(blocks_per_expert, block_expert) - off // MOE_CHUNK_BLOCKS * MOE_CHUNK_BLOCKS)

    items = jnp.arange(n_blocks * ft, dtype=jnp.int32)
    active = items < ft * n_active
    src = jnp.minimum(items, jnp.maximum(ft * n_active - 1, 0))
    cand = jnp.where(is_start[None, :] & (ft * blocks[None, :] <= src[:, None]), blocks[None, :], 0)
    cstart = jnp.max(cand, axis=1).astype(jnp.int32)
    m = jnp.maximum(_lookup(chunk_size, cstart), 1)
    local = src - ft * cstart
    tj = local // m
    tr = local % m
    tb = cstart + tr
    te = _lookup(block_expert, cstart)
    k = items - ft * n_active
    zero_fill = (~active) & (n_active + k < n_blocks)
    tout = jnp.where(active, jnp.where(tj == ft - 1, tb, cstart), jnp.minimum(n_active + k, n_blocks - 1))
    valid_rows = _lookup(counts, te) - (tb - _lookup(first_block, te)) * MOE_BM
    kind = jnp.where(valid_rows <= MOE_BM // 4, ITEM_ACTIVE_QUARTER,
                     jnp.where(valid_rows <= MOE_BM // 2, ITEM_ACTIVE_HALF, ITEM_ACTIVE))
    tflag = jnp.where(active, kind, jnp.where(zero_fill, ITEM_ZERO_FILL, ITEM_IDLE))
    group_start = active & (tr == 0)
    tslot = (jnp.cumsum(group_start.astype(jnp.int32)) - 1) % 2
    nxt = items + m
    has_next = group_start & (nxt < ft * n_active)
    nxt = jnp.minimum(nxt, n_blocks * ft - 1)
    tne, tnj = jnp.take(te, nxt), jnp.take(tj, nxt)
    as_i32 = lambda v: v.astype(jnp.int32)
    return pstart, tuple(map(as_i32, (tb, tj, te, tr, tflag, tout, tslot, tne, tnj, has_next)))


def _moe(xs, tables, wgu, bgu, wd, bd):
    n_slots = xs.shape[0]
    ne, d, f2 = wgu.shape
    f = f2 // 2
    bm, tf = MOE_BM, min(MOE_TF, f)
    nb, ft = n_slots // bm, f // tf
    grid_spec = pltpu.PrefetchScalarGridSpec(
        num_scalar_prefetch=len(tables),
        grid=(nb * ft,),
        in_specs=[pl.BlockSpec((bm,) + _token_tile(d), lambda i, tb, tj, te, *_: (tb[i], 0, 0)),
                  pl.BlockSpec(memory_space=pl.ANY),
                  pl.BlockSpec((None, 1, tf), lambda i, tb, tj, te, *_: (te[i], 0, tj[i])),
                  pl.BlockSpec((None, 1, tf), lambda i, tb, tj, te, *_: (te[i], 0, ft + tj[i])),
                  pl.BlockSpec(memory_space=pl.ANY),
                  pl.BlockSpec((None, 1, d), lambda i, tb, tj, te, *_: (te[i], 0, 0))],
        out_specs=pl.BlockSpec((bm,) + _token_tile(d), lambda i, tb, tj, te, tr, tg, to, *_: (to[i], 0, 0)),
        scratch_shapes=[pltpu.VMEM((d, tf), BF16), pltpu.VMEM((d, tf), BF16), pltpu.VMEM((tf, d), BF16),
                        pltpu.VMEM((MOE_CHUNK_BLOCKS, bm, d), BF16), pltpu.VMEM((MOE_CHUNK_BLOCKS, bm, d), F32),
                        pltpu.VMEM((2, d, tf), F32), pltpu.VMEM((2, d, tf), F32), pltpu.VMEM((2, tf, d), F32),
                        pltpu.SemaphoreType.DMA((2,))],
    )
    return pl.pallas_call(
        functools.partial(_moe_kernel, last_j=ft - 1),
        grid_spec=grid_spec,
        out_shape=jax.ShapeDtypeStruct((n_slots,) + _token_tile(d), jnp.uint32),
        compiler_params=_params("arbitrary"),
        name="moe",
    )(*tables, xs, wgu, bgu, bgu, wd, bd)


def _final_kernel(slot_ref, next_slot_ref, h_ref, gate_ref, mod_ref, fw_ref, y_hbm, o_ref, ybuf, sem):
    i = pl.program_id(0)
    tc, d = h_ref.shape
    half = d // 2
    n_rows = tc * TOP_K
    cur = i % 2

    def gather_rows(slots_ref, buf):
        def issue(grp, carry):
            for q in range(FINAL_ISSUE_UNROLL):
                tk = FINAL_ISSUE_UNROLL * grp + q
                _row_copy(y_hbm, slots_ref[0, 0, tk], ybuf.at[buf], tk, sem.at[buf]).start(priority=q % 2)
            return carry

        lax.fori_loop(0, n_rows // FINAL_ISSUE_UNROLL, issue, 0)

    @pl.when(i == 0)
    def _():
        gather_rows(slot_ref, 0)

    @pl.when(i + 1 < pl.num_programs(0))
    def _():
        gather_rows(next_slot_ref, 1 - cur)

    pltpu.make_async_copy(y_hbm.at[pl.ds(0, n_rows)], ybuf.at[cur], sem.at[cur]).wait()

    ffn_lo = jnp.zeros((tc, half), F32)
    ffn_hi = jnp.zeros((tc, half), F32)
    for k in range(TOP_K):
        w = _tiles_to_rows(ybuf[cur, k * tc:(k + 1) * tc])
        g = gate_ref[:, k:k + 1]
        ffn_lo = ffn_lo + g * lax.bitcast_convert_type(w << 16, F32)
        ffn_hi = ffn_hi + g * lax.bitcast_convert_type(w & jnp.uint32(0xFFFF0000), F32)
    h2_lo = h_ref[:, :half] + mod_ref[0, 5:6, :half] * ffn_lo
    h2_hi = h_ref[:, half:] + mod_ref[0, 5:6, half:] * ffn_hi
    ms = (jnp.sum(h2_lo * h2_lo, axis=-1, keepdims=True) + jnp.sum(h2_hi * h2_hi, axis=-1, keepdims=True)) / d
    inv = lax.rsqrt(ms + EPS)
    o_ref[:, :half] = (h2_lo * inv * fw_ref[:, :half]).astype(o_ref.dtype)
    o_ref[:, half:] = (h2_hi * inv * fw_ref[:, half:]).astype(o_ref.dtype)


def _final(h1, gates_tk, slot_kt, mod3, final_norm_w, y_slots, seq, out_dtype):
    t, d = h1.shape
    tc = min(FINAL_TC, seq)
    tiles_per_seq = seq // tc
    slots3 = slot_kt.reshape(TOP_K, t // tc, tc).transpose(1, 0, 2).reshape(t // tc, 1, TOP_K * tc)
    return pl.pallas_call(
        _final_kernel,
        grid=(t // tc,),
        in_specs=[pl.BlockSpec((1, 1, TOP_K * tc), lambda i: (i, 0, 0), memory_space=pltpu.SMEM),
                  pl.BlockSpec((1, 1, TOP_K * tc), lambda i: (jnp.minimum(i + 1, t // tc - 1), 0, 0),
                               memory_space=pltpu.SMEM),
                  pl.BlockSpec((tc, d), lambda i: (i, 0)),
                  pl.BlockSpec((tc, gates_tk.shape[1]), lambda i: (i, 0)),
                  pl.BlockSpec((1, N_MOD, d), lambda i: (i // tiles_per_seq, 0, 0)),
                  pl.BlockSpec((1, d), lambda i: (0, 0)),
                  pl.BlockSpec(memory_space=pl.ANY)],
        out_specs=pl.BlockSpec((tc, d), lambda i: (i, 0)),
        out_shape=jax.ShapeDtypeStruct((t, d), out_dtype),
        scratch_shapes=[pltpu.VMEM((2, TOP_K * tc) + _token_tile(d), jnp.uint32), pltpu.SemaphoreType.DMA((2,))],
        compiler_params=_params("arbitrary"),
        name="final",
    )(slots3, slots3, h1, gates_tk, mod3, final_norm_w.astype(F32).reshape(1, d), y_slots)


def kernel(x, c, w_ada, b_ada, w_in, pool_w, pool_scale, conv_w, conv_b, dt_bias, a_log, d_skip, ssd_norm_w,
           w_branch_pool, w_branch_ssd, w_out, w_router, b_router, w_gate_up, b_gate_up, w_down, b_down,
           final_norm_w):
    bsz, seq, d = x.shape
    depth = w_ada.shape[0]
    t = bsz * seq
    inner = ssd_norm_w.shape[1]
    heads = dt_bias.shape[1]
    bc = SSD_GROUPS * SSD_STATE
    ne = w_router.shape[2]
    assert depth == 1, "the final RMSNorm is fused into the last kernel of the single layer"
    assert SEQ_TILE == 2 * LANES
    assert seq % SEQ_TILE == 0 and heads <= LANES and (7 * d) % bc == 0 and (4 * d) % inner == 0
    assert inner // SSD_GROUPS % LANES == 0 and d % (len(POOL_WINDOWS) * LANES) == 0
    assert d % (2 * SUBLANES * LANES) == 0

    s1 = d
    s2 = s1 + inner
    s3 = s2 + inner + 2 * bc
    s4 = s3 + heads
    s5 = s4 + d
    blk = {"z": 0, "xs": 1, "p": (2 * inner) // d, "B": (2 * inner + 3 * d) // bc, "C": (2 * inner + 3 * d) // bc + 1}
    gp_off, gs_off = 2 * inner + d, 2 * inner + 2 * d

    h = x.astype(F32).reshape(t, d)
    for layer in range(depth):
        wl = w_in[layer]
        w_main = jnp.concatenate([wl[:, s1:s2], wl[:, s2:s2 + inner], wl[:, :s1], wl[:, s4:s5], wl[:, s5:],
                                  wl[:, s2 + inner:s2 + inner + bc], wl[:, s2 + inner + bc:s3]], axis=1).astype(BF16)
        w_dt = jnp.pad(wl[:, s3:s4].astype(F32), ((0, 0), (0, LANES - heads)))

        mod3 = _ada(c, w_ada[layer], b_ada[layer]).reshape(bsz, N_MOD, d)
        proj, dt_raw = _inproj(h, mod3, w_main, w_dt, seq)
        ypool = _pool(proj, blk["p"], pool_w[layer].astype(BF16), pool_scale[layer], seq, d)
        yssd = _ssd(proj, dt_raw, blk, conv_w[layer], conv_b[layer], dt_bias[layer], a_log[layer], d_skip[layer],
                    ssd_norm_w[layer], seq)
        merged = _merge(ypool, yssd, proj, gp_off, gs_off, w_branch_pool[layer].astype(BF16),
                        w_branch_ssd[layer].astype(BF16))
        h1, u2, idx_kt, gate_kt, rank_kt, counts = _outproj(merged, w_out[layer].astype(BF16), h, mod3,
                                                            w_router[layer], b_router[layer], seq)

        n_blocks = (t * TOP_K) // MOE_BM + ne
        ft = w_down.shape[2] // min(MOE_TF, w_down.shape[2])
        pstart, tables = _moe_tables(counts[:, 0], n_blocks, ft)
        eids = jnp.arange(ne, dtype=jnp.int32)[:, None, None]
        slot_kt = (jnp.sum(jnp.where(idx_kt[:TOP_K][None] == eids, pstart[:, None, None], 0), axis=0)
                   + rank_kt[:TOP_K]).astype(jnp.int32)

        xs = _dispatch(u2, slot_kt.T, n_blocks * MOE_BM)
        y_slots = _moe(xs, tables, w_gate_up[layer].astype(F32), b_gate_up[layer].astype(F32)[:, None, :],
                       w_down[layer].astype(F32), b_down[layer].astype(F32)[:, None, :])
        h = _final(h1, gate_kt.T, slot_kt, mod3, final_norm_w, y_slots, seq, x.dtype)
    return h.reshape(bsz, seq, d)
```
